```python
import jax
import jax.numpy as jnp
from jax import lax
import numpy as np

D_MODEL = 4096
BATCH = 2
SEQ = 4096
DEPTH = 2

HEAD_DIM = 128
GROUP_HEADS = D_MODEL // HEAD_DIM // 2
GROUP_WIDTH = GROUP_HEADS * HEAD_DIM
D_MIX = 2 * GROUP_WIDTH
ROPE_THETA = 10000.0
QBLK = 128
EPS = 1e-6

DSA_Q_LORA = D_MODEL // 4
DSA_IDX_HEADS = D_MODEL // 128
DSA_IDX_DIM = 128
DSA_IDX_ROPE = 64
DSA_TOPK = 256

MOBA_BLOCK = 256
MOBA_TOPK = 3
MOBA_QCHUNK = 16

SWA_KV_HEADS = 2
SWA_WINDOW = 128

N_EXPERTS = 32
MOE_TOPK = 4
EXPERT_FF = D_MODEL // 8
SWIGLU_LIMIT = 7.0
SWIGLU_ALPHA = 1.702

FOX_FORGET_BIAS_INIT = 3.0

N_EVEN = (DEPTH + 1) // 2
N_ODD = DEPTH // 2

EVEN_SPLITS = (DSA_Q_LORA, HEAD_DIM, HEAD_DIM, DSA_IDX_DIM, DSA_IDX_HEADS,
               GROUP_WIDTH, GROUP_WIDTH, GROUP_WIDTH, GROUP_HEADS, GROUP_WIDTH)
ODD_SPLITS = (GROUP_WIDTH, GROUP_WIDTH, GROUP_WIDTH,
              GROUP_WIDTH, SWA_KV_HEADS * HEAD_DIM, SWA_KV_HEADS * HEAD_DIM)
D_IN_EVEN = sum(EVEN_SPLITS)
D_IN_ODD = sum(ODD_SPLITS)

kernel_name = 'hybrid_dsa_fox_moba_swa_moe_adaln'


def rms_norm(x, g):
    xf = x.astype(jnp.float32)
    y = xf * lax.rsqrt(jnp.mean(xf * xf, axis=-1, keepdims=True) + EPS)
    return (y * g.astype(jnp.float32)).astype(x.dtype)


def layer_norm(x, g, b):
    xf = x.astype(jnp.float32)
    mu = jnp.mean(xf, axis=-1, keepdims=True)
    var = jnp.mean(jnp.square(xf - mu), axis=-1, keepdims=True)
    y = (xf - mu) * lax.rsqrt(var + EPS)
    return (y * g.astype(jnp.float32) + b.astype(jnp.float32)).astype(x.dtype)


def rope(x, pos, rot_dim):
    half = rot_dim // 2
    inv_freq = ROPE_THETA ** (-jnp.arange(half, dtype=jnp.float32) / half)
    ang = pos.astype(jnp.float32)[:, None] * inv_freq[None, :]
    cos = jnp.cos(ang)[:, None, :]
    sin = jnp.sin(ang)[:, None, :]
    xr = x[..., :rot_dim].astype(jnp.float32)
    x1, x2 = xr[..., :half], xr[..., half:]
    rot = jnp.concatenate([x1 * cos - x2 * sin, x2 * cos + x1 * sin], axis=-1).astype(x.dtype)
    return jnp.concatenate([rot, x[..., rot_dim:]], axis=-1)


def split_cols(z, sizes):
    return jnp.split(z, np.cumsum(sizes)[:-1].tolist(), axis=-1)


def to_blocks(a, blk):
    b, s = a.shape[:2]
    return jnp.moveaxis(a.reshape((b, s // blk, blk) + a.shape[2:]), 1, 0)


def from_blocks(o):
    nb, b, blk = o.shape[:3]
    return jnp.moveaxis(o, 0, 1).reshape(b, nb * blk, -1)


def modulate(h, shift, scale):
    return h * (1.0 + scale[:, None, :]) + shift[:, None, :]


def dsa_attention(q, k, v, q_idx, k_idx, w_idx):
    s_len = q.shape[1]
    topk = min(DSA_TOPK, s_len // 4)
    key_pos = jnp.arange(s_len)
    scale = HEAD_DIM ** -0.5

    def step(args):
        i, qb, qib, wb = args
        t = i * QBLK + jnp.arange(QBLK)
        causal = key_pos[None, :] <= t[:, None]
        dots = jnp.einsum('bthd,bsd->bths', qib, k_idx, preferred_element_type=jnp.float32)
        score = jnp.einsum('bths,bth->bts', jax.nn.relu(dots), wb)
        score = jnp.where(causal[None], score, -jnp.inf)
        _, idx = lax.top_k(score, topk)
        valid = idx <= t[None, :, None]
        kg = jax.vmap(lambda kb, ib: kb[ib])(k, idx)
        vg = jax.vmap(lambda vb, ib: vb[ib])(v, idx)
        logits = jnp.einsum('bthd,btkd->bthk', qb, kg, preferred_element_type=jnp.float32) * scale
        logits = jnp.where(valid[:, :, None, :], logits, -jnp.inf)
        p = jax.nn.softmax(logits, axis=-1).astype(v.dtype)
        return jnp.einsum('bthk,btkd->bthd', p, vg)

    n_blk = s_len // QBLK
    out = lax.map(step, (jnp.arange(n_blk), to_blocks(q, QBLK), to_blocks(q_idx, QBLK), to_blocks(w_idx, QBLK)))
    return from_blocks(out)


def fox_attention(q, k, v, log_f):
    s_len = q.shape[1]
    cum = lax.cumsum(log_f, axis=1)
    cum_k = jnp.transpose(cum, (0, 2, 1))
    key_pos = jnp.arange(s_len)
    scale = HEAD_DIM ** -0.5

    def step(args):
        i, qb, cb = args
        t = i * QBLK + jnp.arange(QBLK)
        causal = key_pos[None, :] <= t[:, None]
        logits = jnp.einsum('bthd,bshd->bhts', qb, k, preferred_element_type=jnp.float32) * scale
        logits = logits + jnp.transpose(cb, (0, 2, 1))[:, :, :, None] - cum_k[:, :, None, :]
        logits = jnp.where(causal[None, None], logits, -jnp.inf)
        p = jax.nn.softmax(logits, axis=-1).astype(v.dtype)
        return jnp.einsum('bhts,bshd->bthd', p, v)

    out = lax.map(step, (jnp.arange(s_len // QBLK), to_blocks(q, QBLK), to_blocks(cum, QBLK)))
    return from_blocks(out)


def moba_attention(q, k, v):
    b, s_len, h, dh = q.shape
    nb = -(-s_len // MOBA_BLOCK)
    padw = ((0, 0), (0, nb * MOBA_BLOCK - s_len), (0, 0), (0, 0))
    kblk = jnp.pad(k, padw).reshape(b, nb, MOBA_BLOCK, h, dh)
    vblk = jnp.pad(v, padw).reshape(b, nb, MOBA_BLOCK, h, dh)
    kmean = jnp.mean(kblk.astype(jnp.float32), axis=2)
    k_bh = jnp.transpose(kblk, (0, 3, 1, 2, 4))
    v_bh = jnp.transpose(vblk, (0, 3, 1, 2, 4))
    n_sel = min(MOBA_TOPK, nb - 1)
    bi = jnp.arange(b)[:, None, None, None]
    hi = jnp.arange(h)[None, None, :, None]
    blk_ids = jnp.arange(nb)
    scale = dh ** -0.5

    def step(args):
        i, qb = args
        t = i * MOBA_QCHUNK + jnp.arange(MOBA_QCHUNK)
        bt = (i * MOBA_QCHUNK) // MOBA_BLOCK
        own_k = lax.dynamic_index_in_dim(kblk, bt, axis=1, keepdims=False)
        own_v = lax.dynamic_index_in_dim(vblk, bt, axis=1, keepdims=False)
        own_pos = bt * MOBA_BLOCK + jnp.arange(MOBA_BLOCK)
        own_logits = jnp.einsum('bthd,bshd->bths', qb, own_k, preferred_element_type=jnp.float32) * scale
        own_logits = jnp.where((own_pos[None, :] <= t[:, None])[None, :, None, :], own_logits, -jnp.inf)
        if n_sel == 0:
            p_only = jax.nn.softmax(own_logits, axis=-1).astype(v.dtype)
            return jnp.einsum('bths,bshd->bthd', p_only, own_v)
        gate = jnp.einsum('bthd,bnhd->bthn', qb, kmean, preferred_element_type=jnp.float32)
        gate = jnp.where((blk_ids < bt)[None, None, None, :], gate, -jnp.inf)
        _, sel = lax.top_k(gate, n_sel)
        valid = sel < bt
        ksel = k_bh[bi, hi, sel]
        vsel = v_bh[bi, hi, sel]
        sel_logits = jnp.einsum('bthd,bthnsd->bthns', qb, ksel, preferred_element_type=jnp.float32) * scale
        sel_logits = jnp.where(valid[..., None], sel_logits, -jnp.inf)
        sel_logits = sel_logits.reshape(b, MOBA_QCHUNK, h, n_sel * MOBA_BLOCK)
        p = jax.nn.softmax(jnp.concatenate([sel_logits, own_logits], axis=-1), axis=-1).astype(v.dtype)
        p_sel = p[..., :n_sel * MOBA_BLOCK].reshape(b, MOBA_QCHUNK, h, n_sel, MOBA_BLOCK)
        p_own = p[..., n_sel * MOBA_BLOCK:]
        return (jnp.einsum('bthns,bthnsd->bthd', p_sel, vsel)
                + jnp.einsum('bths,bshd->bthd', p_own, own_v))

    out = lax.map(step, (jnp.arange(s_len // MOBA_QCHUNK), to_blocks(q, MOBA_QCHUNK)))
    return from_blocks(out)


def swa_sink_attention(q, k, v, sinks):
    b, s_len, hq, dh = q.shape
    hkv = k.shape[2]
    g = hq // hkv
    w = SWA_WINDOW
    nb = s_len // w
    qb = q.reshape(b, nb, w, hkv, g, dh)
    kb = k.reshape(b, nb, w, hkv, dh)
    vb = v.reshape(b, nb, w, hkv, dh)
    shift = ((0, 0), (1, 0), (0, 0), (0, 0), (0, 0))
    kband = jnp.concatenate([jnp.pad(kb, shift)[:, :-1], kb], axis=2)
    vband = jnp.concatenate([jnp.pad(vb, shift)[:, :-1], vb], axis=2)
    qi = jnp.arange(w)[:, None] + w
    kj = jnp.arange(2 * w)[None, :]
    band = (kj <= qi) & (qi - kj < SWA_WINDOW)
    exists = (jnp.arange(nb)[:, None] * w + jnp.arange(2 * w)[None, :] - w) >= 0
    mask = band[None] & exists[:, None, :]
    logits = jnp.einsum('bnqkgd,bnskd->bnkgqs', qb, kband, preferred_element_type=jnp.float32) * dh ** -0.5
    logits = jnp.where(mask[None, :, None, None], logits, -jnp.inf)
    sink = jnp.broadcast_to(sinks.astype(jnp.float32).reshape(hkv, g)[None, None, :, :, None, None],
                            logits.shape[:-1] + (1,))
    p = jax.nn.softmax(jnp.concatenate([logits, sink], axis=-1), axis=-1)[..., :-1].astype(v.dtype)
    out = jnp.einsum('bnkgqs,bnskd->bnqkgd', p, vband)
    return out.reshape(b, s_len, hq * dh)


def clamped_swiglu(hh):
    x_glu = jnp.minimum(hh[..., ::2], SWIGLU_LIMIT)
    x_lin = jnp.clip(hh[..., 1::2], -SWIGLU_LIMIT, SWIGLU_LIMIT)
    return x_glu * jax.nn.sigmoid(SWIGLU_ALPHA * x_glu) * (x_lin + 1.0)


def moe_ffn(h, w_router, b_router, w_exp_in, b_exp_in, w_exp_out, b_exp_out):
    b, s_len, d = h.shape
    xt = h.reshape(b * s_len, d)
    logits = jnp.dot(xt, w_router, preferred_element_type=jnp.float32) + b_router.astype(jnp.float32)
    top_val, top_idx = lax.top_k(logits, MOE_TOPK)
    top_w = jax.nn.softmax(top_val, axis=-1)
    combine = jnp.sum(jax.nn.one_hot(top_idx, N_EXPERTS, dtype=jnp.float32) * top_w[..., None], axis=1)
    acc = jnp.zeros((b * s_len, d), jnp.float32)
    for e in range(N_EXPERTS):
        y = clamped_swiglu(xt @ w_exp_in[e] + b_exp_in[e]) @ w_exp_out[e] + b_exp_out[e]
        acc = acc + combine[:, e:e + 1] * y.astype(jnp.float32)
    return acc.astype(h.dtype).reshape(b, s_len, d)


def even_mixer(h, pos, w_in, g_cq, w_uq, w_iq, gq_a, gk_a, g_kidx, b_kidx, gq_b, gk_b, b_f, w_out):
    b, s_len, _ = h.shape
    z = h @ w_in
    cq, ka, va, kidx, widx, qb, kb, vb, fb, gb = split_cols(z, EVEN_SPLITS)
    cq = rms_norm(cq, g_cq)
    qa = rope(rms_norm((cq @ w_uq).reshape(b, s_len, GROUP_HEADS, HEAD_DIM), gq_a), pos, HEAD_DIM)
    ka = rope(rms_norm(ka, gk_a)[:, :, None], pos, HEAD_DIM)[:, :, 0]
    qi = rope((cq @ w_iq).reshape(b, s_len, DSA_IDX_HEADS, DSA_IDX_DIM), pos, DSA_IDX_ROPE)
    ki = rope(layer_norm(kidx, g_kidx, b_kidx)[:, :, None], pos, DSA_IDX_ROPE)[:, :, 0]
    wi = widx.astype(jnp.float32) * (DSA_IDX_HEADS ** -0.5 * DSA_IDX_DIM ** -0.5)
    o_a = dsa_attention(qa, ka, va, qi, ki, wi)
    qb = rms_norm(qb.reshape(b, s_len, GROUP_HEADS, HEAD_DIM), gq_b)
    kb = rms_norm(kb.reshape(b, s_len, GROUP_HEADS, HEAD_DIM), gk_b)
    vb = vb.reshape(b, s_len, GROUP_HEADS, HEAD_DIM)
    log_f = jax.nn.log_sigmoid(fb.astype(jnp.float32) + b_f.astype(jnp.float32))
    o_b = fox_attention(qb, kb, vb, log_f) * jax.nn.sigmoid(gb)
    return jnp.concatenate([o_a, o_b], axis=-1) @ w_out


def odd_mixer(h, pos, w_in, gq_c, gk_c, gq_d, gk_d, sinks, w_out):
    b, s_len, _ = h.shape
    z = h @ w_in
    qc, kc, vc, qd, kd, vd = split_cols(z, ODD_SPLITS)
    qc = rope(rms_norm(qc.reshape(b, s_len, GROUP_HEADS, HEAD_DIM), gq_c), pos, HEAD_DIM)
    kc = rope(rms_norm(kc.reshape(b, s_len, GROUP_HEADS, HEAD_DIM), gk_c), pos, HEAD_DIM)
    vc = vc.reshape(b, s_len, GROUP_HEADS, HEAD_DIM)
    o_c = moba_attention(qc, kc, vc)
    qd = rope(rms_norm(qd.reshape(b, s_len, GROUP_HEADS, HEAD_DIM), gq_d), pos, HEAD_DIM)
    kd = rope(rms_norm(kd.reshape(b, s_len, SWA_KV_HEADS, HEAD_DIM), gk_d), pos, HEAD_DIM)
    vd = vd.reshape(b, s_len, SWA_KV_HEADS, HEAD_DIM)
    o_d = swa_sink_attention(qd, kd, vd, sinks)
    return jnp.concatenate([o_c, o_d], axis=-1) @ w_out


def setup_inputs(seed: int = 0) -> dict:
    key = jax.random.key(seed)
    keys = jax.random.split(key, 32)
    counter = iter(range(32))
    d = D_MODEL

    def nrm(shape, scale):
        return jax.random.normal(keys[next(counter)], shape, jnp.float32) * scale

    def gain(shape):
        return 1.0 + nrm(shape, 0.05)

    return {
        'x': nrm((BATCH, SEQ, d), 1.0),
        'c': nrm((BATCH, d), 1.0),
        'g_norm_mix': gain((DEPTH, d)),
        'g_norm_ffn': gain((DEPTH, d)),
        'w_ada': nrm((DEPTH, d, 6 * d), 0.5 * d ** -0.5),
        'b_ada': nrm((DEPTH, 6 * d), 0.02),
        'w_in_even': nrm((N_EVEN, d, D_IN_EVEN), d ** -0.5),
        'g_cq': gain((N_EVEN, DSA_Q_LORA)),
        'w_uq': nrm((N_EVEN, DSA_Q_LORA, GROUP_WIDTH), DSA_Q_LORA ** -0.5),
        'w_iq': nrm((N_EVEN, DSA_Q_LORA, DSA_IDX_HEADS * DSA_IDX_DIM), DSA_Q_LORA ** -0.5),
        'gq_a': gain((N_EVEN, HEAD_DIM)),
        'gk_a': gain((N_EVEN, HEAD_DIM)),
        'g_kidx': gain((N_EVEN, DSA_IDX_DIM)),
        'b_kidx': nrm((N_EVEN, DSA_IDX_DIM), 0.02),
        'gq_b': gain((N_EVEN, HEAD_DIM)),
        'gk_b': gain((N_EVEN, HEAD_DIM)),
        'b_forget': FOX_FORGET_BIAS_INIT + nrm((N_EVEN, GROUP_HEADS), 0.5),
        'w_out_even': nrm((N_EVEN, D_MIX, d), D_MIX ** -0.5),
        'w_in_odd': nrm((N_ODD, d, D_IN_ODD), d ** -0.5),
        'gq_c': gain((N_ODD, HEAD_DIM)),
        'gk_c': gain((N_ODD, HEAD_DIM)),
        'gq_d': gain((N_ODD, HEAD_DIM)),
        'gk_d': gain((N_ODD, HEAD_DIM)),
        'sinks_d': nrm((N_ODD, GROUP_HEADS), 0.5),
        'w_out_odd': nrm((N_ODD, D_MIX, d), D_MIX ** -0.5),
        'w_router': nrm((DEPTH, d, N_EXPERTS), d ** -0.5),
        'b_router': nrm((DEPTH, N_EXPERTS), 0.01),
        'w_exp_in': nrm((DEPTH, N_EXPERTS, d, 2 * EXPERT_FF), d ** -0.5),
        'b_exp_in': nrm((DEPTH, N_EXPERTS, 2 * EXPERT_FF), 0.02),
        'w_exp_out': nrm((DEPTH, N_EXPERTS, EXPERT_FF, d), EXPERT_FF ** -0.5),
        'b_exp_out': nrm((DEPTH, N_EXPERTS, d), 0.02),
    }


def reference(x, c, g_norm_mix, g_norm_ffn, w_ada, b_ada, w_in_even, g_cq, w_uq, w_iq,
              gq_a, gk_a, g_kidx, b_kidx, gq_b, gk_b, b_forget, w_out_even,
              w_in_odd, gq_c, gk_c, gq_d, gk_d, sinks_d, w_out_odd,
              w_router, b_router, w_exp_in, b_exp_in, w_exp_out, b_exp_out):
    s_len = x.shape[1]
    pos = jnp.arange(s_len, dtype=jnp.int32)
    cond = jax.nn.silu(c)
    for layer in range(DEPTH):
        mod = cond @ w_ada[layer] + b_ada[layer]
        shift_m, scale_m, gate_m, shift_f, scale_f, gate_f = jnp.split(mod, 6, axis=-1)
        h = modulate(rms_norm(x, g_norm_mix[layer]), shift_m, scale_m)
        j = layer // 2
        if layer % 2 == 0:
            mix = even_mixer(h, pos, w_in_even[j], g_cq[j], w_uq[j], w_iq[j], gq_a[j], gk_a[j],
                             g_kidx[j], b_kidx[j], gq_b[j], gk_b[j], b_forget[j], w_out_even[j])
        else:
            mix = odd_mixer(h, pos, w_in_odd[j], gq_c[j], gk_c[j], gq_d[j], gk_d[j],
                            sinks_d[j], w_out_odd[j])
        x = x + gate_m[:, None, :] * mix
        h = modulate(rms_norm(x, g_norm_ffn[layer]), shift_f, scale_f)
        x = x + gate_f[:, None, :] * moe_ffn(h, w_router[layer], b_router[layer], w_exp_in[layer],
                                             b_exp_in[layer], w_exp_out[layer], b_exp_out[layer])
    return x
```

```python
import functools

import numpy as np
import jax
import jax.numpy as jnp
from jax import lax
from jax.experimental import pallas as pl
from jax.experimental.pallas import tpu as pltpu

F32 = jnp.float32
BF16 = jnp.bfloat16
I32 = jnp.int32

LANES = 128
HEAD_DIM = 128
GROUP_HEADS = 16
GROUP_WIDTH = GROUP_HEADS * HEAD_DIM
ROPE_THETA = 10000.0
EPS = 1e-6
DSA_Q_LORA = 1024
DSA_IDX_HEADS = 32
DSA_IDX_ROPE = 64
DSA_TOPK = 256
DSA_QBLK = 128
DSA_CHUNK = 256
DSA_STAGES = 4
DSA_HEAD_GROUP = 2
MOBA_BLOCK = 256
MOBA_TOPK = 3
SWA_KV_HEADS = 2
SWA_WINDOW = 128
N_EXPERTS = 32
MOE_TOPK = 4
EXPERT_FF = 512
SWIGLU_LIMIT = 7.0
SWIGLU_ALPHA = 1.702
MOE_ROW_TILE = 256
NEG_BIG = -1e30
VMEM_LIMIT = 56 * 1024 * 1024


def _cparams(semantics, vmem=VMEM_LIMIT):
    return pltpu.CompilerParams(dimension_semantics=semantics, vmem_limit_bytes=vmem)


def _ada_kernel(cb_ref, w_ref, b_ref, o_ref):
    nb = cb_ref.shape[0]
    tn = w_ref.shape[2]
    for c in range(tn // LANES):
        w = w_ref[0, :, c * LANES:(c + 1) * LANES]
        for b in range(nb):
            o_ref[0, b:b + 1, c * LANES:(c + 1) * LANES] = (
                jnp.sum(w * cb_ref[b], axis=0, keepdims=True) + b_ref[0, :, c * LANES:(c + 1) * LANES])


def _ada_modulation(c, w_ada, b_ada, tn=512):
    depth, d, n6 = w_ada.shape
    nb = c.shape[0]
    cb = jnp.broadcast_to(jax.nn.silu(c)[:, :, None], (nb, d, LANES))
    return pl.pallas_call(
        _ada_kernel,
        grid=(depth, n6 // tn),
        in_specs=[pl.BlockSpec((nb, d, LANES), lambda l, j: (0, 0, 0)),
                  pl.BlockSpec((1, d, tn), lambda l, j: (l, 0, j)),
                  pl.BlockSpec((1, 1, tn), lambda l, j: (l, 0, j))],
        out_specs=pl.BlockSpec((1, nb, tn), lambda l, j: (l, 0, j)),
        out_shape=jax.ShapeDtypeStruct((depth, nb, n6), F32),
        compiler_params=_cparams(("parallel", "parallel")),
        name="ada_modulation",
    )(cb, w_ada, b_ada.reshape(depth, 1, n6))


def _normmod_kernel(x_ref, g_ref, sc_ref, sh_ref, o_ref):
    x = x_ref[...]
    y = x * lax.rsqrt(jnp.mean(x * x, axis=-1, keepdims=True) + EPS) * g_ref[...]
    o_ref[...] = (y * (1.0 + sc_ref[0]) + sh_ref[0]).astype(o_ref.dtype)


def _norm_modulate(x2, g, scale, shift, rows_per_batch, tm=256):
    n, d = x2.shape
    tpb = rows_per_batch // tm
    return pl.pallas_call(
        _normmod_kernel,
        grid=(n // tm,),
        in_specs=[pl.BlockSpec((tm, d), lambda i: (i, 0)),
                  pl.BlockSpec((1, d), lambda i: (0, 0)),
                  pl.BlockSpec((1, 1, d), lambda i: (i // tpb, 0, 0)),
                  pl.BlockSpec((1, 1, d), lambda i: (i // tpb, 0, 0))],
        out_specs=pl.BlockSpec((tm, d), lambda i: (i, 0)),
        out_shape=jax.ShapeDtypeStruct((n, d), BF16),
        compiler_params=_cparams(("parallel",)),
        name="norm_modulate",
    )(x2, g.reshape(1, d), scale[:, None, :], shift[:, None, :])


def _mm_kernel(a_ref, b_ref, o_ref, acc_ref, *, nk):
    k = pl.program_id(2)

    @pl.when(k == 0)
    def _():
        acc_ref[...] = jnp.zeros_like(acc_ref)

    acc_ref[...] += jnp.dot(a_ref[...], b_ref[...], preferred_element_type=F32)

    @pl.when(k == nk - 1)
    def _():
        o_ref[...] = acc_ref[...].astype(o_ref.dtype)


def _mm_residual_kernel(a_ref, b_ref, x_ref, g_ref, o_ref, acc_ref, *, nk):
    k = pl.program_id(2)

    @pl.when(k == 0)
    def _():
        acc_ref[...] = jnp.zeros_like(acc_ref)

    acc_ref[...] += jnp.dot(a_ref[...], b_ref[...], preferred_element_type=F32)

    @pl.when(k == nk - 1)
    def _():
        o_ref[...] = x_ref[...] + g_ref[0] * acc_ref[...]


def _pick(n, prefs):
    for t in prefs:
        if n % t == 0:
            return t
    return n


def _matmul(a, b, out_dtype, name, residual=None, gate=None, rows_per_batch=None):
    m, k = a.shape
    n = b.shape[1]
    tm = _pick(m, (1024, 512, 256, 128))
    tn = _pick(n, (1024, 512, 640, 384, 256, 128))
    tk = _pick(k, (512, 256, 128))
    nk = k // tk
    grid = (m // tm, n // tn, nk)
    a_spec = pl.BlockSpec((tm, tk), lambda i, j, kk: (i, kk))
    b_spec = pl.BlockSpec((tk, tn), lambda i, j, kk: (kk, j))
    o_spec = pl.BlockSpec((tm, tn), lambda i, j, kk: (i, j))
    scratch = [pltpu.VMEM((tm, tn), F32)]
    params = _cparams(("parallel", "parallel", "arbitrary"))
    if residual is None:
        return pl.pallas_call(
            functools.partial(_mm_kernel, nk=nk), grid=grid, in_specs=[a_spec, b_spec], out_specs=o_spec,
            out_shape=jax.ShapeDtypeStruct((m, n), out_dtype), scratch_shapes=scratch,
            compiler_params=params, name=name)(a, b)
    tpb = rows_per_batch // tm
    return pl.pallas_call(
        functools.partial(_mm_residual_kernel, nk=nk), grid=grid,
        in_specs=[a_spec, b_spec, o_spec, pl.BlockSpec((1, 1, tn), lambda i, j, kk: (i // tpb, 0, j))],
        out_specs=o_spec, out_shape=jax.ShapeDtypeStruct((m, n), F32), scratch_shapes=scratch,
        compiler_params=params, name=name)(a, b, residual, gate[:, None, :])


def _rope_tables(s_len, rot_dim):
    half = rot_dim // 2
    inv_freq = ROPE_THETA ** (-jnp.arange(half, dtype=F32) / half)
    ang = jnp.arange(s_len, dtype=F32)[:, None] * inv_freq[None, :]
    cos, sin = jnp.cos(ang), jnp.sin(ang)
    pad = HEAD_DIM - rot_dim
    ones = jnp.ones((s_len, pad), F32)
    zeros = jnp.zeros((s_len, pad), F32)
    zh = jnp.zeros((s_len, half), F32)
    cos_t = jnp.concatenate([cos, cos, ones], axis=1)
    sa = jnp.concatenate([-sin, zh, zeros], axis=1)
    sb = jnp.concatenate([zh, sin, zeros], axis=1)
    return cos_t, sa, sb


def _rowprep_kernel(*refs, hw, norm, rope_half, out_scale, head_major, want_mean):
    it = iter(refs)
    z_ref = next(it)
    g_ref = next(it) if norm != "none" else None
    beta_ref = next(it) if norm == "ln" else None
    if rope_half:
        cos_ref, sa_ref, sb_ref = next(it), next(it), next(it)
    o_ref = next(it)
    mean_ref = next(it) if want_mean else None
    width = z_ref.shape[1]
    for h in range(width // hw):
        x = z_ref[:, h * hw:(h + 1) * hw].astype(F32)
        if norm == "rms":
            x = x * lax.rsqrt(jnp.mean(x * x, axis=-1, keepdims=True) + EPS) * g_ref[...]
        elif norm == "ln":
            mu = jnp.mean(x, axis=-1, keepdims=True)
            xc = x - mu
            x = xc * lax.rsqrt(jnp.mean(xc * xc, axis=-1, keepdims=True) + EPS) * g_ref[...] + beta_ref[...]
        if rope_half:
            if rope_half * 2 == hw:
                x = x * cos_ref[...] + pltpu.roll(x, rope_half, 1) * (sa_ref[...] + sb_ref[...])
            else:
                x = (x * cos_ref[...] + pltpu.roll(x, hw - rope_half, 1) * sa_ref[...]
                     + pltpu.roll(x, rope_half, 1) * sb_ref[...])
        if want_mean:
            mean_ref[0, 0, h:h + 1, :] = jnp.mean(x, axis=0, keepdims=True)
        if out_scale != 1.0:
            x = x * out_scale
        if head_major:
            o_ref[0, h] = x.astype(o_ref.dtype)
        else:
            o_ref[:, h * hw:(h + 1) * hw] = x.astype(o_ref.dtype)


def _rowprep(z, col0, width, batch, s_len, *, hw=HEAD_DIM, norm="none", gain=None, beta=None, rope_dim=0,
             out_scale=1.0, head_major=False, want_mean=False, name="rowprep", tm=256):
    n = z.shape[0]
    tpb = s_len // tm
    assert col0 % width == 0 and n == batch * s_len
    cb = col0 // width
    args = [z]
    specs = [pl.BlockSpec((tm, width), lambda i: (i, cb))]
    if norm != "none":
        args.append(gain.reshape(1, hw).astype(F32))
        specs.append(pl.BlockSpec((1, hw), lambda i: (0, 0)))
    if norm == "ln":
        args.append(beta.reshape(1, hw).astype(F32))
        specs.append(pl.BlockSpec((1, hw), lambda i: (0, 0)))
    if rope_dim:
        args += list(_rope_tables(s_len, rope_dim))
        specs += [pl.BlockSpec((tm, HEAD_DIM), lambda i: (i % tpb, 0))] * 3
    nh = width // hw
    if head_major:
        out_shape = [jax.ShapeDtypeStruct((batch, nh, s_len, hw), BF16)]
        out_specs = [pl.BlockSpec((1, nh, tm, hw), lambda i: (i // tpb, 0, i % tpb, 0))]
    else:
        out_shape = [jax.ShapeDtypeStruct((n, width), BF16)]
        out_specs = [pl.BlockSpec((tm, width), lambda i: (i, 0))]
    if want_mean:
        assert tm == MOBA_BLOCK
        out_shape.append(jax.ShapeDtypeStruct((batch, tpb, nh, hw), F32))
        out_specs.append(pl.BlockSpec((1, 1, nh, hw), lambda i: (i // tpb, i % tpb, 0, 0)))
    kern = functools.partial(_rowprep_kernel, hw=hw, norm=norm, rope_half=rope_dim // 2, out_scale=out_scale,
                             head_major=head_major, want_mean=want_mean)
    res = pl.pallas_call(kern, grid=(n // tm,), in_specs=specs, out_specs=out_specs, out_shape=out_shape,
                         compiler_params=_cparams(("parallel",)), name=name)(*args)
    return res if want_mean else res[0]


def _fox_cum_kernel(z_ref, bf_ref, tri_ref, o_ref, carry_ref):
    @pl.when(pl.program_id(1) == 0)
    def _():
        carry_ref[...] = jnp.zeros_like(carry_ref)

    xv = z_ref[...] + bf_ref[...]
    lf = jnp.minimum(xv, 0.0) - jnp.log(1.0 + jnp.exp(-jnp.abs(xv)))
    hi = lf.astype(BF16)
    r1 = lf - hi.astype(F32)
    mid = r1.astype(BF16)
    lo = (r1 - mid.astype(F32)).astype(BF16)
    tri = tri_ref[...]
    cs = (jnp.dot(tri, hi, preferred_element_type=F32) + jnp.dot(tri, mid, preferred_element_type=F32)
          + jnp.dot(tri, lo, preferred_element_type=F32)) + carry_ref[...]
    o_ref[...] = cs
    tm = cs.shape[0]
    carry_ref[...] = cs[tm - 1:tm, :]


def _fox_cumsum(zs, bias_row, batch, s_len, tm=256):
    n = zs.shape[0]
    tpb = s_len // tm
    tri = jnp.tril(jnp.ones((tm, tm), F32)).astype(BF16)
    return pl.pallas_call(
        _fox_cum_kernel,
        grid=(batch, tpb),
        in_specs=[pl.BlockSpec((tm, LANES), lambda b, i: (b * tpb + i, 0)),
                  pl.BlockSpec((1, LANES), lambda b, i: (0, 0)),
                  pl.BlockSpec((tm, tm), lambda b, i: (0, 0))],
        out_specs=pl.BlockSpec((tm, LANES), lambda b, i: (b * tpb + i, 0)),
        out_shape=jax.ShapeDtypeStruct((n, LANES), F32),
        scratch_shapes=[pltpu.VMEM((1, LANES), F32)],
        compiler_params=_cparams(("parallel", "arbitrary")),
        name="fox_cumsum",
    )(zs, bias_row, tri)


def _dsa_kernel(qi_ref, kit_ref, wi_ref, qa_ref, kat_ref, va_ref, o_ref, wb_ref, key_ref, mb_ref,
                *, sk, q0, topk):
    q = DSA_QBLK
    ch = DSA_CHUNK
    nch = sk // ch
    n_idx = qi_ref.shape[1]
    n_heads = qa_ref.shape[1]
    t0 = (q0 + pl.program_id(1)) * q
    row = t0 + lax.broadcasted_iota(I32, (q, ch), 0)
    lane = lax.broadcasted_iota(I32, (q, ch), 1)

    wi = wi_ref[...]
    for h in range(n_idx):
        wb_ref[h] = jnp.broadcast_to(wi[:, h:h + 1], (q, ch))
    qi = qi_ref[0].reshape(n_idx * q, HEAD_DIM)

    def idx_chunk(c, carry):
        d = jnp.dot(qi, kit_ref[0, c], preferred_element_type=F32)
        acc = jnp.zeros((q, ch), F32)
        for h in range(n_idx):
            acc = acc + jnp.maximum(d[h * q:(h + 1) * q], 0.0) * wb_ref[h]
        score = jnp.where(c * ch + lane <= row, acc + 0.0, -jnp.inf)
        bits = lax.bitcast_convert_type(score, I32)
        key_ref[c] = jnp.where(bits >= 0, bits, bits ^ jnp.int32(0x7FFFFFFF))
        return carry

    lax.fori_loop(0, nch, idx_chunk, 0)

    def count_ge(cand):
        def body(c, acc):
            return acc + jnp.where(key_ref[c] >= cand, 1.0, 0.0)
        acc = lax.fori_loop(0, nch, body, jnp.zeros((q, ch), F32))
        return jnp.sum(acc, axis=-1, keepdims=True)

    int_min = jnp.int32(-2 ** 31)
    thr = jnp.where(count_ge(jnp.zeros((q, 1), I32)) >= topk, jnp.int32(0), int_min)

    def bit_step(j, thr):
        cand = thr | jnp.left_shift(jnp.int32(1), 30 - j)
        return jnp.where(count_ge(cand) >= topk, cand, thr)

    thr = lax.fori_loop(0, 31, bit_step, thr)

    for c in range(nch):
        keep = (key_ref[c] >= thr) & (c * ch + lane <= row)
        mb_ref[:, c * ch:(c + 1) * ch] = jnp.where(keep, 0.0, -jnp.inf)

    g = DSA_HEAD_GROUP

    def head_group(hg, carry):
        qh = qa_ref[0, pl.ds(hg * g, g)].reshape(g * q, HEAD_DIM)
        lg = jnp.dot(qh, kat_ref[0], preferred_element_type=F32).reshape(g, q, sk) + mb_ref[...][None]
        m = jnp.max(lg, axis=-1, keepdims=True)
        p = jnp.exp(lg - m)
        l = jnp.sum(p, axis=-1, keepdims=True)
        o = jnp.dot(p.reshape(g * q, sk).astype(BF16), va_ref[0], preferred_element_type=F32)
        o_ref[0, pl.ds(hg * g, g)] = (o.reshape(g, q, HEAD_DIM) / l).astype(o_ref.dtype)
        return carry

    lax.fori_loop(0, n_heads // g, head_group, 0)


def _dsa_attention(qi, ki, wi, qa, ka, va, batch, s_len):
    q, ch = DSA_QBLK, DSA_CHUNK
    n_idx, n_heads = qi.shape[1], qa.shape[1]
    topk = min(DSA_TOPK, s_len // 4)
    nqb = s_len // q
    kit = ki.reshape(batch, s_len // ch, ch, HEAD_DIM).transpose(0, 1, 3, 2)
    kat = ka.reshape(batch, s_len, HEAD_DIM).transpose(0, 2, 1)
    va3 = va.reshape(batch, s_len, HEAD_DIM)
    per_stage = max(nqb // DSA_STAGES, ch // q)
    outs = []
    for q0 in range(0, nqb, per_stage):
        sk = (q0 + per_stage) * q
        nch = sk // ch
        kern = functools.partial(_dsa_kernel, sk=sk, q0=q0, topk=topk)
        outs.append(pl.pallas_call(
            kern,
            grid=(batch, per_stage),
            in_specs=[pl.BlockSpec((1, n_idx, q, HEAD_DIM), lambda b, i, q0=q0: (b, 0, q0 + i, 0)),
                      pl.BlockSpec((1, nch, HEAD_DIM, ch), lambda b, i: (b, 0, 0, 0)),
                      pl.BlockSpec((q, LANES), lambda b, i, q0=q0: (b * nqb + q0 + i, 0)),
                      pl.BlockSpec((1, n_heads, q, HEAD_DIM), lambda b, i, q0=q0: (b, 0, q0 + i, 0)),
                      pl.BlockSpec((1, HEAD_DIM, sk), lambda b, i: (b, 0, 0)),
                      pl.BlockSpec((1, sk, HEAD_DIM), lambda b, i: (b, 0, 0))],
            out_specs=pl.BlockSpec((1, n_heads, q, HEAD_DIM), lambda b, i: (b, 0, i, 0)),
            out_shape=jax.ShapeDtypeStruct((batch, n_heads, per_stage * q, HEAD_DIM), BF16),
            scratch_shapes=[pltpu.VMEM((n_idx, q, ch), F32),
                            pltpu.VMEM((nch, q, ch), I32),
                            pltpu.VMEM((q, sk), F32)],
            compiler_params=_cparams(("parallel", "parallel")),
            name=f"dsa_attention_k{sk}",
        )(qi, kit, wi, qa, kat, va3))
    return jnp.concatenate(outs, axis=2)


def _tri_tables(nq, ratio):
    qt, kt = [], []
    for i in range(nq):
        for j in range((i + 1) * ratio):
            qt.append(i)
            kt.append(j)
    return jnp.asarray(qt, I32), jnp.asarray(kt, I32)


def _fox_kernel(qt_ref, kt_ref, q_ref, k_ref, v_ref, cq_ref, ck_ref, g_ref, o_ref,
                m_ref, l_ref, acc_ref, cqs_ref, *, tq):
    h = pl.program_id(1)
    step = pl.program_id(2)
    qi = qt_ref[step]
    kj = kt_ref[step]

    @pl.when(kj == 0)
    def _():
        m_ref[...] = jnp.full_like(m_ref, -jnp.inf)
        l_ref[...] = jnp.zeros_like(l_ref)
        acc_ref[...] = jnp.zeros_like(acc_ref)
        lane = lax.broadcasted_iota(I32, cq_ref.shape, 1)
        cqs_ref[...] = jnp.sum(jnp.where(lane == h, cq_ref[...], 0.0), axis=-1, keepdims=True)

    s = lax.dot_general(q_ref[0, 0], k_ref[0, 0], (((1,), (1,)), ((), ())), preferred_element_type=F32)
    s = s + (cqs_ref[...] - ck_ref[0, 0])
    row = qi * tq + lax.broadcasted_iota(I32, s.shape, 0)
    col = kj * tq + lax.broadcasted_iota(I32, s.shape, 1)
    s = jnp.where(col <= row, s, -jnp.inf)
    m_prev = m_ref[...]
    m_new = jnp.maximum(m_prev, jnp.max(s, axis=-1, keepdims=True))
    alpha = jnp.exp(m_prev - m_new)
    p = jnp.exp(s - m_new)
    l_ref[...] = alpha * l_ref[...] + jnp.sum(p, axis=-1, keepdims=True)
    acc_ref[...] = alpha * acc_ref[...] + jnp.dot(p.astype(BF16), v_ref[...], preferred_element_type=F32)
    m_ref[...] = m_new

    @pl.when(kj == qi)
    def _():
        gate = g_ref[...].astype(F32)
        o_ref[...] = (acc_ref[...] / l_ref[...] * (1.0 / (1.0 + jnp.exp(-gate)))).astype(o_ref.dtype)


def _fox_attention(qb, kb, z, v_col0, g_col0, cum, batch, s_len, tq=1024):
    tq = min(tq, s_len)
    nq = s_len // tq
    n_heads = qb.shape[1]
    qt, kt = _tri_tables(nq, 1)
    cum_t = cum.reshape(batch, s_len, LANES)[:, :, :n_heads].transpose(0, 2, 1)[:, :, None, :]
    vb0, gb0 = v_col0 // HEAD_DIM, g_col0 // HEAD_DIM
    grid_spec = pltpu.PrefetchScalarGridSpec(
        num_scalar_prefetch=2,
        grid=(batch, n_heads, int(qt.shape[0])),
        in_specs=[pl.BlockSpec((1, 1, tq, HEAD_DIM), lambda b, h, s, qt, kt: (b, h, qt[s], 0)),
                  pl.BlockSpec((1, 1, tq, HEAD_DIM), lambda b, h, s, qt, kt: (b, h, kt[s], 0)),
                  pl.BlockSpec((tq, HEAD_DIM), lambda b, h, s, qt, kt: (b * nq + kt[s], vb0 + h)),
                  pl.BlockSpec((tq, LANES), lambda b, h, s, qt, kt: (b * nq + qt[s], 0)),
                  pl.BlockSpec((1, 1, 1, tq), lambda b, h, s, qt, kt: (b, h, 0, kt[s])),
                  pl.BlockSpec((tq, HEAD_DIM), lambda b, h, s, qt, kt: (b * nq + qt[s], gb0 + h))],
        out_specs=pl.BlockSpec((tq, HEAD_DIM), lambda b, h, s, qt, kt: (b * nq + qt[s], h)),
        scratch_shapes=[pltpu.VMEM((tq, 1), F32), pltpu.VMEM((tq, 1), F32),
                        pltpu.VMEM((tq, HEAD_DIM), F32), pltpu.VMEM((tq, 1), F32)])
    return pl.pallas_call(
        functools.partial(_fox_kernel, tq=tq), grid_spec=grid_spec,
        out_shape=jax.ShapeDtypeStruct((batch * s_len, n_heads * HEAD_DIM), BF16),
        compiler_params=_cparams(("parallel", "parallel", "arbitrary")),
        name="fox_attention",
    )(qt, kt, qb, kb, z, cum, cum_t, z)


def _moba_kernel(qt_ref, kt_ref, q_ref, k_ref, v_ref, km_ref, o_ref, m_ref, l_ref, acc_ref, sel_ref,
                 *, tq, tk, n_sel):
    step = pl.program_id(2)
    qi = qt_ref[step]
    kj = kt_ref[step]
    blk = MOBA_BLOCK
    row1 = qi * tq + lax.broadcasted_iota(I32, (tq, 1), 0)
    own = row1 // blk

    @pl.when(kj == 0)
    def _():
        m_ref[...] = jnp.full_like(m_ref, NEG_BIG)
        l_ref[...] = jnp.zeros_like(l_ref)
        acc_ref[...] = jnp.zeros_like(acc_ref)
        gate = jnp.dot(q_ref[0, 0].astype(F32), km_ref[0, 0], preferred_element_type=F32,
                       precision=lax.Precision.HIGHEST)
        lane = lax.broadcasted_iota(I32, gate.shape, 1)
        lane_f = lane.astype(F32)
        gate = jnp.where(lane < own, gate, -jnp.inf)
        sel = jnp.zeros(gate.shape, F32)
        for _ in range(n_sel):
            best = jnp.max(gate, axis=-1, keepdims=True)
            first = jnp.min(jnp.where(gate == best, lane_f, float(LANES)), axis=-1, keepdims=True)
            pick = (lane_f == first) & (best > -jnp.inf)
            sel = jnp.where(pick, 1.0, sel)
            gate = jnp.where(pick, -jnp.inf, gate)
        sel_ref[...] = jnp.where(sel > 0.0, 0.0, NEG_BIG)

    q = q_ref[0, 0]
    lane_s = lax.broadcasted_iota(I32, sel_ref.shape, 1)
    for sb in range(tk // blk):
        j = kj * (tk // blk) + sb
        kblk = k_ref[0, 0, sb * blk:(sb + 1) * blk]
        s = lax.dot_general(q, kblk, (((1,), (1,)), ((), ())), preferred_element_type=F32)
        sel_col = jnp.sum(jnp.where(lane_s == j, sel_ref[...], 0.0), axis=-1, keepdims=True)
        col = j * blk + lax.broadcasted_iota(I32, s.shape, 1)
        causal = jnp.where(col <= row1, s, NEG_BIG)
        s = jnp.where(own == j, causal, s + sel_col)
        m_prev = m_ref[...]
        m_new = jnp.maximum(m_prev, jnp.max(s, axis=-1, keepdims=True))
        alpha = jnp.exp(m_prev - m_new)
        p = jnp.exp(s - m_new)
        l_ref[...] = alpha * l_ref[...] + jnp.sum(p, axis=-1, keepdims=True)
        acc_ref[...] = alpha * acc_ref[...] + jnp.dot(
            p.astype(BF16), v_ref[sb * blk:(sb + 1) * blk, :], preferred_element_type=F32)
        m_ref[...] = m_new

    @pl.when(kj == qi)
    def _():
        o_ref[...] = (acc_ref[...] / l_ref[...]).astype(o_ref.dtype)


def _moba_attention(qc, kc, kmean, z, v_col0, batch, s_len, tq=1024):
    tq = min(tq, s_len)
    nq = s_len // tq
    n_heads = qc.shape[1]
    nb = s_len // MOBA_BLOCK
    assert nb <= LANES
    n_sel = min(MOBA_TOPK, nb - 1)
    qt, kt = _tri_tables(nq, 1)
    km_t = jnp.pad(kmean.transpose(0, 2, 3, 1), ((0, 0), (0, 0), (0, 0), (0, LANES - nb)))
    vb0 = v_col0 // HEAD_DIM
    grid_spec = pltpu.PrefetchScalarGridSpec(
        num_scalar_prefetch=2,
        grid=(batch, n_heads, int(qt.shape[0])),
        in_specs=[pl.BlockSpec((1, 1, tq, HEAD_DIM), lambda b, h, s, qt, kt: (b, h, qt[s], 0)),
                  pl.BlockSpec((1, 1, tq, HEAD_DIM), lambda b, h, s, qt, kt: (b, h, kt[s], 0)),
                  pl.BlockSpec((tq, HEAD_DIM), lambda b, h, s, qt, kt: (b * nq + kt[s], vb0 + h)),
                  pl.BlockSpec((1, 1, HEAD_DIM, LANES), lambda b, h, s, qt, kt: (b, h, 0, 0))],
        out_specs=pl.BlockSpec((tq, HEAD_DIM), lambda b, h, s, qt, kt: (b * nq + qt[s], h)),
        scratch_shapes=[pltpu.VMEM((tq, 1), F32), pltpu.VMEM((tq, 1), F32),
                        pltpu.VMEM((tq, HEAD_DIM), F32), pltpu.VMEM((tq, LANES), F32)])
    return pl.pallas_call(
        functools.partial(_moba_kernel, tq=tq, tk=tq, n_sel=n_sel), grid_spec=grid_spec,
        out_shape=jax.ShapeDtypeStruct((batch * s_len, n_heads * HEAD_DIM), BF16),
        compiler_params=_cparams(("parallel", "parallel", "arbitrary")),
        name="moba_attention",
    )(qt, kt, qc, kc, z, km_t)


def _swa_kernel(q_ref, kp_ref, kc_ref, vp_ref, vc_ref, sink_ref, o_ref):
    n = pl.program_id(1)
    hq, w = q_ref.shape[1], q_ref.shape[2]
    hkv = kc_ref.shape[1]
    grp = hq // hkv
    ti = lax.broadcasted_iota(I32, (w, w), 0)
    si = lax.broadcasted_iota(I32, (w, w), 1)
    cur_ok = (si <= ti)[None]
    prev_ok = ((si > ti) & (n > 0))[None]
    nt = (((1,), (1,)), ((), ()))
    for kv in range(hkv):
        q = q_ref[0, kv * grp:(kv + 1) * grp].reshape(grp * w, HEAD_DIM)
        sc = lax.dot_general(q, kc_ref[0, kv], nt, preferred_element_type=F32).reshape(grp, w, w)
        sp = lax.dot_general(q, kp_ref[0, kv], nt, preferred_element_type=F32).reshape(grp, w, w)
        sc = jnp.where(cur_ok, sc, -jnp.inf)
        sp = jnp.where(prev_ok, sp, -jnp.inf)
        sink = sink_ref[kv * grp * w:(kv + 1) * grp * w].reshape(grp, w, LANES)[:, :, :1]
        m = jnp.maximum(jnp.maximum(jnp.max(sc, axis=-1, keepdims=True), jnp.max(sp, axis=-1, keepdims=True)), sink)
        pc = jnp.exp(sc - m)
        pp = jnp.exp(sp - m)
        den = jnp.sum(pc, axis=-1, keepdims=True) + jnp.sum(pp, axis=-1, keepdims=True) + jnp.exp(sink - m)
        o = (jnp.dot(pc.reshape(grp * w, w).astype(BF16), vc_ref[:, kv * HEAD_DIM:(kv + 1) * HEAD_DIM],
                     preferred_element_type=F32)
             + jnp.dot(pp.reshape(grp * w, w).astype(BF16), vp_ref[:, kv * HEAD_DIM:(kv + 1) * HEAD_DIM],
                       preferred_element_type=F32))
        o = o.reshape(grp, w, HEAD_DIM) / den
        for gh in range(grp):
            hh = kv * grp + gh
            o_ref[:, hh * HEAD_DIM:(hh + 1) * HEAD_DIM] = o[gh].astype(o_ref.dtype)


def _swa_attention(qd, kd, z, v_col0, sinks, batch, s_len):
    w = SWA_WINDOW
    nb = s_len // w
    hq, hkv = qd.shape[1], kd.shape[1]
    vw = hkv * HEAD_DIM
    vb0 = v_col0 // vw
    sink_b = jnp.broadcast_to(sinks.astype(F32)[:, None, None], (hq, w, LANES)).reshape(hq * w, LANES)
    return pl.pallas_call(
        _swa_kernel,
        grid=(batch, nb),
        in_specs=[pl.BlockSpec((1, hq, w, HEAD_DIM), lambda b, n: (b, 0, n, 0)),
                  pl.BlockSpec((1, hkv, w, HEAD_DIM), lambda b, n: (b, 0, jnp.maximum(n - 1, 0), 0)),
                  pl.BlockSpec((1, hkv, w, HEAD_DIM), lambda b, n: (b, 0, n, 0)),
                  pl.BlockSpec((w, vw), lambda b, n: (b * nb + jnp.maximum(n - 1, 0), vb0)),
                  pl.BlockSpec((w, vw), lambda b, n: (b * nb + n, vb0)),
                  pl.BlockSpec((hq * w, LANES), lambda b, n: (0, 0))],
        out_specs=pl.BlockSpec((w, hq * HEAD_DIM), lambda b, n: (b * nb + n, 0)),
        out_shape=jax.ShapeDtypeStruct((batch * s_len, hq * HEAD_DIM), BF16),
        compiler_params=_cparams(("parallel", "parallel")),
        name="swa_attention",
    )(qd, kd, kd, z, z, sink_b)


def _route_kernel(x_ref, g_ref, sc_ref, sh_ref, wr_ref, br_ref, tri_ref,
                  h_ref, idx_ref, wgt_ref, rank_ref, cnt_ref, carry_ref):
    @pl.when(pl.program_id(0) == 0)
    def _():
        carry_ref[...] = jnp.zeros_like(carry_ref)

    x = x_ref[...]
    y = x * lax.rsqrt(jnp.mean(x * x, axis=-1, keepdims=True) + EPS) * g_ref[...]
    hmod = y * (1.0 + sc_ref[0]) + sh_ref[0]
    h_ref[...] = hmod
    logits = jnp.dot(hmod, wr_ref[...], preferred_element_type=F32, precision=lax.Precision.HIGHEST) + br_ref[...]
    lane = lax.broadcasted_iota(I32, logits.shape, 1)
    lane_f = lane.astype(F32)
    logits = jnp.where(lane < N_EXPERTS, logits, -jnp.inf)
    onehots, vals, firsts = [], [], []
    for _ in range(MOE_TOPK):
        best = jnp.max(logits, axis=-1, keepdims=True)
        first = jnp.min(jnp.where(logits == best, lane_f, float(LANES)), axis=-1, keepdims=True)
        pick = lane_f == first
        onehots.append(pick)
        vals.append(best)
        firsts.append(first.astype(I32))
        logits = jnp.where(pick, -jnp.inf, logits)
    exps = [jnp.exp(v - vals[0]) for v in vals]
    den = exps[0]
    for e in exps[1:]:
        den = den + e
    chosen_f = jnp.zeros(logits.shape, F32)
    for o in onehots:
        chosen_f = jnp.where(o, 1.0, chosen_f)
    before = jnp.dot(tri_ref[...], chosen_f.astype(BF16), preferred_element_type=F32) + carry_ref[...]
    idx_out = jnp.zeros(logits.shape, I32)
    wgt_out = jnp.zeros(logits.shape, F32)
    rank_out = jnp.zeros(logits.shape, I32)
    for k in range(MOE_TOPK):
        rk = jnp.sum(jnp.where(onehots[k], before, 0.0), axis=-1, keepdims=True).astype(I32)
        idx_out = jnp.where(lane == k, firsts[k], idx_out)
        wgt_out = jnp.where(lane == k, exps[k] / den, wgt_out)
        rank_out = jnp.where(lane == k, rk, rank_out)
    idx_ref[...] = idx_out
    wgt_ref[...] = wgt_out
    rank_ref[...] = rank_out
    carry_ref[...] = carry_ref[...] + jnp.sum(chosen_f, axis=0, keepdims=True)
    cnt_ref[...] = carry_ref[...]


def _moe_route(x2, g, scale, shift, w_router, b_router, rows_per_batch, tm=256):
    n, d = x2.shape
    tpb = rows_per_batch // tm
    wr = jnp.pad(w_router, ((0, 0), (0, LANES - N_EXPERTS)))
    br = jnp.pad(b_router, (0, LANES - N_EXPERTS)).reshape(1, LANES)
    tri = jnp.tril(jnp.ones((tm, tm), F32), -1).astype(BF16)
    tok_spec = pl.BlockSpec((tm, LANES), lambda i: (i, 0))
    return pl.pallas_call(
        _route_kernel,
        grid=(n // tm,),
        in_specs=[pl.BlockSpec((tm, d), lambda i: (i, 0)),
                  pl.BlockSpec((1, d), lambda i: (0, 0)),
                  pl.BlockSpec((1, 1, d), lambda i: (i // tpb, 0, 0)),
                  pl.BlockSpec((1, 1, d), lambda i: (i // tpb, 0, 0)),
                  pl.BlockSpec((d, LANES), lambda i: (0, 0)),
                  pl.BlockSpec((1, LANES), lambda i: (0, 0)),
                  pl.BlockSpec((tm, tm), lambda i: (0, 0))],
        out_specs=[pl.BlockSpec((tm, d), lambda i: (i, 0)), tok_spec, tok_spec, tok_spec,
                   pl.BlockSpec((1, LANES), lambda i: (0, 0))],
        out_shape=[jax.ShapeDtypeStruct((n, d), F32), jax.ShapeDtypeStruct((n, LANES), I32),
                   jax.ShapeDtypeStruct((n, LANES), F32), jax.ShapeDtypeStruct((n, LANES), I32),
                   jax.ShapeDtypeStruct((1, LANES), F32)],
        scratch_shapes=[pltpu.VMEM((1, LANES), F32)],
        compiler_params=_cparams(("arbitrary",)),
        name="moe_route",
    )(x2, g.reshape(1, d), scale[:, None, :], shift[:, None, :], wr, br, tri)


def _dispatch_kernel(pos_ref, fill_ref, h_ref, xs_ref, sem, *, n_assign, n_fill, window):
    def copy(src_row, dst_row):
        return pltpu.make_async_copy(h_ref.at[pl.ds(src_row, 1)], xs_ref.at[pl.ds(dst_row, 1)], sem)

    def issue(a, carry):
        copy(a // MOE_TOPK, pos_ref[a]).start()

        @pl.when(a >= window)
        def _():
            copy(0, 0).wait()
        return carry

    lax.fori_loop(0, n_assign, issue, 0)

    def issue_fill(j, carry):
        dst = fill_ref[j]

        @pl.when(dst >= 0)
        def _():
            copy(0, dst).start()
            copy(0, 0).wait()
        return carry

    lax.fori_loop(0, n_fill, issue_fill, 0)

    def drain(a, carry):
        copy(0, 0).wait()
        return carry

    lax.fori_loop(0, window, drain, 0)


def _moe_dispatch(h2, pos, fill, n_rows):
    n, d = h2.shape
    n_assign = pos.shape[0]
    n_fill = fill.shape[0]
    window = 64
    kern = functools.partial(_dispatch_kernel, n_assign=n_assign, n_fill=n_fill, window=window)
    return pl.pallas_call(
        kern,
        grid_spec=pltpu.PrefetchScalarGridSpec(
            num_scalar_prefetch=2, grid=(1,),
            in_specs=[pl.BlockSpec(memory_space=pl.ANY)],
            out_specs=pl.BlockSpec(memory_space=pl.ANY),
            scratch_shapes=[pltpu.SemaphoreType.DMA(())]),
        out_shape=jax.ShapeDtypeStruct((n_rows, d), h2.dtype),
        compiler_params=_cparams(("arbitrary",)),
        name="moe_dispatch",
    )(pos, fill, h2)


def _expert_kernel(te_ref, nt_ref, x_ref, wi_ref, bi_ref, wo_ref, bo_ref, y_ref):
    i = pl.program_id(0)

    @pl.when(i < nt_ref[0])
    def _():
        x = x_ref[...].astype(BF16)
        hh = jnp.dot(x, wi_ref[0], preferred_element_type=F32) + bi_ref[0]
        x_glu = jnp.minimum(hh[:, :EXPERT_FF], SWIGLU_LIMIT)
        x_lin = jnp.clip(hh[:, EXPERT_FF:], -SWIGLU_LIMIT, SWIGLU_LIMIT)
        act = x_glu * (1.0 / (1.0 + jnp.exp(-SWIGLU_ALPHA * x_glu))) * (x_lin + 1.0)
        y_ref[...] = jnp.dot(act.astype(BF16), wo_ref[0], preferred_element_type=F32) + bo_ref[0]

    @pl.when(i >= nt_ref[0])
    def _():
        y_ref[...] = jnp.zeros_like(y_ref)


def _moe_experts(xs, tile_expert, n_tiles_used, w_in, b_in, w_out, b_out):
    r, d = xs.shape
    tm = MOE_ROW_TILE
    f2 = w_in.shape[2]
    ff = w_out.shape[1]
    grid_spec = pltpu.PrefetchScalarGridSpec(
        num_scalar_prefetch=2, grid=(r // tm,),
        in_specs=[pl.BlockSpec((tm, d), lambda i, te, nt: (jnp.where(i < nt[0], i, 0), 0)),
                  pl.BlockSpec((1, d, f2), lambda i, te, nt: (te[i], 0, 0)),
                  pl.BlockSpec((1, 1, f2), lambda i, te, nt: (te[i], 0, 0)),
                  pl.BlockSpec((1, ff, d), lambda i, te, nt: (te[i], 0, 0)),
                  pl.BlockSpec((1, 1, d), lambda i, te, nt: (te[i], 0, 0))],
        out_specs=pl.BlockSpec((tm, d), lambda i, te, nt: (i, 0)))
    return pl.pallas_call(
        _expert_kernel, grid_spec=grid_spec,
        out_shape=jax.ShapeDtypeStruct((r, d), F32),
        compiler_params=_cparams(("arbitrary",)),
        name="moe_experts",
    )(tile_expert, n_tiles_used, xs, w_in, b_in, w_out, b_out)


def _combine_kernel(pos_ref, y_ref, x_ref, w_ref, g_ref, o_ref, buf_ref, sem, *, tm):
    i = pl.program_id(0)
    base = i * tm * MOE_TOPK

    def copy(src_row, k, t):
        return pltpu.make_async_copy(y_ref.at[pl.ds(src_row, 1)], buf_ref.at[k, pl.ds(t, 1)], sem)

    def issue(t, carry):
        for k in range(MOE_TOPK):
            copy(pos_ref[base + t * MOE_TOPK + k], k, t).start()
        return carry

    lax.fori_loop(0, tm, issue, 0)

    def drain(t, carry):
        for k in range(MOE_TOPK):
            copy(0, k, t).wait()
        return carry

    lax.fori_loop(0, tm, drain, 0)
    w = w_ref[...]
    mix = buf_ref[0] * w[:, 0:1]
    for k in range(1, MOE_TOPK):
        mix = mix + buf_ref[k] * w[:, k:k + 1]
    o_ref[...] = x_ref[...] + g_ref[0] * mix


def _moe_combine(y, pos, x2, wgt, gate, rows_per_batch, tm=128):
    n, d = x2.shape
    tpb = rows_per_batch // tm
    grid_spec = pltpu.PrefetchScalarGridSpec(
        num_scalar_prefetch=1, grid=(n // tm,),
        in_specs=[pl.BlockSpec(memory_space=pl.ANY),
                  pl.BlockSpec((tm, d), lambda i, pos: (i, 0)),
                  pl.BlockSpec((tm, LANES), lambda i, pos: (i, 0)),
                  pl.BlockSpec((1, 1, d), lambda i, pos: (i // tpb, 0, 0))],
        out_specs=pl.BlockSpec((tm, d), lambda i, pos: (i, 0)),
        scratch_shapes=[pltpu.VMEM((MOE_TOPK, tm, d), F32), pltpu.SemaphoreType.DMA(())])
    return pl.pallas_call(
        functools.partial(_combine_kernel, tm=tm), grid_spec=grid_spec,
        out_shape=jax.ShapeDtypeStruct((n, d), F32),
        compiler_params=_cparams(("arbitrary",)),
        name="moe_combine",
    )(pos, y, x2, wgt, gate[:, None, :])


def _moe_block(x2, g, scale, shift, gate, w_router, b_router, w_exp_in, b_exp_in, w_exp_out, b_exp_out,
               rows_per_batch):
    n, d = x2.shape
    tm = MOE_ROW_TILE
    h2, idx, wgt, rank, counts = _moe_route(x2, g, scale, shift, w_router, b_router, rows_per_batch)
    cnt = counts[0, :N_EXPERTS].astype(I32)
    padded = (cnt + tm - 1) // tm * tm
    ends = jnp.cumsum(padded)
    starts = ends - padded
    n_rows = n * MOE_TOPK + N_EXPERTS * tm
    n_tiles = n_rows // tm
    tile_expert = jnp.minimum(jnp.searchsorted(ends, jnp.arange(n_tiles, dtype=I32) * tm, side="right"),
                              N_EXPERTS - 1).astype(I32)
    n_tiles_used = (ends[-1] // tm).astype(I32).reshape(1)
    e_flat = idx[:, :MOE_TOPK].reshape(-1)
    pos = (starts[e_flat] + rank[:, :MOE_TOPK].reshape(-1)).astype(I32)
    j = jnp.arange(tm - 1, dtype=I32)[None, :]
    fill = jnp.where(j < (padded - cnt)[:, None], (starts + cnt)[:, None] + j, -1).reshape(-1).astype(I32)
    xs = _moe_dispatch(h2, pos, fill, n_rows)
    w_in = jnp.concatenate([w_exp_in[..., 0::2], w_exp_in[..., 1::2]], axis=-1).astype(BF16)
    b_in = jnp.concatenate([b_exp_in[..., 0::2], b_exp_in[..., 1::2]], axis=-1)[:, None, :]
    y = _moe_experts(xs, tile_expert, n_tiles_used, w_in, b_in, w_exp_out.astype(BF16), b_exp_out[:, None, :])
    return _moe_combine(y, pos, x2, wgt, gate, rows_per_batch)


def _even_mixer(h, x2, gate, batch, s_len, w_in, g_cq, w_uq, w_iq, gq_a, gk_a, g_kidx, b_kidx, gq_b, gk_b,
                b_f, w_out):
    gw = GROUP_WIDTH
    o_cq, o_ka, o_va, o_ki, o_wi, o_qb, o_kb, o_vb, o_fb, o_gb = np.cumsum(
        [0, DSA_Q_LORA, HEAD_DIM, HEAD_DIM, HEAD_DIM, DSA_IDX_HEADS, gw, gw, gw, GROUP_HEADS]).tolist()
    cols = lambda o, wdt: w_in[:, o:o + wdt]
    w_main = jnp.concatenate([cols(o_qb, gw), cols(o_kb, gw), cols(o_vb, gw), cols(o_gb, gw),
                              cols(o_cq, DSA_Q_LORA), cols(o_ka, HEAD_DIM), cols(o_va, HEAD_DIM),
                              cols(o_ki, HEAD_DIM)], axis=1).astype(BF16)
    c_qb, c_kb, c_vb, c_gb, c_cq = 0, gw, 2 * gw, 3 * gw, 4 * gw
    c_ka = c_cq + DSA_Q_LORA
    c_va, c_ki = c_ka + HEAD_DIM, c_ka + 2 * HEAD_DIM
    d = w_in.shape[0]
    w_small = jnp.concatenate([cols(o_fb, GROUP_HEADS), cols(o_wi, DSA_IDX_HEADS),
                               jnp.zeros((d, LANES - GROUP_HEADS - DSA_IDX_HEADS), F32)], axis=1).astype(BF16)
    z = _matmul(h, w_main, BF16, "even_in_proj")
    zs = _matmul(h, w_small, F32, "even_in_proj_small")
    scale = HEAD_DIM ** -0.5

    cq = _rowprep(z, c_cq, DSA_Q_LORA, batch, s_len, hw=DSA_Q_LORA, norm="rms", gain=g_cq, name="dsa_cq_norm")
    qa_raw = _matmul(cq, w_uq.astype(BF16), BF16, "dsa_q_up")
    qi_raw = _matmul(cq, w_iq.astype(BF16), BF16, "dsa_idx_q_up")
    qa = _rowprep(qa_raw, 0, gw, batch, s_len, norm="rms", gain=gq_a, rope_dim=HEAD_DIM, out_scale=scale,
                  head_major=True, name="dsa_q_prep")
    qi = _rowprep(qi_raw, 0, DSA_IDX_HEADS * HEAD_DIM, batch, s_len, rope_dim=DSA_IDX_ROPE, head_major=True,
                  name="dsa_idx_q_prep")
    ka = _rowprep(z, c_ka, HEAD_DIM, batch, s_len, norm="rms", gain=gk_a, rope_dim=HEAD_DIM, name="dsa_k_prep")
    ki = _rowprep(z, c_ki, HEAD_DIM, batch, s_len, norm="ln", gain=g_kidx, beta=b_kidx, rope_dim=DSA_IDX_ROPE,
                  name="dsa_idx_k_prep")
    va = z[:, c_va:c_va + HEAD_DIM]
    wi = jnp.pad(zs[:, GROUP_HEADS:GROUP_HEADS + DSA_IDX_HEADS] * (DSA_IDX_HEADS ** -0.5 * HEAD_DIM ** -0.5),
                 ((0, 0), (0, LANES - DSA_IDX_HEADS)))
    o_a = _dsa_attention(qi, ki, wi, qa, ka, va, batch, s_len)
    o_a = o_a.transpose(0, 2, 1, 3).reshape(batch * s_len, gw)

    qb = _rowprep(z, c_qb, gw, batch, s_len, norm="rms", gain=gq_b, out_scale=scale, head_major=True,
                  name="fox_q_prep")
    kb = _rowprep(z, c_kb, gw, batch, s_len, norm="rms", gain=gk_b, head_major=True, name="fox_k_prep")
    bias_row = jnp.pad(b_f.astype(F32), (0, LANES - GROUP_HEADS)).reshape(1, LANES)
    cum = _fox_cumsum(zs, bias_row, batch, s_len)
    o_b = _fox_attention(qb, kb, z, c_vb, c_gb, cum, batch, s_len)
    o_cat = jnp.concatenate([o_a, o_b], axis=1)
    return _matmul(o_cat, w_out.astype(BF16), F32, "even_out_proj", residual=x2, gate=gate,
                   rows_per_batch=s_len)


def _odd_mixer(h, x2, gate, batch, s_len, w_in, gq_c, gk_c, gq_d, gk_d, sinks, w_out):
    gw = GROUP_WIDTH
    kvw = SWA_KV_HEADS * HEAD_DIM
    c_qc, c_kc, c_vc, c_qd, c_kd = 0, gw, 2 * gw, 3 * gw, 4 * gw
    c_vd = c_kd + kvw
    z = _matmul(h, w_in.astype(BF16), BF16, "odd_in_proj")
    scale = HEAD_DIM ** -0.5
    qc = _rowprep(z, c_qc, gw, batch, s_len, norm="rms", gain=gq_c, rope_dim=HEAD_DIM, out_scale=scale,
                  head_major=True, name="moba_q_prep")
    kc, kmean = _rowprep(z, c_kc, gw, batch, s_len, norm="rms", gain=gk_c, rope_dim=HEAD_DIM, head_major=True,
                         want_mean=True, name="moba_k_prep")
    o_c = _moba_attention(qc, kc, kmean, z, c_vc, batch, s_len)
    qd = _rowprep(z, c_qd, gw, batch, s_len, norm="rms", gain=gq_d, rope_dim=HEAD_DIM, out_scale=scale,
                  head_major=True, name="swa_q_prep")
    kd = _rowprep(z, c_kd, kvw, batch, s_len, norm="rms", gain=gk_d, rope_dim=HEAD_DIM, head_major=True,
                  name="swa_k_prep")
    o_d = _swa_attention(qd, kd, z, c_vd, sinks, batch, s_len)
    o_cat = jnp.concatenate([o_c, o_d], axis=1)
    return _matmul(o_cat, w_out.astype(BF16), F32, "odd_out_proj", residual=x2, gate=gate, rows_per_batch=s_len)


def kernel(x, c, g_norm_mix, g_norm_ffn, w_ada, b_ada, w_in_even, g_cq, w_uq, w_iq, gq_a, gk_a, g_kidx, b_kidx,
           gq_b, gk_b, b_forget, w_out_even, w_in_odd, gq_c, gk_c, gq_d, gk_d, sinks_d, w_out_odd,
           w_router, b_router, w_exp_in, b_exp_in, w_exp_out, b_exp_out):
    batch, s_len, d = x.shape
    depth = w_ada.shape[0]
    mod = _ada_modulation(c, w_ada, b_ada)
    x2 = x.reshape(batch * s_len, d)
    for layer in range(depth):
        shift_m, scale_m, gate_m, shift_f, scale_f, gate_f = [mod[layer, :, i * d:(i + 1) * d] for i in range(6)]
        h = _norm_modulate(x2, g_norm_mix[layer], scale_m, shift_m, s_len)
        j = layer // 2
        if layer % 2 == 0:
            x2 = _even_mixer(h, x2, gate_m, batch, s_len, w_in_even[j], g_cq[j], w_uq[j], w_iq[j], gq_a[j],
                             gk_a[j], g_kidx[j], b_kidx[j], gq_b[j], gk_b[j], b_forget[j], w_out_even[j])
        else:
            x2 = _odd_mixer(h, x2, gate_m, batch, s_len, w_in_odd[j], gq_c[j], gk_c[j], gq_d[j], gk_d[j],
                            sinks_d[j], w_out_odd[j])
        x2 = _moe_block(x2, g_norm_ffn[layer], scale_f, shift_f, gate_f, w_router[layer], b_router[layer],
                        w_exp_in[layer], b_exp_in[layer], w_exp_out[layer], b_exp_out[layer], s_len)
    return x2.reshape(batch, s_len, d)
```

```python
import functools

import numpy as np
import jax
import jax.numpy as jnp
from jax import lax
from jax.experimental import pallas as pl
from jax.experimental.pallas import tpu as pltpu

F32 = jnp.float32
BF16 = jnp.bfloat16
I32 = jnp.int32

LANES = 128
HEAD_DIM = 128
GROUP_HEADS = 16
GROUP_WIDTH = GROUP_HEADS * HEAD_DIM
ROPE_THETA = 10000.0
EPS = 1e-6
DSA_Q_LORA = 1024
DSA_IDX_HEADS = 32
DSA_IDX_ROPE = 64
DSA_TOPK = 256
DSA_QBLK = 128
DSA_CHUNK = 256
DSA_STAGES = 4
DSA_HEAD_GROUP = 2
MOBA_BLOCK = 256
MOBA_TOPK = 3
SWA_KV_HEADS = 2
SWA_WINDOW = 128
N_EXPERTS = 32
MOE_TOPK = 4
EXPERT_FF = 512
SWIGLU_LIMIT = 7.0
SWIGLU_ALPHA = 1.702
MOE_ROW_TILE = 256
NEG_BIG = -1e30
LOG2_E = 1.4426950408889634
Q_SCALE = HEAD_DIM ** -0.5 * LOG2_E
VMEM_LIMIT = 56 * 1024 * 1024


def _cparams(semantics, vmem=VMEM_LIMIT):
    return pltpu.CompilerParams(dimension_semantics=semantics, vmem_limit_bytes=vmem)


def _ada_kernel(cb_ref, w_ref, b_ref, o_ref):
    nb = cb_ref.shape[0]
    tn = w_ref.shape[2]
    for c in range(tn // LANES):
        w = w_ref[0, :, c * LANES:(c + 1) * LANES]
        for b in range(nb):
            o_ref[0, b:b + 1, c * LANES:(c + 1) * LANES] = (
                jnp.sum(w * cb_ref[b], axis=0, keepdims=True) + b_ref[0, :, c * LANES:(c + 1) * LANES])


def _ada_modulation(c, w_ada, b_ada, tn=512):
    depth, d, n6 = w_ada.shape
    nb = c.shape[0]
    cb = jnp.broadcast_to(jax.nn.silu(c)[:, :, None], (nb, d, LANES))
    return pl.pallas_call(
        _ada_kernel,
        grid=(depth, n6 // tn),
        in_specs=[pl.BlockSpec((nb, d, LANES), lambda l, j: (0, 0, 0)),
                  pl.BlockSpec((1, d, tn), lambda l, j: (l, 0, j)),
                  pl.BlockSpec((1, 1, tn), lambda l, j: (l, 0, j))],
        out_specs=pl.BlockSpec((1, nb, tn), lambda l, j: (l, 0, j)),
        out_shape=jax.ShapeDtypeStruct((depth, nb, n6), F32),
        compiler_params=_cparams(("parallel", "parallel")),
        name="ada_modulation",
    )(cb, w_ada, b_ada.reshape(depth, 1, n6))


def _normmod_kernel(x_ref, g_ref, sc_ref, sh_ref, o_ref):
    x = x_ref[...]
    y = x * lax.rsqrt(jnp.mean(x * x, axis=-1, keepdims=True) + EPS) * g_ref[...]
    o_ref[...] = (y * (1.0 + sc_ref[0]) + sh_ref[0]).astype(o_ref.dtype)


def _norm_modulate(x2, g, scale, shift, rows_per_batch, tm=256):
    n, d = x2.shape
    tpb = rows_per_batch // tm
    return pl.pallas_call(
        _normmod_kernel,
        grid=(n // tm,),
        in_specs=[pl.BlockSpec((tm, d), lambda i: (i, 0)),
                  pl.BlockSpec((1, d), lambda i: (0, 0)),
                  pl.BlockSpec((1, 1, d), lambda i: (i // tpb, 0, 0)),
                  pl.BlockSpec((1, 1, d), lambda i: (i // tpb, 0, 0))],
        out_specs=pl.BlockSpec((tm, d), lambda i: (i, 0)),
        out_shape=jax.ShapeDtypeStruct((n, d), BF16),
        compiler_params=_cparams(("parallel",)),
        name="norm_modulate",
    )(x2, g.reshape(1, d), scale[:, None, :], shift[:, None, :])


def _mm_kernel(a_ref, b_ref, o_ref):
    o_ref[...] = jnp.dot(a_ref[...], b_ref[...], preferred_element_type=F32).astype(o_ref.dtype)


def _mm_residual_kernel(a_ref, b_ref, x_ref, g_ref, o_ref):
    o_ref[...] = x_ref[...] + g_ref[0] * jnp.dot(a_ref[...], b_ref[...], preferred_element_type=F32)


def _pick(n, prefs):
    for t in prefs:
        if n % t == 0:
            return t
    return n


def _matmul(a, b, out_dtype, name, residual=None, gate=None, rows_per_batch=None):
    m, k = a.shape
    n = b.shape[1]
    tm = _pick(m, (1024, 512, 256, 128))
    tn = _pick(n, (512, 640, 384, 256, 128) if k > 2048 else (1024, 512, 640, 384, 256, 128))
    grid = (m // tm, n // tn)
    a_spec = pl.BlockSpec((tm, k), lambda i, j: (i, 0))
    b_spec = pl.BlockSpec((k, tn), lambda i, j: (0, j))
    o_spec = pl.BlockSpec((tm, tn), lambda i, j: (i, j))
    params = _cparams(("parallel", "arbitrary"))
    if residual is None:
        return pl.pallas_call(
            _mm_kernel, grid=grid, in_specs=[a_spec, b_spec], out_specs=o_spec,
            out_shape=jax.ShapeDtypeStruct((m, n), out_dtype), compiler_params=params, name=name)(a, b)
    tpb = rows_per_batch // tm
    return pl.pallas_call(
        _mm_residual_kernel, grid=grid,
        in_specs=[a_spec, b_spec, o_spec, pl.BlockSpec((1, 1, tn), lambda i, j: (i // tpb, 0, j))],
        out_specs=o_spec, out_shape=jax.ShapeDtypeStruct((m, n), F32),
        compiler_params=params, name=name)(a, b, residual, gate[:, None, :])


def _rope_tables(s_len, rot_dim):
    half = rot_dim // 2
    inv_freq = ROPE_THETA ** (-jnp.arange(half, dtype=F32) / half)
    ang = jnp.arange(s_len, dtype=F32)[:, None] * inv_freq[None, :]
    cos, sin = jnp.cos(ang), jnp.sin(ang)
    pad = HEAD_DIM - rot_dim
    ones = jnp.ones((s_len, pad), F32)
    zeros = jnp.zeros((s_len, pad), F32)
    zh = jnp.zeros((s_len, half), F32)
    cos_t = jnp.concatenate([cos, cos, ones], axis=1)
    sa = jnp.concatenate([-sin, zh, zeros], axis=1)
    sb = jnp.concatenate([zh, sin, zeros], axis=1)
    return cos_t, sa, sb


def _rowprep_kernel(*refs, hw, norm, rope_half, out_scale, head_major, want_mean):
    it = iter(refs)
    z_ref = next(it)
    g_ref = next(it) if norm != "none" else None
    beta_ref = next(it) if norm == "ln" else None
    if rope_half:
        cos_ref, sa_ref, sb_ref = next(it), next(it), next(it)
    o_ref = next(it)
    mean_ref = next(it) if want_mean else None
    width = z_ref.shape[1]
    for h in range(width // hw):
        x = z_ref[:, h * hw:(h + 1) * hw].astype(F32)
        if norm == "rms":
            x = x * lax.rsqrt(jnp.mean(x * x, axis=-1, keepdims=True) + EPS) * g_ref[...]
        elif norm == "ln":
            mu = jnp.mean(x, axis=-1, keepdims=True)
            xc = x - mu
            x = xc * lax.rsqrt(jnp.mean(xc * xc, axis=-1, keepdims=True) + EPS) * g_ref[...] + beta_ref[...]
        if rope_half:
            if rope_half * 2 == hw:
                x = x * cos_ref[...] + pltpu.roll(x, rope_half, 1) * (sa_ref[...] + sb_ref[...])
            else:
                x = (x * cos_ref[...] + pltpu.roll(x, hw - rope_half, 1) * sa_ref[...]
                     + pltpu.roll(x, rope_half, 1) * sb_ref[...])
        if want_mean:
            mean_ref[0, 0, h:h + 1, :] = jnp.mean(x, axis=0, keepdims=True)
        if out_scale != 1.0:
            x = x * out_scale
        if head_major:
            o_ref[0, h] = x.astype(o_ref.dtype)
        else:
            o_ref[:, h * hw:(h + 1) * hw] = x.astype(o_ref.dtype)


def _rowprep(z, col0, width, batch, s_len, *, hw=HEAD_DIM, norm="none", gain=None, beta=None, rope_dim=0,
             out_scale=1.0, head_major=False, want_mean=False, name="rowprep", tm=256):
    n = z.shape[0]
    tpb = s_len // tm
    assert col0 % width == 0 and n == batch * s_len
    cb = col0 // width
    args = [z]
    specs = [pl.BlockSpec((tm, width), lambda i: (i, cb))]
    if norm != "none":
        args.append(gain.reshape(1, hw).astype(F32))
        specs.append(pl.BlockSpec((1, hw), lambda i: (0, 0)))
    if norm == "ln":
        args.append(beta.reshape(1, hw).astype(F32))
        specs.append(pl.BlockSpec((1, hw), lambda i: (0, 0)))
    if rope_dim:
        args += list(_rope_tables(s_len, rope_dim))
        specs += [pl.BlockSpec((tm, HEAD_DIM), lambda i: (i % tpb, 0))] * 3
    nh = width // hw
    if head_major:
        out_shape = [jax.ShapeDtypeStruct((batch, nh, s_len, hw), BF16)]
        out_specs = [pl.BlockSpec((1, nh, tm, hw), lambda i: (i // tpb, 0, i % tpb, 0))]
    else:
        out_shape = [jax.ShapeDtypeStruct((n, width), BF16)]
        out_specs = [pl.BlockSpec((tm, width), lambda i: (i, 0))]
    if want_mean:
        assert tm == MOBA_BLOCK
        out_shape.append(jax.ShapeDtypeStruct((batch, tpb, nh, hw), F32))
        out_specs.append(pl.BlockSpec((1, 1, nh, hw), lambda i: (i // tpb, i % tpb, 0, 0)))
    kern = functools.partial(_rowprep_kernel, hw=hw, norm=norm, rope_half=rope_dim // 2, out_scale=out_scale,
                             head_major=head_major, want_mean=want_mean)
    res = pl.pallas_call(kern, grid=(n // tm,), in_specs=specs, out_specs=out_specs, out_shape=out_shape,
                         compiler_params=_cparams(("parallel",)), name=name)(*args)
    return res if want_mean else res[0]


def _fox_cum_kernel(z_ref, bf_ref, tri_ref, o_ref, carry_ref):
    @pl.when(pl.program_id(1) == 0)
    def _():
        carry_ref[...] = jnp.zeros_like(carry_ref)

    xv = z_ref[...] + bf_ref[...]
    lf = jnp.minimum(xv, 0.0) - jnp.log(1.0 + jnp.exp(-jnp.abs(xv)))
    hi = lf.astype(BF16)
    r1 = lf - hi.astype(F32)
    mid = r1.astype(BF16)
    lo = (r1 - mid.astype(F32)).astype(BF16)
    tri = tri_ref[...]
    cs = (jnp.dot(tri, hi, preferred_element_type=F32) + jnp.dot(tri, mid, preferred_element_type=F32)
          + jnp.dot(tri, lo, preferred_element_type=F32)) + carry_ref[...]
    o_ref[...] = cs * LOG2_E
    tm = cs.shape[0]
    carry_ref[...] = cs[tm - 1:tm, :]


def _fox_cumsum(zs, bias_row, batch, s_len, tm=256):
    n = zs.shape[0]
    tpb = s_len // tm
    tri = jnp.tril(jnp.ones((tm, tm), F32)).astype(BF16)
    return pl.pallas_call(
        _fox_cum_kernel,
        grid=(batch, tpb),
        in_specs=[pl.BlockSpec((tm, LANES), lambda b, i: (b * tpb + i, 0)),
                  pl.BlockSpec((1, LANES), lambda b, i: (0, 0)),
                  pl.BlockSpec((tm, tm), lambda b, i: (0, 0))],
        out_specs=pl.BlockSpec((tm, LANES), lambda b, i: (b * tpb + i, 0)),
        out_shape=jax.ShapeDtypeStruct((n, LANES), F32),
        scratch_shapes=[pltpu.VMEM((1, LANES), F32)],
        compiler_params=_cparams(("parallel", "arbitrary")),
        name="fox_cumsum",
    )(zs, bias_row, tri)


def _dsa_kernel(qi_ref, kit_ref, wi_ref, qa_ref, kat_ref, va_ref, o_ref, wb_ref, key_ref, mb_ref,
                *, sk, q0, topk):
    q = DSA_QBLK
    ch = DSA_CHUNK
    nch = sk // ch
    n_idx = qi_ref.shape[1]
    n_heads = qa_ref.shape[1]
    t0 = (q0 + pl.program_id(1)) * q
    row = t0 + lax.broadcasted_iota(I32, (q, ch), 0)
    lane = lax.broadcasted_iota(I32, (q, ch), 1)

    wi = wi_ref[...]
    for h in range(n_idx):
        wb_ref[h] = jnp.broadcast_to(wi[:, h:h + 1], (q, ch))
    qi = qi_ref[0].reshape(n_idx * q, HEAD_DIM)

    def idx_chunk(c, carry):
        d = jnp.dot(qi, kit_ref[0, c], preferred_element_type=F32)
        acc = jnp.zeros((q, ch), F32)
        for h in range(n_idx):
            acc = acc + jnp.maximum(d[h * q:(h + 1) * q], 0.0) * wb_ref[h]
        score = jnp.where(c * ch + lane <= row, acc + 0.0, -jnp.inf)
        bits = lax.bitcast_convert_type(score, I32)
        key_ref[c] = jnp.where(bits >= 0, bits, bits ^ jnp.int32(0x7FFFFFFF))
        return carry

    lax.fori_loop(0, nch, idx_chunk, 0)

    def count_ge(cand):
        def body(c, acc):
            return acc + jnp.where(key_ref[c] >= cand, 1.0, 0.0)
        acc = lax.fori_loop(0, nch, body, jnp.zeros((q, ch), F32))
        return jnp.sum(acc, axis=-1, keepdims=True)

    int_min = jnp.int32(-2 ** 31)
    thr = jnp.where(count_ge(jnp.zeros((q, 1), I32)) >= topk, jnp.int32(0), int_min)

    def bit_step(j, thr):
        cand = thr | jnp.left_shift(jnp.int32(1), 30 - j)
        return jnp.where(count_ge(cand) >= topk, cand, thr)

    thr = lax.fori_loop(0, 31, bit_step, thr)

    for c in range(nch):
        keep = (key_ref[c] >= thr) & (c * ch + lane <= row)
        mb_ref[:, c * ch:(c + 1) * ch] = jnp.where(keep, 0.0, -jnp.inf)

    g = DSA_HEAD_GROUP

    def head_group(hg, carry):
        qh = qa_ref[0, pl.ds(hg * g, g)].reshape(g * q, HEAD_DIM)
        lg = jnp.dot(qh, kat_ref[0], preferred_element_type=F32).reshape(g, q, sk) + mb_ref[...][None]
        m = jnp.max(lg, axis=-1, keepdims=True)
        p = jnp.exp2(lg - m)
        l = jnp.sum(p, axis=-1, keepdims=True)
        o = jnp.dot(p.reshape(g * q, sk).astype(BF16), va_ref[0], preferred_element_type=F32)
        o_ref[0, pl.ds(hg * g, g)] = (o.reshape(g, q, HEAD_DIM) / l).astype(o_ref.dtype)
        return carry

    lax.fori_loop(0, n_heads // g, head_group, 0)


def _dsa_attention(qi, ki, wi, qa, ka, va, batch, s_len):
    q, ch = DSA_QBLK, DSA_CHUNK
    n_idx, n_heads = qi.shape[1], qa.shape[1]
    topk = min(DSA_TOPK, s_len // 4)
    nqb = s_len // q
    kit = ki.reshape(batch, s_len // ch, ch, HEAD_DIM).transpose(0, 1, 3, 2)
    kat = ka.reshape(batch, s_len, HEAD_DIM).transpose(0, 2, 1)
    va3 = va.reshape(batch, s_len, HEAD_DIM)
    per_stage = max(nqb // DSA_STAGES, ch // q)
    outs = []
    for q0 in range(0, nqb, per_stage):
        sk = (q0 + per_stage) * q
        nch = sk // ch
        kern = functools.partial(_dsa_kernel, sk=sk, q0=q0, topk=topk)
        outs.append(pl.pallas_call(
            kern,
            grid=(batch, per_stage),
            in_specs=[pl.BlockSpec((1, n_idx, q, HEAD_DIM), lambda b, i, q0=q0: (b, 0, q0 + i, 0)),
                      pl.BlockSpec((1, nch, HEAD_DIM, ch), lambda b, i: (b, 0, 0, 0)),
                      pl.BlockSpec((q, LANES), lambda b, i, q0=q0: (b * nqb + q0 + i, 0)),
                      pl.BlockSpec((1, n_heads, q, HEAD_DIM), lambda b, i, q0=q0: (b, 0, q0 + i, 0)),
                      pl.BlockSpec((1, HEAD_DIM, sk), lambda b, i: (b, 0, 0)),
                      pl.BlockSpec((1, sk, HEAD_DIM), lambda b, i: (b, 0, 0))],
            out_specs=pl.BlockSpec((1, n_heads, q, HEAD_DIM), lambda b, i: (b, 0, i, 0)),
            out_shape=jax.ShapeDtypeStruct((batch, n_heads, per_stage * q, HEAD_DIM), BF16),
            scratch_shapes=[pltpu.VMEM((n_idx, q, ch), F32),
                            pltpu.VMEM((nch, q, ch), I32),
                            pltpu.VMEM((q, sk), F32)],
            compiler_params=_cparams(("parallel", "parallel")),
            name=f"dsa_attention_k{sk}",
        )(qi, kit, wi, qa, kat, va3))
    return jnp.concatenate(outs, axis=2)


def _tri_tables(nq, ratio):
    qt, kt = [], []
    for i in range(nq):
        for j in range((i + 1) * ratio):
            qt.append(i)
            kt.append(j)
    return jnp.asarray(qt, I32), jnp.asarray(kt, I32)


def _fox_kernel(qt_ref, kt_ref, q_ref, k_ref, v_ref, cq_ref, ck_ref, g_ref, o_ref,
                m_ref, l_ref, acc_ref, cqs_ref, *, tq):
    h = pl.program_id(1)
    step = pl.program_id(2)
    qi = qt_ref[step]
    kj = kt_ref[step]

    @pl.when(kj == 0)
    def _():
        m_ref[...] = jnp.full_like(m_ref, -jnp.inf)
        l_ref[...] = jnp.zeros_like(l_ref)
        acc_ref[...] = jnp.zeros_like(acc_ref)
        lane = lax.broadcasted_iota(I32, cq_ref.shape, 1)
        cqs_ref[...] = jnp.sum(jnp.where(lane == h, cq_ref[...], 0.0), axis=-1, keepdims=True)

    s = lax.dot_general(q_ref[0, 0], k_ref[0, 0], (((1,), (1,)), ((), ())), preferred_element_type=F32)
    s = s + (cqs_ref[...] - ck_ref[0, 0])

    def update(s):
        m_prev = m_ref[...]
        m_new = jnp.maximum(m_prev, jnp.max(s, axis=-1, keepdims=True))
        alpha = jnp.exp2(m_prev - m_new)
        p = jnp.exp2(s - m_new)
        l_ref[...] = alpha * l_ref[...] + jnp.sum(p, axis=-1, keepdims=True)
        acc_ref[...] = alpha * acc_ref[...] + jnp.dot(p.astype(BF16), v_ref[...], preferred_element_type=F32)
        m_ref[...] = m_new

    @pl.when(kj < qi)
    def _():
        update(s)

    @pl.when(kj == qi)
    def _():
        row = lax.broadcasted_iota(I32, s.shape, 0)
        col = lax.broadcasted_iota(I32, s.shape, 1)
        update(jnp.where(col <= row, s, -jnp.inf))
        gate = g_ref[...].astype(F32)
        o_ref[...] = (acc_ref[...] / l_ref[...] * (1.0 / (1.0 + jnp.exp(-gate)))).astype(o_ref.dtype)


def _fox_attention(qb, kb, z, v_col0, g_col0, cum, batch, s_len, tq=1024):
    tq = min(tq, s_len)
    nq = s_len // tq
    n_heads = qb.shape[1]
    qt, kt = _tri_tables(nq, 1)
    cum_t = cum.reshape(batch, s_len, LANES)[:, :, :n_heads].transpose(0, 2, 1)[:, :, None, :]
    vb0, gb0 = v_col0 // HEAD_DIM, g_col0 // HEAD_DIM
    grid_spec = pltpu.PrefetchScalarGridSpec(
        num_scalar_prefetch=2,
        grid=(batch, n_heads, int(qt.shape[0])),
        in_specs=[pl.BlockSpec((1, 1, tq, HEAD_DIM), lambda b, h, s, qt, kt: (b, h, qt[s], 0)),
                  pl.BlockSpec((1, 1, tq, HEAD_DIM), lambda b, h, s, qt, kt: (b, h, kt[s], 0)),
                  pl.BlockSpec((tq, HEAD_DIM), lambda b, h, s, qt, kt: (b * nq + kt[s], vb0 + h)),
                  pl.BlockSpec((tq, LANES), lambda b, h, s, qt, kt: (b * nq + qt[s], 0)),
                  pl.BlockSpec((1, 1, 1, tq), lambda b, h, s, qt, kt: (b, h, 0, kt[s])),
                  pl.BlockSpec((tq, HEAD_DIM), lambda b, h, s, qt, kt: (b * nq + qt[s], gb0 + h))],
        out_specs=pl.BlockSpec((tq, HEAD_DIM), lambda b, h, s, qt, kt: (b * nq + qt[s], h)),
        scratch_shapes=[pltpu.VMEM((tq, 1), F32), pltpu.VMEM((tq, 1), F32),
                        pltpu.VMEM((tq, HEAD_DIM), F32), pltpu.VMEM((tq, 1), F32)])
    return pl.pallas_call(
        functools.partial(_fox_kernel, tq=tq), grid_spec=grid_spec,
        out_shape=jax.ShapeDtypeStruct((batch * s_len, n_heads * HEAD_DIM), BF16),
        compiler_params=_cparams(("parallel", "parallel", "arbitrary")),
        name="fox_attention",
    )(qt, kt, qb, kb, z, cum, cum_t, z)


def _moba_kernel(qt_ref, kt_ref, q_ref, k_ref, e_ref, v_ref, km_ref, o_ref, m_ref, l_ref, acc_ref, qa_ref,
                 *, tq, n_sel):
    step = pl.program_id(2)
    qi = qt_ref[step]
    kj = kt_ref[step]

    @pl.when(kj == 0)
    def _():
        m_ref[...] = jnp.full_like(m_ref, NEG_BIG)
        l_ref[...] = jnp.zeros_like(l_ref)
        acc_ref[...] = jnp.zeros_like(acc_ref)
        own = (qi * tq + lax.broadcasted_iota(I32, (tq, 1), 0)) // MOBA_BLOCK
        gate = jnp.dot(q_ref[0, 0].astype(F32), km_ref[0, 0], preferred_element_type=F32,
                       precision=lax.Precision.HIGHEST)
        lane = lax.broadcasted_iota(I32, gate.shape, 1)
        lane_f = lane.astype(F32)
        gate = jnp.where(lane < own, gate, -jnp.inf)
        allowed = jnp.where(lane == own, 1.0, 0.0)
        for _ in range(n_sel):
            best = jnp.max(gate, axis=-1, keepdims=True)
            first = jnp.min(jnp.where(gate == best, lane_f, float(LANES)), axis=-1, keepdims=True)
            pick = (lane_f == first) & (best > -jnp.inf)
            allowed = jnp.where(pick, 1.0, allowed)
            gate = jnp.where(pick, -jnp.inf, gate)
        qa_ref[:, :HEAD_DIM] = q_ref[0, 0]
        qa_ref[:, HEAD_DIM:] = jnp.where(allowed > 0.0, 0.0, NEG_BIG).astype(qa_ref.dtype)

    k_aug = jnp.concatenate([k_ref[0, 0], e_ref[...]], axis=1)
    s = lax.dot_general(qa_ref[...], k_aug, (((1,), (1,)), ((), ())), preferred_element_type=F32)

    def update(s):
        m_prev = m_ref[...]
        m_new = jnp.maximum(m_prev, jnp.max(s, axis=-1, keepdims=True))
        alpha = jnp.exp2(m_prev - m_new)
        p = jnp.exp2(s - m_new)
        l_ref[...] = alpha * l_ref[...] + jnp.sum(p, axis=-1, keepdims=True)
        acc_ref[...] = alpha * acc_ref[...] + jnp.dot(p.astype(BF16), v_ref[...], preferred_element_type=F32)
        m_ref[...] = m_new

    @pl.when(kj < qi)
    def _():
        update(s)

    @pl.when(kj == qi)
    def _():
        row = lax.broadcasted_iota(I32, s.shape, 0)
        col = lax.broadcasted_iota(I32, s.shape, 1)
        update(jnp.where(col <= row, s, NEG_BIG))
        o_ref[...] = (acc_ref[...] / l_ref[...]).astype(o_ref.dtype)


def _moba_attention(qc, kc, kmean, z, v_col0, batch, s_len, tq=1024):
    tq = min(tq, s_len)
    nq = s_len // tq
    n_heads = qc.shape[1]
    nb = s_len // MOBA_BLOCK
    assert nb <= LANES
    n_sel = min(MOBA_TOPK, nb - 1)
    qt, kt = _tri_tables(nq, 1)
    km_t = jnp.pad(kmean.transpose(0, 2, 3, 1), ((0, 0), (0, 0), (0, 0), (0, LANES - nb)))
    block_onehot = (jnp.arange(s_len, dtype=I32)[:, None] // MOBA_BLOCK
                    == jnp.arange(LANES, dtype=I32)[None, :]).astype(BF16)
    vb0 = v_col0 // HEAD_DIM
    grid_spec = pltpu.PrefetchScalarGridSpec(
        num_scalar_prefetch=2,
        grid=(batch, n_heads, int(qt.shape[0])),
        in_specs=[pl.BlockSpec((1, 1, tq, HEAD_DIM), lambda b, h, s, qt, kt: (b, h, qt[s], 0)),
                  pl.BlockSpec((1, 1, tq, HEAD_DIM), lambda b, h, s, qt, kt: (b, h, kt[s], 0)),
                  pl.BlockSpec((tq, LANES), lambda b, h, s, qt, kt: (kt[s], 0)),
                  pl.BlockSpec((tq, HEAD_DIM), lambda b, h, s, qt, kt: (b * nq + kt[s], vb0 + h)),
                  pl.BlockSpec((1, 1, HEAD_DIM, LANES), lambda b, h, s, qt, kt: (b, h, 0, 0))],
        out_specs=pl.BlockSpec((tq, HEAD_DIM), lambda b, h, s, qt, kt: (b * nq + qt[s], h)),
        scratch_shapes=[pltpu.VMEM((tq, 1), F32), pltpu.VMEM((tq, 1), F32),
                        pltpu.VMEM((tq, HEAD_DIM), F32), pltpu.VMEM((tq, HEAD_DIM + LANES), BF16)])
    return pl.pallas_call(
        functools.partial(_moba_kernel, tq=tq, n_sel=n_sel), grid_spec=grid_spec,
        out_shape=jax.ShapeDtypeStruct((batch * s_len, n_heads * HEAD_DIM), BF16),
        compiler_params=_cparams(("parallel", "parallel", "arbitrary")),
        name="moba_attention",
    )(qt, kt, qc, kc, block_onehot, z, km_t)


def _swa_kernel(q_ref, kp_ref, kc_ref, vp_ref, vc_ref, sink_ref, o_ref):
    n = pl.program_id(1)
    hq, w = q_ref.shape[1], q_ref.shape[2]
    hkv = kc_ref.shape[1]
    grp = hq // hkv
    ti = lax.broadcasted_iota(I32, (w, w), 0)
    si = lax.broadcasted_iota(I32, (w, w), 1)
    cur_ok = (si <= ti)[None]
    prev_ok = ((si > ti) & (n > 0))[None]
    nt = (((1,), (1,)), ((), ()))
    for kv in range(hkv):
        q = q_ref[0, kv * grp:(kv + 1) * grp].reshape(grp * w, HEAD_DIM)
        sc = lax.dot_general(q, kc_ref[0, kv], nt, preferred_element_type=F32).reshape(grp, w, w)
        sp = lax.dot_general(q, kp_ref[0, kv], nt, preferred_element_type=F32).reshape(grp, w, w)
        sc = jnp.where(cur_ok, sc, -jnp.inf)
        sp = jnp.where(prev_ok, sp, -jnp.inf)
        sink = sink_ref[kv * grp * w:(kv + 1) * grp * w].reshape(grp, w, LANES)[:, :, :1]
        m = jnp.maximum(jnp.maximum(jnp.max(sc, axis=-1, keepdims=True), jnp.max(sp, axis=-1, keepdims=True)), sink)
        pc = jnp.exp2(sc - m)
        pp = jnp.exp2(sp - m)
        den = jnp.sum(pc, axis=-1, keepdims=True) + jnp.sum(pp, axis=-1, keepdims=True) + jnp.exp2(sink - m)
        o = (jnp.dot(pc.reshape(grp * w, w).astype(BF16), vc_ref[:, kv * HEAD_DIM:(kv + 1) * HEAD_DIM],
                     preferred_element_type=F32)
             + jnp.dot(pp.reshape(grp * w, w).astype(BF16), vp_ref[:, kv * HEAD_DIM:(kv + 1) * HEAD_DIM],
                       preferred_element_type=F32))
        o = o.reshape(grp, w, HEAD_DIM) / den
        for gh in range(grp):
            hh = kv * grp + gh
            o_ref[:, hh * HEAD_DIM:(hh + 1) * HEAD_DIM] = o[gh].astype(o_ref.dtype)


def _swa_attention(qd, kd, z, v_col0, sinks, batch, s_len):
    w = SWA_WINDOW
    nb = s_len // w
    hq, hkv = qd.shape[1], kd.shape[1]
    vw = hkv * HEAD_DIM
    vb0 = v_col0 // vw
    sink_b = jnp.broadcast_to((sinks.astype(F32) * LOG2_E)[:, None, None], (hq, w, LANES)).reshape(hq * w, LANES)
    return pl.pallas_call(
        _swa_kernel,
        grid=(batch, nb),
        in_specs=[pl.BlockSpec((1, hq, w, HEAD_DIM), lambda b, n: (b, 0, n, 0)),
                  pl.BlockSpec((1, hkv, w, HEAD_DIM), lambda b, n: (b, 0, jnp.maximum(n - 1, 0), 0)),
                  pl.BlockSpec((1, hkv, w, HEAD_DIM), lambda b, n: (b, 0, n, 0)),
                  pl.BlockSpec((w, vw), lambda b, n: (b * nb + jnp.maximum(n - 1, 0), vb0)),
                  pl.BlockSpec((w, vw), lambda b, n: (b * nb + n, vb0)),
                  pl.BlockSpec((hq * w, LANES), lambda b, n: (0, 0))],
        out_specs=pl.BlockSpec((w, hq * HEAD_DIM), lambda b, n: (b * nb + n, 0)),
        out_shape=jax.ShapeDtypeStruct((batch * s_len, hq * HEAD_DIM), BF16),
        compiler_params=_cparams(("parallel", "parallel")),
        name="swa_attention",
    )(qd, kd, kd, z, z, sink_b)


def _route_kernel(x_ref, g_ref, sc_ref, sh_ref, wr_ref, br_ref, tri_ref,
                  h_ref, idx_ref, wgt_ref, rank_ref, cnt_ref, carry_ref):
    @pl.when(pl.program_id(0) == 0)
    def _():
        carry_ref[...] = jnp.zeros_like(carry_ref)

    x = x_ref[...]
    y = x * lax.rsqrt(jnp.mean(x * x, axis=-1, keepdims=True) + EPS) * g_ref[...]
    hmod = y * (1.0 + sc_ref[0]) + sh_ref[0]
    h_ref[...] = hmod
    logits = jnp.dot(hmod, wr_ref[...], preferred_element_type=F32, precision=lax.Precision.HIGHEST) + br_ref[...]
    lane = lax.broadcasted_iota(I32, logits.shape, 1)
    lane_f = lane.astype(F32)
    logits = jnp.where(lane < N_EXPERTS, logits, -jnp.inf)
    onehots, vals, firsts = [], [], []
    for _ in range(MOE_TOPK):
        best = jnp.max(logits, axis=-1, keepdims=True)
        first = jnp.min(jnp.where(logits == best, lane_f, float(LANES)), axis=-1, keepdims=True)
        pick = lane_f == first
        onehots.append(pick)
        vals.append(best)
        firsts.append(first.astype(I32))
        logits = jnp.where(pick, -jnp.inf, logits)
    exps = [jnp.exp(v - vals[0]) for v in vals]
    den = exps[0]
    for e in exps[1:]:
        den = den + e
    chosen_f = jnp.zeros(logits.shape, F32)
    for o in onehots:
        chosen_f = jnp.where(o, 1.0, chosen_f)
    before = jnp.dot(tri_ref[...], chosen_f.astype(BF16), preferred_element_type=F32) + carry_ref[...]
    idx_out = jnp.zeros(logits.shape, I32)
    wgt_out = jnp.zeros(logits.shape, F32)
    rank_out = jnp.zeros(logits.shape, I32)
    for k in range(MOE_TOPK):
        rk = jnp.sum(jnp.where(onehots[k], before, 0.0), axis=-1, keepdims=True).astype(I32)
        idx_out = jnp.where(lane == k, firsts[k], idx_out)
        wgt_out = jnp.where(lane == k, exps[k] / den, wgt_out)
        rank_out = jnp.where(lane == k, rk, rank_out)
    idx_ref[...] = idx_out
    wgt_ref[...] = wgt_out
    rank_ref[...] = rank_out
    carry_ref[...] = carry_ref[...] + jnp.sum(chosen_f, axis=0, keepdims=True)
    cnt_ref[...] = carry_ref[...]


def _moe_route(x2, g, scale, shift, w_router, b_router, rows_per_batch, tm=256):
    n, d = x2.shape
    tpb = rows_per_batch // tm
    wr = jnp.pad(w_router, ((0, 0), (0, LANES - N_EXPERTS)))
    br = jnp.pad(b_router, (0, LANES - N_EXPERTS)).reshape(1, LANES)
    tri = jnp.tril(jnp.ones((tm, tm), F32), -1).astype(BF16)
    tok_spec = pl.BlockSpec((tm, LANES), lambda i: (i, 0))
    return pl.pallas_call(
        _route_kernel,
        grid=(n // tm,),
        in_specs=[pl.BlockSpec((tm, d), lambda i: (i, 0)),
                  pl.BlockSpec((1, d), lambda i: (0, 0)),
                  pl.BlockSpec((1, 1, d), lambda i: (i // tpb, 0, 0)),
                  pl.BlockSpec((1, 1, d), lambda i: (i // tpb, 0, 0)),
                  pl.BlockSpec((d, LANES), lambda i: (0, 0)),
                  pl.BlockSpec((1, LANES), lambda i: (0, 0)),
                  pl.BlockSpec((tm, tm), lambda i: (0, 0))],
        out_specs=[pl.BlockSpec((tm, d), lambda i: (i, 0)), tok_spec, tok_spec, tok_spec,
                   pl.BlockSpec((1, LANES), lambda i: (0, 0))],
        out_shape=[jax.ShapeDtypeStruct((n, d), F32), jax.ShapeDtypeStruct((n, LANES), I32),
                   jax.ShapeDtypeStruct((n, LANES), F32), jax.ShapeDtypeStruct((n, LANES), I32),
                   jax.ShapeDtypeStruct((1, LANES), F32)],
        scratch_shapes=[pltpu.VMEM((1, LANES), F32)],
        compiler_params=_cparams(("arbitrary",)),
        name="moe_route",
    )(x2, g.reshape(1, d), scale[:, None, :], shift[:, None, :], wr, br, tri)


def _invert_kernel(pos_ref, src_ref, *, chunk, n_rows):
    step = pl.program_id(0)

    @pl.when(step == 0)
    def _():
        def zero(p, carry):
            src_ref[p] = 0
            return carry
        lax.fori_loop(0, n_rows, zero, 0)

    def put(j, carry):
        src_ref[pos_ref[j]] = (step * chunk + j) // MOE_TOPK
        return carry

    lax.fori_loop(0, chunk, put, 0)


def _moe_invert(pos, n_rows, chunk=2048):
    n_assign = pos.shape[0]
    return pl.pallas_call(
        functools.partial(_invert_kernel, chunk=chunk, n_rows=n_rows),
        grid=(n_assign // chunk,),
        in_specs=[pl.BlockSpec((chunk,), lambda i: (i,), memory_space=pltpu.SMEM)],
        out_specs=pl.BlockSpec((n_rows,), lambda i: (0,), memory_space=pltpu.SMEM),
        out_shape=jax.ShapeDtypeStruct((n_rows,), I32),
        compiler_params=_cparams(("arbitrary",)),
        name="moe_invert",
    )(pos)


def _expert_kernel(te_ref, nt_ref, src_ref, h_ref, wi_ref, bi_ref, perm_ref, wo_ref, bo_ref, y_ref, xbuf_ref, sem):
    i = pl.program_id(0)
    n_used = nt_ref[0]
    tm = xbuf_ref.shape[1]

    def gather(tile, slot):
        def issue(t, carry):
            pltpu.make_async_copy(h_ref.at[pl.ds(src_ref[tile * tm + t], 1)],
                                  xbuf_ref.at[slot, pl.ds(t, 1)], sem.at[slot]).start()
            return carry
        lax.fori_loop(0, tm, issue, 0)

    @pl.when(i == 0)
    def _():
        gather(0, 0)

    @pl.when(i + 1 < n_used)
    def _():
        gather(i + 1, (i + 1) % 2)

    @pl.when(i < n_used)
    def _():
        slot = i % 2
        pltpu.make_async_copy(h_ref.at[pl.ds(0, tm)], xbuf_ref.at[slot], sem.at[slot]).wait()
        x = xbuf_ref[slot].astype(BF16)
        hh = (jnp.dot(x, wi_ref[0], preferred_element_type=F32) + bi_ref[0]).astype(BF16)
        hp = jnp.dot(hh, perm_ref[...], preferred_element_type=F32)
        x_glu = jnp.minimum(hp[:, :EXPERT_FF], SWIGLU_LIMIT)
        x_lin = jnp.clip(hp[:, EXPERT_FF:], -SWIGLU_LIMIT, SWIGLU_LIMIT)
        act = x_glu * (1.0 / (1.0 + jnp.exp(-SWIGLU_ALPHA * x_glu))) * (x_lin + 1.0)
        y_ref[...] = jnp.dot(act.astype(BF16), wo_ref[0], preferred_element_type=F32) + bo_ref[0]

    @pl.when(i >= n_used)
    def _():
        y_ref[...] = jnp.zeros_like(y_ref)


def _moe_experts(h2, src, tile_expert, n_tiles_used, w_in, b_in, w_out, b_out):
    n, d = h2.shape
    r = src.shape[0]
    tm = MOE_ROW_TILE
    f2 = w_in.shape[2]
    ff = w_out.shape[1]
    col = np.arange(f2)
    perm = np.zeros((f2, f2), np.float32)
    perm[col, np.where(col % 2 == 0, col // 2, ff + col // 2)] = 1.0
    grid_spec = pltpu.PrefetchScalarGridSpec(
        num_scalar_prefetch=3, grid=(r // tm,),
        in_specs=[pl.BlockSpec(memory_space=pl.ANY),
                  pl.BlockSpec((1, d, f2), lambda i, te, nt, src: (te[i], 0, 0)),
                  pl.BlockSpec((1, 1, f2), lambda i, te, nt, src: (te[i], 0, 0)),
                  pl.BlockSpec((f2, f2), lambda i, te, nt, src: (0, 0)),
                  pl.BlockSpec((1, ff, d), lambda i, te, nt, src: (te[i], 0, 0)),
                  pl.BlockSpec((1, 1, d), lambda i, te, nt, src: (te[i], 0, 0))],
        out_specs=pl.BlockSpec((tm, d), lambda i, te, nt, src: (i, 0)),
        scratch_shapes=[pltpu.VMEM((2, tm, d), F32), pltpu.SemaphoreType.DMA((2,))])
    return pl.pallas_call(
        _expert_kernel, grid_spec=grid_spec,
        out_shape=jax.ShapeDtypeStruct((r, d), F32),
        compiler_params=_cparams(("arbitrary",)),
        name="moe_experts",
    )(tile_expert, n_tiles_used, src, h2, w_in, b_in, jnp.asarray(perm, BF16), w_out, b_out)


def _combine_kernel(pos_ref, y_ref, x_ref, w_ref, g_ref, o_ref, buf_ref, sem, *, tm):
    i = pl.program_id(0)
    base = i * tm * MOE_TOPK

    def copy(src_row, k, t):
        return pltpu.make_async_copy(y_ref.at[pl.ds(src_row, 1)], buf_ref.at[k, pl.ds(t, 1)], sem)

    def issue(t, carry):
        for k in range(MOE_TOPK):
            copy(pos_ref[base + t * MOE_TOPK + k], k, t).start()
        return carry

    lax.fori_loop(0, tm, issue, 0)

    def drain(t, carry):
        for k in range(MOE_TOPK):
            copy(0, k, t).wait()
        return carry

    lax.fori_loop(0, tm, drain, 0)
    w = w_ref[...]
    mix = buf_ref[0] * w[:, 0:1]
    for k in range(1, MOE_TOPK):
        mix = mix + buf_ref[k] * w[:, k:k + 1]
    o_ref[...] = x_ref[...] + g_ref[0] * mix


def _moe_combine(y, pos, x2, wgt, gate, rows_per_batch, tm=128):
    n, d = x2.shape
    tpb = rows_per_batch // tm
    grid_spec = pltpu.PrefetchScalarGridSpec(
        num_scalar_prefetch=1, grid=(n // tm,),
        in_specs=[pl.BlockSpec(memory_space=pl.ANY),
                  pl.BlockSpec((tm, d), lambda i, pos: (i, 0)),
                  pl.BlockSpec((tm, LANES), lambda i, pos: (i, 0)),
                  pl.BlockSpec((1, 1, d), lambda i, pos: (i // tpb, 0, 0))],
        out_specs=pl.BlockSpec((tm, d), lambda i, pos: (i, 0)),
        scratch_shapes=[pltpu.VMEM((MOE_TOPK, tm, d), F32), pltpu.SemaphoreType.DMA(())])
    return pl.pallas_call(
        functools.partial(_combine_kernel, tm=tm), grid_spec=grid_spec,
        out_shape=jax.ShapeDtypeStruct((n, d), F32),
        compiler_params=_cparams(("arbitrary",)),
        name="moe_combine",
    )(pos, y, x2, wgt, gate[:, None, :])


def _moe_block(x2, g, scale, shift, gate, w_router, b_router, w_exp_in, b_exp_in, w_exp_out, b_exp_out,
               rows_per_batch):
    n, d = x2.shape
    tm = MOE_ROW_TILE
    h2, idx, wgt, rank, counts = _moe_route(x2, g, scale, shift, w_router, b_router, rows_per_batch)
    cnt = counts[0, :N_EXPERTS].astype(I32)
    padded = (cnt + tm - 1) // tm * tm
    ends = jnp.cumsum(padded)
    starts = ends - padded
    n_rows = n * MOE_TOPK + N_EXPERTS * tm
    n_tiles = n_rows // tm
    tile_expert = jnp.minimum(jnp.searchsorted(ends, jnp.arange(n_tiles, dtype=I32) * tm, side="right"),
                              N_EXPERTS - 1).astype(I32)
    n_tiles_used = (ends[-1] // tm).astype(I32).reshape(1)
    e_flat = idx[:, :MOE_TOPK].reshape(-1)
    pos = (starts[e_flat] + rank[:, :MOE_TOPK].reshape(-1)).astype(I32)
    src = _moe_invert(pos, n_rows)
    y = _moe_experts(h2, src, tile_expert, n_tiles_used, w_exp_in.astype(BF16), b_exp_in[:, None, :],
                     w_exp_out.astype(BF16), b_exp_out[:, None, :])
    return _moe_combine(y, pos, x2, wgt, gate, rows_per_batch)


def _even_mixer(h, x2, gate, batch, s_len, w_in, g_cq, w_uq, w_iq, gq_a, gk_a, g_kidx, b_kidx, gq_b, gk_b,
                b_f, w_out):
    gw = GROUP_WIDTH
    o_cq, o_ka, o_va, o_ki, o_wi, o_qb, o_kb, o_vb, o_fb, o_gb = np.cumsum(
        [0, DSA_Q_LORA, HEAD_DIM, HEAD_DIM, HEAD_DIM, DSA_IDX_HEADS, gw, gw, gw, GROUP_HEADS]).tolist()
    cols = lambda o, wdt: w_in[:, o:o + wdt]
    w_main = jnp.concatenate([cols(o_qb, gw), cols(o_kb, gw), cols(o_vb, gw), cols(o_gb, gw),
                              cols(o_cq, DSA_Q_LORA), cols(o_ka, HEAD_DIM), cols(o_va, HEAD_DIM),
                              cols(o_ki, HEAD_DIM)], axis=1).astype(BF16)
    c_qb, c_kb, c_vb, c_gb, c_cq = 0, gw, 2 * gw, 3 * gw, 4 * gw
    c_ka = c_cq + DSA_Q_LORA
    c_va, c_ki = c_ka + HEAD_DIM, c_ka + 2 * HEAD_DIM
    d = w_in.shape[0]
    w_small = jnp.concatenate([cols(o_fb, GROUP_HEADS), cols(o_wi, DSA_IDX_HEADS),
                               jnp.zeros((d, LANES - GROUP_HEADS - DSA_IDX_HEADS), F32)], axis=1).astype(BF16)
    z = _matmul(h, w_main, BF16, "even_in_proj")
    zs = _matmul(h, w_small, F32, "even_in_proj_small")
    scale = Q_SCALE

    cq = _rowprep(z, c_cq, DSA_Q_LORA, batch, s_len, hw=DSA_Q_LORA, norm="rms", gain=g_cq, name="dsa_cq_norm")
    qa_raw = _matmul(cq, w_uq.astype(BF16), BF16, "dsa_q_up")
    qi_raw = _matmul(cq, w_iq.astype(BF16), BF16, "dsa_idx_q_up")
    qa = _rowprep(qa_raw, 0, gw, batch, s_len, norm="rms", gain=gq_a, rope_dim=HEAD_DIM, out_scale=scale,
                  head_major=True, name="dsa_q_prep")
    qi = _rowprep(qi_raw, 0, DSA_IDX_HEADS * HEAD_DIM, batch, s_len, rope_dim=DSA_IDX_ROPE, head_major=True,
                  name="dsa_idx_q_prep")
    ka = _rowprep(z, c_ka, HEAD_DIM, batch, s_len, norm="rms", gain=gk_a, rope_dim=HEAD_DIM, name="dsa_k_prep")
    ki = _rowprep(z, c_ki, HEAD_DIM, batch, s_len, norm="ln", gain=g_kidx, beta=b_kidx, rope_dim=DSA_IDX_ROPE,
                  name="dsa_idx_k_prep")
    va = z[:, c_va:c_va + HEAD_DIM]
    wi = jnp.pad(zs[:, GROUP_HEADS:GROUP_HEADS + DSA_IDX_HEADS] * (DSA_IDX_HEADS ** -0.5 * HEAD_DIM ** -0.5),
                 ((0, 0), (0, LANES - DSA_IDX_HEADS)))
    o_a = _dsa_attention(qi, ki, wi, qa, ka, va, batch, s_len)
    o_a = o_a.transpose(0, 2, 1, 3).reshape(batch * s_len, gw)

    qb = _rowprep(z, c_qb, gw, batch, s_len, norm="rms", gain=gq_b, out_scale=scale, head_major=True,
                  name="fox_q_prep")
    kb = _rowprep(z, c_kb, gw, batch, s_len, norm="rms", gain=gk_b, head_major=True, name="fox_k_prep")
    bias_row = jnp.pad(b_f.astype(F32), (0, LANES - GROUP_HEADS)).reshape(1, LANES)
    cum = _fox_cumsum(zs, bias_row, batch, s_len)
    o_b = _fox_attention(qb, kb, z, c_vb, c_gb, cum, batch, s_len)
    o_cat = jnp.concatenate([o_a, o_b], axis=1)
    return _matmul(o_cat, w_out.astype(BF16), F32, "even_out_proj", residual=x2, gate=gate,
                   rows_per_batch=s_len)


def _odd_mixer(h, x2, gate, batch, s_len, w_in, gq_c, gk_c, gq_d, gk_d, sinks, w_out):
    gw = GROUP_WIDTH
    kvw = SWA_KV_HEADS * HEAD_DIM
    c_qc, c_kc, c_vc, c_qd, c_kd = 0, gw, 2 * gw, 3 * gw, 4 * gw
    c_vd = c_kd + kvw
    z = _matmul(h, w_in.astype(BF16), BF16, "odd_in_proj")
    scale = Q_SCALE
    qc = _rowprep(z, c_qc, gw, batch, s_len, norm="rms", gain=gq_c, rope_dim=HEAD_DIM, out_scale=scale,
                  head_major=True, name="moba_q_prep")
    kc, kmean = _rowprep(z, c_kc, gw, batch, s_len, norm="rms", gain=gk_c, rope_dim=HEAD_DIM, head_major=True,
                         want_mean=True, name="moba_k_prep")
    o_c = _moba_attention(qc, kc, kmean, z, c_vc, batch, s_len)
    qd = _rowprep(z, c_qd, gw, batch, s_len, norm="rms", gain=gq_d, rope_dim=HEAD_DIM, out_scale=scale,
                  head_major=True, name="swa_q_prep")
    kd = _rowprep(z, c_kd, kvw, batch, s_len, norm="rms", gain=gk_d, rope_dim=HEAD_DIM, head_major=True,
                  name="swa_k_prep")
    o_d = _swa_attention(qd, kd, z, c_vd, sinks, batch, s_len)
    o_cat = jnp.concatenate([o_c, o_d], axis=1)
    return _matmul(o_cat, w_out.astype(BF16), F32, "odd_out_proj", residual=x2, gate=gate, rows_per_batch=s_len)


def kernel(x, c, g_norm_mix, g_norm_ffn, w_ada, b_ada, w_in_even, g_cq, w_uq, w_iq, gq_a, gk_a, g_kidx, b_kidx,
           gq_b, gk_b, b_forget, w_out_even, w_in_odd, gq_c, gk_c, gq_d, gk_d, sinks_d, w_out_odd,
           w_router, b_router, w_exp_in, b_exp_in, w_exp_out, b_exp_out):
    batch, s_len, d = x.shape
    depth = w_ada.shape[0]
    mod = _ada_modulation(c, w_ada, b_ada)
    x2 = x.reshape(batch * s_len, d)
    for layer in range(depth):
        shift_m, scale_m, gate_m, shift_f, scale_f, gate_f = [mod[layer, :, i * d:(i + 1) * d] for i in range(6)]
        h = _norm_modulate(x2, g_norm_mix[layer], scale_m, shift_m, s_len)
        j = layer // 2
        if layer % 2 == 0:
            x2 = _even_mixer(h, x2, gate_m, batch, s_len, w_in_even[j], g_cq[j], w_uq[j], w_iq[j], gq_a[j],
                             gk_a[j], g_kidx[j], b_kidx[j], gq_b[j], gk_b[j], b_forget[j], w_out_even[j])
        else:
            x2 = _odd_mixer(h, x2, gate_m, batch, s_len, w_in_odd[j], gq_c[j], gk_c[j], gq_d[j], gk_d[j],
                            sinks_d[j], w_out_odd[j])
        x2 = _moe_block(x2, g_norm_ffn[layer], scale_f, shift_f, gate_f, w_router[layer], b_router[layer],
                        w_exp_in[layer], b_exp_in[layer], w_exp_out[layer], b_exp_out[layer], s_len)
    return x2.reshape(batch, s_len, d)
```

```python
import functools

import numpy as np
import jax
import jax.numpy as jnp
from jax import lax
from jax.experimental import pallas as pl
from jax.experimental.pallas import tpu as pltpu

F32 = jnp.float32
BF16 = jnp.bfloat16
I32 = jnp.int32

LANES = 128
HEAD_DIM = 128
GROUP_HEADS = 16
GROUP_WIDTH = GROUP_HEADS * HEAD_DIM
ROPE_THETA = 10000.0
EPS = 1e-6
DSA_Q_LORA = 1024
DSA_IDX_HEADS = 32
DSA_IDX_ROPE = 64
DSA_TOPK = 256
DSA_QBLK = 128
DSA_CHUNK = 256
DSA_STAGES = 4
DSA_HEAD_GROUP = 2
MOBA_BLOCK = 256
MOBA_TOPK = 3
SWA_KV_HEADS = 2
SWA_WINDOW = 128
N_EXPERTS = 32
MOE_TOPK = 4
EXPERT_FF = 512
SWIGLU_LIMIT = 7.0
SWIGLU_ALPHA = 1.702
MOE_ROW_TILE = 256
NEG_BIG = -1e30
LOG2_E = 1.4426950408889634
Q_SCALE = HEAD_DIM ** -0.5 * LOG2_E
VMEM_LIMIT = 56 * 1024 * 1024


def _cparams(semantics, vmem=VMEM_LIMIT, unchecked=False):
    return pltpu.CompilerParams(dimension_semantics=semantics, vmem_limit_bytes=vmem,
                                disable_bounds_checks=unchecked)


def _ada_kernel(cb_ref, w_ref, b_ref, o_ref):
    nb = cb_ref.shape[0]
    tn = w_ref.shape[2]
    for c in range(tn // LANES):
        w = w_ref[0, :, c * LANES:(c + 1) * LANES]
        for b in range(nb):
            o_ref[0, b:b + 1, c * LANES:(c + 1) * LANES] = (
                jnp.sum(w * cb_ref[b], axis=0, keepdims=True) + b_ref[0, :, c * LANES:(c + 1) * LANES])


def _ada_modulation(c, w_ada, b_ada, tn=512):
    depth, d, n6 = w_ada.shape
    nb = c.shape[0]
    cb = jnp.broadcast_to(jax.nn.silu(c)[:, :, None], (nb, d, LANES))
    return pl.pallas_call(
        _ada_kernel,
        grid=(depth, n6 // tn),
        in_specs=[pl.BlockSpec((nb, d, LANES), lambda l, j: (0, 0, 0)),
                  pl.BlockSpec((1, d, tn), lambda l, j: (l, 0, j)),
                  pl.BlockSpec((1, 1, tn), lambda l, j: (l, 0, j))],
        out_specs=pl.BlockSpec((1, nb, tn), lambda l, j: (l, 0, j)),
        out_shape=jax.ShapeDtypeStruct((depth, nb, n6), F32),
        compiler_params=_cparams(("parallel", "parallel")),
        name="ada_modulation",
    )(cb, w_ada, b_ada.reshape(depth, 1, n6))


def _normmod_kernel(x_ref, g_ref, sc_ref, sh_ref, o_ref):
    x = x_ref[...]
    y = x * lax.rsqrt(jnp.mean(x * x, axis=-1, keepdims=True) + EPS) * g_ref[...]
    o_ref[...] = (y * (1.0 + sc_ref[0]) + sh_ref[0]).astype(o_ref.dtype)


def _norm_modulate(x2, g, scale, shift, rows_per_batch, tm=256):
    n, d = x2.shape
    tpb = rows_per_batch // tm
    return pl.pallas_call(
        _normmod_kernel,
        grid=(n // tm,),
        in_specs=[pl.BlockSpec((tm, d), lambda i: (i, 0)),
                  pl.BlockSpec((1, d), lambda i: (0, 0)),
                  pl.BlockSpec((1, 1, d), lambda i: (i // tpb, 0, 0)),
                  pl.BlockSpec((1, 1, d), lambda i: (i // tpb, 0, 0))],
        out_specs=pl.BlockSpec((tm, d), lambda i: (i, 0)),
        out_shape=jax.ShapeDtypeStruct((n, d), BF16),
        compiler_params=_cparams(("parallel",)),
        name="norm_modulate",
    )(x2, g.reshape(1, d), scale[:, None, :], shift[:, None, :])


def _mm_kernel(a_ref, b_ref, o_ref):
    o_ref[...] = jnp.dot(a_ref[...], b_ref[...], preferred_element_type=F32).astype(o_ref.dtype)


def _mm_residual_kernel(a_ref, b_ref, x_ref, g_ref, o_ref):
    o_ref[...] = x_ref[...] + g_ref[0] * jnp.dot(a_ref[...], b_ref[...], preferred_element_type=F32)


def _pick(n, prefs):
    for t in prefs:
        if n % t == 0:
            return t
    return n


def _matmul(a, b, out_dtype, name, residual=None, gate=None, rows_per_batch=None):
    m, k = a.shape
    n = b.shape[1]
    tm = _pick(m, (1024, 512, 256, 128))
    tn = _pick(n, (512, 640, 384, 256, 128) if k > 2048 else (1024, 512, 640, 384, 256, 128))
    grid = (m // tm, n // tn)
    a_spec = pl.BlockSpec((tm, k), lambda i, j: (i, 0))
    b_spec = pl.BlockSpec((k, tn), lambda i, j: (0, j))
    o_spec = pl.BlockSpec((tm, tn), lambda i, j: (i, j))
    params = _cparams(("parallel", "arbitrary"))
    if residual is None:
        return pl.pallas_call(
            _mm_kernel, grid=grid, in_specs=[a_spec, b_spec], out_specs=o_spec,
            out_shape=jax.ShapeDtypeStruct((m, n), out_dtype), compiler_params=params, name=name)(a, b)
    tpb = rows_per_batch // tm
    return pl.pallas_call(
        _mm_residual_kernel, grid=grid,
        in_specs=[a_spec, b_spec, o_spec, pl.BlockSpec((1, 1, tn), lambda i, j: (i // tpb, 0, j))],
        out_specs=o_spec, out_shape=jax.ShapeDtypeStruct((m, n), F32),
        compiler_params=params, name=name)(a, b, residual, gate[:, None, :])


def _rope_tables(s_len, rot_dim):
    half = rot_dim // 2
    inv_freq = ROPE_THETA ** (-jnp.arange(half, dtype=F32) / half)
    ang = jnp.arange(s_len, dtype=F32)[:, None] * inv_freq[None, :]
    cos, sin = jnp.cos(ang), jnp.sin(ang)
    pad = HEAD_DIM - rot_dim
    ones = jnp.ones((s_len, pad), F32)
    zeros = jnp.zeros((s_len, pad), F32)
    zh = jnp.zeros((s_len, half), F32)
    cos_t = jnp.concatenate([cos, cos, ones], axis=1)
    sa = jnp.concatenate([-sin, zh, zeros], axis=1)
    sb = jnp.concatenate([zh, sin, zeros], axis=1)
    return cos_t, sa, sb


def _rowprep_kernel(*refs, hw, norm, rope_half, out_scale, head_major, want_mean):
    it = iter(refs)
    z_ref = next(it)
    g_ref = next(it) if norm != "none" else None
    beta_ref = next(it) if norm == "ln" else None
    if rope_half:
        cos_ref, sa_ref, sb_ref = next(it), next(it), next(it)
    o_ref = next(it)
    mean_ref = next(it) if want_mean else None
    width = z_ref.shape[1]
    for h in range(width // hw):
        x = z_ref[:, h * hw:(h + 1) * hw].astype(F32)
        if norm == "rms":
            x = x * lax.rsqrt(jnp.mean(x * x, axis=-1, keepdims=True) + EPS) * g_ref[...]
        elif norm == "ln":
            mu = jnp.mean(x, axis=-1, keepdims=True)
            xc = x - mu
            x = xc * lax.rsqrt(jnp.mean(xc * xc, axis=-1, keepdims=True) + EPS) * g_ref[...] + beta_ref[...]
        if rope_half:
            if rope_half * 2 == hw:
                x = x * cos_ref[...] + pltpu.roll(x, rope_half, 1) * (sa_ref[...] + sb_ref[...])
            else:
                x = (x * cos_ref[...] + pltpu.roll(x, hw - rope_half, 1) * sa_ref[...]
                     + pltpu.roll(x, rope_half, 1) * sb_ref[...])
        if want_mean:
            mean_ref[0, 0, h:h + 1, :] = jnp.mean(x, axis=0, keepdims=True)
        if out_scale != 1.0:
            x = x * out_scale
        if head_major:
            o_ref[0, h] = x.astype(o_ref.dtype)
        else:
            o_ref[:, h * hw:(h + 1) * hw] = x.astype(o_ref.dtype)


def _rowprep(z, col0, width, batch, s_len, *, hw=HEAD_DIM, norm="none", gain=None, beta=None, rope_dim=0,
             out_scale=1.0, head_major=False, want_mean=False, name="rowprep", tm=256):
    n = z.shape[0]
    tpb = s_len // tm
    assert col0 % width == 0 and n == batch * s_len
    cb = col0 // width
    args = [z]
    specs = [pl.BlockSpec((tm, width), lambda i: (i, cb))]
    if norm != "none":
        args.append(gain.reshape(1, hw).astype(F32))
        specs.append(pl.BlockSpec((1, hw), lambda i: (0, 0)))
    if norm == "ln":
        args.append(beta.reshape(1, hw).astype(F32))
        specs.append(pl.BlockSpec((1, hw), lambda i: (0, 0)))
    if rope_dim:
        args += list(_rope_tables(s_len, rope_dim))
        specs += [pl.BlockSpec((tm, HEAD_DIM), lambda i: (i % tpb, 0))] * 3
    nh = width // hw
    if head_major:
        out_shape = [jax.ShapeDtypeStruct((batch, nh, s_len, hw), BF16)]
        out_specs = [pl.BlockSpec((1, nh, tm, hw), lambda i: (i // tpb, 0, i % tpb, 0))]
    else:
        out_shape = [jax.ShapeDtypeStruct((n, width), BF16)]
        out_specs = [pl.BlockSpec((tm, width), lambda i: (i, 0))]
    if want_mean:
        assert tm == MOBA_BLOCK
        out_shape.append(jax.ShapeDtypeStruct((batch, tpb, nh, hw), F32))
        out_specs.append(pl.BlockSpec((1, 1, nh, hw), lambda i: (i // tpb, i % tpb, 0, 0)))
    kern = functools.partial(_rowprep_kernel, hw=hw, norm=norm, rope_half=rope_dim // 2, out_scale=out_scale,
                             head_major=head_major, want_mean=want_mean)
    res = pl.pallas_call(kern, grid=(n // tm,), in_specs=specs, out_specs=out_specs, out_shape=out_shape,
                         compiler_params=_cparams(("parallel",)), name=name)(*args)
    return res if want_mean else res[0]


def _fox_cum_kernel(z_ref, bf_ref, tri_ref, o_ref, carry_ref):
    @pl.when(pl.program_id(1) == 0)
    def _():
        carry_ref[...] = jnp.zeros_like(carry_ref)

    xv = z_ref[...] + bf_ref[...]
    lf = jnp.minimum(xv, 0.0) - jnp.log(1.0 + jnp.exp(-jnp.abs(xv)))
    hi = lf.astype(BF16)
    r1 = lf - hi.astype(F32)
    mid = r1.astype(BF16)
    lo = (r1 - mid.astype(F32)).astype(BF16)
    tri = tri_ref[...]
    cs = (jnp.dot(tri, hi, preferred_element_type=F32) + jnp.dot(tri, mid, preferred_element_type=F32)
          + jnp.dot(tri, lo, preferred_element_type=F32)) + carry_ref[...]
    o_ref[...] = cs * LOG2_E
    tm = cs.shape[0]
    carry_ref[...] = cs[tm - 1:tm, :]


def _fox_cumsum(zs, bias_row, batch, s_len, tm=256):
    n = zs.shape[0]
    tpb = s_len // tm
    tri = jnp.tril(jnp.ones((tm, tm), F32)).astype(BF16)
    return pl.pallas_call(
        _fox_cum_kernel,
        grid=(batch, tpb),
        in_specs=[pl.BlockSpec((tm, LANES), lambda b, i: (b * tpb + i, 0)),
                  pl.BlockSpec((1, LANES), lambda b, i: (0, 0)),
                  pl.BlockSpec((tm, tm), lambda b, i: (0, 0))],
        out_specs=pl.BlockSpec((tm, LANES), lambda b, i: (b * tpb + i, 0)),
        out_shape=jax.ShapeDtypeStruct((n, LANES), F32),
        scratch_shapes=[pltpu.VMEM((1, LANES), F32)],
        compiler_params=_cparams(("parallel", "arbitrary")),
        name="fox_cumsum",
    )(zs, bias_row, tri)


def _dsa_kernel(qi_ref, kit_ref, wi_ref, qa_ref, kat_ref, va_ref, o_ref, wb_ref, key_ref, mb_ref,
                *, sk, q0, topk):
    q = DSA_QBLK
    ch = DSA_CHUNK
    nch = sk // ch
    n_idx = qi_ref.shape[1]
    n_heads = qa_ref.shape[1]
    t0 = (q0 + pl.program_id(1)) * q
    row = t0 + lax.broadcasted_iota(I32, (q, ch), 0)
    lane = lax.broadcasted_iota(I32, (q, ch), 1)

    wi = wi_ref[...]
    for h in range(n_idx):
        wb_ref[h] = jnp.broadcast_to(wi[:, h:h + 1], (q, ch))
    qi = qi_ref[0].reshape(n_idx * q, HEAD_DIM)

    def idx_chunk(c, carry):
        d = jnp.dot(qi, kit_ref[0, c], preferred_element_type=F32)
        acc = jnp.zeros((q, ch), F32)
        for h in range(n_idx):
            acc = acc + jnp.maximum(d[h * q:(h + 1) * q], 0.0) * wb_ref[h]
        score = jnp.where(c * ch + lane <= row, acc + 0.0, -jnp.inf)
        bits = lax.bitcast_convert_type(score, I32)
        key_ref[c] = jnp.where(bits >= 0, bits, bits ^ jnp.int32(0x7FFFFFFF))
        return carry

    lax.fori_loop(0, nch, idx_chunk, 0)

    def count_ge(cand):
        def body(c, acc):
            return acc + jnp.where(key_ref[c] >= cand, 1.0, 0.0)
        acc = lax.fori_loop(0, nch, body, jnp.zeros((q, ch), F32))
        return jnp.sum(acc, axis=-1, keepdims=True)

    int_min = jnp.int32(-2 ** 31)
    thr = jnp.where(count_ge(jnp.zeros((q, 1), I32)) >= topk, jnp.int32(0), int_min)

    def bit_step(j, thr):
        cand = thr | jnp.left_shift(jnp.int32(1), 30 - j)
        return jnp.where(count_ge(cand) >= topk, cand, thr)

    thr = lax.fori_loop(0, 31, bit_step, thr)

    for c in range(nch):
        keep = (key_ref[c] >= thr) & (c * ch + lane <= row)
        mb_ref[:, c * ch:(c + 1) * ch] = jnp.where(keep, 0.0, -jnp.inf)

    g = DSA_HEAD_GROUP

    def head_group(hg, carry):
        qh = qa_ref[0, pl.ds(hg * g, g)].reshape(g * q, HEAD_DIM)
        lg = jnp.dot(qh, kat_ref[0], preferred_element_type=F32).reshape(g, q, sk) + mb_ref[...][None]
        m = jnp.max(lg, axis=-1, keepdims=True)
        p = jnp.exp2(lg - m)
        l = jnp.sum(p, axis=-1, keepdims=True)
        o = jnp.dot(p.reshape(g * q, sk).astype(BF16), va_ref[0], preferred_element_type=F32)
        o_ref[0, pl.ds(hg * g, g)] = (o.reshape(g, q, HEAD_DIM) / l).astype(o_ref.dtype)
        return carry

    lax.fori_loop(0, n_heads // g, head_group, 0)


def _dsa_attention(qi, ki, wi, qa, ka, va, batch, s_len):
    q, ch = DSA_QBLK, DSA_CHUNK
    n_idx, n_heads = qi.shape[1], qa.shape[1]
    topk = min(DSA_TOPK, s_len // 4)
    nqb = s_len // q
    kit = ki.reshape(batch, s_len // ch, ch, HEAD_DIM).transpose(0, 1, 3, 2)
    kat = ka.reshape(batch, s_len, HEAD_DIM).transpose(0, 2, 1)
    va3 = va.reshape(batch, s_len, HEAD_DIM)
    per_stage = max(nqb // DSA_STAGES, ch // q)
    outs = []
    for q0 in range(0, nqb, per_stage):
        sk = (q0 + per_stage) * q
        nch = sk // ch
        kern = functools.partial(_dsa_kernel, sk=sk, q0=q0, topk=topk)
        outs.append(pl.pallas_call(
            kern,
            grid=(batch, per_stage),
            in_specs=[pl.BlockSpec((1, n_idx, q, HEAD_DIM), lambda b, i, q0=q0: (b, 0, q0 + i, 0)),
                      pl.BlockSpec((1, nch, HEAD_DIM, ch), lambda b, i: (b, 0, 0, 0)),
                      pl.BlockSpec((q, LANES), lambda b, i, q0=q0: (b * nqb + q0 + i, 0)),
                      pl.BlockSpec((1, n_heads, q, HEAD_DIM), lambda b, i, q0=q0: (b, 0, q0 + i, 0)),
                      pl.BlockSpec((1, HEAD_DIM, sk), lambda b, i: (b, 0, 0)),
                      pl.BlockSpec((1, sk, HEAD_DIM), lambda b, i: (b, 0, 0))],
            out_specs=pl.BlockSpec((1, n_heads, q, HEAD_DIM), lambda b, i: (b, 0, i, 0)),
            out_shape=jax.ShapeDtypeStruct((batch, n_heads, per_stage * q, HEAD_DIM), BF16),
            scratch_shapes=[pltpu.VMEM((n_idx, q, ch), F32),
                            pltpu.VMEM((nch, q, ch), I32),
                            pltpu.VMEM((q, sk), F32)],
            compiler_params=_cparams(("parallel", "parallel")),
            name=f"dsa_attention_k{sk}",
        )(qi, kit, wi, qa, kat, va3))
    return jnp.concatenate(outs, axis=2)


def _tri_tables(nq, ratio):
    qt, kt = [], []
    for i in range(nq):
        for j in range((i + 1) * ratio):
            qt.append(i)
            kt.append(j)
    return jnp.asarray(qt, I32), jnp.asarray(kt, I32)


def _fox_kernel(qt_ref, kt_ref, q_ref, k_ref, v_ref, cq_ref, ck_ref, g_ref, o_ref,
                m_ref, l_ref, acc_ref, cqs_ref, *, tq):
    h = pl.program_id(1)
    step = pl.program_id(2)
    qi = qt_ref[step]
    kj = kt_ref[step]

    @pl.when(kj == 0)
    def _():
        m_ref[...] = jnp.full_like(m_ref, -jnp.inf)
        l_ref[...] = jnp.zeros_like(l_ref)
        acc_ref[...] = jnp.zeros_like(acc_ref)
        lane = lax.broadcasted_iota(I32, cq_ref.shape, 1)
        cqs_ref[...] = jnp.sum(jnp.where(lane == h, cq_ref[...], 0.0), axis=-1, keepdims=True)

    s = lax.dot_general(q_ref[0, 0], k_ref[0, 0], (((1,), (1,)), ((), ())), preferred_element_type=F32)
    s = s + (cqs_ref[...] - ck_ref[0, 0])

    def update(s):
        m_prev = m_ref[...]
        m_new = jnp.maximum(m_prev, jnp.max(s, axis=-1, keepdims=True))
        alpha = jnp.exp2(m_prev - m_new)
        p = jnp.exp2(s - m_new)
        l_ref[...] = alpha * l_ref[...] + jnp.sum(p, axis=-1, keepdims=True)
        acc_ref[...] = alpha * acc_ref[...] + jnp.dot(p.astype(BF16), v_ref[...], preferred_element_type=F32)
        m_ref[...] = m_new

    @pl.when(kj < qi)
    def _():
        update(s)

    @pl.when(kj == qi)
    def _():
        row = lax.broadcasted_iota(I32, s.shape, 0)
        col = lax.broadcasted_iota(I32, s.shape, 1)
        update(jnp.where(col <= row, s, -jnp.inf))
        gate = g_ref[...].astype(F32)
        o_ref[...] = (acc_ref[...] / l_ref[...] * (1.0 / (1.0 + jnp.exp(-gate)))).astype(o_ref.dtype)


def _fox_attention(qb, kb, z, v_col0, g_col0, cum, batch, s_len, tq=1024):
    tq = min(tq, s_len)
    nq = s_len // tq
    n_heads = qb.shape[1]
    qt, kt = _tri_tables(nq, 1)
    cum_t = cum.reshape(batch, s_len, LANES)[:, :, :n_heads].transpose(0, 2, 1)[:, :, None, :]
    vb0, gb0 = v_col0 // HEAD_DIM, g_col0 // HEAD_DIM
    grid_spec = pltpu.PrefetchScalarGridSpec(
        num_scalar_prefetch=2,
        grid=(batch, n_heads, int(qt.shape[0])),
        in_specs=[pl.BlockSpec((1, 1, tq, HEAD_DIM), lambda b, h, s, qt, kt: (b, h, qt[s], 0)),
                  pl.BlockSpec((1, 1, tq, HEAD_DIM), lambda b, h, s, qt, kt: (b, h, kt[s], 0)),
                  pl.BlockSpec((tq, HEAD_DIM), lambda b, h, s, qt, kt: (b * nq + kt[s], vb0 + h)),
                  pl.BlockSpec((tq, LANES), lambda b, h, s, qt, kt: (b * nq + qt[s], 0)),
                  pl.BlockSpec((1, 1, 1, tq), lambda b, h, s, qt, kt: (b, h, 0, kt[s])),
                  pl.BlockSpec((tq, HEAD_DIM), lambda b, h, s, qt, kt: (b * nq + qt[s], gb0 + h))],
        out_specs=pl.BlockSpec((tq, HEAD_DIM), lambda b, h, s, qt, kt: (b * nq + qt[s], h)),
        scratch_shapes=[pltpu.VMEM((tq, 1), F32), pltpu.VMEM((tq, 1), F32),
                        pltpu.VMEM((tq, HEAD_DIM), F32), pltpu.VMEM((tq, 1), F32)])
    return pl.pallas_call(
        functools.partial(_fox_kernel, tq=tq), grid_spec=grid_spec,
        out_shape=jax.ShapeDtypeStruct((batch * s_len, n_heads * HEAD_DIM), BF16),
        compiler_params=_cparams(("parallel", "parallel", "arbitrary")),
        name="fox_attention",
    )(qt, kt, qb, kb, z, cum, cum_t, z)


def _moba_kernel(qt_ref, kt_ref, q_ref, k_ref, e_ref, v_ref, km_ref, o_ref, m_ref, l_ref, acc_ref, qa_ref,
                 *, tq, n_sel):
    step = pl.program_id(2)
    qi = qt_ref[step]
    kj = kt_ref[step]

    @pl.when(kj == 0)
    def _():
        m_ref[...] = jnp.full_like(m_ref, NEG_BIG)
        l_ref[...] = jnp.zeros_like(l_ref)
        acc_ref[...] = jnp.zeros_like(acc_ref)
        own = (qi * tq + lax.broadcasted_iota(I32, (tq, 1), 0)) // MOBA_BLOCK
        gate = jnp.dot(q_ref[0, 0].astype(F32), km_ref[0, 0], preferred_element_type=F32,
                       precision=lax.Precision.HIGHEST)
        lane = lax.broadcasted_iota(I32, gate.shape, 1)
        lane_f = lane.astype(F32)
        gate = jnp.where(lane < own, gate, -jnp.inf)
        allowed = jnp.where(lane == own, 1.0, 0.0)
        for _ in range(n_sel):
            best = jnp.max(gate, axis=-1, keepdims=True)
            first = jnp.min(jnp.where(gate == best, lane_f, float(LANES)), axis=-1, keepdims=True)
            pick = (lane_f == first) & (best > -jnp.inf)
            allowed = jnp.where(pick, 1.0, allowed)
            gate = jnp.where(pick, -jnp.inf, gate)
        qa_ref[:, :HEAD_DIM] = q_ref[0, 0]
        qa_ref[:, HEAD_DIM:] = jnp.where(allowed > 0.0, 0.0, NEG_BIG).astype(qa_ref.dtype)

    k_aug = jnp.concatenate([k_ref[0, 0], e_ref[...]], axis=1)
    s = lax.dot_general(qa_ref[...], k_aug, (((1,), (1,)), ((), ())), preferred_element_type=F32)

    def update(s):
        m_prev = m_ref[...]
        m_new = jnp.maximum(m_prev, jnp.max(s, axis=-1, keepdims=True))
        alpha = jnp.exp2(m_prev - m_new)
        p = jnp.exp2(s - m_new)
        l_ref[...] = alpha * l_ref[...] + jnp.sum(p, axis=-1, keepdims=True)
        acc_ref[...] = alpha * acc_ref[...] + jnp.dot(p.astype(BF16), v_ref[...], preferred_element_type=F32)
        m_ref[...] = m_new

    @pl.when(kj < qi)
    def _():
        update(s)

    @pl.when(kj == qi)
    def _():
        row = lax.broadcasted_iota(I32, s.shape, 0)
        col = lax.broadcasted_iota(I32, s.shape, 1)
        update(jnp.where(col <= row, s, NEG_BIG))
        o_ref[...] = (acc_ref[...] / l_ref[...]).astype(o_ref.dtype)


def _moba_attention(qc, kc, kmean, z, v_col0, batch, s_len, tq=1024):
    tq = min(tq, s_len)
    nq = s_len // tq
    n_heads = qc.shape[1]
    nb = s_len // MOBA_BLOCK
    assert nb <= LANES
    n_sel = min(MOBA_TOPK, nb - 1)
    qt, kt = _tri_tables(nq, 1)
    km_t = jnp.pad(kmean.transpose(0, 2, 3, 1), ((0, 0), (0, 0), (0, 0), (0, LANES - nb)))
    block_onehot = (jnp.arange(s_len, dtype=I32)[:, None] // MOBA_BLOCK
                    == jnp.arange(LANES, dtype=I32)[None, :]).astype(BF16)
    vb0 = v_col0 // HEAD_DIM
    grid_spec = pltpu.PrefetchScalarGridSpec(
        num_scalar_prefetch=2,
        grid=(batch, n_heads, int(qt.shape[0])),
        in_specs=[pl.BlockSpec((1, 1, tq, HEAD_DIM), lambda b, h, s, qt, kt: (b, h, qt[s], 0)),
                  pl.BlockSpec((1, 1, tq, HEAD_DIM), lambda b, h, s, qt, kt: (b, h, kt[s], 0)),
                  pl.BlockSpec((tq, LANES), lambda b, h, s, qt, kt: (kt[s], 0)),
                  pl.BlockSpec((tq, HEAD_DIM), lambda b, h, s, qt, kt: (b * nq + kt[s], vb0 + h)),
                  pl.BlockSpec((1, 1, HEAD_DIM, LANES), lambda b, h, s, qt, kt: (b, h, 0, 0))],
        out_specs=pl.BlockSpec((tq, HEAD_DIM), lambda b, h, s, qt, kt: (b * nq + qt[s], h)),
        scratch_shapes=[pltpu.VMEM((tq, 1), F32), pltpu.VMEM((tq, 1), F32),
                        pltpu.VMEM((tq, HEAD_DIM), F32), pltpu.VMEM((tq, HEAD_DIM + LANES), BF16)])
    return pl.pallas_call(
        functools.partial(_moba_kernel, tq=tq, n_sel=n_sel), grid_spec=grid_spec,
        out_shape=jax.ShapeDtypeStruct((batch * s_len, n_heads * HEAD_DIM), BF16),
        compiler_params=_cparams(("parallel", "parallel", "arbitrary")),
        name="moba_attention",
    )(qt, kt, qc, kc, block_onehot, z, km_t)


def _swa_kernel(q_ref, kp_ref, kc_ref, vp_ref, vc_ref, sink_ref, o_ref):
    n = pl.program_id(1)
    hq, w = q_ref.shape[1], q_ref.shape[2]
    hkv = kc_ref.shape[1]
    grp = hq // hkv
    ti = lax.broadcasted_iota(I32, (w, w), 0)
    si = lax.broadcasted_iota(I32, (w, w), 1)
    cur_ok = (si <= ti)[None]
    prev_ok = ((si > ti) & (n > 0))[None]
    nt = (((1,), (1,)), ((), ()))
    for kv in range(hkv):
        q = q_ref[0, kv * grp:(kv + 1) * grp].reshape(grp * w, HEAD_DIM)
        sc = lax.dot_general(q, kc_ref[0, kv], nt, preferred_element_type=F32).reshape(grp, w, w)
        sp = lax.dot_general(q, kp_ref[0, kv], nt, preferred_element_type=F32).reshape(grp, w, w)
        sc = jnp.where(cur_ok, sc, -jnp.inf)
        sp = jnp.where(prev_ok, sp, -jnp.inf)
        sink = sink_ref[kv * grp * w:(kv + 1) * grp * w].reshape(grp, w, LANES)[:, :, :1]
        m = jnp.maximum(jnp.maximum(jnp.max(sc, axis=-1, keepdims=True), jnp.max(sp, axis=-1, keepdims=True)), sink)
        pc = jnp.exp2(sc - m)
        pp = jnp.exp2(sp - m)
        den = jnp.sum(pc, axis=-1, keepdims=True) + jnp.sum(pp, axis=-1, keepdims=True) + jnp.exp2(sink - m)
        o = (jnp.dot(pc.reshape(grp * w, w).astype(BF16), vc_ref[:, kv * HEAD_DIM:(kv + 1) * HEAD_DIM],
                     preferred_element_type=F32)
             + jnp.dot(pp.reshape(grp * w, w).astype(BF16), vp_ref[:, kv * HEAD_DIM:(kv + 1) * HEAD_DIM],
                       preferred_element_type=F32))
        o = o.reshape(grp, w, HEAD_DIM) / den
        for gh in range(grp):
            hh = kv * grp + gh
            o_ref[:, hh * HEAD_DIM:(hh + 1) * HEAD_DIM] = o[gh].astype(o_ref.dtype)


def _swa_attention(qd, kd, z, v_col0, sinks, batch, s_len):
    w = SWA_WINDOW
    nb = s_len // w
    hq, hkv = qd.shape[1], kd.shape[1]
    vw = hkv * HEAD_DIM
    vb0 = v_col0 // vw
    sink_b = jnp.broadcast_to((sinks.astype(F32) * LOG2_E)[:, None, None], (hq, w, LANES)).reshape(hq * w, LANES)
    return pl.pallas_call(
        _swa_kernel,
        grid=(batch, nb),
        in_specs=[pl.BlockSpec((1, hq, w, HEAD_DIM), lambda b, n: (b, 0, n, 0)),
                  pl.BlockSpec((1, hkv, w, HEAD_DIM), lambda b, n: (b, 0, jnp.maximum(n - 1, 0), 0)),
                  pl.BlockSpec((1, hkv, w, HEAD_DIM), lambda b, n: (b, 0, n, 0)),
                  pl.BlockSpec((w, vw), lambda b, n: (b * nb + jnp.maximum(n - 1, 0), vb0)),
                  pl.BlockSpec((w, vw), lambda b, n: (b * nb + n, vb0)),
                  pl.BlockSpec((hq * w, LANES), lambda b, n: (0, 0))],
        out_specs=pl.BlockSpec((w, hq * HEAD_DIM), lambda b, n: (b * nb + n, 0)),
        out_shape=jax.ShapeDtypeStruct((batch * s_len, hq * HEAD_DIM), BF16),
        compiler_params=_cparams(("parallel", "parallel")),
        name="swa_attention",
    )(qd, kd, kd, z, z, sink_b)


def _route_kernel(x_ref, g_ref, sc_ref, sh_ref, wr_ref, br_ref, tri_ref,
                  h_ref, idx_ref, wgt_ref, rank_ref, cnt_ref, carry_ref):
    @pl.when(pl.program_id(0) == 0)
    def _():
        carry_ref[...] = jnp.zeros_like(carry_ref)

    x = x_ref[...]
    y = x * lax.rsqrt(jnp.mean(x * x, axis=-1, keepdims=True) + EPS) * g_ref[...]
    hmod = y * (1.0 + sc_ref[0]) + sh_ref[0]
    h_ref[...] = hmod
    logits = jnp.dot(hmod, wr_ref[...], preferred_element_type=F32, precision=lax.Precision.HIGHEST) + br_ref[...]
    lane = lax.broadcasted_iota(I32, logits.shape, 1)
    lane_f = lane.astype(F32)
    logits = jnp.where(lane < N_EXPERTS, logits, -jnp.inf)
    onehots, vals, firsts = [], [], []
    for _ in range(MOE_TOPK):
        best = jnp.max(logits, axis=-1, keepdims=True)
        first = jnp.min(jnp.where(logits == best, lane_f, float(LANES)), axis=-1, keepdims=True)
        pick = lane_f == first
        onehots.append(pick)
        vals.append(best)
        firsts.append(first.astype(I32))
        logits = jnp.where(pick, -jnp.inf, logits)
    exps = [jnp.exp(v - vals[0]) for v in vals]
    den = exps[0]
    for e in exps[1:]:
        den = den + e
    chosen_f = jnp.zeros(logits.shape, F32)
    for o in onehots:
        chosen_f = jnp.where(o, 1.0, chosen_f)
    before = jnp.dot(tri_ref[...], chosen_f.astype(BF16), preferred_element_type=F32) + carry_ref[...]
    idx_out = jnp.zeros(logits.shape, I32)
    wgt_out = jnp.zeros(logits.shape, F32)
    rank_out = jnp.zeros(logits.shape, I32)
    for k in range(MOE_TOPK):
        rk = jnp.sum(jnp.where(onehots[k], before, 0.0), axis=-1, keepdims=True).astype(I32)
        idx_out = jnp.where(lane == k, firsts[k], idx_out)
        wgt_out = jnp.where(lane == k, exps[k] / den, wgt_out)
        rank_out = jnp.where(lane == k, rk, rank_out)
    idx_ref[...] = idx_out
    wgt_ref[...] = wgt_out
    rank_ref[...] = rank_out
    carry_ref[...] = carry_ref[...] + jnp.sum(chosen_f, axis=0, keepdims=True)
    cnt_ref[...] = carry_ref[...]


def _moe_route(x2, g, scale, shift, w_router, b_router, rows_per_batch, tm=256):
    n, d = x2.shape
    tpb = rows_per_batch // tm
    wr = jnp.pad(w_router, ((0, 0), (0, LANES - N_EXPERTS)))
    br = jnp.pad(b_router, (0, LANES - N_EXPERTS)).reshape(1, LANES)
    tri = jnp.tril(jnp.ones((tm, tm), F32), -1).astype(BF16)
    tok_spec = pl.BlockSpec((tm, LANES), lambda i: (i, 0))
    return pl.pallas_call(
        _route_kernel,
        grid=(n // tm,),
        in_specs=[pl.BlockSpec((tm, d), lambda i: (i, 0)),
                  pl.BlockSpec((1, d), lambda i: (0, 0)),
                  pl.BlockSpec((1, 1, d), lambda i: (i // tpb, 0, 0)),
                  pl.BlockSpec((1, 1, d), lambda i: (i // tpb, 0, 0)),
                  pl.BlockSpec((d, LANES), lambda i: (0, 0)),
                  pl.BlockSpec((1, LANES), lambda i: (0, 0)),
                  pl.BlockSpec((tm, tm), lambda i: (0, 0))],
        out_specs=[pl.BlockSpec((tm, d), lambda i: (i, 0)), tok_spec, tok_spec, tok_spec,
                   pl.BlockSpec((1, LANES), lambda i: (0, 0))],
        out_shape=[jax.ShapeDtypeStruct((n, d), F32), jax.ShapeDtypeStruct((n, LANES), I32),
                   jax.ShapeDtypeStruct((n, LANES), F32), jax.ShapeDtypeStruct((n, LANES), I32),
                   jax.ShapeDtypeStruct((1, LANES), F32)],
        scratch_shapes=[pltpu.VMEM((1, LANES), F32)],
        compiler_params=_cparams(("arbitrary",)),
        name="moe_route",
    )(x2, g.reshape(1, d), scale[:, None, :], shift[:, None, :], wr, br, tri)


def _invert_kernel(lo_ref, hi_ref, pos_ref, src_ref, *, chunk):
    step = pl.program_id(0)

    @pl.when(step == 0)
    def _():
        def fill_group(g, carry):
            def zero(p, c):
                src_ref[p] = 0
                return c
            lax.fori_loop(lo_ref[g], hi_ref[g], zero, 0)
            return carry
        lax.fori_loop(0, lo_ref.shape[0], fill_group, 0)

    def put(j, carry):
        src_ref[pos_ref[j]] = (step * chunk + j) // MOE_TOPK
        return carry

    lax.fori_loop(0, chunk, put, 0, unroll=8)


def _moe_invert(pos, pad_lo, pad_hi, n_rows, chunk=2048):
    n_assign = pos.shape[0]
    return pl.pallas_call(
        functools.partial(_invert_kernel, chunk=chunk),
        grid_spec=pltpu.PrefetchScalarGridSpec(
            num_scalar_prefetch=2, grid=(n_assign // chunk,),
            in_specs=[pl.BlockSpec((chunk,), lambda i, lo, hi: (i,), memory_space=pltpu.SMEM)],
            out_specs=pl.BlockSpec((n_rows,), lambda i, lo, hi: (0,), memory_space=pltpu.SMEM)),
        out_shape=jax.ShapeDtypeStruct((n_rows,), I32),
        compiler_params=_cparams(("arbitrary",), unchecked=True),
        name="moe_invert",
    )(pad_lo, pad_hi, pos)


def _expert_kernel(te_ref, nt_ref, src_ref, h_ref, wi_ref, bi_ref, perm_ref, wo_ref, bo_ref, y_ref, xbuf_ref, sem):
    i = pl.program_id(0)
    n_used = nt_ref[0]
    tm = xbuf_ref.shape[1]

    def gather(tile, slot):
        def issue(t, carry):
            pltpu.make_async_copy(h_ref.at[pl.ds(src_ref[tile * tm + t], 1)],
                                  xbuf_ref.at[slot, pl.ds(t, 1)], sem.at[slot]).start()
            return carry
        lax.fori_loop(0, tm, issue, 0, unroll=8)

    @pl.when(i == 0)
    def _():
        gather(0, 0)

    @pl.when(i + 1 < n_used)
    def _():
        gather(i + 1, (i + 1) % 2)

    @pl.when(i < n_used)
    def _():
        slot = i % 2
        pltpu.make_async_copy(h_ref.at[pl.ds(0, tm)], xbuf_ref.at[slot], sem.at[slot]).wait()
        x = xbuf_ref[slot].astype(BF16)
        hh = (jnp.dot(x, wi_ref[0], preferred_element_type=F32) + bi_ref[0]).astype(BF16)
        hp = jnp.dot(hh, perm_ref[...], preferred_element_type=F32)
        x_glu = jnp.minimum(hp[:, :EXPERT_FF], SWIGLU_LIMIT)
        x_lin = jnp.clip(hp[:, EXPERT_FF:], -SWIGLU_LIMIT, SWIGLU_LIMIT)
        act = x_glu * (1.0 / (1.0 + jnp.exp(-SWIGLU_ALPHA * x_glu))) * (x_lin + 1.0)
        y_ref[...] = jnp.dot(act.astype(BF16), wo_ref[0], preferred_element_type=F32) + bo_ref[0]

    @pl.when(i >= n_used)
    def _():
        y_ref[...] = jnp.zeros_like(y_ref)


def _moe_experts(h2, src, tile_expert, n_tiles_used, w_in, b_in, w_out, b_out):
    n, d = h2.shape
    r = src.shape[0]
    tm = MOE_ROW_TILE
    f2 = w_in.shape[2]
    ff = w_out.shape[1]
    col = np.arange(f2)
    perm = np.zeros((f2, f2), np.float32)
    perm[col, np.where(col % 2 == 0, col // 2, ff + col // 2)] = 1.0
    grid_spec = pltpu.PrefetchScalarGridSpec(
        num_scalar_prefetch=3, grid=(r // tm,),
        in_specs=[pl.BlockSpec(memory_space=pl.ANY),
                  pl.BlockSpec((1, d, f2), lambda i, te, nt, src: (te[i], 0, 0)),
                  pl.BlockSpec((1, 1, f2), lambda i, te, nt, src: (te[i], 0, 0)),
                  pl.BlockSpec((f2, f2), lambda i, te, nt, src: (0, 0)),
                  pl.BlockSpec((1, ff, d), lambda i, te, nt, src: (te[i], 0, 0)),
                  pl.BlockSpec((1, 1, d), lambda i, te, nt, src: (te[i], 0, 0))],
        out_specs=pl.BlockSpec((tm, d), lambda i, te, nt, src: (i, 0)),
        scratch_shapes=[pltpu.VMEM((2, tm, d), F32), pltpu.SemaphoreType.DMA((2,))])
    return pl.pallas_call(
        _expert_kernel, grid_spec=grid_spec,
        out_shape=jax.ShapeDtypeStruct((r, d), F32),
        compiler_params=_cparams(("arbitrary",), unchecked=True),
        name="moe_experts",
    )(tile_expert, n_tiles_used, src, h2, w_in, b_in, jnp.asarray(perm, BF16), w_out, b_out)


def _combine_kernel(pos_ref, y_ref, x_ref, w_ref, g_ref, o_ref, buf_ref, sem, *, tm):
    i = pl.program_id(0)
    n_steps = pl.num_programs(0)

    def gather(tile, slot):
        base = tile * tm * MOE_TOPK

        def issue(t, carry):
            for k in range(MOE_TOPK):
                pltpu.make_async_copy(y_ref.at[pl.ds(pos_ref[base + t * MOE_TOPK + k], 1)],
                                      buf_ref.at[slot, k, pl.ds(t, 1)], sem.at[slot]).start()
            return carry
        lax.fori_loop(0, tm, issue, 0, unroll=2)

    @pl.when(i == 0)
    def _():
        gather(0, 0)

    @pl.when(i + 1 < n_steps)
    def _():
        gather(i + 1, (i + 1) % 2)

    slot = i % 2
    for k in range(MOE_TOPK):
        pltpu.make_async_copy(y_ref.at[pl.ds(0, tm)], buf_ref.at[slot, k], sem.at[slot]).wait()
    w = w_ref[...]
    mix = buf_ref[slot, 0] * w[:, 0:1]
    for k in range(1, MOE_TOPK):
        mix = mix + buf_ref[slot, k] * w[:, k:k + 1]
    o_ref[...] = x_ref[...] + g_ref[0] * mix


def _moe_combine(y, pos, x2, wgt, gate, rows_per_batch, tm=128):
    n, d = x2.shape
    tpb = rows_per_batch // tm
    grid_spec = pltpu.PrefetchScalarGridSpec(
        num_scalar_prefetch=1, grid=(n // tm,),
        in_specs=[pl.BlockSpec(memory_space=pl.ANY),
                  pl.BlockSpec((tm, d), lambda i, pos: (i, 0)),
                  pl.BlockSpec((tm, LANES), lambda i, pos: (i, 0)),
                  pl.BlockSpec((1, 1, d), lambda i, pos: (i // tpb, 0, 0))],
        out_specs=pl.BlockSpec((tm, d), lambda i, pos: (i, 0)),
        scratch_shapes=[pltpu.VMEM((2, MOE_TOPK, tm, d), F32), pltpu.SemaphoreType.DMA((2,))])
    return pl.pallas_call(
        functools.partial(_combine_kernel, tm=tm), grid_spec=grid_spec,
        out_shape=jax.ShapeDtypeStruct((n, d), F32),
        compiler_params=_cparams(("arbitrary",), unchecked=True),
        name="moe_combine",
    )(pos, y, x2, wgt, gate[:, None, :])


def _moe_block(x2, g, scale, shift, gate, w_router, b_router, w_exp_in, b_exp_in, w_exp_out, b_exp_out,
               rows_per_batch):
    n, d = x2.shape
    tm = MOE_ROW_TILE
    h2, idx, wgt, rank, counts = _moe_route(x2, g, scale, shift, w_router, b_router, rows_per_batch)
    cnt = counts[0, :N_EXPERTS].astype(I32)
    padded = (cnt + tm - 1) // tm * tm
    ends = jnp.cumsum(padded)
    starts = ends - padded
    n_rows = n * MOE_TOPK + N_EXPERTS * tm
    n_tiles = n_rows // tm
    tile_start = jnp.arange(n_tiles, dtype=I32) * tm
    tile_expert = jnp.minimum(jnp.sum((ends[None, :] <= tile_start[:, None]).astype(I32), axis=1), N_EXPERTS - 1)
    n_tiles_used = (ends[-1] // tm).astype(I32).reshape(1)
    e_flat = idx[:, :MOE_TOPK].reshape(-1)
    pos = (starts[e_flat] + rank[:, :MOE_TOPK].reshape(-1)).astype(I32)
    pad_lo = jnp.concatenate([starts + cnt, ends[-1:]]).astype(I32)
    pad_hi = jnp.concatenate([ends, jnp.full((1,), n_rows, I32)]).astype(I32)
    src = _moe_invert(pos, pad_lo, pad_hi, n_rows)
    y = _moe_experts(h2, src, tile_expert, n_tiles_used, w_exp_in.astype(BF16), b_exp_in[:, None, :],
                     w_exp_out.astype(BF16), b_exp_out[:, None, :])
    return _moe_combine(y, pos, x2, wgt, gate, rows_per_batch)


def _even_mixer(h, x2, gate, batch, s_len, w_in, g_cq, w_uq, w_iq, gq_a, gk_a, g_kidx, b_kidx, gq_b, gk_b,
                b_f, w_out):
    gw = GROUP_WIDTH
    o_cq, o_ka, o_va, o_ki, o_wi, o_qb, o_kb, o_vb, o_fb, o_gb = np.cumsum(
        [0, DSA_Q_LORA, HEAD_DIM, HEAD_DIM, HEAD_DIM, DSA_IDX_HEADS, gw, gw, gw, GROUP_HEADS]).tolist()
    cols = lambda o, wdt: w_in[:, o:o + wdt]
    w_main = jnp.concatenate([cols(o_qb, gw), cols(o_kb, gw), cols(o_vb, gw), cols(o_gb, gw),
                              cols(o_cq, DSA_Q_LORA), cols(o_ka, HEAD_DIM), cols(o_va, HEAD_DIM),
                              cols(o_ki, HEAD_DIM)], axis=1).astype(BF16)
    c_qb, c_kb, c_vb, c_gb, c_cq = 0, gw, 2 * gw, 3 * gw, 4 * gw
    c_ka = c_cq + DSA_Q_LORA
    c_va, c_ki = c_ka + HEAD_DIM, c_ka + 2 * HEAD_DIM
    d = w_in.shape[0]
    w_small = jnp.concatenate([cols(o_fb, GROUP_HEADS), cols(o_wi, DSA_IDX_HEADS),
                               jnp.zeros((d, LANES - GROUP_HEADS - DSA_IDX_HEADS), F32)], axis=1).astype(BF16)
    z = _matmul(h, w_main, BF16, "even_in_proj")
    zs = _matmul(h, w_small, F32, "even_in_proj_small")
    scale = Q_SCALE

    cq = _rowprep(z, c_cq, DSA_Q_LORA, batch, s_len, hw=DSA_Q_LORA, norm="rms", gain=g_cq, name="dsa_cq_norm")
    qa_raw = _matmul(cq, w_uq.astype(BF16), BF16, "dsa_q_up")
    qi_raw = _matmul(cq, w_iq.astype(BF16), BF16, "dsa_idx_q_up")
    qa = _rowprep(qa_raw, 0, gw, batch, s_len, norm="rms", gain=gq_a, rope_dim=HEAD_DIM, out_scale=scale,
                  head_major=True, name="dsa_q_prep")
    qi = _rowprep(qi_raw, 0, DSA_IDX_HEADS * HEAD_DIM, batch, s_len, rope_dim=DSA_IDX_ROPE, head_major=True,
                  name="dsa_idx_q_prep")
    ka = _rowprep(z, c_ka, HEAD_DIM, batch, s_len, norm="rms", gain=gk_a, rope_dim=HEAD_DIM, name="dsa_k_prep")
    ki = _rowprep(z, c_ki, HEAD_DIM, batch, s_len, norm="ln", gain=g_kidx, beta=b_kidx, rope_dim=DSA_IDX_ROPE,
                  name="dsa_idx_k_prep")
    va = z[:, c_va:c_va + HEAD_DIM]
    wi = jnp.pad(zs[:, GROUP_HEADS:GROUP_HEADS + DSA_IDX_HEADS] * (DSA_IDX_HEADS ** -0.5 * HEAD_DIM ** -0.5),
                 ((0, 0), (0, LANES - DSA_IDX_HEADS)))
    o_a = _dsa_attention(qi, ki, wi, qa, ka, va, batch, s_len)
    o_a = o_a.transpose(0, 2, 1, 3).reshape(batch * s_len, gw)

    qb = _rowprep(z, c_qb, gw, batch, s_len, norm="rms", gain=gq_b, out_scale=scale, head_major=True,
                  name="fox_q_prep")
    kb = _rowprep(z, c_kb, gw, batch, s_len, norm="rms", gain=gk_b, head_major=True, name="fox_k_prep")
    bias_row = jnp.pad(b_f.astype(F32), (0, LANES - GROUP_HEADS)).reshape(1, LANES)
    cum = _fox_cumsum(zs, bias_row, batch, s_len)
    o_b = _fox_attention(qb, kb, z, c_vb, c_gb, cum, batch, s_len)
    o_cat = jnp.concatenate([o_a, o_b], axis=1)
    return _matmul(o_cat, w_out.astype(BF16), F32, "even_out_proj", residual=x2, gate=gate,
                   rows_per_batch=s_len)


def _odd_mixer(h, x2, gate, batch, s_len, w_in, gq_c, gk_c, gq_d, gk_d, sinks, w_out):
    gw = GROUP_WIDTH
    kvw = SWA_KV_HEADS * HEAD_DIM
    c_qc, c_kc, c_vc, c_qd, c_kd = 0, gw, 2 * gw, 3 * gw, 4 * gw
    c_vd = c_kd + kvw
    z = _matmul(h, w_in.astype(BF16), BF16, "odd_in_proj")
    scale = Q_SCALE
    qc = _rowprep(z, c_qc, gw, batch, s_len, norm="rms", gain=gq_c, rope_dim=HEAD_DIM, out_scale=scale,
                  head_major=True, name="moba_q_prep")
    kc, kmean = _rowprep(z, c_kc, gw, batch, s_len, norm="rms", gain=gk_c, rope_dim=HEAD_DIM, head_major=True,
                         want_mean=True, name="moba_k_prep")
    o_c = _moba_attention(qc, kc, kmean, z, c_vc, batch, s_len)
    qd = _rowprep(z, c_qd, gw, batch, s_len, norm="rms", gain=gq_d, rope_dim=HEAD_DIM, out_scale=scale,
                  head_major=True, name="swa_q_prep")
    kd = _rowprep(z, c_kd, kvw, batch, s_len, norm="rms", gain=gk_d, rope_dim=HEAD_DIM, head_major=True,
                  name="swa_k_prep")
    o_d = _swa_attention(qd, kd, z, c_vd, sinks, batch, s_len)
    o_cat = jnp.concatenate([o_c, o_d], axis=1)
    return _matmul(o_cat, w_out.astype(BF16), F32, "odd_out_proj", residual=x2, gate=gate, rows_per_batch=s_len)


def kernel(x, c, g_norm_mix, g_norm_ffn, w_ada, b_ada, w_in_even, g_cq, w_uq, w_iq, gq_a, gk_a, g_kidx, b_kidx,
           gq_b, gk_b, b_forget, w_out_even, w_in_odd, gq_c, gk_c, gq_d, gk_d, sinks_d, w_out_odd,
           w_router, b_router, w_exp_in, b_exp_in, w_exp_out, b_exp_out):
    batch, s_len, d = x.shape
    depth = w_ada.shape[0]
    mod = _ada_modulation(c, w_ada, b_ada)
    x2 = x.reshape(batch * s_len, d)
    for layer in range(depth):
        shift_m, scale_m, gate_m, shift_f, scale_f, gate_f = [mod[layer, :, i * d:(i + 1) * d] for i in range(6)]
        h = _norm_modulate(x2, g_norm_mix[layer], scale_m, shift_m, s_len)
        j = layer // 2
        if layer % 2 == 0:
            x2 = _even_mixer(h, x2, gate_m, batch, s_len, w_in_even[j], g_cq[j], w_uq[j], w_iq[j], gq_a[j],
                             gk_a[j], g_kidx[j], b_kidx[j], gq_b[j], gk_b[j], b_forget[j], w_out_even[j])
        else:
            x2 = _odd_mixer(h, x2, gate_m, batch, s_len, w_in_odd[j], gq_c[j], gk_c[j], gq_d[j], gk_d[j],
                            sinks_d[j], w_out_odd[j])
        x2 = _moe_block(x2, g_norm_ffn[layer], scale_f, shift_f, gate_f, w_router[layer], b_router[layer],
                        w_exp_in[layer], b_exp_in[layer], w_exp_out[layer], b_exp_out[layer], s_len)
    return x2.reshape(batch, s_len, d)
```

```python
import functools

import numpy as np
import jax
import jax.numpy as jnp
from jax import lax
from jax.experimental import pallas as pl
from jax.experimental.pallas import tpu as pltpu

F32 = jnp.float32
BF16 = jnp.bfloat16
I32 = jnp.int32

LANES = 128
HEAD_DIM = 128
GROUP_HEADS = 16
GROUP_WIDTH = GROUP_HEADS * HEAD_DIM
ROPE_THETA = 10000.0
EPS = 1e-6
DSA_Q_LORA = 1024
DSA_IDX_HEADS = 32
DSA_IDX_ROPE = 64
DSA_TOPK = 256
DSA_QBLK = 128
DSA_CHUNK = 256
DSA_STAGES = 4
DSA_HEAD_GROUP = 2
MOBA_BLOCK = 256
MOBA_TOPK = 3
SWA_KV_HEADS = 2
SWA_WINDOW = 128
N_EXPERTS = 32
MOE_TOPK = 4
EXPERT_FF = 512
SWIGLU_LIMIT = 7.0
SWIGLU_ALPHA = 1.702
MOE_ROW_TILE = 256
NEG_BIG = -1e30
LOG2_E = 1.4426950408889634
Q_SCALE = HEAD_DIM ** -0.5 * LOG2_E
VMEM_LIMIT = 56 * 1024 * 1024


def _cparams(semantics, vmem=VMEM_LIMIT, unchecked=False):
    return pltpu.CompilerParams(dimension_semantics=semantics, vmem_limit_bytes=vmem,
                                disable_bounds_checks=unchecked)


def _ada_kernel(cb_ref, w_ref, b_ref, o_ref):
    nb = cb_ref.shape[0]
    tn = w_ref.shape[2]
    for c in range(tn // LANES):
        w = w_ref[0, :, c * LANES:(c + 1) * LANES]
        for b in range(nb):
            o_ref[0, b:b + 1, c * LANES:(c + 1) * LANES] = (
                jnp.sum(w * cb_ref[b], axis=0, keepdims=True) + b_ref[0, :, c * LANES:(c + 1) * LANES])


def _ada_modulation(c, w_ada, b_ada, tn=512):
    depth, d, n6 = w_ada.shape
    nb = c.shape[0]
    cb = jnp.broadcast_to(jax.nn.silu(c)[:, :, None], (nb, d, LANES))
    return pl.pallas_call(
        _ada_kernel,
        grid=(depth, n6 // tn),
        in_specs=[pl.BlockSpec((nb, d, LANES), lambda l, j: (0, 0, 0)),
                  pl.BlockSpec((1, d, tn), lambda l, j: (l, 0, j)),
                  pl.BlockSpec((1, 1, tn), lambda l, j: (l, 0, j))],
        out_specs=pl.BlockSpec((1, nb, tn), lambda l, j: (l, 0, j)),
        out_shape=jax.ShapeDtypeStruct((depth, nb, n6), F32),
        compiler_params=_cparams(("parallel", "parallel")),
        name="ada_modulation",
    )(cb, w_ada, b_ada.reshape(depth, 1, n6))


def _normmod_kernel(x_ref, g_ref, sc_ref, sh_ref, o_ref):
    x = x_ref[...]
    y = x * lax.rsqrt(jnp.mean(x * x, axis=-1, keepdims=True) + EPS) * g_ref[...]
    o_ref[...] = (y * (1.0 + sc_ref[0]) + sh_ref[0]).astype(o_ref.dtype)


def _norm_modulate(x2, g, scale, shift, rows_per_batch, tm=256):
    n, d = x2.shape
    tpb = rows_per_batch // tm
    return pl.pallas_call(
        _normmod_kernel,
        grid=(n // tm,),
        in_specs=[pl.BlockSpec((tm, d), lambda i: (i, 0)),
                  pl.BlockSpec((1, d), lambda i: (0, 0)),
                  pl.BlockSpec((1, 1, d), lambda i: (i // tpb, 0, 0)),
                  pl.BlockSpec((1, 1, d), lambda i: (i // tpb, 0, 0))],
        out_specs=pl.BlockSpec((tm, d), lambda i: (i, 0)),
        out_shape=jax.ShapeDtypeStruct((n, d), BF16),
        compiler_params=_cparams(("parallel",)),
        name="norm_modulate",
    )(x2, g.reshape(1, d), scale[:, None, :], shift[:, None, :])


def _mm_kernel(a_ref, b_ref, o_ref):
    o_ref[...] = jnp.dot(a_ref[...], b_ref[...], preferred_element_type=F32).astype(o_ref.dtype)


def _mm_residual_kernel(a_ref, b_ref, x_ref, g_ref, o_ref):
    o_ref[...] = x_ref[...] + g_ref[0] * jnp.dot(a_ref[...], b_ref[...], preferred_element_type=F32)


def _pick(n, prefs):
    for t in prefs:
        if n % t == 0:
            return t
    return n


def _matmul(a, b, out_dtype, name, residual=None, gate=None, rows_per_batch=None):
    m, k = a.shape
    n = b.shape[1]
    tm = _pick(m, (1024, 512, 256, 128))
    tn = _pick(n, (512, 640, 384, 256, 128) if k > 2048 else (1024, 512, 640, 384, 256, 128))
    grid = (m // tm, n // tn)
    a_spec = pl.BlockSpec((tm, k), lambda i, j: (i, 0))
    b_spec = pl.BlockSpec((k, tn), lambda i, j: (0, j))
    o_spec = pl.BlockSpec((tm, tn), lambda i, j: (i, j))
    params = _cparams(("parallel", "arbitrary"))
    if residual is None:
        return pl.pallas_call(
            _mm_kernel, grid=grid, in_specs=[a_spec, b_spec], out_specs=o_spec,
            out_shape=jax.ShapeDtypeStruct((m, n), out_dtype), compiler_params=params, name=name)(a, b)
    tpb = rows_per_batch // tm
    return pl.pallas_call(
        _mm_residual_kernel, grid=grid,
        in_specs=[a_spec, b_spec, o_spec, pl.BlockSpec((1, 1, tn), lambda i, j: (i // tpb, 0, j))],
        out_specs=o_spec, out_shape=jax.ShapeDtypeStruct((m, n), F32),
        compiler_params=params, name=name)(a, b, residual, gate[:, None, :])


def _rope_tables(s_len, rot_dim):
    half = rot_dim // 2
    inv_freq = ROPE_THETA ** (-jnp.arange(half, dtype=F32) / half)
    ang = jnp.arange(s_len, dtype=F32)[:, None] * inv_freq[None, :]
    cos, sin = jnp.cos(ang), jnp.sin(ang)
    pad = HEAD_DIM - rot_dim
    ones = jnp.ones((s_len, pad), F32)
    zeros = jnp.zeros((s_len, pad), F32)
    zh = jnp.zeros((s_len, half), F32)
    cos_t = jnp.concatenate([cos, cos, ones], axis=1)
    sa = jnp.concatenate([-sin, zh, zeros], axis=1)
    sb = jnp.concatenate([zh, sin, zeros], axis=1)
    return cos_t, sa, sb


def _rowprep_kernel(*refs, hw, norm, rope_half, out_scale, head_major, want_mean):
    it = iter(refs)
    z_ref = next(it)
    g_ref = next(it) if norm != "none" else None
    beta_ref = next(it) if norm == "ln" else None
    if rope_half:
        cos_ref, sa_ref, sb_ref = next(it), next(it), next(it)
    o_ref = next(it)
    mean_ref = next(it) if want_mean else None
    width = z_ref.shape[1]
    for h in range(width // hw):
        x = z_ref[:, h * hw:(h + 1) * hw].astype(F32)
        if norm == "rms":
            x = x * lax.rsqrt(jnp.mean(x * x, axis=-1, keepdims=True) + EPS) * g_ref[...]
        elif norm == "ln":
            mu = jnp.mean(x, axis=-1, keepdims=True)
            xc = x - mu
            x = xc * lax.rsqrt(jnp.mean(xc * xc, axis=-1, keepdims=True) + EPS) * g_ref[...] + beta_ref[...]
        if rope_half:
            if rope_half * 2 == hw:
                x = x * cos_ref[...] + pltpu.roll(x, rope_half, 1) * (sa_ref[...] + sb_ref[...])
            else:
                x = (x * cos_ref[...] + pltpu.roll(x, hw - rope_half, 1) * sa_ref[...]
                     + pltpu.roll(x, rope_half, 1) * sb_ref[...])
        if want_mean:
            mean_ref[0, 0, h:h + 1, :] = jnp.mean(x, axis=0, keepdims=True)
        if out_scale != 1.0:
            x = x * out_scale
        if head_major:
            o_ref[0, h] = x.astype(o_ref.dtype)
        else:
            o_ref[:, h * hw:(h + 1) * hw] = x.astype(o_ref.dtype)


def _rowprep(z, col0, width, batch, s_len, *, hw=HEAD_DIM, norm="none", gain=None, beta=None, rope_dim=0,
             out_scale=1.0, head_major=False, want_mean=False, name="rowprep", tm=256):
    n = z.shape[0]
    tpb = s_len // tm
    assert col0 % width == 0 and n == batch * s_len
    cb = col0 // width
    args = [z]
    specs = [pl.BlockSpec((tm, width), lambda i: (i, cb))]
    if norm != "none":
        args.append(gain.reshape(1, hw).astype(F32))
        specs.append(pl.BlockSpec((1, hw), lambda i: (0, 0)))
    if norm == "ln":
        args.append(beta.reshape(1, hw).astype(F32))
        specs.append(pl.BlockSpec((1, hw), lambda i: (0, 0)))
    if rope_dim:
        args += list(_rope_tables(s_len, rope_dim))
        specs += [pl.BlockSpec((tm, HEAD_DIM), lambda i: (i % tpb, 0))] * 3
    nh = width // hw
    if head_major:
        out_shape = [jax.ShapeDtypeStruct((batch, nh, s_len, hw), BF16)]
        out_specs = [pl.BlockSpec((1, nh, tm, hw), lambda i: (i // tpb, 0, i % tpb, 0))]
    else:
        out_shape = [jax.ShapeDtypeStruct((n, width), BF16)]
        out_specs = [pl.BlockSpec((tm, width), lambda i: (i, 0))]
    if want_mean:
        assert tm == MOBA_BLOCK
        out_shape.append(jax.ShapeDtypeStruct((batch, tpb, nh, hw), F32))
        out_specs.append(pl.BlockSpec((1, 1, nh, hw), lambda i: (i // tpb, i % tpb, 0, 0)))
    kern = functools.partial(_rowprep_kernel, hw=hw, norm=norm, rope_half=rope_dim // 2, out_scale=out_scale,
                             head_major=head_major, want_mean=want_mean)
    res = pl.pallas_call(kern, grid=(n // tm,), in_specs=specs, out_specs=out_specs, out_shape=out_shape,
                         compiler_params=_cparams(("parallel",)), name=name)(*args)
    return res if want_mean else res[0]


def _fox_cum_kernel(z_ref, bf_ref, tri_ref, o_ref, carry_ref):
    @pl.when(pl.program_id(1) == 0)
    def _():
        carry_ref[...] = jnp.zeros_like(carry_ref)

    xv = z_ref[...] + bf_ref[...]
    lf = jnp.minimum(xv, 0.0) - jnp.log(1.0 + jnp.exp(-jnp.abs(xv)))
    hi = lf.astype(BF16)
    r1 = lf - hi.astype(F32)
    mid = r1.astype(BF16)
    lo = (r1 - mid.astype(F32)).astype(BF16)
    tri = tri_ref[...]
    cs = (jnp.dot(tri, hi, preferred_element_type=F32) + jnp.dot(tri, mid, preferred_element_type=F32)
          + jnp.dot(tri, lo, preferred_element_type=F32)) + carry_ref[...]
    o_ref[...] = cs * LOG2_E
    tm = cs.shape[0]
    carry_ref[...] = cs[tm - 1:tm, :]


def _fox_cumsum(zs, bias_row, batch, s_len, tm=256):
    n = zs.shape[0]
    tpb = s_len // tm
    tri = jnp.tril(jnp.ones((tm, tm), F32)).astype(BF16)
    return pl.pallas_call(
        _fox_cum_kernel,
        grid=(batch, tpb),
        in_specs=[pl.BlockSpec((tm, LANES), lambda b, i: (b * tpb + i, 0)),
                  pl.BlockSpec((1, LANES), lambda b, i: (0, 0)),
                  pl.BlockSpec((tm, tm), lambda b, i: (0, 0))],
        out_specs=pl.BlockSpec((tm, LANES), lambda b, i: (b * tpb + i, 0)),
        out_shape=jax.ShapeDtypeStruct((n, LANES), F32),
        scratch_shapes=[pltpu.VMEM((1, LANES), F32)],
        compiler_params=_cparams(("parallel", "arbitrary")),
        name="fox_cumsum",
    )(zs, bias_row, tri)


def _dsa_kernel(qi_ref, kit_ref, wi_ref, qa_ref, kat_ref, va_ref, o_ref, wb_ref, key_ref, mb_ref,
                *, sk, q0, topk):
    q = DSA_QBLK
    ch = DSA_CHUNK
    nch = sk // ch
    n_idx = qi_ref.shape[1]
    n_heads = qa_ref.shape[1]
    t0 = (q0 + pl.program_id(1)) * q
    row = t0 + lax.broadcasted_iota(I32, (q, ch), 0)
    lane = lax.broadcasted_iota(I32, (q, ch), 1)

    wi = wi_ref[...]
    for h in range(n_idx):
        wb_ref[h] = jnp.broadcast_to(wi[:, h:h + 1], (q, ch))
    qi = qi_ref[0].reshape(n_idx * q, HEAD_DIM)

    def idx_chunk(c, carry):
        d = jnp.dot(qi, kit_ref[0, c], preferred_element_type=F32)
        acc = jnp.zeros((q, ch), F32)
        for h in range(n_idx):
            acc = acc + jnp.maximum(d[h * q:(h + 1) * q], 0.0) * wb_ref[h]
        score = jnp.where(c * ch + lane <= row, acc + 0.0, -jnp.inf)
        bits = lax.bitcast_convert_type(score, I32)
        key_ref[c] = jnp.where(bits >= 0, bits, bits ^ jnp.int32(0x7FFFFFFF))
        return carry

    lax.fori_loop(0, nch, idx_chunk, 0)

    def count_ge(cand):
        acc = jnp.zeros((q, LANES), F32)
        for c in range(nch):
            for part in range(ch // LANES):
                acc = acc + jnp.where(key_ref[c, :, part * LANES:(part + 1) * LANES] >= cand, 1.0, 0.0)
        return jnp.sum(acc, axis=-1, keepdims=True)

    int_min = jnp.int32(-2 ** 31)
    thr = jnp.where(count_ge(jnp.zeros((q, 1), I32)) >= topk, jnp.int32(0), int_min)

    def bit_step(j, thr):
        cand = thr | jnp.left_shift(jnp.int32(1), 30 - j)
        return jnp.where(count_ge(cand) >= topk, cand, thr)

    thr = lax.fori_loop(0, 31, bit_step, thr)

    for c in range(nch):
        keep = (key_ref[c] >= thr) & (c * ch + lane <= row)
        mb_ref[:, c * ch:(c + 1) * ch] = jnp.where(keep, 0.0, -jnp.inf)

    g = DSA_HEAD_GROUP

    def head_group(hg, carry):
        qh = qa_ref[0, pl.ds(hg * g, g)].reshape(g * q, HEAD_DIM)
        lg = jnp.dot(qh, kat_ref[0], preferred_element_type=F32).reshape(g, q, sk) + mb_ref[...][None]
        m = jnp.max(lg, axis=-1, keepdims=True)
        p = jnp.exp2(lg - m)
        l = jnp.sum(p, axis=-1, keepdims=True)
        o = jnp.dot(p.reshape(g * q, sk).astype(BF16), va_ref[0], preferred_element_type=F32)
        o_ref[0, pl.ds(hg * g, g)] = (o.reshape(g, q, HEAD_DIM) / l).astype(o_ref.dtype)
        return carry

    lax.fori_loop(0, n_heads // g, head_group, 0)


def _dsa_attention(qi, ki, wi, qa, ka, va, batch, s_len):
    q, ch = DSA_QBLK, DSA_CHUNK
    n_idx, n_heads = qi.shape[1], qa.shape[1]
    topk = min(DSA_TOPK, s_len // 4)
    nqb = s_len // q
    kit = ki.reshape(batch, s_len // ch, ch, HEAD_DIM).transpose(0, 1, 3, 2)
    kat = ka.reshape(batch, s_len, HEAD_DIM).transpose(0, 2, 1)
    va3 = va.reshape(batch, s_len, HEAD_DIM)
    per_stage = max(nqb // DSA_STAGES, ch // q)
    outs = []
    for q0 in range(0, nqb, per_stage):
        sk = (q0 + per_stage) * q
        nch = sk // ch
        kern = functools.partial(_dsa_kernel, sk=sk, q0=q0, topk=topk)
        outs.append(pl.pallas_call(
            kern,
            grid=(batch, per_stage),
            in_specs=[pl.BlockSpec((1, n_idx, q, HEAD_DIM), lambda b, i, q0=q0: (b, 0, q0 + i, 0)),
                      pl.BlockSpec((1, nch, HEAD_DIM, ch), lambda b, i: (b, 0, 0, 0)),
                      pl.BlockSpec((q, LANES), lambda b, i, q0=q0: (b * nqb + q0 + i, 0)),
                      pl.BlockSpec((1, n_heads, q, HEAD_DIM), lambda b, i, q0=q0: (b, 0, q0 + i, 0)),
                      pl.BlockSpec((1, HEAD_DIM, sk), lambda b, i: (b, 0, 0)),
                      pl.BlockSpec((1, sk, HEAD_DIM), lambda b, i: (b, 0, 0))],
            out_specs=pl.BlockSpec((1, n_heads, q, HEAD_DIM), lambda b, i: (b, 0, i, 0)),
            out_shape=jax.ShapeDtypeStruct((batch, n_heads, per_stage * q, HEAD_DIM), BF16),
            scratch_shapes=[pltpu.VMEM((n_idx, q, ch), F32),
                            pltpu.VMEM((nch, q, ch), I32),
                            pltpu.VMEM((q, sk), F32)],
            compiler_params=_cparams(("parallel", "parallel")),
            name=f"dsa_attention_k{sk}",
        )(qi, kit, wi, qa, kat, va3))
    return jnp.concatenate(outs, axis=2)


def _tri_tables(nq, ratio):
    qt, kt = [], []
    for i in range(nq):
        for j in range((i + 1) * ratio):
            qt.append(i)
            kt.append(j)
    return jnp.asarray(qt, I32), jnp.asarray(kt, I32)


def _fox_kernel(qt_ref, kt_ref, q_ref, k_ref, v_ref, cq_ref, ck_ref, g_ref, o_ref,
                m_ref, l_ref, acc_ref, cqs_ref, *, tq):
    h = pl.program_id(1)
    step = pl.program_id(2)
    qi = qt_ref[step]
    kj = kt_ref[step]

    @pl.when(kj == 0)
    def _():
        m_ref[...] = jnp.full_like(m_ref, -jnp.inf)
        l_ref[...] = jnp.zeros_like(l_ref)
        acc_ref[...] = jnp.zeros_like(acc_ref)
        lane = lax.broadcasted_iota(I32, cq_ref.shape, 1)
        cqs_ref[...] = jnp.sum(jnp.where(lane == h, cq_ref[...], 0.0), axis=-1, keepdims=True)

    s = lax.dot_general(q_ref[0, 0], k_ref[0, 0], (((1,), (1,)), ((), ())), preferred_element_type=F32)
    s = s + (cqs_ref[...] - ck_ref[0, 0])

    def update(s):
        m_prev = m_ref[...]
        m_new = jnp.maximum(m_prev, jnp.max(s, axis=-1, keepdims=True))
        alpha = jnp.exp2(m_prev - m_new)
        p = jnp.exp2(s - m_new)
        l_ref[...] = alpha * l_ref[...] + jnp.sum(p, axis=-1, keepdims=True)
        acc_ref[...] = alpha * acc_ref[...] + jnp.dot(p.astype(BF16), v_ref[...], preferred_element_type=F32)
        m_ref[...] = m_new

    @pl.when(kj < qi)
    def _():
        update(s)

    @pl.when(kj == qi)
    def _():
        row = lax.broadcasted_iota(I32, s.shape, 0)
        col = lax.broadcasted_iota(I32, s.shape, 1)
        update(jnp.where(col <= row, s, -jnp.inf))
        gate = g_ref[...].astype(F32)
        o_ref[...] = (acc_ref[...] / l_ref[...] * (1.0 / (1.0 + jnp.exp(-gate)))).astype(o_ref.dtype)


def _fox_attention(qb, kb, z, v_col0, g_col0, cum, batch, s_len, tq=1024):
    tq = min(tq, s_len)
    nq = s_len // tq
    n_heads = qb.shape[1]
    qt, kt = _tri_tables(nq, 1)
    cum_t = cum.reshape(batch, s_len, LANES)[:, :, :n_heads].transpose(0, 2, 1)[:, :, None, :]
    vb0, gb0 = v_col0 // HEAD_DIM, g_col0 // HEAD_DIM
    grid_spec = pltpu.PrefetchScalarGridSpec(
        num_scalar_prefetch=2,
        grid=(batch, n_heads, int(qt.shape[0])),
        in_specs=[pl.BlockSpec((1, 1, tq, HEAD_DIM), lambda b, h, s, qt, kt: (b, h, qt[s], 0)),
                  pl.BlockSpec((1, 1, tq, HEAD_DIM), lambda b, h, s, qt, kt: (b, h, kt[s], 0)),
                  pl.BlockSpec((tq, HEAD_DIM), lambda b, h, s, qt, kt: (b * nq + kt[s], vb0 + h)),
                  pl.BlockSpec((tq, LANES), lambda b, h, s, qt, kt: (b * nq + qt[s], 0)),
                  pl.BlockSpec((1, 1, 1, tq), lambda b, h, s, qt, kt: (b, h, 0, kt[s])),
                  pl.BlockSpec((tq, HEAD_DIM), lambda b, h, s, qt, kt: (b * nq + qt[s], gb0 + h))],
        out_specs=pl.BlockSpec((tq, HEAD_DIM), lambda b, h, s, qt, kt: (b * nq + qt[s], h)),
        scratch_shapes=[pltpu.VMEM((tq, 1), F32), pltpu.VMEM((tq, 1), F32),
                        pltpu.VMEM((tq, HEAD_DIM), F32), pltpu.VMEM((tq, 1), F32)])
    return pl.pallas_call(
        functools.partial(_fox_kernel, tq=tq), grid_spec=grid_spec,
        out_shape=jax.ShapeDtypeStruct((batch * s_len, n_heads * HEAD_DIM), BF16),
        compiler_params=_cparams(("parallel", "parallel", "arbitrary")),
        name="fox_attention",
    )(qt, kt, qb, kb, z, cum, cum_t, z)


def _moba_kernel(qt_ref, kt_ref, q_ref, k_ref, e_ref, v_ref, km_ref, o_ref, m_ref, l_ref, acc_ref, qa_ref,
                 *, tq, n_sel):
    step = pl.program_id(2)
    qi = qt_ref[step]
    kj = kt_ref[step]

    @pl.when(kj == 0)
    def _():
        m_ref[...] = jnp.full_like(m_ref, NEG_BIG)
        l_ref[...] = jnp.zeros_like(l_ref)
        acc_ref[...] = jnp.zeros_like(acc_ref)
        own = (qi * tq + lax.broadcasted_iota(I32, (tq, 1), 0)) // MOBA_BLOCK
        gate = jnp.dot(q_ref[0, 0].astype(F32), km_ref[0, 0], preferred_element_type=F32,
                       precision=lax.Precision.HIGHEST)
        lane = lax.broadcasted_iota(I32, gate.shape, 1)
        lane_f = lane.astype(F32)
        gate = jnp.where(lane < own, gate, -jnp.inf)
        allowed = jnp.where(lane == own, 1.0, 0.0)
        for _ in range(n_sel):
            best = jnp.max(gate, axis=-1, keepdims=True)
            first = jnp.min(jnp.where(gate == best, lane_f, float(LANES)), axis=-1, keepdims=True)
            pick = (lane_f == first) & (best > -jnp.inf)
            allowed = jnp.where(pick, 1.0, allowed)
            gate = jnp.where(pick, -jnp.inf, gate)
        qa_ref[:, :HEAD_DIM] = q_ref[0, 0]
        qa_ref[:, HEAD_DIM:] = jnp.where(allowed > 0.0, 0.0, NEG_BIG).astype(qa_ref.dtype)

    k_aug = jnp.concatenate([k_ref[0, 0], e_ref[...]], axis=1)
    s = lax.dot_general(qa_ref[...], k_aug, (((1,), (1,)), ((), ())), preferred_element_type=F32)

    def update(s):
        m_prev = m_ref[...]
        m_new = jnp.maximum(m_prev, jnp.max(s, axis=-1, keepdims=True))
        alpha = jnp.exp2(m_prev - m_new)
        p = jnp.exp2(s - m_new)
        l_ref[...] = alpha * l_ref[...] + jnp.sum(p, axis=-1, keepdims=True)
        acc_ref[...] = alpha * acc_ref[...] + jnp.dot(p.astype(BF16), v_ref[...], preferred_element_type=F32)
        m_ref[...] = m_new

    @pl.when(kj < qi)
    def _():
        update(s)

    @pl.when(kj == qi)
    def _():
        row = lax.broadcasted_iota(I32, s.shape, 0)
        col = lax.broadcasted_iota(I32, s.shape, 1)
        update(jnp.where(col <= row, s, NEG_BIG))
        o_ref[...] = (acc_ref[...] / l_ref[...]).astype(o_ref.dtype)


def _moba_attention(qc, kc, kmean, z, v_col0, batch, s_len, tq=1024):
    tq = min(tq, s_len)
    nq = s_len // tq
    n_heads = qc.shape[1]
    nb = s_len // MOBA_BLOCK
    assert nb <= LANES
    n_sel = min(MOBA_TOPK, nb - 1)
    qt, kt = _tri_tables(nq, 1)
    km_t = jnp.pad(kmean.transpose(0, 2, 3, 1), ((0, 0), (0, 0), (0, 0), (0, LANES - nb)))
    block_onehot = (jnp.arange(s_len, dtype=I32)[:, None] // MOBA_BLOCK
                    == jnp.arange(LANES, dtype=I32)[None, :]).astype(BF16)
    vb0 = v_col0 // HEAD_DIM
    grid_spec = pltpu.PrefetchScalarGridSpec(
        num_scalar_prefetch=2,
        grid=(batch, n_heads, int(qt.shape[0])),
        in_specs=[pl.BlockSpec((1, 1, tq, HEAD_DIM), lambda b, h, s, qt, kt: (b, h, qt[s], 0)),
                  pl.BlockSpec((1, 1, tq, HEAD_DIM), lambda b, h, s, qt, kt: (b, h, kt[s], 0)),
                  pl.BlockSpec((tq, LANES), lambda b, h, s, qt, kt: (kt[s], 0)),
                  pl.BlockSpec((tq, HEAD_DIM), lambda b, h, s, qt, kt: (b * nq + kt[s], vb0 + h)),
                  pl.BlockSpec((1, 1, HEAD_DIM, LANES), lambda b, h, s, qt, kt: (b, h, 0, 0))],
        out_specs=pl.BlockSpec((tq, HEAD_DIM), lambda b, h, s, qt, kt: (b * nq + qt[s], h)),
        scratch_shapes=[pltpu.VMEM((tq, 1), F32), pltpu.VMEM((tq, 1), F32),
                        pltpu.VMEM((tq, HEAD_DIM), F32), pltpu.VMEM((tq, HEAD_DIM + LANES), BF16)])
    return pl.pallas_call(
        functools.partial(_moba_kernel, tq=tq, n_sel=n_sel), grid_spec=grid_spec,
        out_shape=jax.ShapeDtypeStruct((batch * s_len, n_heads * HEAD_DIM), BF16),
        compiler_params=_cparams(("parallel", "parallel", "arbitrary")),
        name="moba_attention",
    )(qt, kt, qc, kc, block_onehot, z, km_t)


def _swa_kernel(q_ref, kp_ref, kc_ref, vp_ref, vc_ref, sink_ref, o_ref):
    n = pl.program_id(1)
    hq, w = q_ref.shape[1], q_ref.shape[2]
    hkv = kc_ref.shape[1]
    grp = hq // hkv
    ti = lax.broadcasted_iota(I32, (w, w), 0)
    si = lax.broadcasted_iota(I32, (w, w), 1)
    cur_ok = (si <= ti)[None]
    prev_ok = ((si > ti) & (n > 0))[None]
    nt = (((1,), (1,)), ((), ()))
    for kv in range(hkv):
        q = q_ref[0, kv * grp:(kv + 1) * grp].reshape(grp * w, HEAD_DIM)
        sc = lax.dot_general(q, kc_ref[0, kv], nt, preferred_element_type=F32).reshape(grp, w, w)
        sp = lax.dot_general(q, kp_ref[0, kv], nt, preferred_element_type=F32).reshape(grp, w, w)
        sc = jnp.where(cur_ok, sc, -jnp.inf)
        sp = jnp.where(prev_ok, sp, -jnp.inf)
        sink = sink_ref[kv * grp * w:(kv + 1) * grp * w].reshape(grp, w, LANES)[:, :, :1]
        m = jnp.maximum(jnp.maximum(jnp.max(sc, axis=-1, keepdims=True), jnp.max(sp, axis=-1, keepdims=True)), sink)
        pc = jnp.exp2(sc - m)
        pp = jnp.exp2(sp - m)
        den = jnp.sum(pc, axis=-1, keepdims=True) + jnp.sum(pp, axis=-1, keepdims=True) + jnp.exp2(sink - m)
        o = (jnp.dot(pc.reshape(grp * w, w).astype(BF16), vc_ref[:, kv * HEAD_DIM:(kv + 1) * HEAD_DIM],
                     preferred_element_type=F32)
             + jnp.dot(pp.reshape(grp * w, w).astype(BF16), vp_ref[:, kv * HEAD_DIM:(kv + 1) * HEAD_DIM],
                       preferred_element_type=F32))
        o = o.reshape(grp, w, HEAD_DIM) / den
        for gh in range(grp):
            hh = kv * grp + gh
            o_ref[:, hh * HEAD_DIM:(hh + 1) * HEAD_DIM] = o[gh].astype(o_ref.dtype)


def _swa_attention(qd, kd, z, v_col0, sinks, batch, s_len):
    w = SWA_WINDOW
    nb = s_len // w
    hq, hkv = qd.shape[1], kd.shape[1]
    vw = hkv * HEAD_DIM
    vb0 = v_col0 // vw
    sink_b = jnp.broadcast_to((sinks.astype(F32) * LOG2_E)[:, None, None], (hq, w, LANES)).reshape(hq * w, LANES)
    return pl.pallas_call(
        _swa_kernel,
        grid=(batch, nb),
        in_specs=[pl.BlockSpec((1, hq, w, HEAD_DIM), lambda b, n: (b, 0, n, 0)),
                  pl.BlockSpec((1, hkv, w, HEAD_DIM), lambda b, n: (b, 0, jnp.maximum(n - 1, 0), 0)),
                  pl.BlockSpec((1, hkv, w, HEAD_DIM), lambda b, n: (b, 0, n, 0)),
                  pl.BlockSpec((w, vw), lambda b, n: (b * nb + jnp.maximum(n - 1, 0), vb0)),
                  pl.BlockSpec((w, vw), lambda b, n: (b * nb + n, vb0)),
                  pl.BlockSpec((hq * w, LANES), lambda b, n: (0, 0))],
        out_specs=pl.BlockSpec((w, hq * HEAD_DIM), lambda b, n: (b * nb + n, 0)),
        out_shape=jax.ShapeDtypeStruct((batch * s_len, hq * HEAD_DIM), BF16),
        compiler_params=_cparams(("parallel", "parallel")),
        name="swa_attention",
    )(qd, kd, kd, z, z, sink_b)


def _route_kernel(x_ref, g_ref, sc_ref, sh_ref, wr_ref, br_ref, tri_ref,
                  h_ref, idx_ref, wgt_ref, rank_ref, cnt_ref, carry_ref):
    @pl.when(pl.program_id(0) == 0)
    def _():
        carry_ref[...] = jnp.zeros_like(carry_ref)

    x = x_ref[...]
    y = x * lax.rsqrt(jnp.mean(x * x, axis=-1, keepdims=True) + EPS) * g_ref[...]
    hmod = y * (1.0 + sc_ref[0]) + sh_ref[0]
    h_ref[...] = hmod
    logits = jnp.dot(hmod, wr_ref[...], preferred_element_type=F32, precision=lax.Precision.HIGHEST) + br_ref[...]
    lane = lax.broadcasted_iota(I32, logits.shape, 1)
    lane_f = lane.astype(F32)
    logits = jnp.where(lane < N_EXPERTS, logits, -jnp.inf)
    onehots, vals, firsts = [], [], []
    for _ in range(MOE_TOPK):
        best = jnp.max(logits, axis=-1, keepdims=True)
        first = jnp.min(jnp.where(logits == best, lane_f, float(LANES)), axis=-1, keepdims=True)
        pick = lane_f == first
        onehots.append(pick)
        vals.append(best)
        firsts.append(first.astype(I32))
        logits = jnp.where(pick, -jnp.inf, logits)
    exps = [jnp.exp(v - vals[0]) for v in vals]
    den = exps[0]
    for e in exps[1:]:
        den = den + e
    chosen_f = jnp.zeros(logits.shape, F32)
    for o in onehots:
        chosen_f = jnp.where(o, 1.0, chosen_f)
    before = jnp.dot(tri_ref[...], chosen_f.astype(BF16), preferred_element_type=F32) + carry_ref[...]
    idx_out = jnp.zeros(logits.shape, I32)
    wgt_out = jnp.zeros(logits.shape, F32)
    rank_out = jnp.zeros(logits.shape, I32)
    for k in range(MOE_TOPK):
        rk = jnp.sum(jnp.where(onehots[k], before, 0.0), axis=-1, keepdims=True).astype(I32)
        idx_out = jnp.where(lane == k, firsts[k], idx_out)
        wgt_out = jnp.where(lane == k, exps[k] / den, wgt_out)
        rank_out = jnp.where(lane == k, rk, rank_out)
    idx_ref[...] = idx_out
    wgt_ref[...] = wgt_out
    rank_ref[...] = rank_out
    carry_ref[...] = carry_ref[...] + jnp.sum(chosen_f, axis=0, keepdims=True)
    cnt_ref[...] = carry_ref[...]


def _moe_route(x2, g, scale, shift, w_router, b_router, rows_per_batch, tm=256):
    n, d = x2.shape
    tpb = rows_per_batch // tm
    wr = jnp.pad(w_router, ((0, 0), (0, LANES - N_EXPERTS)))
    br = jnp.pad(b_router, (0, LANES - N_EXPERTS)).reshape(1, LANES)
    tri = jnp.tril(jnp.ones((tm, tm), F32), -1).astype(BF16)
    tok_spec = pl.BlockSpec((tm, LANES), lambda i: (i, 0))
    return pl.pallas_call(
        _route_kernel,
        grid=(n // tm,),
        in_specs=[pl.BlockSpec((tm, d), lambda i: (i, 0)),
                  pl.BlockSpec((1, d), lambda i: (0, 0)),
                  pl.BlockSpec((1, 1, d), lambda i: (i // tpb, 0, 0)),
                  pl.BlockSpec((1, 1, d), lambda i: (i // tpb, 0, 0)),
                  pl.BlockSpec((d, LANES), lambda i: (0, 0)),
                  pl.BlockSpec((1, LANES), lambda i: (0, 0)),
                  pl.BlockSpec((tm, tm), lambda i: (0, 0))],
        out_specs=[pl.BlockSpec((tm, d), lambda i: (i, 0)), tok_spec, tok_spec, tok_spec,
                   pl.BlockSpec((1, LANES), lambda i: (0, 0))],
        out_shape=[jax.ShapeDtypeStruct((n, d), F32), jax.ShapeDtypeStruct((n, LANES), I32),
                   jax.ShapeDtypeStruct((n, LANES), F32), jax.ShapeDtypeStruct((n, LANES), I32),
                   jax.ShapeDtypeStruct((1, LANES), F32)],
        scratch_shapes=[pltpu.VMEM((1, LANES), F32)],
        compiler_params=_cparams(("arbitrary",)),
        name="moe_route",
    )(x2, g.reshape(1, d), scale[:, None, :], shift[:, None, :], wr, br, tri)


def _invert_kernel(lo_ref, hi_ref, pos_ref, src_ref, *, chunk):
    step = pl.program_id(0)

    @pl.when(step == 0)
    def _():
        def fill_group(g, carry):
            def zero(p, c):
                src_ref[p] = 0
                return c
            lax.fori_loop(lo_ref[g], hi_ref[g], zero, 0)
            return carry
        lax.fori_loop(0, lo_ref.shape[0], fill_group, 0)

    def put(j, carry):
        src_ref[pos_ref[j]] = (step * chunk + j) // MOE_TOPK
        return carry

    lax.fori_loop(0, chunk, put, 0, unroll=8)


def _moe_invert(pos, pad_lo, pad_hi, n_rows, chunk=2048):
    n_assign = pos.shape[0]
    return pl.pallas_call(
        functools.partial(_invert_kernel, chunk=chunk),
        grid_spec=pltpu.PrefetchScalarGridSpec(
            num_scalar_prefetch=2, grid=(n_assign // chunk,),
            in_specs=[pl.BlockSpec((chunk,), lambda i, lo, hi: (i,), memory_space=pltpu.SMEM)],
            out_specs=pl.BlockSpec((n_rows,), lambda i, lo, hi: (0,), memory_space=pltpu.SMEM)),
        out_shape=jax.ShapeDtypeStruct((n_rows,), I32),
        compiler_params=_cparams(("arbitrary",), unchecked=True),
        name="moe_invert",
    )(pad_lo, pad_hi, pos)


def _expert_kernel(te_ref, nt_ref, src_ref, h_ref, wi_ref, bi_ref, perm_ref, wo_ref, bo_ref, y_ref,
                   xbuf_ref, xb_ref, sem):
    i = pl.program_id(0)
    n_used = nt_ref[0]
    tm = xbuf_ref.shape[1]

    def row_copy(row, slot, t):
        return pltpu.make_async_copy(h_ref.at[pl.ds(src_ref[row], 1)], xbuf_ref.at[slot, pl.ds(t, 1)], sem.at[slot])

    def wait_slot(slot):
        pltpu.make_async_copy(h_ref.at[pl.ds(0, tm)], xbuf_ref.at[slot], sem.at[slot]).wait()

    @pl.when(i == 0)
    def _():
        def issue(t, carry):
            row_copy(t, 0, t).start()
            return carry
        lax.fori_loop(0, tm, issue, 0, unroll=8)

    @pl.when(i < n_used)
    def _():
        slot = i % 2
        wait_slot(slot)
        xb_ref[...] = xbuf_ref[slot].astype(BF16)
        nxt = jnp.minimum(i + 1, n_used - 1) * tm
        for t in range(tm):
            row_copy(nxt + t, 1 - slot, t).start()
        x = xb_ref[...]
        hh = (jnp.dot(x, wi_ref[0], preferred_element_type=F32) + bi_ref[0]).astype(BF16)
        hp = jnp.dot(hh, perm_ref[...], preferred_element_type=F32)
        x_glu = jnp.minimum(hp[:, :EXPERT_FF], SWIGLU_LIMIT)
        x_lin = jnp.clip(hp[:, EXPERT_FF:], -SWIGLU_LIMIT, SWIGLU_LIMIT)
        act = x_glu * (1.0 / (1.0 + jnp.exp(-SWIGLU_ALPHA * x_glu))) * (x_lin + 1.0)
        y_ref[...] = jnp.dot(act.astype(BF16), wo_ref[0], preferred_element_type=F32) + bo_ref[0]

        @pl.when(i == n_used - 1)
        def _():
            wait_slot(1 - slot)

    @pl.when(i >= n_used)
    def _():
        y_ref[...] = jnp.zeros_like(y_ref)


def _moe_experts(h2, src, tile_expert, n_tiles_used, w_in, b_in, w_out, b_out):
    n, d = h2.shape
    r = src.shape[0]
    tm = MOE_ROW_TILE
    f2 = w_in.shape[2]
    ff = w_out.shape[1]
    col = np.arange(f2)
    perm = np.zeros((f2, f2), np.float32)
    perm[col, np.where(col % 2 == 0, col // 2, ff + col // 2)] = 1.0
    grid_spec = pltpu.PrefetchScalarGridSpec(
        num_scalar_prefetch=3, grid=(r // tm,),
        in_specs=[pl.BlockSpec(memory_space=pl.ANY),
                  pl.BlockSpec((1, d, f2), lambda i, te, nt, src: (te[i], 0, 0)),
                  pl.BlockSpec((1, 1, f2), lambda i, te, nt, src: (te[i], 0, 0)),
                  pl.BlockSpec((f2, f2), lambda i, te, nt, src: (0, 0)),
                  pl.BlockSpec((1, ff, d), lambda i, te, nt, src: (te[i], 0, 0)),
                  pl.BlockSpec((1, 1, d), lambda i, te, nt, src: (te[i], 0, 0))],
        out_specs=pl.BlockSpec((tm, d), lambda i, te, nt, src: (i, 0)),
        scratch_shapes=[pltpu.VMEM((2, tm, d), F32), pltpu.VMEM((tm, d), BF16), pltpu.SemaphoreType.DMA((2,))])
    return pl.pallas_call(
        _expert_kernel, grid_spec=grid_spec,
        out_shape=jax.ShapeDtypeStruct((r, d), F32),
        compiler_params=_cparams(("arbitrary",), unchecked=True),
        name="moe_experts",
    )(tile_expert, n_tiles_used, src, h2, w_in, b_in, jnp.asarray(perm, BF16), w_out, b_out)


def _combine_kernel(pos_ref, y_ref, x_ref, w_ref, g_ref, o_ref, buf_ref, sem, *, tm):
    i = pl.program_id(0)
    n_steps = pl.num_programs(0)

    def gather(tile, slot):
        base = tile * tm * MOE_TOPK

        def issue(t, carry):
            for k in range(MOE_TOPK):
                pltpu.make_async_copy(y_ref.at[pl.ds(pos_ref[base + t * MOE_TOPK + k], 1)],
                                      buf_ref.at[slot, k, pl.ds(t, 1)], sem.at[slot]).start()
            return carry
        lax.fori_loop(0, tm, issue, 0, unroll=2)

    @pl.when(i == 0)
    def _():
        gather(0, 0)

    @pl.when(i + 1 < n_steps)
    def _():
        gather(i + 1, (i + 1) % 2)

    slot = i % 2
    for k in range(MOE_TOPK):
        pltpu.make_async_copy(y_ref.at[pl.ds(0, tm)], buf_ref.at[slot, k], sem.at[slot]).wait()
    w = w_ref[...]
    mix = buf_ref[slot, 0] * w[:, 0:1]
    for k in range(1, MOE_TOPK):
        mix = mix + buf_ref[slot, k] * w[:, k:k + 1]
    o_ref[...] = x_ref[...] + g_ref[0] * mix


def _moe_combine(y, pos, x2, wgt, gate, rows_per_batch, tm=128):
    n, d = x2.shape
    tpb = rows_per_batch // tm
    grid_spec = pltpu.PrefetchScalarGridSpec(
        num_scalar_prefetch=1, grid=(n // tm,),
        in_specs=[pl.BlockSpec(memory_space=pl.ANY),
                  pl.BlockSpec((tm, d), lambda i, pos: (i, 0)),
                  pl.BlockSpec((tm, LANES), lambda i, pos: (i, 0)),
                  pl.BlockSpec((1, 1, d), lambda i, pos: (i // tpb, 0, 0))],
        out_specs=pl.BlockSpec((tm, d), lambda i, pos: (i, 0)),
        scratch_shapes=[pltpu.VMEM((2, MOE_TOPK, tm, d), F32), pltpu.SemaphoreType.DMA((2,))])
    return pl.pallas_call(
        functools.partial(_combine_kernel, tm=tm), grid_spec=grid_spec,
        out_shape=jax.ShapeDtypeStruct((n, d), F32),
        compiler_params=_cparams(("arbitrary",), unchecked=True),
        name="moe_combine",
    )(pos, y, x2, wgt, gate[:, None, :])


def _moe_block(x2, g, scale, shift, gate, w_router, b_router, w_exp_in, b_exp_in, w_exp_out, b_exp_out,
               rows_per_batch):
    n, d = x2.shape
    tm = MOE_ROW_TILE
    h2, idx, wgt, rank, counts = _moe_route(x2, g, scale, shift, w_router, b_router, rows_per_batch)
    cnt = counts[0, :N_EXPERTS].astype(I32)
    padded = (cnt + tm - 1) // tm * tm
    ends = jnp.cumsum(padded)
    starts = ends - padded
    n_rows = n * MOE_TOPK + N_EXPERTS * tm
    n_tiles = n_rows // tm
    tile_start = jnp.arange(n_tiles, dtype=I32) * tm
    tile_expert = jnp.minimum(jnp.sum((ends[None, :] <= tile_start[:, None]).astype(I32), axis=1), N_EXPERTS - 1)
    n_tiles_used = (ends[-1] // tm).astype(I32).reshape(1)
    e_flat = idx[:, :MOE_TOPK].reshape(-1)
    pos = (starts[e_flat] + rank[:, :MOE_TOPK].reshape(-1)).astype(I32)
    pad_lo = jnp.concatenate([starts + cnt, ends[-1:]]).astype(I32)
    pad_hi = jnp.concatenate([ends, jnp.full((1,), n_rows, I32)]).astype(I32)
    src = _moe_invert(pos, pad_lo, pad_hi, n_rows)
    y = _moe_experts(h2, src, tile_expert, n_tiles_used, w_exp_in.astype(BF16), b_exp_in[:, None, :],
                     w_exp_out.astype(BF16), b_exp_out[:, None, :])
    return _moe_combine(y, pos, x2, wgt, gate, rows_per_batch)


def _even_mixer(h, x2, gate, batch, s_len, w_in, g_cq, w_uq, w_iq, gq_a, gk_a, g_kidx, b_kidx, gq_b, gk_b,
                b_f, w_out):
    gw = GROUP_WIDTH
    o_cq, o_ka, o_va, o_ki, o_wi, o_qb, o_kb, o_vb, o_fb, o_gb = np.cumsum(
        [0, DSA_Q_LORA, HEAD_DIM, HEAD_DIM, HEAD_DIM, DSA_IDX_HEADS, gw, gw, gw, GROUP_HEADS]).tolist()
    cols = lambda o, wdt: w_in[:, o:o + wdt]
    w_main = jnp.concatenate([cols(o_qb, gw), cols(o_kb, gw), cols(o_vb, gw), cols(o_gb, gw),
                              cols(o_cq, DSA_Q_LORA), cols(o_ka, HEAD_DIM), cols(o_va, HEAD_DIM),
                              cols(o_ki, HEAD_DIM)], axis=1).astype(BF16)
    c_qb, c_kb, c_vb, c_gb, c_cq = 0, gw, 2 * gw, 3 * gw, 4 * gw
    c_ka = c_cq + DSA_Q_LORA
    c_va, c_ki = c_ka + HEAD_DIM, c_ka + 2 * HEAD_DIM
    d = w_in.shape[0]
    w_small = jnp.concatenate([cols(o_fb, GROUP_HEADS), cols(o_wi, DSA_IDX_HEADS),
                               jnp.zeros((d, LANES - GROUP_HEADS - DSA_IDX_HEADS), F32)], axis=1).astype(BF16)
    z = _matmul(h, w_main, BF16, "even_in_proj")
    zs = _matmul(h, w_small, F32, "even_in_proj_small")
    scale = Q_SCALE

    cq = _rowprep(z, c_cq, DSA_Q_LORA, batch, s_len, hw=DSA_Q_LORA, norm="rms", gain=g_cq, name="dsa_cq_norm")
    qa_raw = _matmul(cq, w_uq.astype(BF16), BF16, "dsa_q_up")
    qi_raw = _matmul(cq, w_iq.astype(BF16), BF16, "dsa_idx_q_up")
    qa = _rowprep(qa_raw, 0, gw, batch, s_len, norm="rms", gain=gq_a, rope_dim=HEAD_DIM, out_scale=scale,
                  head_major=True, name="dsa_q_prep")
    qi = _rowprep(qi_raw, 0, DSA_IDX_HEADS * HEAD_DIM, batch, s_len, rope_dim=DSA_IDX_ROPE, head_major=True,
                  name="dsa_idx_q_prep")
    ka = _rowprep(z, c_ka, HEAD_DIM, batch, s_len, norm="rms", gain=gk_a, rope_dim=HEAD_DIM, name="dsa_k_prep")
    ki = _rowprep(z, c_ki, HEAD_DIM, batch, s_len, norm="ln", gain=g_kidx, beta=b_kidx, rope_dim=DSA_IDX_ROPE,
                  name="dsa_idx_k_prep")
    va = z[:, c_va:c_va + HEAD_DIM]
    wi = jnp.pad(zs[:, GROUP_HEADS:GROUP_HEADS + DSA_IDX_HEADS] * (DSA_IDX_HEADS ** -0.5 * HEAD_DIM ** -0.5),
                 ((0, 0), (0, LANES - DSA_IDX_HEADS)))
    o_a = _dsa_attention(qi, ki, wi, qa, ka, va, batch, s_len)
    o_a = o_a.transpose(0, 2, 1, 3).reshape(batch * s_len, gw)

    qb = _rowprep(z, c_qb, gw, batch, s_len, norm="rms", gain=gq_b, out_scale=scale, head_major=True,
                  name="fox_q_prep")
    kb = _rowprep(z, c_kb, gw, batch, s_len, norm="rms", gain=gk_b, head_major=True, name="fox_k_prep")
    bias_row = jnp.pad(b_f.astype(F32), (0, LANES - GROUP_HEADS)).reshape(1, LANES)
    cum = _fox_cumsum(zs, bias_row, batch, s_len)
    o_b = _fox_attention(qb, kb, z, c_vb, c_gb, cum, batch, s_len)
    o_cat = jnp.concatenate([o_a, o_b], axis=1)
    return _matmul(o_cat, w_out.astype(BF16), F32, "even_out_proj", residual=x2, gate=gate,
                   rows_per_batch=s_len)


def _odd_mixer(h, x2, gate, batch, s_len, w_in, gq_c, gk_c, gq_d, gk_d, sinks, w_out):
    gw = GROUP_WIDTH
    kvw = SWA_KV_HEADS * HEAD_DIM
    c_qc, c_kc, c_vc, c_qd, c_kd = 0, gw, 2 * gw, 3 * gw, 4 * gw
    c_vd = c_kd + kvw
    z = _matmul(h, w_in.astype(BF16), BF16, "odd_in_proj")
    scale = Q_SCALE
    qc = _rowprep(z, c_qc, gw, batch, s_len, norm="rms", gain=gq_c, rope_dim=HEAD_DIM, out_scale=scale,
                  head_major=True, name="moba_q_prep")
    kc, kmean = _rowprep(z, c_kc, gw, batch, s_len, norm="rms", gain=gk_c, rope_dim=HEAD_DIM, head_major=True,
                         want_mean=True, name="moba_k_prep")
    o_c = _moba_attention(qc, kc, kmean, z, c_vc, batch, s_len)
    qd = _rowprep(z, c_qd, gw, batch, s_len, norm="rms", gain=gq_d, rope_dim=HEAD_DIM, out_scale=scale,
                  head_major=True, name="swa_q_prep")
    kd = _rowprep(z, c_kd, kvw, batch, s_len, norm="rms", gain=gk_d, rope_dim=HEAD_DIM, head_major=True,
                  name="swa_k_prep")
    o_d = _swa_attention(qd, kd, z, c_vd, sinks, batch, s_len)
    o_cat = jnp.concatenate([o_c, o_d], axis=1)
    return _matmul(o_cat, w_out.astype(BF16), F32, "odd_out_proj", residual=x2, gate=gate, rows_per_batch=s_len)


def kernel(x, c, g_norm_mix, g_norm_ffn, w_ada, b_ada, w_in_even, g_cq, w_uq, w_iq, gq_a, gk_a, g_kidx, b_kidx,
           gq_b, gk_b, b_forget, w_out_even, w_in_odd, gq_c, gk_c, gq_d, gk_d, sinks_d, w_out_odd,
           w_router, b_router, w_exp_in, b_exp_in, w_exp_out, b_exp_out):
    batch, s_len, d = x.shape
    depth = w_ada.shape[0]
    mod = _ada_modulation(c, w_ada, b_ada)
    x2 = x.reshape(batch * s_len, d)
    for layer in range(depth):
        shift_m, scale_m, gate_m, shift_f, scale_f, gate_f = [mod[layer, :, i * d:(i + 1) * d] for i in range(6)]
        h = _norm_modulate(x2, g_norm_mix[layer], scale_m, shift_m, s_len)
        j = layer // 2
        if layer % 2 == 0:
            x2 = _even_mixer(h, x2, gate_m, batch, s_len, w_in_even[j], g_cq[j], w_uq[j], w_iq[j], gq_a[j],
                             gk_a[j], g_kidx[j], b_kidx[j], gq_b[j], gk_b[j], b_forget[j], w_out_even[j])
        else:
            x2 = _odd_mixer(h, x2, gate_m, batch, s_len, w_in_odd[j], gq_c[j], gk_c[j], gq_d[j], gk_d[j],
                            sinks_d[j], w_out_odd[j])
        x2 = _moe_block(x2, g_norm_ffn[layer], scale_f, shift_f, gate_f, w_router[layer], b_router[layer],
                        w_exp_in[layer], b_exp_in[layer], w_exp_out[layer], b_exp_out[layer], s_len)
    return x2.reshape(batch, s_len, d)
```

```python
import functools

import numpy as np
import jax
import jax.numpy as jnp
from jax import lax
from jax.experimental import pallas as pl
from jax.experimental.pallas import tpu as pltpu

F32 = jnp.float32
BF16 = jnp.bfloat16
I32 = jnp.int32

LANES = 128
HEAD_DIM = 128
GROUP_HEADS = 16
GROUP_WIDTH = GROUP_HEADS * HEAD_DIM
ROPE_THETA = 10000.0
EPS = 1e-6
DSA_Q_LORA = 1024
DSA_IDX_HEADS = 32
DSA_IDX_ROPE = 64
DSA_TOPK = 256
DSA_QBLK = 128
DSA_CHUNK = 256
DSA_STAGES = 8
DSA_HEAD_GROUP = 2
MOBA_BLOCK = 256
MOBA_TOPK = 3
SWA_KV_HEADS = 2
SWA_WINDOW = 128
N_EXPERTS = 32
MOE_TOPK = 4
EXPERT_FF = 512
SWIGLU_LIMIT = 7.0
SWIGLU_ALPHA = 1.702
MOE_ROW_TILE = 256
NEG_BIG = -1e30
LOG2_E = 1.4426950408889634
Q_SCALE = HEAD_DIM ** -0.5 * LOG2_E
VMEM_LIMIT = 56 * 1024 * 1024


def _cparams(semantics, vmem=VMEM_LIMIT, unchecked=False):
    return pltpu.CompilerParams(dimension_semantics=semantics, vmem_limit_bytes=vmem,
                                disable_bounds_checks=unchecked)


def _ada_kernel(cb_ref, w_ref, b_ref, o_ref):
    nb = cb_ref.shape[0]
    tn = w_ref.shape[2]
    for c in range(tn // LANES):
        w = w_ref[0, :, c * LANES:(c + 1) * LANES]
        for b in range(nb):
            o_ref[0, b:b + 1, c * LANES:(c + 1) * LANES] = (
                jnp.sum(w * cb_ref[b], axis=0, keepdims=True) + b_ref[0, :, c * LANES:(c + 1) * LANES])


def _ada_modulation(c, w_ada, b_ada, tn=512):
    depth, d, n6 = w_ada.shape
    nb = c.shape[0]
    cb = jnp.broadcast_to(jax.nn.silu(c)[:, :, None], (nb, d, LANES))
    return pl.pallas_call(
        _ada_kernel,
        grid=(depth, n6 // tn),
        in_specs=[pl.BlockSpec((nb, d, LANES), lambda l, j: (0, 0, 0)),
                  pl.BlockSpec((1, d, tn), lambda l, j: (l, 0, j)),
                  pl.BlockSpec((1, 1, tn), lambda l, j: (l, 0, j))],
        out_specs=pl.BlockSpec((1, nb, tn), lambda l, j: (l, 0, j)),
        out_shape=jax.ShapeDtypeStruct((depth, nb, n6), F32),
        compiler_params=_cparams(("parallel", "parallel")),
        name="ada_modulation",
    )(cb, w_ada, b_ada.reshape(depth, 1, n6))


def _normmod_kernel(x_ref, g_ref, sc_ref, sh_ref, o_ref):
    x = x_ref[...]
    y = x * lax.rsqrt(jnp.mean(x * x, axis=-1, keepdims=True) + EPS) * g_ref[...]
    o_ref[...] = (y * (1.0 + sc_ref[0]) + sh_ref[0]).astype(o_ref.dtype)


def _norm_modulate(x2, g, scale, shift, rows_per_batch, tm=256):
    n, d = x2.shape
    tpb = rows_per_batch // tm
    return pl.pallas_call(
        _normmod_kernel,
        grid=(n // tm,),
        in_specs=[pl.BlockSpec((tm, d), lambda i: (i, 0)),
                  pl.BlockSpec((1, d), lambda i: (0, 0)),
                  pl.BlockSpec((1, 1, d), lambda i: (i // tpb, 0, 0)),
                  pl.BlockSpec((1, 1, d), lambda i: (i // tpb, 0, 0))],
        out_specs=pl.BlockSpec((tm, d), lambda i: (i, 0)),
        out_shape=jax.ShapeDtypeStruct((n, d), BF16),
        compiler_params=_cparams(("parallel",)),
        name="norm_modulate",
    )(x2, g.reshape(1, d), scale[:, None, :], shift[:, None, :])


def _mm_kernel(a_ref, b_ref, o_ref):
    o_ref[...] = jnp.dot(a_ref[...], b_ref[...], preferred_element_type=F32).astype(o_ref.dtype)


def _mm_residual_kernel(a_ref, b_ref, x_ref, g_ref, o_ref):
    o_ref[...] = x_ref[...] + g_ref[0] * jnp.dot(a_ref[...], b_ref[...], preferred_element_type=F32)


def _pick(n, prefs):
    for t in prefs:
        if n % t == 0:
            return t
    return n


def _matmul(a, b, out_dtype, name, residual=None, gate=None, rows_per_batch=None):
    m, k = a.shape
    n = b.shape[1]
    tm = _pick(m, (1024, 512, 256, 128))
    tn = _pick(n, (512, 640, 384, 256, 128) if k > 2048 else (1024, 512, 640, 384, 256, 128))
    grid = (m // tm, n // tn)
    a_spec = pl.BlockSpec((tm, k), lambda i, j: (i, 0))
    b_spec = pl.BlockSpec((k, tn), lambda i, j: (0, j))
    o_spec = pl.BlockSpec((tm, tn), lambda i, j: (i, j))
    params = _cparams(("parallel", "arbitrary"))
    if residual is None:
        return pl.pallas_call(
            _mm_kernel, grid=grid, in_specs=[a_spec, b_spec], out_specs=o_spec,
            out_shape=jax.ShapeDtypeStruct((m, n), out_dtype), compiler_params=params, name=name)(a, b)
    tpb = rows_per_batch // tm
    return pl.pallas_call(
        _mm_residual_kernel, grid=grid,
        in_specs=[a_spec, b_spec, o_spec, pl.BlockSpec((1, 1, tn), lambda i, j: (i // tpb, 0, j))],
        out_specs=o_spec, out_shape=jax.ShapeDtypeStruct((m, n), F32),
        compiler_params=params, name=name)(a, b, residual, gate[:, None, :])


def _rope_tables(s_len, rot_dim):
    half = rot_dim // 2
    inv_freq = ROPE_THETA ** (-jnp.arange(half, dtype=F32) / half)
    ang = jnp.arange(s_len, dtype=F32)[:, None] * inv_freq[None, :]
    cos, sin = jnp.cos(ang), jnp.sin(ang)
    pad = HEAD_DIM - rot_dim
    ones = jnp.ones((s_len, pad), F32)
    zeros = jnp.zeros((s_len, pad), F32)
    zh = jnp.zeros((s_len, half), F32)
    cos_t = jnp.concatenate([cos, cos, ones], axis=1)
    sa = jnp.concatenate([-sin, zh, zeros], axis=1)
    sb = jnp.concatenate([zh, sin, zeros], axis=1)
    return cos_t, sa, sb


def _rowprep_kernel(*refs, hw, norm, rope_half, out_scale, head_major, want_mean):
    it = iter(refs)
    z_ref = next(it)
    g_ref = next(it) if norm != "none" else None
    beta_ref = next(it) if norm == "ln" else None
    if rope_half:
        cos_ref, sa_ref, sb_ref = next(it), next(it), next(it)
    o_ref = next(it)
    mean_ref = next(it) if want_mean else None
    width = z_ref.shape[1]
    for h in range(width // hw):
        x = z_ref[:, h * hw:(h + 1) * hw].astype(F32)
        if norm == "rms":
            x = x * lax.rsqrt(jnp.mean(x * x, axis=-1, keepdims=True) + EPS) * g_ref[...]
        elif norm == "ln":
            mu = jnp.mean(x, axis=-1, keepdims=True)
            xc = x - mu
            x = xc * lax.rsqrt(jnp.mean(xc * xc, axis=-1, keepdims=True) + EPS) * g_ref[...] + beta_ref[...]
        if rope_half:
            if rope_half * 2 == hw:
                x = x * cos_ref[...] + pltpu.roll(x, rope_half, 1) * (sa_ref[...] + sb_ref[...])
            else:
                x = (x * cos_ref[...] + pltpu.roll(x, hw - rope_half, 1) * sa_ref[...]
                     + pltpu.roll(x, rope_half, 1) * sb_ref[...])
        if want_mean:
            mean_ref[0, 0, h:h + 1, :] = jnp.mean(x, axis=0, keepdims=True)
        if out_scale != 1.0:
            x = x * out_scale
        if head_major:
            o_ref[0, h] = x.astype(o_ref.dtype)
        else:
            o_ref[:, h * hw:(h + 1) * hw] = x.astype(o_ref.dtype)


def _rowprep(z, col0, width, batch, s_len, *, hw=HEAD_DIM, norm="none", gain=None, beta=None, rope_dim=0,
             out_scale=1.0, head_major=False, want_mean=False, name="rowprep", tm=256):
    n = z.shape[0]
    tpb = s_len // tm
    assert col0 % width == 0 and n == batch * s_len
    cb = col0 // width
    args = [z]
    specs = [pl.BlockSpec((tm, width), lambda i: (i, cb))]
    if norm != "none":
        args.append(gain.reshape(1, hw).astype(F32))
        specs.append(pl.BlockSpec((1, hw), lambda i: (0, 0)))
    if norm == "ln":
        args.append(beta.reshape(1, hw).astype(F32))
        specs.append(pl.BlockSpec((1, hw), lambda i: (0, 0)))
    if rope_dim:
        args += list(_rope_tables(s_len, rope_dim))
        specs += [pl.BlockSpec((tm, HEAD_DIM), lambda i: (i % tpb, 0))] * 3
    nh = width // hw
    if head_major:
        out_shape = [jax.ShapeDtypeStruct((batch, nh, s_len, hw), BF16)]
        out_specs = [pl.BlockSpec((1, nh, tm, hw), lambda i: (i // tpb, 0, i % tpb, 0))]
    else:
        out_shape = [jax.ShapeDtypeStruct((n, width), BF16)]
        out_specs = [pl.BlockSpec((tm, width), lambda i: (i, 0))]
    if want_mean:
        assert tm == MOBA_BLOCK
        out_shape.append(jax.ShapeDtypeStruct((batch, tpb, nh, hw), F32))
        out_specs.append(pl.BlockSpec((1, 1, nh, hw), lambda i: (i // tpb, i % tpb, 0, 0)))
    kern = functools.partial(_rowprep_kernel, hw=hw, norm=norm, rope_half=rope_dim // 2, out_scale=out_scale,
                             head_major=head_major, want_mean=want_mean)
    res = pl.pallas_call(kern, grid=(n // tm,), in_specs=specs, out_specs=out_specs, out_shape=out_shape,
                         compiler_params=_cparams(("parallel",)), name=name)(*args)
    return res if want_mean else res[0]


def _fox_cum_kernel(z_ref, bf_ref, tri_ref, o_ref, carry_ref):
    @pl.when(pl.program_id(1) == 0)
    def _():
        carry_ref[...] = jnp.zeros_like(carry_ref)

    xv = z_ref[...] + bf_ref[...]
    lf = jnp.minimum(xv, 0.0) - jnp.log(1.0 + jnp.exp(-jnp.abs(xv)))
    hi = lf.astype(BF16)
    r1 = lf - hi.astype(F32)
    mid = r1.astype(BF16)
    lo = (r1 - mid.astype(F32)).astype(BF16)
    tri = tri_ref[...]
    cs = (jnp.dot(tri, hi, preferred_element_type=F32) + jnp.dot(tri, mid, preferred_element_type=F32)
          + jnp.dot(tri, lo, preferred_element_type=F32)) + carry_ref[...]
    o_ref[...] = cs * LOG2_E
    tm = cs.shape[0]
    carry_ref[...] = cs[tm - 1:tm, :]


def _fox_cumsum(zs, bias_row, batch, s_len, tm=256):
    n = zs.shape[0]
    tpb = s_len // tm
    tri = jnp.tril(jnp.ones((tm, tm), F32)).astype(BF16)
    return pl.pallas_call(
        _fox_cum_kernel,
        grid=(batch, tpb),
        in_specs=[pl.BlockSpec((tm, LANES), lambda b, i: (b * tpb + i, 0)),
                  pl.BlockSpec((1, LANES), lambda b, i: (0, 0)),
                  pl.BlockSpec((tm, tm), lambda b, i: (0, 0))],
        out_specs=pl.BlockSpec((tm, LANES), lambda b, i: (b * tpb + i, 0)),
        out_shape=jax.ShapeDtypeStruct((n, LANES), F32),
        scratch_shapes=[pltpu.VMEM((1, LANES), F32)],
        compiler_params=_cparams(("parallel", "arbitrary")),
        name="fox_cumsum",
    )(zs, bias_row, tri)


def _dsa_kernel(qi_ref, kit_ref, wi_ref, qa_ref, kat_ref, va_ref, o_ref, wb_ref, key_ref, mb_ref,
                *, sk, q0, topk):
    q = DSA_QBLK
    ch = DSA_CHUNK
    nch = sk // ch
    n_idx = qi_ref.shape[1]
    n_heads = qa_ref.shape[1]
    t0 = (q0 + pl.program_id(1)) * q
    row = t0 + lax.broadcasted_iota(I32, (q, ch), 0)
    lane = lax.broadcasted_iota(I32, (q, ch), 1)

    wi = wi_ref[...]
    for h in range(n_idx):
        wb_ref[h] = jnp.broadcast_to(wi[:, h:h + 1], (q, ch))
    qi = qi_ref[0].reshape(n_idx * q, HEAD_DIM)

    def idx_chunk(c, carry):
        d = jnp.dot(qi, kit_ref[0, c], preferred_element_type=F32)
        acc = jnp.zeros((q, ch), F32)
        for h in range(n_idx):
            acc = acc + jnp.maximum(d[h * q:(h + 1) * q], 0.0) * wb_ref[h]
        score = jnp.where(c * ch + lane <= row, acc + 0.0, -jnp.inf)
        bits = lax.bitcast_convert_type(score, I32)
        key_ref[c] = jnp.where(bits >= 0, bits, bits ^ jnp.int32(0x7FFFFFFF))
        return carry

    lax.fori_loop(0, nch, idx_chunk, 0)

    def count_ge(cand):
        acc = jnp.zeros((q, LANES), F32)
        for c in range(nch):
            for part in range(ch // LANES):
                acc = acc + jnp.where(key_ref[c, :, part * LANES:(part + 1) * LANES] >= cand, 1.0, 0.0)
        return jnp.sum(acc, axis=-1, keepdims=True)

    int_min = jnp.int32(-2 ** 31)
    thr = jnp.where(count_ge(jnp.zeros((q, 1), I32)) >= topk, jnp.int32(0), int_min)

    def bit_step(j, thr):
        cand = thr | jnp.left_shift(jnp.int32(1), 30 - j)
        return jnp.where(count_ge(cand) >= topk, cand, thr)

    thr = lax.fori_loop(0, 31, bit_step, thr)

    for c in range(nch):
        keep = (key_ref[c] >= thr) & (c * ch + lane <= row)
        mb_ref[:, c * ch:(c + 1) * ch] = jnp.where(keep, 0.0, -jnp.inf)

    g = DSA_HEAD_GROUP

    def head_group(hg, carry):
        qh = qa_ref[0, pl.ds(hg * g, g)].reshape(g * q, HEAD_DIM)
        lg = jnp.dot(qh, kat_ref[0], preferred_element_type=F32).reshape(g, q, sk) + mb_ref[...][None]
        m = jnp.max(lg, axis=-1, keepdims=True)
        p = jnp.exp2(lg - m)
        l = jnp.sum(p, axis=-1, keepdims=True)
        o = jnp.dot(p.reshape(g * q, sk).astype(BF16), va_ref[0], preferred_element_type=F32)
        o_ref[0, pl.ds(hg * g, g)] = (o.reshape(g, q, HEAD_DIM) / l).astype(o_ref.dtype)
        return carry

    lax.fori_loop(0, n_heads // g, head_group, 0)


def _dsa_attention(qi, ki, wi, qa, ka, va, batch, s_len):
    q, ch = DSA_QBLK, DSA_CHUNK
    n_idx, n_heads = qi.shape[1], qa.shape[1]
    topk = min(DSA_TOPK, s_len // 4)
    nqb = s_len // q
    kit = ki.reshape(batch, s_len // ch, ch, HEAD_DIM).transpose(0, 1, 3, 2)
    kat = ka.reshape(batch, s_len, HEAD_DIM).transpose(0, 2, 1)
    va3 = va.reshape(batch, s_len, HEAD_DIM)
    per_stage = max(nqb // DSA_STAGES, ch // q)
    outs = []
    for q0 in range(0, nqb, per_stage):
        sk = (q0 + per_stage) * q
        nch = sk // ch
        kern = functools.partial(_dsa_kernel, sk=sk, q0=q0, topk=topk)
        outs.append(pl.pallas_call(
            kern,
            grid=(batch, per_stage),
            in_specs=[pl.BlockSpec((1, n_idx, q, HEAD_DIM), lambda b, i, q0=q0: (b, 0, q0 + i, 0)),
                      pl.BlockSpec((1, nch, HEAD_DIM, ch), lambda b, i: (b, 0, 0, 0)),
                      pl.BlockSpec((q, LANES), lambda b, i, q0=q0: (b * nqb + q0 + i, 0)),
                      pl.BlockSpec((1, n_heads, q, HEAD_DIM), lambda b, i, q0=q0: (b, 0, q0 + i, 0)),
                      pl.BlockSpec((1, HEAD_DIM, sk), lambda b, i: (b, 0, 0)),
                      pl.BlockSpec((1, sk, HEAD_DIM), lambda b, i: (b, 0, 0))],
            out_specs=pl.BlockSpec((1, n_heads, q, HEAD_DIM), lambda b, i: (b, 0, i, 0)),
            out_shape=jax.ShapeDtypeStruct((batch, n_heads, per_stage * q, HEAD_DIM), BF16),
            scratch_shapes=[pltpu.VMEM((n_idx, q, ch), F32),
                            pltpu.VMEM((nch, q, ch), I32),
                            pltpu.VMEM((q, sk), F32)],
            compiler_params=_cparams(("parallel", "parallel")),
            name=f"dsa_attention_k{sk}",
        )(qi, kit, wi, qa, kat, va3))
    return jnp.concatenate(outs, axis=2)


def _tri_tables(nq, ratio):
    qt, kt = [], []
    for i in range(nq):
        for j in range((i + 1) * ratio):
            qt.append(i)
            kt.append(j)
    return jnp.asarray(qt, I32), jnp.asarray(kt, I32)


def _fox_kernel(qt_ref, kt_ref, q_ref, k_ref, v_ref, cq_ref, ck_ref, g_ref, o_ref,
                m_ref, l_ref, acc_ref, cqs_ref, *, tq):
    h = pl.program_id(1)
    step = pl.program_id(2)
    qi = qt_ref[step]
    kj = kt_ref[step]

    @pl.when(kj == 0)
    def _():
        m_ref[...] = jnp.full_like(m_ref, -jnp.inf)
        l_ref[...] = jnp.zeros_like(l_ref)
        acc_ref[...] = jnp.zeros_like(acc_ref)
        lane = lax.broadcasted_iota(I32, cq_ref.shape, 1)
        cqs_ref[...] = jnp.sum(jnp.where(lane == h, cq_ref[...], 0.0), axis=-1, keepdims=True)

    s = lax.dot_general(q_ref[0, 0], k_ref[0, 0], (((1,), (1,)), ((), ())), preferred_element_type=F32)
    s = s + (cqs_ref[...] - ck_ref[0, 0])

    def update(s):
        m_prev = m_ref[...]
        m_new = jnp.maximum(m_prev, jnp.max(s, axis=-1, keepdims=True))
        alpha = jnp.exp2(m_prev - m_new)
        p = jnp.exp2(s - m_new)
        l_ref[...] = alpha * l_ref[...] + jnp.sum(p, axis=-1, keepdims=True)
        acc_ref[...] = alpha * acc_ref[...] + jnp.dot(p.astype(BF16), v_ref[...], preferred_element_type=F32)
        m_ref[...] = m_new

    @pl.when(kj < qi)
    def _():
        update(s)

    @pl.when(kj == qi)
    def _():
        row = lax.broadcasted_iota(I32, s.shape, 0)
        col = lax.broadcasted_iota(I32, s.shape, 1)
        update(jnp.where(col <= row, s, -jnp.inf))
        gate = g_ref[...].astype(F32)
        o_ref[...] = (acc_ref[...] / l_ref[...] * (1.0 / (1.0 + jnp.exp(-gate)))).astype(o_ref.dtype)


def _fox_attention(qb, kb, z, v_col0, g_col0, cum, batch, s_len, tq=1024):
    tq = min(tq, s_len)
    nq = s_len // tq
    n_heads = qb.shape[1]
    qt, kt = _tri_tables(nq, 1)
    cum_t = cum.reshape(batch, s_len, LANES)[:, :, :n_heads].transpose(0, 2, 1)[:, :, None, :]
    vb0, gb0 = v_col0 // HEAD_DIM, g_col0 // HEAD_DIM
    grid_spec = pltpu.PrefetchScalarGridSpec(
        num_scalar_prefetch=2,
        grid=(batch, n_heads, int(qt.shape[0])),
        in_specs=[pl.BlockSpec((1, 1, tq, HEAD_DIM), lambda b, h, s, qt, kt: (b, h, qt[s], 0)),
                  pl.BlockSpec((1, 1, tq, HEAD_DIM), lambda b, h, s, qt, kt: (b, h, kt[s], 0)),
                  pl.BlockSpec((tq, HEAD_DIM), lambda b, h, s, qt, kt: (b * nq + kt[s], vb0 + h)),
                  pl.BlockSpec((tq, LANES), lambda b, h, s, qt, kt: (b * nq + qt[s], 0)),
                  pl.BlockSpec((1, 1, 1, tq), lambda b, h, s, qt, kt: (b, h, 0, kt[s])),
                  pl.BlockSpec((tq, HEAD_DIM), lambda b, h, s, qt, kt: (b * nq + qt[s], gb0 + h))],
        out_specs=pl.BlockSpec((tq, HEAD_DIM), lambda b, h, s, qt, kt: (b * nq + qt[s], h)),
        scratch_shapes=[pltpu.VMEM((tq, 1), F32), pltpu.VMEM((tq, 1), F32),
                        pltpu.VMEM((tq, HEAD_DIM), F32), pltpu.VMEM((tq, 1), F32)])
    return pl.pallas_call(
        functools.partial(_fox_kernel, tq=tq), grid_spec=grid_spec,
        out_shape=jax.ShapeDtypeStruct((batch * s_len, n_heads * HEAD_DIM), BF16),
        compiler_params=_cparams(("parallel", "parallel", "arbitrary")),
        name="fox_attention",
    )(qt, kt, qb, kb, z, cum, cum_t, z)


def _moba_kernel(qt_ref, kt_ref, q_ref, k_ref, e_ref, v_ref, km_ref, o_ref, m_ref, l_ref, acc_ref, qa_ref,
                 *, tq, n_sel):
    step = pl.program_id(2)
    qi = qt_ref[step]
    kj = kt_ref[step]

    @pl.when(kj == 0)
    def _():
        m_ref[...] = jnp.full_like(m_ref, NEG_BIG)
        l_ref[...] = jnp.zeros_like(l_ref)
        acc_ref[...] = jnp.zeros_like(acc_ref)
        own = (qi * tq + lax.broadcasted_iota(I32, (tq, 1), 0)) // MOBA_BLOCK
        gate = jnp.dot(q_ref[0, 0].astype(F32), km_ref[0, 0], preferred_element_type=F32,
                       precision=lax.Precision.HIGHEST)
        lane = lax.broadcasted_iota(I32, gate.shape, 1)
        lane_f = lane.astype(F32)
        gate = jnp.where(lane < own, gate, -jnp.inf)
        allowed = jnp.where(lane == own, 1.0, 0.0)
        for _ in range(n_sel):
            best = jnp.max(gate, axis=-1, keepdims=True)
            first = jnp.min(jnp.where(gate == best, lane_f, float(LANES)), axis=-1, keepdims=True)
            pick = (lane_f == first) & (best > -jnp.inf)
            allowed = jnp.where(pick, 1.0, allowed)
            gate = jnp.where(pick, -jnp.inf, gate)
        qa_ref[:, :HEAD_DIM] = q_ref[0, 0]
        qa_ref[:, HEAD_DIM:] = jnp.where(allowed > 0.0, 0.0, NEG_BIG).astype(qa_ref.dtype)

    k_aug = jnp.concatenate([k_ref[0, 0], e_ref[...]], axis=1)
    s = lax.dot_general(qa_ref[...], k_aug, (((1,), (1,)), ((), ())), preferred_element_type=F32)

    def update(s):
        m_prev = m_ref[...]
        m_new = jnp.maximum(m_prev, jnp.max(s, axis=-1, keepdims=True))
        alpha = jnp.exp2(m_prev - m_new)
        p = jnp.exp2(s - m_new)
        l_ref[...] = alpha * l_ref[...] + jnp.sum(p, axis=-1, keepdims=True)
        acc_ref[...] = alpha * acc_ref[...] + jnp.dot(p.astype(BF16), v_ref[...], preferred_element_type=F32)
        m_ref[...] = m_new

    @pl.when(kj < qi)
    def _():
        update(s)

    @pl.when(kj == qi)
    def _():
        row = lax.broadcasted_iota(I32, s.shape, 0)
        col = lax.broadcasted_iota(I32, s.shape, 1)
        update(jnp.where(col <= row, s, NEG_BIG))
        o_ref[...] = (acc_ref[...] / l_ref[...]).astype(o_ref.dtype)


def _moba_attention(qc, kc, kmean, z, v_col0, batch, s_len, tq=1024):
    tq = min(tq, s_len)
    nq = s_len // tq
    n_heads = qc.shape[1]
    nb = s_len // MOBA_BLOCK
    assert nb <= LANES
    n_sel = min(MOBA_TOPK, nb - 1)
    qt, kt = _tri_tables(nq, 1)
    km_t = jnp.pad(kmean.transpose(0, 2, 3, 1), ((0, 0), (0, 0), (0, 0), (0, LANES - nb)))
    block_onehot = (jnp.arange(s_len, dtype=I32)[:, None] // MOBA_BLOCK
                    == jnp.arange(LANES, dtype=I32)[None, :]).astype(BF16)
    vb0 = v_col0 // HEAD_DIM
    grid_spec = pltpu.PrefetchScalarGridSpec(
        num_scalar_prefetch=2,
        grid=(batch, n_heads, int(qt.shape[0])),
        in_specs=[pl.BlockSpec((1, 1, tq, HEAD_DIM), lambda b, h, s, qt, kt: (b, h, qt[s], 0)),
                  pl.BlockSpec((1, 1, tq, HEAD_DIM), lambda b, h, s, qt, kt: (b, h, kt[s], 0)),
                  pl.BlockSpec((tq, LANES), lambda b, h, s, qt, kt: (kt[s], 0)),
                  pl.BlockSpec((tq, HEAD_DIM), lambda b, h, s, qt, kt: (b * nq + kt[s], vb0 + h)),
                  pl.BlockSpec((1, 1, HEAD_DIM, LANES), lambda b, h, s, qt, kt: (b, h, 0, 0))],
        out_specs=pl.BlockSpec((tq, HEAD_DIM), lambda b, h, s, qt, kt: (b * nq + qt[s], h)),
        scratch_shapes=[pltpu.VMEM((tq, 1), F32), pltpu.VMEM((tq, 1), F32),
                        pltpu.VMEM((tq, HEAD_DIM), F32), pltpu.VMEM((tq, HEAD_DIM + LANES), BF16)])
    return pl.pallas_call(
        functools.partial(_moba_kernel, tq=tq, n_sel=n_sel), grid_spec=grid_spec,
        out_shape=jax.ShapeDtypeStruct((batch * s_len, n_heads * HEAD_DIM), BF16),
        compiler_params=_cparams(("parallel", "parallel", "arbitrary")),
        name="moba_attention",
    )(qt, kt, qc, kc, block_onehot, z, km_t)


def _swa_kernel(q_ref, kp_ref, kc_ref, vp_ref, vc_ref, sink_ref, o_ref):
    n = pl.program_id(1)
    hq, w = q_ref.shape[1], q_ref.shape[2]
    hkv = kc_ref.shape[1]
    grp = hq // hkv
    ti = lax.broadcasted_iota(I32, (w, w), 0)
    si = lax.broadcasted_iota(I32, (w, w), 1)
    cur_ok = (si <= ti)[None]
    prev_ok = ((si > ti) & (n > 0))[None]
    nt = (((1,), (1,)), ((), ()))
    for kv in range(hkv):
        q = q_ref[0, kv * grp:(kv + 1) * grp].reshape(grp * w, HEAD_DIM)
        sc = lax.dot_general(q, kc_ref[0, kv], nt, preferred_element_type=F32).reshape(grp, w, w)
        sp = lax.dot_general(q, kp_ref[0, kv], nt, preferred_element_type=F32).reshape(grp, w, w)
        sc = jnp.where(cur_ok, sc, -jnp.inf)
        sp = jnp.where(prev_ok, sp, -jnp.inf)
        sink = sink_ref[kv * grp * w:(kv + 1) * grp * w].reshape(grp, w, LANES)[:, :, :1]
        m = jnp.maximum(jnp.maximum(jnp.max(sc, axis=-1, keepdims=True), jnp.max(sp, axis=-1, keepdims=True)), sink)
        pc = jnp.exp2(sc - m)
        pp = jnp.exp2(sp - m)
        den = jnp.sum(pc, axis=-1, keepdims=True) + jnp.sum(pp, axis=-1, keepdims=True) + jnp.exp2(sink - m)
        o = (jnp.dot(pc.reshape(grp * w, w).astype(BF16), vc_ref[:, kv * HEAD_DIM:(kv + 1) * HEAD_DIM],
                     preferred_element_type=F32)
             + jnp.dot(pp.reshape(grp * w, w).astype(BF16), vp_ref[:, kv * HEAD_DIM:(kv + 1) * HEAD_DIM],
                       preferred_element_type=F32))
        o = o.reshape(grp, w, HEAD_DIM) / den
        for gh in range(grp):
            hh = kv * grp + gh
            o_ref[:, hh * HEAD_DIM:(hh + 1) * HEAD_DIM] = o[gh].astype(o_ref.dtype)


def _swa_attention(qd, kd, z, v_col0, sinks, batch, s_len):
    w = SWA_WINDOW
    nb = s_len // w
    hq, hkv = qd.shape[1], kd.shape[1]
    vw = hkv * HEAD_DIM
    vb0 = v_col0 // vw
    sink_b = jnp.broadcast_to((sinks.astype(F32) * LOG2_E)[:, None, None], (hq, w, LANES)).reshape(hq * w, LANES)
    return pl.pallas_call(
        _swa_kernel,
        grid=(batch, nb),
        in_specs=[pl.BlockSpec((1, hq, w, HEAD_DIM), lambda b, n: (b, 0, n, 0)),
                  pl.BlockSpec((1, hkv, w, HEAD_DIM), lambda b, n: (b, 0, jnp.maximum(n - 1, 0), 0)),
                  pl.BlockSpec((1, hkv, w, HEAD_DIM), lambda b, n: (b, 0, n, 0)),
                  pl.BlockSpec((w, vw), lambda b, n: (b * nb + jnp.maximum(n - 1, 0), vb0)),
                  pl.BlockSpec((w, vw), lambda b, n: (b * nb + n, vb0)),
                  pl.BlockSpec((hq * w, LANES), lambda b, n: (0, 0))],
        out_specs=pl.BlockSpec((w, hq * HEAD_DIM), lambda b, n: (b * nb + n, 0)),
        out_shape=jax.ShapeDtypeStruct((batch * s_len, hq * HEAD_DIM), BF16),
        compiler_params=_cparams(("parallel", "parallel")),
        name="swa_attention",
    )(qd, kd, kd, z, z, sink_b)


_HIGH_HALF = -65536


def _pack_halves(x):
    half = x.shape[1] // 2
    lo = lax.bitcast_convert_type(x[:, :half].astype(jnp.bfloat16).astype(F32), I32)
    hi = lax.bitcast_convert_type(x[:, half:].astype(jnp.bfloat16).astype(F32), I32)
    return ((lo >> 16) & 0xFFFF) | (hi & _HIGH_HALF)


def _unpack_halves(p):
    lo = lax.bitcast_convert_type(p << 16, F32)
    hi = lax.bitcast_convert_type(p & _HIGH_HALF, F32)
    return lo, hi


def _route_kernel(x_ref, g_ref, sc_ref, sh_ref, wr_ref, br_ref, tri_ref,
                  h_ref, idx_ref, wgt_ref, rank_ref, cnt_ref, carry_ref):
    @pl.when(pl.program_id(0) == 0)
    def _():
        carry_ref[...] = jnp.zeros_like(carry_ref)

    x = x_ref[...]
    y = x * lax.rsqrt(jnp.mean(x * x, axis=-1, keepdims=True) + EPS) * g_ref[...]
    hmod = y * (1.0 + sc_ref[0]) + sh_ref[0]
    h_ref[...] = _pack_halves(hmod)
    logits = jnp.dot(hmod, wr_ref[...], preferred_element_type=F32, precision=lax.Precision.HIGHEST) + br_ref[...]
    lane = lax.broadcasted_iota(I32, logits.shape, 1)
    lane_f = lane.astype(F32)
    logits = jnp.where(lane < N_EXPERTS, logits, -jnp.inf)
    onehots, vals, firsts = [], [], []
    for _ in range(MOE_TOPK):
        best = jnp.max(logits, axis=-1, keepdims=True)
        first = jnp.min(jnp.where(logits == best, lane_f, float(LANES)), axis=-1, keepdims=True)
        pick = lane_f == first
        onehots.append(pick)
        vals.append(best)
        firsts.append(first.astype(I32))
        logits = jnp.where(pick, -jnp.inf, logits)
    exps = [jnp.exp(v - vals[0]) for v in vals]
    den = exps[0]
    for e in exps[1:]:
        den = den + e
    chosen_f = jnp.zeros(logits.shape, F32)
    for o in onehots:
        chosen_f = jnp.where(o, 1.0, chosen_f)
    before = jnp.dot(tri_ref[...], chosen_f.astype(BF16), preferred_element_type=F32) + carry_ref[...]
    idx_out = jnp.zeros(logits.shape, I32)
    wgt_out = jnp.zeros(logits.shape, F32)
    rank_out = jnp.zeros(logits.shape, I32)
    for k in range(MOE_TOPK):
        rk = jnp.sum(jnp.where(onehots[k], before, 0.0), axis=-1, keepdims=True).astype(I32)
        idx_out = jnp.where(lane == k, firsts[k], idx_out)
        wgt_out = jnp.where(lane == k, exps[k] / den, wgt_out)
        rank_out = jnp.where(lane == k, rk, rank_out)
    idx_ref[...] = idx_out
    wgt_ref[...] = wgt_out
    rank_ref[...] = rank_out
    carry_ref[...] = carry_ref[...] + jnp.sum(chosen_f, axis=0, keepdims=True)
    cnt_ref[...] = carry_ref[...]


def _moe_route(x2, g, scale, shift, w_router, b_router, rows_per_batch, tm=256):
    n, d = x2.shape
    tpb = rows_per_batch // tm
    wr = jnp.pad(w_router, ((0, 0), (0, LANES - N_EXPERTS)))
    br = jnp.pad(b_router, (0, LANES - N_EXPERTS)).reshape(1, LANES)
    tri = jnp.tril(jnp.ones((tm, tm), F32), -1).astype(BF16)
    tok_spec = pl.BlockSpec((tm, LANES), lambda i: (i, 0))
    return pl.pallas_call(
        _route_kernel,
        grid=(n // tm,),
        in_specs=[pl.BlockSpec((tm, d), lambda i: (i, 0)),
                  pl.BlockSpec((1, d), lambda i: (0, 0)),
                  pl.BlockSpec((1, 1, d), lambda i: (i // tpb, 0, 0)),
                  pl.BlockSpec((1, 1, d), lambda i: (i // tpb, 0, 0)),
                  pl.BlockSpec((d, LANES), lambda i: (0, 0)),
                  pl.BlockSpec((1, LANES), lambda i: (0, 0)),
                  pl.BlockSpec((tm, tm), lambda i: (0, 0))],
        out_specs=[pl.BlockSpec((tm, d // 2), lambda i: (i, 0)), tok_spec, tok_spec, tok_spec,
                   pl.BlockSpec((1, LANES), lambda i: (0, 0))],
        out_shape=[jax.ShapeDtypeStruct((n, d // 2), I32), jax.ShapeDtypeStruct((n, LANES), I32),
                   jax.ShapeDtypeStruct((n, LANES), F32), jax.ShapeDtypeStruct((n, LANES), I32),
                   jax.ShapeDtypeStruct((1, LANES), F32)],
        scratch_shapes=[pltpu.VMEM((1, LANES), F32)],
        compiler_params=_cparams(("arbitrary",)),
        name="moe_route",
    )(x2, g.reshape(1, d), scale[:, None, :], shift[:, None, :], wr, br, tri)


def _invert_kernel(lo_ref, hi_ref, pos_ref, src_ref, *, chunk):
    step = pl.program_id(0)

    @pl.when(step == 0)
    def _():
        def fill_group(g, carry):
            def zero(p, c):
                src_ref[p] = 0
                return c
            lax.fori_loop(lo_ref[g], hi_ref[g], zero, 0)
            return carry
        lax.fori_loop(0, lo_ref.shape[0], fill_group, 0)

    def put(j, carry):
        src_ref[pos_ref[j]] = (step * chunk + j) // MOE_TOPK
        return carry

    lax.fori_loop(0, chunk, put, 0, unroll=8)


def _moe_invert(pos, pad_lo, pad_hi, n_rows, chunk=2048):
    n_assign = pos.shape[0]
    return pl.pallas_call(
        functools.partial(_invert_kernel, chunk=chunk),
        grid_spec=pltpu.PrefetchScalarGridSpec(
            num_scalar_prefetch=2, grid=(n_assign // chunk,),
            in_specs=[pl.BlockSpec((chunk,), lambda i, lo, hi: (i,), memory_space=pltpu.SMEM)],
            out_specs=pl.BlockSpec((n_rows,), lambda i, lo, hi: (0,), memory_space=pltpu.SMEM)),
        out_shape=jax.ShapeDtypeStruct((n_rows,), I32),
        compiler_params=_cparams(("arbitrary",), unchecked=True),
        name="moe_invert",
    )(pad_lo, pad_hi, pos)


def _expert_kernel(te_ref, nt_ref, src_ref, h_ref, wi_ref, bi_ref, perm_ref, wo_ref, bo_ref, y_ref,
                   xbuf_ref, sem):
    i = pl.program_id(0)
    n_used = nt_ref[0]
    tm = xbuf_ref.shape[1]

    def row_copy(row, slot, t):
        return pltpu.make_async_copy(h_ref.at[pl.ds(src_ref[row], 1)], xbuf_ref.at[slot, pl.ds(t, 1)], sem.at[slot])

    def wait_slot(slot):
        pltpu.make_async_copy(h_ref.at[pl.ds(0, tm)], xbuf_ref.at[slot], sem.at[slot]).wait()

    def gather(tile, slot):
        def issue(t, carry):
            row_copy(tile * tm + t, slot, t).start()
            return carry
        lax.fori_loop(0, tm, issue, 0, unroll=8)

    @pl.when(i == 0)
    def _():
        gather(0, 0)

    @pl.when(i + 1 < n_used)
    def _():
        gather(i + 1, (i + 1) % 2)

    @pl.when(i < n_used)
    def _():
        slot = i % 2
        wait_slot(slot)
        x_lo, x_hi = _unpack_halves(xbuf_ref[slot])
        half = x_lo.shape[1]
        hh = (jnp.dot(x_lo.astype(BF16), wi_ref[0, :half], preferred_element_type=F32)
              + jnp.dot(x_hi.astype(BF16), wi_ref[0, half:], preferred_element_type=F32) + bi_ref[0]).astype(BF16)
        hp = jnp.dot(hh, perm_ref[...], preferred_element_type=F32)
        x_glu = jnp.minimum(hp[:, :EXPERT_FF], SWIGLU_LIMIT)
        x_lin = jnp.clip(hp[:, EXPERT_FF:], -SWIGLU_LIMIT, SWIGLU_LIMIT)
        act = x_glu * (1.0 / (1.0 + jnp.exp(-SWIGLU_ALPHA * x_glu))) * (x_lin + 1.0)
        y_ref[...] = _pack_halves(jnp.dot(act.astype(BF16), wo_ref[0], preferred_element_type=F32) + bo_ref[0])

    @pl.when(i >= n_used)
    def _():
        y_ref[...] = jnp.zeros_like(y_ref)


def _moe_experts(h2, src, tile_expert, n_tiles_used, w_in, b_in, w_out, b_out):
    n, dh = h2.shape
    d = 2 * dh
    r = src.shape[0]
    tm = MOE_ROW_TILE
    f2 = w_in.shape[2]
    ff = w_out.shape[1]
    col = np.arange(f2)
    perm = np.zeros((f2, f2), np.float32)
    perm[col, np.where(col % 2 == 0, col // 2, ff + col // 2)] = 1.0
    grid_spec = pltpu.PrefetchScalarGridSpec(
        num_scalar_prefetch=3, grid=(r // tm,),
        in_specs=[pl.BlockSpec(memory_space=pl.ANY),
                  pl.BlockSpec((1, d, f2), lambda i, te, nt, src: (te[i], 0, 0)),
                  pl.BlockSpec((1, 1, f2), lambda i, te, nt, src: (te[i], 0, 0)),
                  pl.BlockSpec((f2, f2), lambda i, te, nt, src: (0, 0)),
                  pl.BlockSpec((1, ff, d), lambda i, te, nt, src: (te[i], 0, 0)),
                  pl.BlockSpec((1, 1, d), lambda i, te, nt, src: (te[i], 0, 0))],
        out_specs=pl.BlockSpec((tm, dh), lambda i, te, nt, src: (i, 0)),
        scratch_shapes=[pltpu.VMEM((2, tm, dh), I32), pltpu.SemaphoreType.DMA((2,))])
    return pl.pallas_call(
        _expert_kernel, grid_spec=grid_spec,
        out_shape=jax.ShapeDtypeStruct((r, dh), I32),
        compiler_params=_cparams(("arbitrary",), unchecked=True),
        name="moe_experts",
    )(tile_expert, n_tiles_used, src, h2, w_in, b_in, jnp.asarray(perm, BF16), w_out, b_out)


def _combine_kernel(pos_ref, y_ref, x_ref, w_ref, g_ref, o_ref, buf_ref, wb_ref, sem, *, tm):
    i = pl.program_id(0)
    n_steps = pl.num_programs(0)

    def gather(tile, slot):
        base = tile * tm * MOE_TOPK

        def issue(t, carry):
            for k in range(MOE_TOPK):
                pltpu.make_async_copy(y_ref.at[pl.ds(pos_ref[base + t * MOE_TOPK + k], 1)],
                                      buf_ref.at[slot, k, pl.ds(t, 1)], sem.at[slot]).start()
            return carry
        lax.fori_loop(0, tm, issue, 0, unroll=2)

    @pl.when(i == 0)
    def _():
        gather(0, 0)

    @pl.when(i + 1 < n_steps)
    def _():
        gather(i + 1, (i + 1) % 2)

    slot = i % 2
    for k in range(MOE_TOPK):
        pltpu.make_async_copy(y_ref.at[pl.ds(0, tm)], buf_ref.at[slot, k], sem.at[slot]).wait()
    w = w_ref[...]
    half = buf_ref.shape[3]
    for k in range(MOE_TOPK):
        wb_ref[k] = jnp.broadcast_to(w[:, k:k + 1], (tm, LANES))
    for c in range(half // LANES):
        lo_cols = slice(c * LANES, (c + 1) * LANES)
        hi_cols = slice(half + c * LANES, half + (c + 1) * LANES)
        mix_lo = jnp.zeros((tm, LANES), F32)
        mix_hi = jnp.zeros((tm, LANES), F32)
        for k in range(MOE_TOPK):
            y_lo, y_hi = _unpack_halves(buf_ref[slot, k, :, lo_cols])
            mix_lo = mix_lo + y_lo * wb_ref[k]
            mix_hi = mix_hi + y_hi * wb_ref[k]
        o_ref[:, lo_cols] = x_ref[:, lo_cols] + g_ref[0, :, lo_cols] * mix_lo
        o_ref[:, hi_cols] = x_ref[:, hi_cols] + g_ref[0, :, hi_cols] * mix_hi


def _moe_combine(y, pos, x2, wgt, gate, rows_per_batch, tm=128):
    n, d = x2.shape
    tpb = rows_per_batch // tm
    grid_spec = pltpu.PrefetchScalarGridSpec(
        num_scalar_prefetch=1, grid=(n // tm,),
        in_specs=[pl.BlockSpec(memory_space=pl.ANY),
                  pl.BlockSpec((tm, d), lambda i, pos: (i, 0)),
                  pl.BlockSpec((tm, LANES), lambda i, pos: (i, 0)),
                  pl.BlockSpec((1, 1, d), lambda i, pos: (i // tpb, 0, 0))],
        out_specs=pl.BlockSpec((tm, d), lambda i, pos: (i, 0)),
        scratch_shapes=[pltpu.VMEM((2, MOE_TOPK, tm, d // 2), I32), pltpu.VMEM((MOE_TOPK, tm, LANES), F32),
                        pltpu.SemaphoreType.DMA((2,))])
    return pl.pallas_call(
        functools.partial(_combine_kernel, tm=tm), grid_spec=grid_spec,
        out_shape=jax.ShapeDtypeStruct((n, d), F32),
        compiler_params=_cparams(("arbitrary",), unchecked=True),
        name="moe_combine",
    )(pos, y, x2, wgt, gate[:, None, :])


def _moe_block(x2, g, scale, shift, gate, w_router, b_router, w_exp_in, b_exp_in, w_exp_out, b_exp_out,
               rows_per_batch):
    n, d = x2.shape
    tm = MOE_ROW_TILE
    h2, idx, wgt, rank, counts = _moe_route(x2, g, scale, shift, w_router, b_router, rows_per_batch)
    cnt = counts[0, :N_EXPERTS].astype(I32)
    padded = (cnt + tm - 1) // tm * tm
    ends = jnp.cumsum(padded)
    starts = ends - padded
    n_rows = n * MOE_TOPK + N_EXPERTS * tm
    n_tiles = n_rows // tm
    tile_start = jnp.arange(n_tiles, dtype=I32) * tm
    tile_expert = jnp.minimum(jnp.sum((ends[None, :] <= tile_start[:, None]).astype(I32), axis=1), N_EXPERTS - 1)
    n_tiles_used = (ends[-1] // tm).astype(I32).reshape(1)
    e_flat = idx[:, :MOE_TOPK].reshape(-1)
    pos = (starts[e_flat] + rank[:, :MOE_TOPK].reshape(-1)).astype(I32)
    pad_lo = jnp.concatenate([starts + cnt, ends[-1:]]).astype(I32)
    pad_hi = jnp.concatenate([ends, jnp.full((1,), n_rows, I32)]).astype(I32)
    src = _moe_invert(pos, pad_lo, pad_hi, n_rows)
    y = _moe_experts(h2, src, tile_expert, n_tiles_used, w_exp_in.astype(BF16), b_exp_in[:, None, :],
                     w_exp_out.astype(BF16), b_exp_out[:, None, :])
    return _moe_combine(y, pos, x2, wgt, gate, rows_per_batch)


def _even_mixer(h, x2, gate, batch, s_len, w_in, g_cq, w_uq, w_iq, gq_a, gk_a, g_kidx, b_kidx, gq_b, gk_b,
                b_f, w_out):
    gw = GROUP_WIDTH
    o_cq, o_ka, o_va, o_ki, o_wi, o_qb, o_kb, o_vb, o_fb, o_gb = np.cumsum(
        [0, DSA_Q_LORA, HEAD_DIM, HEAD_DIM, HEAD_DIM, DSA_IDX_HEADS, gw, gw, gw, GROUP_HEADS]).tolist()
    cols = lambda o, wdt: w_in[:, o:o + wdt]
    w_main = jnp.concatenate([cols(o_qb, gw), cols(o_kb, gw), cols(o_vb, gw), cols(o_gb, gw),
                              cols(o_cq, DSA_Q_LORA), cols(o_ka, HEAD_DIM), cols(o_va, HEAD_DIM),
                              cols(o_ki, HEAD_DIM)], axis=1).astype(BF16)
    c_qb, c_kb, c_vb, c_gb, c_cq = 0, gw, 2 * gw, 3 * gw, 4 * gw
    c_ka = c_cq + DSA_Q_LORA
    c_va, c_ki = c_ka + HEAD_DIM, c_ka + 2 * HEAD_DIM
    d = w_in.shape[0]
    w_small = jnp.concatenate([cols(o_fb, GROUP_HEADS), cols(o_wi, DSA_IDX_HEADS),
                               jnp.zeros((d, LANES - GROUP_HEADS - DSA_IDX_HEADS), F32)], axis=1).astype(BF16)
    z = _matmul(h, w_main, BF16, "even_in_proj")
    zs = _matmul(h, w_small, F32, "even_in_proj_small")
    scale = Q_SCALE

    cq = _rowprep(z, c_cq, DSA_Q_LORA, batch, s_len, hw=DSA_Q_LORA, norm="rms", gain=g_cq, name="dsa_cq_norm")
    qa_raw = _matmul(cq, w_uq.astype(BF16), BF16, "dsa_q_up")
    qi_raw = _matmul(cq, w_iq.astype(BF16), BF16, "dsa_idx_q_up")
    qa = _rowprep(qa_raw, 0, gw, batch, s_len, norm="rms", gain=gq_a, rope_dim=HEAD_DIM, out_scale=scale,
                  head_major=True, name="dsa_q_prep")
    qi = _rowprep(qi_raw, 0, DSA_IDX_HEADS * HEAD_DIM, batch, s_len, rope_dim=DSA_IDX_ROPE, head_major=True,
                  name="dsa_idx_q_prep")
    ka = _rowprep(z, c_ka, HEAD_DIM, batch, s_len, norm="rms", gain=gk_a, rope_dim=HEAD_DIM, name="dsa_k_prep")
    ki = _rowprep(z, c_ki, HEAD_DIM, batch, s_len, norm="ln", gain=g_kidx, beta=b_kidx, rope_dim=DSA_IDX_ROPE,
                  name="dsa_idx_k_prep")
    va = z[:, c_va:c_va + HEAD_DIM]
    wi = jnp.pad(zs[:, GROUP_HEADS:GROUP_HEADS + DSA_IDX_HEADS] * (DSA_IDX_HEADS ** -0.5 * HEAD_DIM ** -0.5),
                 ((0, 0), (0, LANES - DSA_IDX_HEADS)))
    o_a = _dsa_attention(qi, ki, wi, qa, ka, va, batch, s_len)
    o_a = o_a.transpose(0, 2, 1, 3).reshape(batch * s_len, gw)

    qb = _rowprep(z, c_qb, gw, batch, s_len, norm="rms", gain=gq_b, out_scale=scale, head_major=True,
                  name="fox_q_prep")
    kb = _rowprep(z, c_kb, gw, batch, s_len, norm="rms", gain=gk_b, head_major=True, name="fox_k_prep")
    bias_row = jnp.pad(b_f.astype(F32), (0, LANES - GROUP_HEADS)).reshape(1, LANES)
    cum = _fox_cumsum(zs, bias_row, batch, s_len)
    o_b = _fox_attention(qb, kb, z, c_vb, c_gb, cum, batch, s_len)
    o_cat = jnp.concatenate([o_a, o_b], axis=1)
    return _matmul(o_cat, w_out.astype(BF16), F32, "even_out_proj", residual=x2, gate=gate,
                   rows_per_batch=s_len)


def _odd_mixer(h, x2, gate, batch, s_len, w_in, gq_c, gk_c, gq_d, gk_d, sinks, w_out):
    gw = GROUP_WIDTH
    kvw = SWA_KV_HEADS * HEAD_DIM
    c_qc, c_kc, c_vc, c_qd, c_kd = 0, gw, 2 * gw, 3 * gw, 4 * gw
    c_vd = c_kd + kvw
    z = _matmul(h, w_in.astype(BF16), BF16, "odd_in_proj")
    scale = Q_SCALE
    qc = _rowprep(z, c_qc, gw, batch, s_len, norm="rms", gain=gq_c, rope_dim=HEAD_DIM, out_scale=scale,
                  head_major=True, name="moba_q_prep")
    kc, kmean = _rowprep(z, c_kc, gw, batch, s_len, norm="rms", gain=gk_c, rope_dim=HEAD_DIM, head_major=True,
                         want_mean=True, name="moba_k_prep")
    o_c = _moba_attention(qc, kc, kmean, z, c_vc, batch, s_len)
    qd = _rowprep(z, c_qd, gw, batch, s_len, norm="rms", gain=gq_d, rope_dim=HEAD_DIM, out_scale=scale,
                  head_major=True, name="swa_q_prep")
    kd = _rowprep(z, c_kd, kvw, batch, s_len, norm="rms", gain=gk_d, rope_dim=HEAD_DIM, head_major=True,
                  name="swa_k_prep")
    o_d = _swa_attention(qd, kd, z, c_vd, sinks, batch, s_len)
    o_cat = jnp.concatenate([o_c, o_d], axis=1)
    return _matmul(o_cat, w_out.astype(BF16), F32, "odd_out_proj", residual=x2, gate=gate, rows_per_batch=s_len)


def kernel(x, c, g_norm_mix, g_norm_ffn, w_ada, b_ada, w_in_even, g_cq, w_uq, w_iq, gq_a, gk_a, g_kidx, b_kidx,
           gq_b, gk_b, b_forget, w_out_even, w_in_odd, gq_c, gk_c, gq_d, gk_d, sinks_d, w_out_odd,
           w_router, b_router, w_exp_in, b_exp_in, w_exp_out, b_exp_out):
    batch, s_len, d = x.shape
    depth = w_ada.shape[0]
    mod = _ada_modulation(c, w_ada, b_ada)
    x2 = x.reshape(batch * s_len, d)
    for layer in range(depth):
        shift_m, scale_m, gate_m, shift_f, scale_f, gate_f = [mod[layer, :, i * d:(i + 1) * d] for i in range(6)]
        h = _norm_modulate(x2, g_norm_mix[layer], scale_m, shift_m, s_len)
        j = layer // 2
        if layer % 2 == 0:
            x2 = _even_mixer(h, x2, gate_m, batch, s_len, w_in_even[j], g_cq[j], w_uq[j], w_iq[j], gq_a[j],
                             gk_a[j], g_kidx[j], b_kidx[j], gq_b[j], gk_b[j], b_forget[j], w_out_even[j])
        else:
            x2 = _odd_mixer(h, x2, gate_m, batch, s_len, w_in_odd[j], gq_c[j], gk_c[j], gq_d[j], gk_d[j],
                            sinks_d[j], w_out_odd[j])
        x2 = _moe_block(x2, g_norm_ffn[layer], scale_f, shift_f, gate_f, w_router[layer], b_router[layer],
                        w_exp_in[layer], b_exp_in[layer], w_exp_out[layer], b_exp_out[layer], s_len)
    return x2.reshape(batch, s_len, d)
```

```python
import functools

import numpy as np
import jax
import jax.numpy as jnp
from jax import lax
from jax.experimental import pallas as pl
from jax.experimental.pallas import tpu as pltpu

F32 = jnp.float32
BF16 = jnp.bfloat16
I32 = jnp.int32

LANES = 128
HEAD_DIM = 128
GROUP_HEADS = 16
GROUP_WIDTH = GROUP_HEADS * HEAD_DIM
ROPE_THETA = 10000.0
EPS = 1e-6
DSA_Q_LORA = 1024
DSA_IDX_HEADS = 32
DSA_IDX_ROPE = 64
DSA_TOPK = 256
DSA_QBLK = 128
DSA_CHUNK = 256
DSA_STAGES = 8
DSA_HEAD_GROUP = 2
MOBA_BLOCK = 256
MOBA_TOPK = 3
SWA_KV_HEADS = 2
SWA_WINDOW = 128
N_EXPERTS = 32
MOE_TOPK = 4
EXPERT_FF = 512
SWIGLU_LIMIT = 7.0
SWIGLU_ALPHA = 1.702
MOE_ROW_TILE = 256
NEG_BIG = -1e30
LOG2_E = 1.4426950408889634
Q_SCALE = HEAD_DIM ** -0.5 * LOG2_E
VMEM_LIMIT = 56 * 1024 * 1024


def _cparams(semantics, vmem=VMEM_LIMIT, unchecked=False):
    return pltpu.CompilerParams(dimension_semantics=semantics, vmem_limit_bytes=vmem,
                                disable_bounds_checks=unchecked)


def _ada_kernel(cb_ref, w_ref, b_ref, o_ref):
    nb = cb_ref.shape[0]
    tn = w_ref.shape[2]
    for c in range(tn // LANES):
        w = w_ref[0, :, c * LANES:(c + 1) * LANES]
        for b in range(nb):
            o_ref[0, b:b + 1, c * LANES:(c + 1) * LANES] = (
                jnp.sum(w * cb_ref[b], axis=0, keepdims=True) + b_ref[0, :, c * LANES:(c + 1) * LANES])


def _ada_modulation(c, w_ada, b_ada, tn=512):
    depth, d, n6 = w_ada.shape
    nb = c.shape[0]
    cb = jnp.broadcast_to(jax.nn.silu(c)[:, :, None], (nb, d, LANES))
    return pl.pallas_call(
        _ada_kernel,
        grid=(depth, n6 // tn),
        in_specs=[pl.BlockSpec((nb, d, LANES), lambda l, j: (0, 0, 0)),
                  pl.BlockSpec((1, d, tn), lambda l, j: (l, 0, j)),
                  pl.BlockSpec((1, 1, tn), lambda l, j: (l, 0, j))],
        out_specs=pl.BlockSpec((1, nb, tn), lambda l, j: (l, 0, j)),
        out_shape=jax.ShapeDtypeStruct((depth, nb, n6), F32),
        compiler_params=_cparams(("parallel", "parallel")),
        name="ada_modulation",
    )(cb, w_ada, b_ada.reshape(depth, 1, n6))


def _normmod_kernel(x_ref, g_ref, sc_ref, sh_ref, o_ref):
    x = x_ref[...]
    y = x * lax.rsqrt(jnp.mean(x * x, axis=-1, keepdims=True) + EPS) * g_ref[...]
    o_ref[...] = (y * (1.0 + sc_ref[0]) + sh_ref[0]).astype(o_ref.dtype)


def _norm_modulate(x2, g, scale, shift, rows_per_batch, tm=256):
    n, d = x2.shape
    tpb = rows_per_batch // tm
    return pl.pallas_call(
        _normmod_kernel,
        grid=(n // tm,),
        in_specs=[pl.BlockSpec((tm, d), lambda i: (i, 0)),
                  pl.BlockSpec((1, d), lambda i: (0, 0)),
                  pl.BlockSpec((1, 1, d), lambda i: (i // tpb, 0, 0)),
                  pl.BlockSpec((1, 1, d), lambda i: (i // tpb, 0, 0))],
        out_specs=pl.BlockSpec((tm, d), lambda i: (i, 0)),
        out_shape=jax.ShapeDtypeStruct((n, d), BF16),
        compiler_params=_cparams(("parallel",)),
        name="norm_modulate",
    )(x2, g.reshape(1, d), scale[:, None, :], shift[:, None, :])


def _mm_kernel(a_ref, b_ref, o_ref):
    o_ref[...] = jnp.dot(a_ref[...], b_ref[...], preferred_element_type=F32).astype(o_ref.dtype)


def _mm_residual_kernel(a_ref, b_ref, x_ref, g_ref, o_ref):
    o_ref[...] = x_ref[...] + g_ref[0] * jnp.dot(a_ref[...], b_ref[...], preferred_element_type=F32)


def _pick(n, prefs):
    for t in prefs:
        if n % t == 0:
            return t
    return n


def _matmul(a, b, out_dtype, name, residual=None, gate=None, rows_per_batch=None):
    m, k = a.shape
    n = b.shape[1]
    tm = _pick(m, (1024, 512, 256, 128))
    tn = _pick(n, (512, 640, 384, 256, 128) if k > 2048 else (1024, 512, 640, 384, 256, 128))
    grid = (m // tm, n // tn)
    a_spec = pl.BlockSpec((tm, k), lambda i, j: (i, 0))
    b_spec = pl.BlockSpec((k, tn), lambda i, j: (0, j))
    o_spec = pl.BlockSpec((tm, tn), lambda i, j: (i, j))
    params = _cparams(("parallel", "arbitrary"))
    if residual is None:
        return pl.pallas_call(
            _mm_kernel, grid=grid, in_specs=[a_spec, b_spec], out_specs=o_spec,
            out_shape=jax.ShapeDtypeStruct((m, n), out_dtype), compiler_params=params, name=name)(a, b)
    tpb = rows_per_batch // tm
    return pl.pallas_call(
        _mm_residual_kernel, grid=grid,
        in_specs=[a_spec, b_spec, o_spec, pl.BlockSpec((1, 1, tn), lambda i, j: (i // tpb, 0, j))],
        out_specs=o_spec, out_shape=jax.ShapeDtypeStruct((m, n), F32),
        compiler_params=params, name=name)(a, b, residual, gate[:, None, :])


def _rope_tables(s_len, rot_dim):
    half = rot_dim // 2
    inv_freq = ROPE_THETA ** (-jnp.arange(half, dtype=F32) / half)
    ang = jnp.arange(s_len, dtype=F32)[:, None] * inv_freq[None, :]
    cos, sin = jnp.cos(ang), jnp.sin(ang)
    pad = HEAD_DIM - rot_dim
    ones = jnp.ones((s_len, pad), F32)
    zeros = jnp.zeros((s_len, pad), F32)
    zh = jnp.zeros((s_len, half), F32)
    cos_t = jnp.concatenate([cos, cos, ones], axis=1)
    sa = jnp.concatenate([-sin, zh, zeros], axis=1)
    sb = jnp.concatenate([zh, sin, zeros], axis=1)
    return cos_t, sa, sb


def _rowprep_kernel(*refs, hw, norm, rope_half, out_scale, head_major, want_mean):
    it = iter(refs)
    z_ref = next(it)
    g_ref = next(it) if norm != "none" else None
    beta_ref = next(it) if norm == "ln" else None
    if rope_half:
        cos_ref, sa_ref, sb_ref = next(it), next(it), next(it)
    o_ref = next(it)
    mean_ref = next(it) if want_mean else None
    width = z_ref.shape[1]
    for h in range(width // hw):
        x = z_ref[:, h * hw:(h + 1) * hw].astype(F32)
        if norm == "rms":
            x = x * lax.rsqrt(jnp.mean(x * x, axis=-1, keepdims=True) + EPS) * g_ref[...]
        elif norm == "ln":
            mu = jnp.mean(x, axis=-1, keepdims=True)
            xc = x - mu
            x = xc * lax.rsqrt(jnp.mean(xc * xc, axis=-1, keepdims=True) + EPS) * g_ref[...] + beta_ref[...]
        if rope_half:
            if rope_half * 2 == hw:
                x = x * cos_ref[...] + pltpu.roll(x, rope_half, 1) * (sa_ref[...] + sb_ref[...])
            else:
                x = (x * cos_ref[...] + pltpu.roll(x, hw - rope_half, 1) * sa_ref[...]
                     + pltpu.roll(x, rope_half, 1) * sb_ref[...])
        if want_mean:
            mean_ref[0, 0, h:h + 1, :] = jnp.mean(x, axis=0, keepdims=True)
        if out_scale != 1.0:
            x = x * out_scale
        if head_major:
            o_ref[0, h] = x.astype(o_ref.dtype)
        else:
            o_ref[:, h * hw:(h + 1) * hw] = x.astype(o_ref.dtype)


def _rowprep(z, col0, width, batch, s_len, *, hw=HEAD_DIM, norm="none", gain=None, beta=None, rope_dim=0,
             out_scale=1.0, head_major=False, want_mean=False, name="rowprep", tm=256):
    n = z.shape[0]
    tpb = s_len // tm
    assert col0 % width == 0 and n == batch * s_len
    cb = col0 // width
    args = [z]
    specs = [pl.BlockSpec((tm, width), lambda i: (i, cb))]
    if norm != "none":
        args.append(gain.reshape(1, hw).astype(F32))
        specs.append(pl.BlockSpec((1, hw), lambda i: (0, 0)))
    if norm == "ln":
        args.append(beta.reshape(1, hw).astype(F32))
        specs.append(pl.BlockSpec((1, hw), lambda i: (0, 0)))
    if rope_dim:
        args += list(_rope_tables(s_len, rope_dim))
        specs += [pl.BlockSpec((tm, HEAD_DIM), lambda i: (i % tpb, 0))] * 3
    nh = width // hw
    if head_major:
        out_shape = [jax.ShapeDtypeStruct((batch, nh, s_len, hw), BF16)]
        out_specs = [pl.BlockSpec((1, nh, tm, hw), lambda i: (i // tpb, 0, i % tpb, 0))]
    else:
        out_shape = [jax.ShapeDtypeStruct((n, width), BF16)]
        out_specs = [pl.BlockSpec((tm, width), lambda i: (i, 0))]
    if want_mean:
        assert tm == MOBA_BLOCK
        out_shape.append(jax.ShapeDtypeStruct((batch, tpb, nh, hw), F32))
        out_specs.append(pl.BlockSpec((1, 1, nh, hw), lambda i: (i // tpb, i % tpb, 0, 0)))
    kern = functools.partial(_rowprep_kernel, hw=hw, norm=norm, rope_half=rope_dim // 2, out_scale=out_scale,
                             head_major=head_major, want_mean=want_mean)
    res = pl.pallas_call(kern, grid=(n // tm,), in_specs=specs, out_specs=out_specs, out_shape=out_shape,
                         compiler_params=_cparams(("parallel",)), name=name)(*args)
    return res if want_mean else res[0]


def _fox_cum_kernel(z_ref, bf_ref, tri_ref, o_ref, carry_ref):
    @pl.when(pl.program_id(1) == 0)
    def _():
        carry_ref[...] = jnp.zeros_like(carry_ref)

    xv = z_ref[...] + bf_ref[...]
    lf = jnp.minimum(xv, 0.0) - jnp.log(1.0 + jnp.exp(-jnp.abs(xv)))
    hi = lf.astype(BF16)
    r1 = lf - hi.astype(F32)
    mid = r1.astype(BF16)
    lo = (r1 - mid.astype(F32)).astype(BF16)
    tri = tri_ref[...]
    cs = (jnp.dot(tri, hi, preferred_element_type=F32) + jnp.dot(tri, mid, preferred_element_type=F32)
          + jnp.dot(tri, lo, preferred_element_type=F32)) + carry_ref[...]
    o_ref[...] = cs * LOG2_E
    tm = cs.shape[0]
    carry_ref[...] = cs[tm - 1:tm, :]


def _fox_cumsum(zs, bias_row, batch, s_len, tm=256):
    n = zs.shape[0]
    tpb = s_len // tm
    tri = jnp.tril(jnp.ones((tm, tm), F32)).astype(BF16)
    return pl.pallas_call(
        _fox_cum_kernel,
        grid=(batch, tpb),
        in_specs=[pl.BlockSpec((tm, LANES), lambda b, i: (b * tpb + i, 0)),
                  pl.BlockSpec((1, LANES), lambda b, i: (0, 0)),
                  pl.BlockSpec((tm, tm), lambda b, i: (0, 0))],
        out_specs=pl.BlockSpec((tm, LANES), lambda b, i: (b * tpb + i, 0)),
        out_shape=jax.ShapeDtypeStruct((n, LANES), F32),
        scratch_shapes=[pltpu.VMEM((1, LANES), F32)],
        compiler_params=_cparams(("parallel", "arbitrary")),
        name="fox_cumsum",
    )(zs, bias_row, tri)


def _dsa_kernel(qi_ref, kit_ref, wi_ref, qa_ref, kat_ref, va_ref, o_ref, wb_ref, key_ref, mb_ref,
                *, sk, q0, topk):
    q = DSA_QBLK
    ch = DSA_CHUNK
    nch = sk // ch
    n_idx = qi_ref.shape[1]
    n_heads = qa_ref.shape[1]
    t0 = (q0 + pl.program_id(1)) * q
    row = t0 + lax.broadcasted_iota(I32, (q, ch), 0)
    lane = lax.broadcasted_iota(I32, (q, ch), 1)

    wi = wi_ref[...]
    for h in range(n_idx):
        wb_ref[h] = jnp.broadcast_to(wi[:, h:h + 1], (q, ch))
    qi = qi_ref[0].reshape(n_idx * q, HEAD_DIM)

    def idx_chunk(c, carry):
        d = jnp.dot(qi, kit_ref[0, c], preferred_element_type=F32)
        acc = jnp.zeros((q, ch), F32)
        for h in range(n_idx):
            acc = acc + jnp.maximum(d[h * q:(h + 1) * q], 0.0) * wb_ref[h]
        score = jnp.where(c * ch + lane <= row, acc + 0.0, -jnp.inf)
        bits = lax.bitcast_convert_type(score, I32)
        key_ref[c] = jnp.where(bits >= 0, bits, bits ^ jnp.int32(0x7FFFFFFF))
        return carry

    lax.fori_loop(0, nch, idx_chunk, 0)

    def count_ge(cand):
        acc = jnp.zeros((q, LANES), F32)
        for c in range(nch):
            for part in range(ch // LANES):
                acc = acc + jnp.where(key_ref[c, :, part * LANES:(part + 1) * LANES] >= cand, 1.0, 0.0)
        return jnp.sum(acc, axis=-1, keepdims=True)

    int_min = jnp.int32(-2 ** 31)
    thr = jnp.where(count_ge(jnp.zeros((q, 1), I32)) >= topk, jnp.int32(0), int_min)

    def bit_step(j, thr):
        cand = thr | jnp.left_shift(jnp.int32(1), 30 - j)
        return jnp.where(count_ge(cand) >= topk, cand, thr)

    thr = lax.fori_loop(0, 31, bit_step, thr)

    for c in range(nch):
        keep = (key_ref[c] >= thr) & (c * ch + lane <= row)
        mb_ref[:, c * ch:(c + 1) * ch] = jnp.where(keep, 0.0, -jnp.inf)

    g = DSA_HEAD_GROUP

    def head_group(hg, carry):
        qh = qa_ref[0, pl.ds(hg * g, g)].reshape(g * q, HEAD_DIM)
        lg = jnp.dot(qh, kat_ref[0], preferred_element_type=F32).reshape(g, q, sk) + mb_ref[...][None]
        m = jnp.max(lg, axis=-1, keepdims=True)
        p = jnp.exp2(lg - m)
        l = jnp.sum(p, axis=-1, keepdims=True)
        o = jnp.dot(p.reshape(g * q, sk).astype(BF16), va_ref[0], preferred_element_type=F32)
        o_ref[0, pl.ds(hg * g, g)] = (o.reshape(g, q, HEAD_DIM) / l).astype(o_ref.dtype)
        return carry

    lax.fori_loop(0, n_heads // g, head_group, 0)


def _dsa_attention(qi, ki, wi, qa, ka, va, batch, s_len):
    q, ch = DSA_QBLK, DSA_CHUNK
    n_idx, n_heads = qi.shape[1], qa.shape[1]
    topk = min(DSA_TOPK, s_len // 4)
    nqb = s_len // q
    kit = ki.reshape(batch, s_len // ch, ch, HEAD_DIM).transpose(0, 1, 3, 2)
    kat = ka.reshape(batch, s_len, HEAD_DIM).transpose(0, 2, 1)
    va3 = va.reshape(batch, s_len, HEAD_DIM)
    per_stage = max(nqb // DSA_STAGES, ch // q)
    outs = []
    for q0 in range(0, nqb, per_stage):
        sk = (q0 + per_stage) * q
        nch = sk // ch
        kern = functools.partial(_dsa_kernel, sk=sk, q0=q0, topk=topk)
        outs.append(pl.pallas_call(
            kern,
            grid=(batch, per_stage),
            in_specs=[pl.BlockSpec((1, n_idx, q, HEAD_DIM), lambda b, i, q0=q0: (b, 0, q0 + i, 0)),
                      pl.BlockSpec((1, nch, HEAD_DIM, ch), lambda b, i: (b, 0, 0, 0)),
                      pl.BlockSpec((q, LANES), lambda b, i, q0=q0: (b * nqb + q0 + i, 0)),
                      pl.BlockSpec((1, n_heads, q, HEAD_DIM), lambda b, i, q0=q0: (b, 0, q0 + i, 0)),
                      pl.BlockSpec((1, HEAD_DIM, sk), lambda b, i: (b, 0, 0)),
                      pl.BlockSpec((1, sk, HEAD_DIM), lambda b, i: (b, 0, 0))],
            out_specs=pl.BlockSpec((1, n_heads, q, HEAD_DIM), lambda b, i: (b, 0, i, 0)),
            out_shape=jax.ShapeDtypeStruct((batch, n_heads, per_stage * q, HEAD_DIM), BF16),
            scratch_shapes=[pltpu.VMEM((n_idx, q, ch), F32),
                            pltpu.VMEM((nch, q, ch), I32),
                            pltpu.VMEM((q, sk), F32)],
            compiler_params=_cparams(("parallel", "parallel")),
            name=f"dsa_attention_k{sk}",
        )(qi, kit, wi, qa, kat, va3))
    return jnp.concatenate(outs, axis=2)


def _tri_tables(nq, ratio):
    qt, kt = [], []
    for i in range(nq):
        for j in range((i + 1) * ratio):
            qt.append(i)
            kt.append(j)
    return jnp.asarray(qt, I32), jnp.asarray(kt, I32)


def _fox_kernel(qt_ref, kt_ref, q_ref, k_ref, v_ref, cq_ref, ck_ref, g_ref, o_ref,
                m_ref, l_ref, acc_ref, cqs_ref, *, tq):
    h = pl.program_id(1)
    step = pl.program_id(2)
    qi = qt_ref[step]
    kj = kt_ref[step]

    @pl.when(kj == 0)
    def _():
        m_ref[...] = jnp.full_like(m_ref, -jnp.inf)
        l_ref[...] = jnp.zeros_like(l_ref)
        acc_ref[...] = jnp.zeros_like(acc_ref)
        lane = lax.broadcasted_iota(I32, cq_ref.shape, 1)
        cqs_ref[...] = jnp.sum(jnp.where(lane == h, cq_ref[...], 0.0), axis=-1, keepdims=True)

    s = lax.dot_general(q_ref[0, 0], k_ref[0, 0], (((1,), (1,)), ((), ())), preferred_element_type=F32)
    s = s + (cqs_ref[...] - ck_ref[0, 0])

    def update(s):
        m_prev = m_ref[...]
        m_new = jnp.maximum(m_prev, jnp.max(s, axis=-1, keepdims=True))
        alpha = jnp.exp2(m_prev - m_new)
        p = jnp.exp2(s - m_new)
        l_ref[...] = alpha * l_ref[...] + jnp.sum(p, axis=-1, keepdims=True)
        acc_ref[...] = alpha * acc_ref[...] + jnp.dot(p.astype(BF16), v_ref[...], preferred_element_type=F32)
        m_ref[...] = m_new

    @pl.when(kj < qi)
    def _():
        update(s)

    @pl.when(kj == qi)
    def _():
        row = lax.broadcasted_iota(I32, s.shape, 0)
        col = lax.broadcasted_iota(I32, s.shape, 1)
        update(jnp.where(col <= row, s, -jnp.inf))
        gate = g_ref[...].astype(F32)
        o_ref[...] = (acc_ref[...] / l_ref[...] * (1.0 / (1.0 + jnp.exp(-gate)))).astype(o_ref.dtype)


def _fox_attention(qb, kb, z, v_col0, g_col0, cum, batch, s_len, tq=1024):
    tq = min(tq, s_len)
    nq = s_len // tq
    n_heads = qb.shape[1]
    qt, kt = _tri_tables(nq, 1)
    cum_t = cum.reshape(batch, s_len, LANES)[:, :, :n_heads].transpose(0, 2, 1)[:, :, None, :]
    vb0, gb0 = v_col0 // HEAD_DIM, g_col0 // HEAD_DIM
    grid_spec = pltpu.PrefetchScalarGridSpec(
        num_scalar_prefetch=2,
        grid=(batch, n_heads, int(qt.shape[0])),
        in_specs=[pl.BlockSpec((1, 1, tq, HEAD_DIM), lambda b, h, s, qt, kt: (b, h, qt[s], 0)),
                  pl.BlockSpec((1, 1, tq, HEAD_DIM), lambda b, h, s, qt, kt: (b, h, kt[s], 0)),
                  pl.BlockSpec((tq, HEAD_DIM), lambda b, h, s, qt, kt: (b * nq + kt[s], vb0 + h)),
                  pl.BlockSpec((tq, LANES), lambda b, h, s, qt, kt: (b * nq + qt[s], 0)),
                  pl.BlockSpec((1, 1, 1, tq), lambda b, h, s, qt, kt: (b, h, 0, kt[s])),
                  pl.BlockSpec((tq, HEAD_DIM), lambda b, h, s, qt, kt: (b * nq + qt[s], gb0 + h))],
        out_specs=pl.BlockSpec((tq, HEAD_DIM), lambda b, h, s, qt, kt: (b * nq + qt[s], h)),
        scratch_shapes=[pltpu.VMEM((tq, 1), F32), pltpu.VMEM((tq, 1), F32),
                        pltpu.VMEM((tq, HEAD_DIM), F32), pltpu.VMEM((tq, 1), F32)])
    return pl.pallas_call(
        functools.partial(_fox_kernel, tq=tq), grid_spec=grid_spec,
        out_shape=jax.ShapeDtypeStruct((batch * s_len, n_heads * HEAD_DIM), BF16),
        compiler_params=_cparams(("parallel", "parallel", "arbitrary")),
        name="fox_attention",
    )(qt, kt, qb, kb, z, cum, cum_t, z)


def _moba_kernel(qt_ref, kt_ref, q_ref, k_ref, e_ref, v_ref, km_ref, o_ref, m_ref, l_ref, acc_ref, qa_ref,
                 *, tq, n_sel):
    step = pl.program_id(2)
    qi = qt_ref[step]
    kj = kt_ref[step]

    @pl.when(kj == 0)
    def _():
        m_ref[...] = jnp.full_like(m_ref, NEG_BIG)
        l_ref[...] = jnp.zeros_like(l_ref)
        acc_ref[...] = jnp.zeros_like(acc_ref)
        own = (qi * tq + lax.broadcasted_iota(I32, (tq, 1), 0)) // MOBA_BLOCK
        gate = jnp.dot(q_ref[0, 0].astype(F32), km_ref[0, 0], preferred_element_type=F32,
                       precision=lax.Precision.HIGHEST)
        lane = lax.broadcasted_iota(I32, gate.shape, 1)
        lane_f = lane.astype(F32)
        gate = jnp.where(lane < own, gate, -jnp.inf)
        allowed = jnp.where(lane == own, 1.0, 0.0)
        for _ in range(n_sel):
            best = jnp.max(gate, axis=-1, keepdims=True)
            first = jnp.min(jnp.where(gate == best, lane_f, float(LANES)), axis=-1, keepdims=True)
            pick = (lane_f == first) & (best > -jnp.inf)
            allowed = jnp.where(pick, 1.0, allowed)
            gate = jnp.where(pick, -jnp.inf, gate)
        qa_ref[:, :HEAD_DIM] = q_ref[0, 0]
        qa_ref[:, HEAD_DIM:] = jnp.where(allowed > 0.0, 0.0, NEG_BIG).astype(qa_ref.dtype)

    k_aug = jnp.concatenate([k_ref[0, 0], e_ref[...]], axis=1)
    s = lax.dot_general(qa_ref[...], k_aug, (((1,), (1,)), ((), ())), preferred_element_type=F32)

    def update(s):
        m_prev = m_ref[...]
        m_new = jnp.maximum(m_prev, jnp.max(s, axis=-1, keepdims=True))
        alpha = jnp.exp2(m_prev - m_new)
        p = jnp.exp2(s - m_new)
        l_ref[...] = alpha * l_ref[...] + jnp.sum(p, axis=-1, keepdims=True)
        acc_ref[...] = alpha * acc_ref[...] + jnp.dot(p.astype(BF16), v_ref[...], preferred_element_type=F32)
        m_ref[...] = m_new

    @pl.when(kj < qi)
    def _():
        update(s)

    @pl.when(kj == qi)
    def _():
        row = lax.broadcasted_iota(I32, s.shape, 0)
        col = lax.broadcasted_iota(I32, s.shape, 1)
        update(jnp.where(col <= row, s, NEG_BIG))
        o_ref[...] = (acc_ref[...] / l_ref[...]).astype(o_ref.dtype)


def _moba_attention(qc, kc, kmean, z, v_col0, batch, s_len, tq=1024):
    tq = min(tq, s_len)
    nq = s_len // tq
    n_heads = qc.shape[1]
    nb = s_len // MOBA_BLOCK
    assert nb <= LANES
    n_sel = min(MOBA_TOPK, nb - 1)
    qt, kt = _tri_tables(nq, 1)
    km_t = jnp.pad(kmean.transpose(0, 2, 3, 1), ((0, 0), (0, 0), (0, 0), (0, LANES - nb)))
    block_onehot = (jnp.arange(s_len, dtype=I32)[:, None] // MOBA_BLOCK
                    == jnp.arange(LANES, dtype=I32)[None, :]).astype(BF16)
    vb0 = v_col0 // HEAD_DIM
    grid_spec = pltpu.PrefetchScalarGridSpec(
        num_scalar_prefetch=2,
        grid=(batch, n_heads, int(qt.shape[0])),
        in_specs=[pl.BlockSpec((1, 1, tq, HEAD_DIM), lambda b, h, s, qt, kt: (b, h, qt[s], 0)),
                  pl.BlockSpec((1, 1, tq, HEAD_DIM), lambda b, h, s, qt, kt: (b, h, kt[s], 0)),
                  pl.BlockSpec((tq, LANES), lambda b, h, s, qt, kt: (kt[s], 0)),
                  pl.BlockSpec((tq, HEAD_DIM), lambda b, h, s, qt, kt: (b * nq + kt[s], vb0 + h)),
                  pl.BlockSpec((1, 1, HEAD_DIM, LANES), lambda b, h, s, qt, kt: (b, h, 0, 0))],
        out_specs=pl.BlockSpec((tq, HEAD_DIM), lambda b, h, s, qt, kt: (b * nq + qt[s], h)),
        scratch_shapes=[pltpu.VMEM((tq, 1), F32), pltpu.VMEM((tq, 1), F32),
                        pltpu.VMEM((tq, HEAD_DIM), F32), pltpu.VMEM((tq, HEAD_DIM + LANES), BF16)])
    return pl.pallas_call(
        functools.partial(_moba_kernel, tq=tq, n_sel=n_sel), grid_spec=grid_spec,
        out_shape=jax.ShapeDtypeStruct((batch * s_len, n_heads * HEAD_DIM), BF16),
        compiler_params=_cparams(("parallel", "parallel", "arbitrary")),
        name="moba_attention",
    )(qt, kt, qc, kc, block_onehot, z, km_t)


def _swa_kernel(q_ref, kp_ref, kc_ref, vp_ref, vc_ref, sink_ref, o_ref):
    n = pl.program_id(1)
    hq, w = q_ref.shape[1], q_ref.shape[2]
    hkv = kc_ref.shape[1]
    grp = hq // hkv
    ti = lax.broadcasted_iota(I32, (w, w), 0)
    si = lax.broadcasted_iota(I32, (w, w), 1)
    cur_ok = (si <= ti)[None]
    prev_ok = ((si > ti) & (n > 0))[None]
    nt = (((1,), (1,)), ((), ()))
    for kv in range(hkv):
        q = q_ref[0, kv * grp:(kv + 1) * grp].reshape(grp * w, HEAD_DIM)
        sc = lax.dot_general(q, kc_ref[0, kv], nt, preferred_element_type=F32).reshape(grp, w, w)
        sp = lax.dot_general(q, kp_ref[0, kv], nt, preferred_element_type=F32).reshape(grp, w, w)
        sc = jnp.where(cur_ok, sc, -jnp.inf)
        sp = jnp.where(prev_ok, sp, -jnp.inf)
        sink = sink_ref[kv * grp * w:(kv + 1) * grp * w].reshape(grp, w, LANES)[:, :, :1]
        m = jnp.maximum(jnp.maximum(jnp.max(sc, axis=-1, keepdims=True), jnp.max(sp, axis=-1, keepdims=True)), sink)
        pc = jnp.exp2(sc - m)
        pp = jnp.exp2(sp - m)
        den = jnp.sum(pc, axis=-1, keepdims=True) + jnp.sum(pp, axis=-1, keepdims=True) + jnp.exp2(sink - m)
        o = (jnp.dot(pc.reshape(grp * w, w).astype(BF16), vc_ref[:, kv * HEAD_DIM:(kv + 1) * HEAD_DIM],
                     preferred_element_type=F32)
             + jnp.dot(pp.reshape(grp * w, w).astype(BF16), vp_ref[:, kv * HEAD_DIM:(kv + 1) * HEAD_DIM],
                       preferred_element_type=F32))
        o = o.reshape(grp, w, HEAD_DIM) / den
        for gh in range(grp):
            hh = kv * grp + gh
            o_ref[:, hh * HEAD_DIM:(hh + 1) * HEAD_DIM] = o[gh].astype(o_ref.dtype)


def _swa_attention(qd, kd, z, v_col0, sinks, batch, s_len):
    w = SWA_WINDOW
    nb = s_len // w
    hq, hkv = qd.shape[1], kd.shape[1]
    vw = hkv * HEAD_DIM
    vb0 = v_col0 // vw
    sink_b = jnp.broadcast_to((sinks.astype(F32) * LOG2_E)[:, None, None], (hq, w, LANES)).reshape(hq * w, LANES)
    return pl.pallas_call(
        _swa_kernel,
        grid=(batch, nb),
        in_specs=[pl.BlockSpec((1, hq, w, HEAD_DIM), lambda b, n: (b, 0, n, 0)),
                  pl.BlockSpec((1, hkv, w, HEAD_DIM), lambda b, n: (b, 0, jnp.maximum(n - 1, 0), 0)),
                  pl.BlockSpec((1, hkv, w, HEAD_DIM), lambda b, n: (b, 0, n, 0)),
                  pl.BlockSpec((w, vw), lambda b, n: (b * nb + jnp.maximum(n - 1, 0), vb0)),
                  pl.BlockSpec((w, vw), lambda b, n: (b * nb + n, vb0)),
                  pl.BlockSpec((hq * w, LANES), lambda b, n: (0, 0))],
        out_specs=pl.BlockSpec((w, hq * HEAD_DIM), lambda b, n: (b * nb + n, 0)),
        out_shape=jax.ShapeDtypeStruct((batch * s_len, hq * HEAD_DIM), BF16),
        compiler_params=_cparams(("parallel", "parallel")),
        name="swa_attention",
    )(qd, kd, kd, z, z, sink_b)


_HIGH_HALF = -65536


def _pack_halves(x):
    half = x.shape[1] // 2
    lo = lax.bitcast_convert_type(x[:, :half].astype(jnp.bfloat16).astype(F32), I32)
    hi = lax.bitcast_convert_type(x[:, half:].astype(jnp.bfloat16).astype(F32), I32)
    return ((lo >> 16) & 0xFFFF) | (hi & _HIGH_HALF)


def _unpack_halves(p):
    lo = lax.bitcast_convert_type(p << 16, F32)
    hi = lax.bitcast_convert_type(p & _HIGH_HALF, F32)
    return lo, hi


def _route_kernel(x_ref, g_ref, sc_ref, sh_ref, wr_ref, br_ref, tri_ref,
                  h_ref, idx_ref, wgt_ref, rank_ref, cnt_ref, carry_ref):
    @pl.when(pl.program_id(0) == 0)
    def _():
        carry_ref[...] = jnp.zeros_like(carry_ref)

    x = x_ref[...]
    y = x * lax.rsqrt(jnp.mean(x * x, axis=-1, keepdims=True) + EPS) * g_ref[...]
    hmod = y * (1.0 + sc_ref[0]) + sh_ref[0]
    h_ref[...] = _pack_halves(hmod)
    logits = jnp.dot(hmod, wr_ref[...], preferred_element_type=F32, precision=lax.Precision.HIGHEST) + br_ref[...]
    lane = lax.broadcasted_iota(I32, logits.shape, 1)
    lane_f = lane.astype(F32)
    logits = jnp.where(lane < N_EXPERTS, logits, -jnp.inf)
    onehots, vals, firsts = [], [], []
    for _ in range(MOE_TOPK):
        best = jnp.max(logits, axis=-1, keepdims=True)
        first = jnp.min(jnp.where(logits == best, lane_f, float(LANES)), axis=-1, keepdims=True)
        pick = lane_f == first
        onehots.append(pick)
        vals.append(best)
        firsts.append(first.astype(I32))
        logits = jnp.where(pick, -jnp.inf, logits)
    exps = [jnp.exp(v - vals[0]) for v in vals]
    den = exps[0]
    for e in exps[1:]:
        den = den + e
    chosen_f = jnp.zeros(logits.shape, F32)
    for o in onehots:
        chosen_f = jnp.where(o, 1.0, chosen_f)
    before = jnp.dot(tri_ref[...], chosen_f.astype(BF16), preferred_element_type=F32) + carry_ref[...]
    idx_out = jnp.zeros(logits.shape, I32)
    wgt_out = jnp.zeros(logits.shape, F32)
    rank_out = jnp.zeros(logits.shape, I32)
    for k in range(MOE_TOPK):
        rk = jnp.sum(jnp.where(onehots[k], before, 0.0), axis=-1, keepdims=True).astype(I32)
        idx_out = jnp.where(lane == k, firsts[k], idx_out)
        wgt_out = jnp.where(lane == k, exps[k] / den, wgt_out)
        rank_out = jnp.where(lane == k, rk, rank_out)
    idx_ref[...] = idx_out
    wgt_ref[...] = wgt_out
    rank_ref[...] = rank_out
    carry_ref[...] = carry_ref[...] + jnp.sum(chosen_f, axis=0, keepdims=True)
    cnt_ref[...] = carry_ref[...]


def _moe_route(x2, g, scale, shift, w_router, b_router, rows_per_batch, tm=256):
    n, d = x2.shape
    tpb = rows_per_batch // tm
    wr = jnp.pad(w_router, ((0, 0), (0, LANES - N_EXPERTS)))
    br = jnp.pad(b_router, (0, LANES - N_EXPERTS)).reshape(1, LANES)
    tri = jnp.tril(jnp.ones((tm, tm), F32), -1).astype(BF16)
    tok_spec = pl.BlockSpec((tm, LANES), lambda i: (i, 0))
    return pl.pallas_call(
        _route_kernel,
        grid=(n // tm,),
        in_specs=[pl.BlockSpec((tm, d), lambda i: (i, 0)),
                  pl.BlockSpec((1, d), lambda i: (0, 0)),
                  pl.BlockSpec((1, 1, d), lambda i: (i // tpb, 0, 0)),
                  pl.BlockSpec((1, 1, d), lambda i: (i // tpb, 0, 0)),
                  pl.BlockSpec((d, LANES), lambda i: (0, 0)),
                  pl.BlockSpec((1, LANES), lambda i: (0, 0)),
                  pl.BlockSpec((tm, tm), lambda i: (0, 0))],
        out_specs=[pl.BlockSpec((tm, d // 2), lambda i: (i, 0)), tok_spec, tok_spec, tok_spec,
                   pl.BlockSpec((1, LANES), lambda i: (0, 0))],
        out_shape=[jax.ShapeDtypeStruct((n, d // 2), I32), jax.ShapeDtypeStruct((n, LANES), I32),
                   jax.ShapeDtypeStruct((n, LANES), F32), jax.ShapeDtypeStruct((n, LANES), I32),
                   jax.ShapeDtypeStruct((1, LANES), F32)],
        scratch_shapes=[pltpu.VMEM((1, LANES), F32)],
        compiler_params=_cparams(("arbitrary",)),
        name="moe_route",
    )(x2, g.reshape(1, d), scale[:, None, :], shift[:, None, :], wr, br, tri)


def _invert_kernel(lo_ref, hi_ref, pos_ref, src_ref, *, chunk):
    step = pl.program_id(0)

    @pl.when(step == 0)
    def _():
        def fill_group(g, carry):
            def zero(p, c):
                src_ref[p] = 0
                return c
            lax.fori_loop(lo_ref[g], hi_ref[g], zero, 0)
            return carry
        lax.fori_loop(0, lo_ref.shape[0], fill_group, 0)

    tokens = chunk // MOE_TOPK

    def put(t, carry):
        for k in range(MOE_TOPK):
            src_ref[pos_ref[t * MOE_TOPK + k]] = step * tokens + t
        return carry

    lax.fori_loop(0, tokens, put, 0, unroll=4)


def _moe_invert(pos, pad_lo, pad_hi, n_rows, chunk=2048):
    n_assign = pos.shape[0]
    return pl.pallas_call(
        functools.partial(_invert_kernel, chunk=chunk),
        grid_spec=pltpu.PrefetchScalarGridSpec(
            num_scalar_prefetch=2, grid=(n_assign // chunk,),
            in_specs=[pl.BlockSpec((chunk,), lambda i, lo, hi: (i,), memory_space=pltpu.SMEM)],
            out_specs=pl.BlockSpec((n_rows,), lambda i, lo, hi: (0,), memory_space=pltpu.SMEM)),
        out_shape=jax.ShapeDtypeStruct((n_rows,), I32),
        compiler_params=_cparams(("arbitrary",), unchecked=True),
        name="moe_invert",
    )(pad_lo, pad_hi, pos)


def _expert_kernel(te_ref, nt_ref, src_ref, h_ref, wi_ref, bi_ref, perm_ref, wo_ref, bo_ref, y_ref,
                   xbuf_ref, sem):
    i = pl.program_id(0)
    n_used = nt_ref[0]
    tm = xbuf_ref.shape[1]

    def row_copy(row, slot, t):
        return pltpu.make_async_copy(h_ref.at[pl.ds(src_ref[row], 1)], xbuf_ref.at[slot, pl.ds(t, 1)], sem.at[slot])

    def wait_slot(slot):
        pltpu.make_async_copy(h_ref.at[pl.ds(0, tm)], xbuf_ref.at[slot], sem.at[slot]).wait()

    def gather(tile, slot):
        for t in range(tm):
            row_copy(tile * tm + t, slot, t).start()

    @pl.when(i == 0)
    def _():
        gather(0, 0)

    @pl.when((i + 1 < n_used) & (i % 2 == 0))
    def _():
        gather(i + 1, 1)

    @pl.when((i + 1 < n_used) & (i % 2 == 1))
    def _():
        gather(i + 1, 0)

    @pl.when(i < n_used)
    def _():
        slot = i % 2
        wait_slot(slot)
        x_lo, x_hi = _unpack_halves(xbuf_ref[slot])
        half = x_lo.shape[1]
        hh = (jnp.dot(x_lo.astype(BF16), wi_ref[0, :half], preferred_element_type=F32)
              + jnp.dot(x_hi.astype(BF16), wi_ref[0, half:], preferred_element_type=F32) + bi_ref[0]).astype(BF16)
        hp = jnp.dot(hh, perm_ref[...], preferred_element_type=F32)
        x_glu = jnp.minimum(hp[:, :EXPERT_FF], SWIGLU_LIMIT)
        x_lin = jnp.clip(hp[:, EXPERT_FF:], -SWIGLU_LIMIT, SWIGLU_LIMIT)
        act = x_glu * (1.0 / (1.0 + jnp.exp(-SWIGLU_ALPHA * x_glu))) * (x_lin + 1.0)
        y_ref[...] = _pack_halves(jnp.dot(act.astype(BF16), wo_ref[0], preferred_element_type=F32) + bo_ref[0])

    @pl.when(i >= n_used)
    def _():
        y_ref[...] = jnp.zeros_like(y_ref)


def _moe_experts(h2, src, tile_expert, n_tiles_used, w_in, b_in, w_out, b_out):
    n, dh = h2.shape
    d = 2 * dh
    r = src.shape[0]
    tm = MOE_ROW_TILE
    f2 = w_in.shape[2]
    ff = w_out.shape[1]
    col = np.arange(f2)
    perm = np.zeros((f2, f2), np.float32)
    perm[col, np.where(col % 2 == 0, col // 2, ff + col // 2)] = 1.0
    grid_spec = pltpu.PrefetchScalarGridSpec(
        num_scalar_prefetch=3, grid=(r // tm,),
        in_specs=[pl.BlockSpec(memory_space=pl.ANY),
                  pl.BlockSpec((1, d, f2), lambda i, te, nt, src: (te[i], 0, 0)),
                  pl.BlockSpec((1, 1, f2), lambda i, te, nt, src: (te[i], 0, 0)),
                  pl.BlockSpec((f2, f2), lambda i, te, nt, src: (0, 0)),
                  pl.BlockSpec((1, ff, d), lambda i, te, nt, src: (te[i], 0, 0)),
                  pl.BlockSpec((1, 1, d), lambda i, te, nt, src: (te[i], 0, 0))],
        out_specs=pl.BlockSpec((tm, dh), lambda i, te, nt, src: (i, 0)),
        scratch_shapes=[pltpu.VMEM((2, tm, dh), I32), pltpu.SemaphoreType.DMA((2,))])
    return pl.pallas_call(
        _expert_kernel, grid_spec=grid_spec,
        out_shape=jax.ShapeDtypeStruct((r, dh), I32),
        compiler_params=_cparams(("arbitrary",), unchecked=True),
        name="moe_experts",
    )(tile_expert, n_tiles_used, src, h2, w_in, b_in, jnp.asarray(perm, BF16), w_out, b_out)


def _combine_kernel(pos_ref, y_ref, x_ref, w_ref, g_ref, o_ref, buf_ref, wb_ref, sem, *, tm):
    i = pl.program_id(0)
    n_steps = pl.num_programs(0)

    def gather(tile, slot):
        base = tile * tm * MOE_TOPK
        for t in range(tm):
            for k in range(MOE_TOPK):
                pltpu.make_async_copy(y_ref.at[pl.ds(pos_ref[base + t * MOE_TOPK + k], 1)],
                                      buf_ref.at[slot, k, pl.ds(t, 1)], sem.at[slot]).start()

    @pl.when(i == 0)
    def _():
        gather(0, 0)

    @pl.when((i + 1 < n_steps) & (i % 2 == 0))
    def _():
        gather(i + 1, 1)

    @pl.when((i + 1 < n_steps) & (i % 2 == 1))
    def _():
        gather(i + 1, 0)

    slot = i % 2
    for k in range(MOE_TOPK):
        pltpu.make_async_copy(y_ref.at[pl.ds(0, tm)], buf_ref.at[slot, k], sem.at[slot]).wait()
    w = w_ref[...]
    half = buf_ref.shape[3]
    for k in range(MOE_TOPK):
        wb_ref[k] = jnp.broadcast_to(w[:, k:k + 1], (tm, LANES))
    for c in range(half // LANES):
        lo_cols = slice(c * LANES, (c + 1) * LANES)
        hi_cols = slice(half + c * LANES, half + (c + 1) * LANES)
        mix_lo = jnp.zeros((tm, LANES), F32)
        mix_hi = jnp.zeros((tm, LANES), F32)
        for k in range(MOE_TOPK):
            y_lo, y_hi = _unpack_halves(buf_ref[slot, k, :, lo_cols])
            mix_lo = mix_lo + y_lo * wb_ref[k]
            mix_hi = mix_hi + y_hi * wb_ref[k]
        o_ref[:, lo_cols] = x_ref[:, lo_cols] + g_ref[0, :, lo_cols] * mix_lo
        o_ref[:, hi_cols] = x_ref[:, hi_cols] + g_ref[0, :, hi_cols] * mix_hi


def _moe_combine(y, pos, x2, wgt, gate, rows_per_batch, tm=128):
    n, d = x2.shape
    tpb = rows_per_batch // tm
    grid_spec = pltpu.PrefetchScalarGridSpec(
        num_scalar_prefetch=1, grid=(n // tm,),
        in_specs=[pl.BlockSpec(memory_space=pl.ANY),
                  pl.BlockSpec((tm, d), lambda i, pos: (i, 0)),
                  pl.BlockSpec((tm, LANES), lambda i, pos: (i, 0)),
                  pl.BlockSpec((1, 1, d), lambda i, pos: (i // tpb, 0, 0))],
        out_specs=pl.BlockSpec((tm, d), lambda i, pos: (i, 0)),
        scratch_shapes=[pltpu.VMEM((2, MOE_TOPK, tm, d // 2), I32), pltpu.VMEM((MOE_TOPK, tm, LANES), F32),
                        pltpu.SemaphoreType.DMA((2,))])
    return pl.pallas_call(
        functools.partial(_combine_kernel, tm=tm), grid_spec=grid_spec,
        out_shape=jax.ShapeDtypeStruct((n, d), F32),
        compiler_params=_cparams(("arbitrary",), unchecked=True),
        name="moe_combine",
    )(pos, y, x2, wgt, gate[:, None, :])


def _moe_block(x2, g, scale, shift, gate, w_router, b_router, w_exp_in, b_exp_in, w_exp_out, b_exp_out,
               rows_per_batch):
    n, d = x2.shape
    tm = MOE_ROW_TILE
    h2, idx, wgt, rank, counts = _moe_route(x2, g, scale, shift, w_router, b_router, rows_per_batch)
    cnt = counts[0, :N_EXPERTS].astype(I32)
    padded = (cnt + tm - 1) // tm * tm
    ends = jnp.cumsum(padded)
    starts = ends - padded
    n_rows = n * MOE_TOPK + N_EXPERTS * tm
    n_tiles = n_rows // tm
    tile_start = jnp.arange(n_tiles, dtype=I32) * tm
    tile_expert = jnp.minimum(jnp.sum((ends[None, :] <= tile_start[:, None]).astype(I32), axis=1), N_EXPERTS - 1)
    n_tiles_used = (ends[-1] // tm).astype(I32).reshape(1)
    e_flat = idx[:, :MOE_TOPK].reshape(-1)
    pos = (starts[e_flat] + rank[:, :MOE_TOPK].reshape(-1)).astype(I32)
    pad_lo = jnp.concatenate([starts + cnt, ends[-1:]]).astype(I32)
    pad_hi = jnp.concatenate([ends, jnp.full((1,), n_rows, I32)]).astype(I32)
    src = _moe_invert(pos, pad_lo, pad_hi, n_rows)
    y = _moe_experts(h2, src, tile_expert, n_tiles_used, w_exp_in.astype(BF16), b_exp_in[:, None, :],
                     w_exp_out.astype(BF16), b_exp_out[:, None, :])
    return _moe_combine(y, pos, x2, wgt, gate, rows_per_batch)


def _even_mixer(h, x2, gate, batch, s_len, w_in, g_cq, w_uq, w_iq, gq_a, gk_a, g_kidx, b_kidx, gq_b, gk_b,
                b_f, w_out):
    gw = GROUP_WIDTH
    o_cq, o_ka, o_va, o_ki, o_wi, o_qb, o_kb, o_vb, o_fb, o_gb = np.cumsum(
        [0, DSA_Q_LORA, HEAD_DIM, HEAD_DIM, HEAD_DIM, DSA_IDX_HEADS, gw, gw, gw, GROUP_HEADS]).tolist()
    cols = lambda o, wdt: w_in[:, o:o + wdt]
    w_main = jnp.concatenate([cols(o_qb, gw), cols(o_kb, gw), cols(o_vb, gw), cols(o_gb, gw),
                              cols(o_cq, DSA_Q_LORA), cols(o_ka, HEAD_DIM), cols(o_va, HEAD_DIM),
                              cols(o_ki, HEAD_DIM)], axis=1).astype(BF16)
    c_qb, c_kb, c_vb, c_gb, c_cq = 0, gw, 2 * gw, 3 * gw, 4 * gw
    c_ka = c_cq + DSA_Q_LORA
    c_va, c_ki = c_ka + HEAD_DIM, c_ka + 2 * HEAD_DIM
    d = w_in.shape[0]
    w_small = jnp.concatenate([cols(o_fb, GROUP_HEADS), cols(o_wi, DSA_IDX_HEADS),
                               jnp.zeros((d, LANES - GROUP_HEADS - DSA_IDX_HEADS), F32)], axis=1).astype(BF16)
    z = _matmul(h, w_main, BF16, "even_in_proj")
    zs = _matmul(h, w_small, F32, "even_in_proj_small")
    scale = Q_SCALE

    cq = _rowprep(z, c_cq, DSA_Q_LORA, batch, s_len, hw=DSA_Q_LORA, norm="rms", gain=g_cq, name="dsa_cq_norm")
    qa_raw = _matmul(cq, w_uq.astype(BF16), BF16, "dsa_q_up")
    qi_raw = _matmul(cq, w_iq.astype(BF16), BF16, "dsa_idx_q_up")
    qa = _rowprep(qa_raw, 0, gw, batch, s_len, norm="rms", gain=gq_a, rope_dim=HEAD_DIM, out_scale=scale,
                  head_major=True, name="dsa_q_prep")
    qi = _rowprep(qi_raw, 0, DSA_IDX_HEADS * HEAD_DIM, batch, s_len, rope_dim=DSA_IDX_ROPE, head_major=True,
                  name="dsa_idx_q_prep")
    ka = _rowprep(z, c_ka, HEAD_DIM, batch, s_len, norm="rms", gain=gk_a, rope_dim=HEAD_DIM, name="dsa_k_prep")
    ki = _rowprep(z, c_ki, HEAD_DIM, batch, s_len, norm="ln", gain=g_kidx, beta=b_kidx, rope_dim=DSA_IDX_ROPE,
                  name="dsa_idx_k_prep")
    va = z[:, c_va:c_va + HEAD_DIM]
    wi = jnp.pad(zs[:, GROUP_HEADS:GROUP_HEADS + DSA_IDX_HEADS] * (DSA_IDX_HEADS ** -0.5 * HEAD_DIM ** -0.5),
                 ((0, 0), (0, LANES - DSA_IDX_HEADS)))
    o_a = _dsa_attention(qi, ki, wi, qa, ka, va, batch, s_len)
    o_a = o_a.transpose(0, 2, 1, 3).reshape(batch * s_len, gw)

    qb = _rowprep(z, c_qb, gw, batch, s_len, norm="rms", gain=gq_b, out_scale=scale, head_major=True,
                  name="fox_q_prep")
    kb = _rowprep(z, c_kb, gw, batch, s_len, norm="rms", gain=gk_b, head_major=True, name="fox_k_prep")
    bias_row = jnp.pad(b_f.astype(F32), (0, LANES - GROUP_HEADS)).reshape(1, LANES)
    cum = _fox_cumsum(zs, bias_row, batch, s_len)
    o_b = _fox_attention(qb, kb, z, c_vb, c_gb, cum, batch, s_len)
    o_cat = jnp.concatenate([o_a, o_b], axis=1)
    return _matmul(o_cat, w_out.astype(BF16), F32, "even_out_proj", residual=x2, gate=gate,
                   rows_per_batch=s_len)


def _odd_mixer(h, x2, gate, batch, s_len, w_in, gq_c, gk_c, gq_d, gk_d, sinks, w_out):
    gw = GROUP_WIDTH
    kvw = SWA_KV_HEADS * HEAD_DIM
    c_qc, c_kc, c_vc, c_qd, c_kd = 0, gw, 2 * gw, 3 * gw, 4 * gw
    c_vd = c_kd + kvw
    z = _matmul(h, w_in.astype(BF16), BF16, "odd_in_proj")
    scale = Q_SCALE
    qc = _rowprep(z, c_qc, gw, batch, s_len, norm="rms", gain=gq_c, rope_dim=HEAD_DIM, out_scale=scale,
                  head_major=True, name="moba_q_prep")
    kc, kmean = _rowprep(z, c_kc, gw, batch, s_len, norm="rms", gain=gk_c, rope_dim=HEAD_DIM, head_major=True,
                         want_mean=True, name="moba_k_prep")
    o_c = _moba_attention(qc, kc, kmean, z, c_vc, batch, s_len)
    qd = _rowprep(z, c_qd, gw, batch, s_len, norm="rms", gain=gq_d, rope_dim=HEAD_DIM, out_scale=scale,
                  head_major=True, name="swa_q_prep")
    kd = _rowprep(z, c_kd, kvw, batch, s_len, norm="rms", gain=gk_d, rope_dim=HEAD_DIM, head_major=True,
                  name="swa_k_prep")
    o_d = _swa_attention(qd, kd, z, c_vd, sinks, batch, s_len)
    o_cat = jnp.concatenate([o_c, o_d], axis=1)
    return _matmul(o_cat, w_out.astype(BF16), F32, "odd_out_proj", residual=x2, gate=gate, rows_per_batch=s_len)


def kernel(x, c, g_norm_mix, g_norm_ffn, w_ada, b_ada, w_in_even, g_cq, w_uq, w_iq, gq_a, gk_a, g_kidx, b_kidx,
           gq_b, gk_b, b_forget, w_out_even, w_in_odd, gq_c, gk_c, gq_d, gk_d, sinks_d, w_out_odd,
           w_router, b_router, w_exp_in, b_exp_in, w_exp_out, b_exp_out):
    batch, s_len, d = x.shape
    depth = w_ada.shape[0]
    mod = _ada_modulation(c, w_ada, b_ada)
    x2 = x.reshape(batch * s_len, d)
    for layer in range(depth):
        shift_m, scale_m, gate_m, shift_f, scale_f, gate_f = [mod[layer, :, i * d:(i + 1) * d] for i in range(6)]
        h = _norm_modulate(x2, g_norm_mix[layer], scale_m, shift_m, s_len)
        j = layer // 2
        if layer % 2 == 0:
            x2 = _even_mixer(h, x2, gate_m, batch, s_len, w_in_even[j], g_cq[j], w_uq[j], w_iq[j], gq_a[j],
                             gk_a[j], g_kidx[j], b_kidx[j], gq_b[j], gk_b[j], b_forget[j], w_out_even[j])
        else:
            x2 = _odd_mixer(h, x2, gate_m, batch, s_len, w_in_odd[j], gq_c[j], gk_c[j], gq_d[j], gk_d[j],
                            sinks_d[j], w_out_odd[j])
        x2 = _moe_block(x2, g_norm_ffn[layer], scale_f, shift_f, gate_f, w_router[layer], b_router[layer],
                        w_exp_in[layer], b_exp_in[layer], w_exp_out[layer], b_exp_out[layer], s_len)
    return x2.reshape(batch, s_len, d)
```

```python
import functools

import numpy as np
import jax
import jax.numpy as jnp
from jax import lax
from jax.experimental import pallas as pl
from jax.experimental.pallas import tpu as pltpu

F32 = jnp.float32
BF16 = jnp.bfloat16
I32 = jnp.int32

LANES = 128
HEAD_DIM = 128
GROUP_HEADS = 16
GROUP_WIDTH = GROUP_HEADS * HEAD_DIM
ROPE_THETA = 10000.0
EPS = 1e-6
DSA_Q_LORA = 1024
DSA_IDX_HEADS = 32
DSA_IDX_ROPE = 64
DSA_TOPK = 256
DSA_QBLK = 128
DSA_CHUNK = 256
DSA_STAGES = 8
DSA_HEAD_GROUP = 2
MOBA_BLOCK = 256
MOBA_TOPK = 3
SWA_KV_HEADS = 2
SWA_WINDOW = 128
N_EXPERTS = 32
MOE_TOPK = 4
EXPERT_FF = 512
SWIGLU_LIMIT = 7.0
SWIGLU_ALPHA = 1.702
MOE_ROW_TILE = 256
NEG_BIG = -1e30
LOG2_E = 1.4426950408889634
Q_SCALE = HEAD_DIM ** -0.5 * LOG2_E
VMEM_LIMIT = 56 * 1024 * 1024


def _cparams(semantics, vmem=VMEM_LIMIT, unchecked=False):
    return pltpu.CompilerParams(dimension_semantics=semantics, vmem_limit_bytes=vmem,
                                disable_bounds_checks=unchecked)


def _ada_kernel(cb_ref, w_ref, b_ref, o_ref):
    nb = cb_ref.shape[0]
    tn = w_ref.shape[2]
    for c in range(tn // LANES):
        w = w_ref[0, :, c * LANES:(c + 1) * LANES]
        for b in range(nb):
            o_ref[0, b:b + 1, c * LANES:(c + 1) * LANES] = (
                jnp.sum(w * cb_ref[b], axis=0, keepdims=True) + b_ref[0, :, c * LANES:(c + 1) * LANES])


def _ada_modulation(c, w_ada, b_ada, tn=512):
    depth, d, n6 = w_ada.shape
    nb = c.shape[0]
    cb = jnp.broadcast_to(jax.nn.silu(c)[:, :, None], (nb, d, LANES))
    return pl.pallas_call(
        _ada_kernel,
        grid=(depth, n6 // tn),
        in_specs=[pl.BlockSpec((nb, d, LANES), lambda l, j: (0, 0, 0)),
                  pl.BlockSpec((1, d, tn), lambda l, j: (l, 0, j)),
                  pl.BlockSpec((1, 1, tn), lambda l, j: (l, 0, j))],
        out_specs=pl.BlockSpec((1, nb, tn), lambda l, j: (l, 0, j)),
        out_shape=jax.ShapeDtypeStruct((depth, nb, n6), F32),
        compiler_params=_cparams(("parallel", "parallel")),
        name="ada_modulation",
    )(cb, w_ada, b_ada.reshape(depth, 1, n6))


def _normmod_kernel(x_ref, g_ref, sc_ref, sh_ref, o_ref):
    x = x_ref[...]
    y = x * lax.rsqrt(jnp.mean(x * x, axis=-1, keepdims=True) + EPS) * g_ref[...]
    o_ref[...] = (y * (1.0 + sc_ref[0]) + sh_ref[0]).astype(o_ref.dtype)


def _norm_modulate(x2, g, scale, shift, rows_per_batch, tm=256):
    n, d = x2.shape
    tpb = rows_per_batch // tm
    return pl.pallas_call(
        _normmod_kernel,
        grid=(n // tm,),
        in_specs=[pl.BlockSpec((tm, d), lambda i: (i, 0)),
                  pl.BlockSpec((1, d), lambda i: (0, 0)),
                  pl.BlockSpec((1, 1, d), lambda i: (i // tpb, 0, 0)),
                  pl.BlockSpec((1, 1, d), lambda i: (i // tpb, 0, 0))],
        out_specs=pl.BlockSpec((tm, d), lambda i: (i, 0)),
        out_shape=jax.ShapeDtypeStruct((n, d), BF16),
        compiler_params=_cparams(("parallel",)),
        name="norm_modulate",
    )(x2, g.reshape(1, d), scale[:, None, :], shift[:, None, :])


def _mm_kernel(a_ref, b_ref, o_ref):
    o_ref[...] = jnp.dot(a_ref[...], b_ref[...], preferred_element_type=F32).astype(o_ref.dtype)


def _mm_residual_kernel(a_ref, b_ref, x_ref, g_ref, o_ref):
    o_ref[...] = x_ref[...] + g_ref[0] * jnp.dot(a_ref[...], b_ref[...], preferred_element_type=F32)


def _pick(n, prefs):
    for t in prefs:
        if n % t == 0:
            return t
    return n


def _matmul(a, b, out_dtype, name, residual=None, gate=None, rows_per_batch=None):
    m, k = a.shape
    n = b.shape[1]
    tm = _pick(m, (1024, 512, 256, 128))
    tn = _pick(n, (512, 640, 384, 256, 128) if k > 2048 else (1024, 512, 640, 384, 256, 128))
    grid = (m // tm, n // tn)
    a_spec = pl.BlockSpec((tm, k), lambda i, j: (i, 0))
    b_spec = pl.BlockSpec((k, tn), lambda i, j: (0, j))
    o_spec = pl.BlockSpec((tm, tn), lambda i, j: (i, j))
    params = _cparams(("parallel", "arbitrary"))
    if residual is None:
        return pl.pallas_call(
            _mm_kernel, grid=grid, in_specs=[a_spec, b_spec], out_specs=o_spec,
            out_shape=jax.ShapeDtypeStruct((m, n), out_dtype), compiler_params=params, name=name)(a, b)
    tpb = rows_per_batch // tm
    return pl.pallas_call(
        _mm_residual_kernel, grid=grid,
        in_specs=[a_spec, b_spec, o_spec, pl.BlockSpec((1, 1, tn), lambda i, j: (i // tpb, 0, j))],
        out_specs=o_spec, out_shape=jax.ShapeDtypeStruct((m, n), F32),
        compiler_params=params, name=name)(a, b, residual, gate[:, None, :])


def _rope_tables(s_len, rot_dim):
    half = rot_dim // 2
    inv_freq = ROPE_THETA ** (-jnp.arange(half, dtype=F32) / half)
    ang = jnp.arange(s_len, dtype=F32)[:, None] * inv_freq[None, :]
    cos, sin = jnp.cos(ang), jnp.sin(ang)
    pad = HEAD_DIM - rot_dim
    ones = jnp.ones((s_len, pad), F32)
    zeros = jnp.zeros((s_len, pad), F32)
    zh = jnp.zeros((s_len, half), F32)
    cos_t = jnp.concatenate([cos, cos, ones], axis=1)
    sa = jnp.concatenate([-sin, zh, zeros], axis=1)
    sb = jnp.concatenate([zh, sin, zeros], axis=1)
    return cos_t, sa, sb


def _rowprep_kernel(*refs, hw, norm, rope_half, out_scale, head_major, want_mean):
    it = iter(refs)
    z_ref = next(it)
    g_ref = next(it) if norm != "none" else None
    beta_ref = next(it) if norm == "ln" else None
    if rope_half:
        cos_ref, sa_ref, sb_ref = next(it), next(it), next(it)
    o_ref = next(it)
    mean_ref = next(it) if want_mean else None
    width = z_ref.shape[1]
    for h in range(width // hw):
        x = z_ref[:, h * hw:(h + 1) * hw].astype(F32)
        if norm == "rms":
            x = x * lax.rsqrt(jnp.mean(x * x, axis=-1, keepdims=True) + EPS) * g_ref[...]
        elif norm == "ln":
            mu = jnp.mean(x, axis=-1, keepdims=True)
            xc = x - mu
            x = xc * lax.rsqrt(jnp.mean(xc * xc, axis=-1, keepdims=True) + EPS) * g_ref[...] + beta_ref[...]
        if rope_half:
            if rope_half * 2 == hw:
                x = x * cos_ref[...] + pltpu.roll(x, rope_half, 1) * (sa_ref[...] + sb_ref[...])
            else:
                x = (x * cos_ref[...] + pltpu.roll(x, hw - rope_half, 1) * sa_ref[...]
                     + pltpu.roll(x, rope_half, 1) * sb_ref[...])
        if want_mean:
            mean_ref[0, 0, h:h + 1, :] = jnp.mean(x, axis=0, keepdims=True)
        if out_scale != 1.0:
            x = x * out_scale
        if head_major:
            o_ref[0, h] = x.astype(o_ref.dtype)
        else:
            o_ref[:, h * hw:(h + 1) * hw] = x.astype(o_ref.dtype)


def _rowprep(z, col0, width, batch, s_len, *, hw=HEAD_DIM, norm="none", gain=None, beta=None, rope_dim=0,
             out_scale=1.0, head_major=False, want_mean=False, name="rowprep", tm=256):
    n = z.shape[0]
    tpb = s_len // tm
    assert col0 % width == 0 and n == batch * s_len
    cb = col0 // width
    args = [z]
    specs = [pl.BlockSpec((tm, width), lambda i: (i, cb))]
    if norm != "none":
        args.append(gain.reshape(1, hw).astype(F32))
        specs.append(pl.BlockSpec((1, hw), lambda i: (0, 0)))
    if norm == "ln":
        args.append(beta.reshape(1, hw).astype(F32))
        specs.append(pl.BlockSpec((1, hw), lambda i: (0, 0)))
    if rope_dim:
        args += list(_rope_tables(s_len, rope_dim))
        specs += [pl.BlockSpec((tm, HEAD_DIM), lambda i: (i % tpb, 0))] * 3
    nh = width // hw
    if head_major:
        out_shape = [jax.ShapeDtypeStruct((batch, nh, s_len, hw), BF16)]
        out_specs = [pl.BlockSpec((1, nh, tm, hw), lambda i: (i // tpb, 0, i % tpb, 0))]
    else:
        out_shape = [jax.ShapeDtypeStruct((n, width), BF16)]
        out_specs = [pl.BlockSpec((tm, width), lambda i: (i, 0))]
    if want_mean:
        assert tm == MOBA_BLOCK
        out_shape.append(jax.ShapeDtypeStruct((batch, tpb, nh, hw), F32))
        out_specs.append(pl.BlockSpec((1, 1, nh, hw), lambda i: (i // tpb, i % tpb, 0, 0)))
    kern = functools.partial(_rowprep_kernel, hw=hw, norm=norm, rope_half=rope_dim // 2, out_scale=out_scale,
                             head_major=head_major, want_mean=want_mean)
    res = pl.pallas_call(kern, grid=(n // tm,), in_specs=specs, out_specs=out_specs, out_shape=out_shape,
                         compiler_params=_cparams(("parallel",)), name=name)(*args)
    return res if want_mean else res[0]


def _fox_cum_kernel(z_ref, bf_ref, tri_ref, o_ref, carry_ref):
    @pl.when(pl.program_id(1) == 0)
    def _():
        carry_ref[...] = jnp.zeros_like(carry_ref)

    xv = z_ref[...] + bf_ref[...]
    lf = jnp.minimum(xv, 0.0) - jnp.log(1.0 + jnp.exp(-jnp.abs(xv)))
    hi = lf.astype(BF16)
    r1 = lf - hi.astype(F32)
    mid = r1.astype(BF16)
    lo = (r1 - mid.astype(F32)).astype(BF16)
    tri = tri_ref[...]
    cs = (jnp.dot(tri, hi, preferred_element_type=F32) + jnp.dot(tri, mid, preferred_element_type=F32)
          + jnp.dot(tri, lo, preferred_element_type=F32)) + carry_ref[...]
    o_ref[...] = cs * LOG2_E
    tm = cs.shape[0]
    carry_ref[...] = cs[tm - 1:tm, :]


def _fox_cumsum(zs, bias_row, batch, s_len, tm=256):
    n = zs.shape[0]
    tpb = s_len // tm
    tri = jnp.tril(jnp.ones((tm, tm), F32)).astype(BF16)
    return pl.pallas_call(
        _fox_cum_kernel,
        grid=(batch, tpb),
        in_specs=[pl.BlockSpec((tm, LANES), lambda b, i: (b * tpb + i, 0)),
                  pl.BlockSpec((1, LANES), lambda b, i: (0, 0)),
                  pl.BlockSpec((tm, tm), lambda b, i: (0, 0))],
        out_specs=pl.BlockSpec((tm, LANES), lambda b, i: (b * tpb + i, 0)),
        out_shape=jax.ShapeDtypeStruct((n, LANES), F32),
        scratch_shapes=[pltpu.VMEM((1, LANES), F32)],
        compiler_params=_cparams(("parallel", "arbitrary")),
        name="fox_cumsum",
    )(zs, bias_row, tri)


def _dsa_kernel(qi_ref, kit_ref, wi_ref, qa_ref, kat_ref, va_ref, o_ref, wb_ref, key_ref, mb_ref,
                *, sk, q0, topk):
    q = DSA_QBLK
    ch = DSA_CHUNK
    nch = sk // ch
    n_idx = qi_ref.shape[1]
    n_heads = qa_ref.shape[1]
    t0 = (q0 + pl.program_id(1)) * q
    row = t0 + lax.broadcasted_iota(I32, (q, ch), 0)
    lane = lax.broadcasted_iota(I32, (q, ch), 1)

    wi = wi_ref[...]
    for h in range(n_idx):
        wb_ref[h] = jnp.broadcast_to(wi[:, h:h + 1], (q, ch))
    qi = qi_ref[0].reshape(n_idx * q, HEAD_DIM)

    def idx_chunk(c, carry):
        d = jnp.dot(qi, kit_ref[0, c], preferred_element_type=F32)
        acc = jnp.zeros((q, ch), F32)
        for h in range(n_idx):
            acc = acc + jnp.maximum(d[h * q:(h + 1) * q], 0.0) * wb_ref[h]
        score = jnp.where(c * ch + lane <= row, acc + 0.0, -jnp.inf)
        bits = lax.bitcast_convert_type(score, I32)
        key_ref[c] = jnp.where(bits >= 0, bits, bits ^ jnp.int32(0x7FFFFFFF))
        return carry

    lax.fori_loop(0, nch, idx_chunk, 0)

    def count_ge(cand):
        acc = jnp.zeros((q, LANES), F32)
        for c in range(nch):
            for part in range(ch // LANES):
                acc = acc + jnp.where(key_ref[c, :, part * LANES:(part + 1) * LANES] >= cand, 1.0, 0.0)
        return jnp.sum(acc, axis=-1, keepdims=True)

    int_min = jnp.int32(-2 ** 31)
    thr = jnp.where(count_ge(jnp.zeros((q, 1), I32)) >= topk, jnp.int32(0), int_min)

    def bit_step(j, thr):
        cand = thr | jnp.left_shift(jnp.int32(1), 30 - j)
        return jnp.where(count_ge(cand) >= topk, cand, thr)

    thr = lax.fori_loop(0, 31, bit_step, thr)

    for c in range(nch):
        keep = (key_ref[c] >= thr) & (c * ch + lane <= row)
        mb_ref[:, c * ch:(c + 1) * ch] = jnp.where(keep, 0.0, -jnp.inf)

    g = DSA_HEAD_GROUP

    for hg in range(n_heads // g):
        qh = qa_ref[0, hg * g:(hg + 1) * g].reshape(g * q, HEAD_DIM)
        lg = jnp.dot(qh, kat_ref[0], preferred_element_type=F32).reshape(g, q, sk) + mb_ref[...][None]
        m = jnp.max(lg, axis=-1, keepdims=True)
        p = jnp.exp2(lg - m)
        l = jnp.sum(p, axis=-1, keepdims=True)
        o = jnp.dot(p.reshape(g * q, sk).astype(BF16), va_ref[0], preferred_element_type=F32)
        o = (o.reshape(g, q, HEAD_DIM) / l).astype(o_ref.dtype)
        for hh in range(g):
            head = hg * g + hh
            o_ref[0, :, head * HEAD_DIM:(head + 1) * HEAD_DIM] = o[hh]


def _dsa_attention(qi, ki, wi, qa, ka, va, batch, s_len):
    q, ch = DSA_QBLK, DSA_CHUNK
    n_idx, n_heads = qi.shape[1], qa.shape[1]
    topk = min(DSA_TOPK, s_len // 4)
    nqb = s_len // q
    kit = ki.reshape(batch, s_len // ch, ch, HEAD_DIM).transpose(0, 1, 3, 2)
    kat = ka.reshape(batch, s_len, HEAD_DIM).transpose(0, 2, 1)
    va3 = va.reshape(batch, s_len, HEAD_DIM)
    per_stage = max(nqb // DSA_STAGES, ch // q)
    outs = []
    for q0 in range(0, nqb, per_stage):
        sk = (q0 + per_stage) * q
        nch = sk // ch
        kern = functools.partial(_dsa_kernel, sk=sk, q0=q0, topk=topk)
        outs.append(pl.pallas_call(
            kern,
            grid=(batch, per_stage),
            in_specs=[pl.BlockSpec((1, n_idx, q, HEAD_DIM), lambda b, i, q0=q0: (b, 0, q0 + i, 0)),
                      pl.BlockSpec((1, nch, HEAD_DIM, ch), lambda b, i: (b, 0, 0, 0)),
                      pl.BlockSpec((q, LANES), lambda b, i, q0=q0: (b * nqb + q0 + i, 0)),
                      pl.BlockSpec((1, n_heads, q, HEAD_DIM), lambda b, i, q0=q0: (b, 0, q0 + i, 0)),
                      pl.BlockSpec((1, HEAD_DIM, sk), lambda b, i: (b, 0, 0)),
                      pl.BlockSpec((1, sk, HEAD_DIM), lambda b, i: (b, 0, 0))],
            out_specs=pl.BlockSpec((1, q, n_heads * HEAD_DIM), lambda b, i: (b, i, 0)),
            out_shape=jax.ShapeDtypeStruct((batch, per_stage * q, n_heads * HEAD_DIM), BF16),
            scratch_shapes=[pltpu.VMEM((n_idx, q, ch), F32),
                            pltpu.VMEM((nch, q, ch), I32),
                            pltpu.VMEM((q, sk), F32)],
            compiler_params=_cparams(("parallel", "parallel")),
            name=f"dsa_attention_k{sk}",
        )(qi, kit, wi, qa, kat, va3))
    return jnp.concatenate(outs, axis=1).reshape(batch * s_len, n_heads * HEAD_DIM)


def _tri_tables(nq, ratio):
    qt, kt = [], []
    for i in range(nq):
        for j in range((i + 1) * ratio):
            qt.append(i)
            kt.append(j)
    return jnp.asarray(qt, I32), jnp.asarray(kt, I32)


def _fox_kernel(qt_ref, kt_ref, q_ref, k_ref, v_ref, cq_ref, ck_ref, g_ref, o_ref,
                m_ref, l_ref, acc_ref, cqs_ref, *, tq):
    h = pl.program_id(1)
    step = pl.program_id(2)
    qi = qt_ref[step]
    kj = kt_ref[step]

    @pl.when(kj == 0)
    def _():
        m_ref[...] = jnp.full_like(m_ref, -jnp.inf)
        l_ref[...] = jnp.zeros_like(l_ref)
        acc_ref[...] = jnp.zeros_like(acc_ref)
        lane = lax.broadcasted_iota(I32, cq_ref.shape, 1)
        cqs_ref[...] = jnp.sum(jnp.where(lane == h, cq_ref[...], 0.0), axis=-1, keepdims=True)

    s = lax.dot_general(q_ref[0, 0], k_ref[0, 0], (((1,), (1,)), ((), ())), preferred_element_type=F32)
    s = s + (cqs_ref[...] - ck_ref[0, 0])

    def update(s):
        m_prev = m_ref[...]
        m_new = jnp.maximum(m_prev, jnp.max(s, axis=-1, keepdims=True))
        alpha = jnp.exp2(m_prev - m_new)
        p = jnp.exp2(s - m_new)
        l_ref[...] = alpha * l_ref[...] + jnp.sum(p, axis=-1, keepdims=True)
        acc_ref[...] = alpha * acc_ref[...] + jnp.dot(p.astype(BF16), v_ref[...], preferred_element_type=F32)
        m_ref[...] = m_new

    @pl.when(kj < qi)
    def _():
        update(s)

    @pl.when(kj == qi)
    def _():
        row = lax.broadcasted_iota(I32, s.shape, 0)
        col = lax.broadcasted_iota(I32, s.shape, 1)
        update(jnp.where(col <= row, s, -jnp.inf))
        gate = g_ref[...].astype(F32)
        o_ref[...] = (acc_ref[...] / l_ref[...] * (1.0 / (1.0 + jnp.exp(-gate)))).astype(o_ref.dtype)


def _fox_attention(qb, kb, z, v_col0, g_col0, cum, batch, s_len, tq=1024):
    tq = min(tq, s_len)
    nq = s_len // tq
    n_heads = qb.shape[1]
    qt, kt = _tri_tables(nq, 1)
    cum_t = cum.reshape(batch, s_len, LANES)[:, :, :n_heads].transpose(0, 2, 1)[:, :, None, :]
    vb0, gb0 = v_col0 // HEAD_DIM, g_col0 // HEAD_DIM
    grid_spec = pltpu.PrefetchScalarGridSpec(
        num_scalar_prefetch=2,
        grid=(batch, n_heads, int(qt.shape[0])),
        in_specs=[pl.BlockSpec((1, 1, tq, HEAD_DIM), lambda b, h, s, qt, kt: (b, h, qt[s], 0)),
                  pl.BlockSpec((1, 1, tq, HEAD_DIM), lambda b, h, s, qt, kt: (b, h, kt[s], 0)),
                  pl.BlockSpec((tq, HEAD_DIM), lambda b, h, s, qt, kt: (b * nq + kt[s], vb0 + h)),
                  pl.BlockSpec((tq, LANES), lambda b, h, s, qt, kt: (b * nq + qt[s], 0)),
                  pl.BlockSpec((1, 1, 1, tq), lambda b, h, s, qt, kt: (b, h, 0, kt[s])),
                  pl.BlockSpec((tq, HEAD_DIM), lambda b, h, s, qt, kt: (b * nq + qt[s], gb0 + h))],
        out_specs=pl.BlockSpec((tq, HEAD_DIM), lambda b, h, s, qt, kt: (b * nq + qt[s], h)),
        scratch_shapes=[pltpu.VMEM((tq, 1), F32), pltpu.VMEM((tq, 1), F32),
                        pltpu.VMEM((tq, HEAD_DIM), F32), pltpu.VMEM((tq, 1), F32)])
    return pl.pallas_call(
        functools.partial(_fox_kernel, tq=tq), grid_spec=grid_spec,
        out_shape=jax.ShapeDtypeStruct((batch * s_len, n_heads * HEAD_DIM), BF16),
        compiler_params=_cparams(("parallel", "parallel", "arbitrary")),
        name="fox_attention",
    )(qt, kt, qb, kb, z, cum, cum_t, z)


def _moba_kernel(qt_ref, kt_ref, q_ref, k_ref, e_ref, v_ref, km_ref, o_ref, m_ref, l_ref, acc_ref, qa_ref,
                 *, tq, n_sel):
    step = pl.program_id(2)
    qi = qt_ref[step]
    kj = kt_ref[step]

    @pl.when(kj == 0)
    def _():
        m_ref[...] = jnp.full_like(m_ref, NEG_BIG)
        l_ref[...] = jnp.zeros_like(l_ref)
        acc_ref[...] = jnp.zeros_like(acc_ref)
        own = (qi * tq + lax.broadcasted_iota(I32, (tq, 1), 0)) // MOBA_BLOCK
        gate = jnp.dot(q_ref[0, 0].astype(F32), km_ref[0, 0], preferred_element_type=F32,
                       precision=lax.Precision.HIGHEST)
        lane = lax.broadcasted_iota(I32, gate.shape, 1)
        lane_f = lane.astype(F32)
        gate = jnp.where(lane < own, gate, -jnp.inf)
        allowed = jnp.where(lane == own, 1.0, 0.0)
        for _ in range(n_sel):
            best = jnp.max(gate, axis=-1, keepdims=True)
            first = jnp.min(jnp.where(gate == best, lane_f, float(LANES)), axis=-1, keepdims=True)
            pick = (lane_f == first) & (best > -jnp.inf)
            allowed = jnp.where(pick, 1.0, allowed)
            gate = jnp.where(pick, -jnp.inf, gate)
        qa_ref[:, :HEAD_DIM] = q_ref[0, 0]
        qa_ref[:, HEAD_DIM:] = jnp.where(allowed > 0.0, 0.0, NEG_BIG).astype(qa_ref.dtype)

    k_aug = jnp.concatenate([k_ref[0, 0], e_ref[...]], axis=1)
    s = lax.dot_general(qa_ref[...], k_aug, (((1,), (1,)), ((), ())), preferred_element_type=F32)

    def update(s):
        m_prev = m_ref[...]
        m_new = jnp.maximum(m_prev, jnp.max(s, axis=-1, keepdims=True))
        alpha = jnp.exp2(m_prev - m_new)
        p = jnp.exp2(s - m_new)
        l_ref[...] = alpha * l_ref[...] + jnp.sum(p, axis=-1, keepdims=True)
        acc_ref[...] = alpha * acc_ref[...] + jnp.dot(p.astype(BF16), v_ref[...], preferred_element_type=F32)
        m_ref[...] = m_new

    @pl.when(kj < qi)
    def _():
        update(s)

    @pl.when(kj == qi)
    def _():
        row = lax.broadcasted_iota(I32, s.shape, 0)
        col = lax.broadcasted_iota(I32, s.shape, 1)
        update(jnp.where(col <= row, s, NEG_BIG))
        o_ref[...] = (acc_ref[...] / l_ref[...]).astype(o_ref.dtype)


def _moba_attention(qc, kc, kmean, z, v_col0, batch, s_len, tq=1024):
    tq = min(tq, s_len)
    nq = s_len // tq
    n_heads = qc.shape[1]
    nb = s_len // MOBA_BLOCK
    assert nb <= LANES
    n_sel = min(MOBA_TOPK, nb - 1)
    qt, kt = _tri_tables(nq, 1)
    km_t = jnp.pad(kmean.transpose(0, 2, 3, 1), ((0, 0), (0, 0), (0, 0), (0, LANES - nb)))
    block_onehot = (jnp.arange(s_len, dtype=I32)[:, None] // MOBA_BLOCK
                    == jnp.arange(LANES, dtype=I32)[None, :]).astype(BF16)
    vb0 = v_col0 // HEAD_DIM
    grid_spec = pltpu.PrefetchScalarGridSpec(
        num_scalar_prefetch=2,
        grid=(batch, n_heads, int(qt.shape[0])),
        in_specs=[pl.BlockSpec((1, 1, tq, HEAD_DIM), lambda b, h, s, qt, kt: (b, h, qt[s], 0)),
                  pl.BlockSpec((1, 1, tq, HEAD_DIM), lambda b, h, s, qt, kt: (b, h, kt[s], 0)),
                  pl.BlockSpec((tq, LANES), lambda b, h, s, qt, kt: (kt[s], 0)),
                  pl.BlockSpec((tq, HEAD_DIM), lambda b, h, s, qt, kt: (b * nq + kt[s], vb0 + h)),
                  pl.BlockSpec((1, 1, HEAD_DIM, LANES), lambda b, h, s, qt, kt: (b, h, 0, 0))],
        out_specs=pl.BlockSpec((tq, HEAD_DIM), lambda b, h, s, qt, kt: (b * nq + qt[s], h)),
        scratch_shapes=[pltpu.VMEM((tq, 1), F32), pltpu.VMEM((tq, 1), F32),
                        pltpu.VMEM((tq, HEAD_DIM), F32), pltpu.VMEM((tq, HEAD_DIM + LANES), BF16)])
    return pl.pallas_call(
        functools.partial(_moba_kernel, tq=tq, n_sel=n_sel), grid_spec=grid_spec,
        out_shape=jax.ShapeDtypeStruct((batch * s_len, n_heads * HEAD_DIM), BF16),
        compiler_params=_cparams(("parallel", "parallel", "arbitrary")),
        name="moba_attention",
    )(qt, kt, qc, kc, block_onehot, z, km_t)


def _swa_kernel(q_ref, kp_ref, kc_ref, vp_ref, vc_ref, sink_ref, o_ref):
    n = pl.program_id(1)
    hq, w = q_ref.shape[1], q_ref.shape[2]
    hkv = kc_ref.shape[1]
    grp = hq // hkv
    ti = lax.broadcasted_iota(I32, (w, w), 0)
    si = lax.broadcasted_iota(I32, (w, w), 1)
    cur_ok = (si <= ti)[None]
    prev_ok = ((si > ti) & (n > 0))[None]
    nt = (((1,), (1,)), ((), ()))
    for kv in range(hkv):
        q = q_ref[0, kv * grp:(kv + 1) * grp].reshape(grp * w, HEAD_DIM)
        sc = lax.dot_general(q, kc_ref[0, kv], nt, preferred_element_type=F32).reshape(grp, w, w)
        sp = lax.dot_general(q, kp_ref[0, kv], nt, preferred_element_type=F32).reshape(grp, w, w)
        sc = jnp.where(cur_ok, sc, -jnp.inf)
        sp = jnp.where(prev_ok, sp, -jnp.inf)
        sink = sink_ref[kv * grp * w:(kv + 1) * grp * w].reshape(grp, w, LANES)[:, :, :1]
        m = jnp.maximum(jnp.maximum(jnp.max(sc, axis=-1, keepdims=True), jnp.max(sp, axis=-1, keepdims=True)), sink)
        pc = jnp.exp2(sc - m)
        pp = jnp.exp2(sp - m)
        den = jnp.sum(pc, axis=-1, keepdims=True) + jnp.sum(pp, axis=-1, keepdims=True) + jnp.exp2(sink - m)
        o = (jnp.dot(pc.reshape(grp * w, w).astype(BF16), vc_ref[:, kv * HEAD_DIM:(kv + 1) * HEAD_DIM],
                     preferred_element_type=F32)
             + jnp.dot(pp.reshape(grp * w, w).astype(BF16), vp_ref[:, kv * HEAD_DIM:(kv + 1) * HEAD_DIM],
                       preferred_element_type=F32))
        o = o.reshape(grp, w, HEAD_DIM) / den
        for gh in range(grp):
            hh = kv * grp + gh
            o_ref[:, hh * HEAD_DIM:(hh + 1) * HEAD_DIM] = o[gh].astype(o_ref.dtype)


def _swa_attention(qd, kd, z, v_col0, sinks, batch, s_len):
    w = SWA_WINDOW
    nb = s_len // w
    hq, hkv = qd.shape[1], kd.shape[1]
    vw = hkv * HEAD_DIM
    vb0 = v_col0 // vw
    sink_b = jnp.broadcast_to((sinks.astype(F32) * LOG2_E)[:, None, None], (hq, w, LANES)).reshape(hq * w, LANES)
    return pl.pallas_call(
        _swa_kernel,
        grid=(batch, nb),
        in_specs=[pl.BlockSpec((1, hq, w, HEAD_DIM), lambda b, n: (b, 0, n, 0)),
                  pl.BlockSpec((1, hkv, w, HEAD_DIM), lambda b, n: (b, 0, jnp.maximum(n - 1, 0), 0)),
                  pl.BlockSpec((1, hkv, w, HEAD_DIM), lambda b, n: (b, 0, n, 0)),
                  pl.BlockSpec((w, vw), lambda b, n: (b * nb + jnp.maximum(n - 1, 0), vb0)),
                  pl.BlockSpec((w, vw), lambda b, n: (b * nb + n, vb0)),
                  pl.BlockSpec((hq * w, LANES), lambda b, n: (0, 0))],
        out_specs=pl.BlockSpec((w, hq * HEAD_DIM), lambda b, n: (b * nb + n, 0)),
        out_shape=jax.ShapeDtypeStruct((batch * s_len, hq * HEAD_DIM), BF16),
        compiler_params=_cparams(("parallel", "parallel")),
        name="swa_attention",
    )(qd, kd, kd, z, z, sink_b)


_HIGH_HALF = -65536


def _pack_halves(x):
    half = x.shape[1] // 2
    lo = lax.bitcast_convert_type(x[:, :half].astype(jnp.bfloat16).astype(F32), I32)
    hi = lax.bitcast_convert_type(x[:, half:].astype(jnp.bfloat16).astype(F32), I32)
    return ((lo >> 16) & 0xFFFF) | (hi & _HIGH_HALF)


def _unpack_halves(p):
    lo = lax.bitcast_convert_type(p << 16, F32)
    hi = lax.bitcast_convert_type(p & _HIGH_HALF, F32)
    return lo, hi


def _route_kernel(x_ref, g_ref, sc_ref, sh_ref, wr_ref, br_ref, tri_ref,
                  h_ref, idx_ref, wgt_ref, rank_ref, cnt_ref, carry_ref):
    @pl.when(pl.program_id(0) == 0)
    def _():
        carry_ref[...] = jnp.zeros_like(carry_ref)

    x = x_ref[...]
    y = x * lax.rsqrt(jnp.mean(x * x, axis=-1, keepdims=True) + EPS) * g_ref[...]
    hmod = y * (1.0 + sc_ref[0]) + sh_ref[0]
    h_ref[...] = _pack_halves(hmod)
    logits = jnp.dot(hmod, wr_ref[...], preferred_element_type=F32, precision=lax.Precision.HIGHEST) + br_ref[...]
    lane = lax.broadcasted_iota(I32, logits.shape, 1)
    lane_f = lane.astype(F32)
    logits = jnp.where(lane < N_EXPERTS, logits, -jnp.inf)
    onehots, vals, firsts = [], [], []
    for _ in range(MOE_TOPK):
        best = jnp.max(logits, axis=-1, keepdims=True)
        first = jnp.min(jnp.where(logits == best, lane_f, float(LANES)), axis=-1, keepdims=True)
        pick = lane_f == first
        onehots.append(pick)
        vals.append(best)
        firsts.append(first.astype(I32))
        logits = jnp.where(pick, -jnp.inf, logits)
    exps = [jnp.exp(v - vals[0]) for v in vals]
    den = exps[0]
    for e in exps[1:]:
        den = den + e
    chosen_f = jnp.zeros(logits.shape, F32)
    for o in onehots:
        chosen_f = jnp.where(o, 1.0, chosen_f)
    before = jnp.dot(tri_ref[...], chosen_f.astype(BF16), preferred_element_type=F32) + carry_ref[...]
    idx_out = jnp.zeros(logits.shape, I32)
    wgt_out = jnp.zeros(logits.shape, F32)
    rank_out = jnp.zeros(logits.shape, I32)
    for k in range(MOE_TOPK):
        rk = jnp.sum(jnp.where(onehots[k], before, 0.0), axis=-1, keepdims=True).astype(I32)
        idx_out = jnp.where(lane == k, firsts[k], idx_out)
        wgt_out = jnp.where(lane == k, exps[k] / den, wgt_out)
        rank_out = jnp.where(lane == k, rk, rank_out)
    idx_ref[...] = idx_out
    wgt_ref[...] = wgt_out
    rank_ref[...] = rank_out
    carry_ref[...] = carry_ref[...] + jnp.sum(chosen_f, axis=0, keepdims=True)
    cnt_ref[...] = carry_ref[...]


def _moe_route(x2, g, scale, shift, w_router, b_router, rows_per_batch, tm=256):
    n, d = x2.shape
    tpb = rows_per_batch // tm
    wr = jnp.pad(w_router, ((0, 0), (0, LANES - N_EXPERTS)))
    br = jnp.pad(b_router, (0, LANES - N_EXPERTS)).reshape(1, LANES)
    tri = jnp.tril(jnp.ones((tm, tm), F32), -1).astype(BF16)
    tok_spec = pl.BlockSpec((tm, LANES), lambda i: (i, 0))
    return pl.pallas_call(
        _route_kernel,
        grid=(n // tm,),
        in_specs=[pl.BlockSpec((tm, d), lambda i: (i, 0)),
                  pl.BlockSpec((1, d), lambda i: (0, 0)),
                  pl.BlockSpec((1, 1, d), lambda i: (i // tpb, 0, 0)),
                  pl.BlockSpec((1, 1, d), lambda i: (i // tpb, 0, 0)),
                  pl.BlockSpec((d, LANES), lambda i: (0, 0)),
                  pl.BlockSpec((1, LANES), lambda i: (0, 0)),
                  pl.BlockSpec((tm, tm), lambda i: (0, 0))],
        out_specs=[pl.BlockSpec((tm, d // 2), lambda i: (i, 0)), tok_spec, tok_spec, tok_spec,
                   pl.BlockSpec((1, LANES), lambda i: (0, 0))],
        out_shape=[jax.ShapeDtypeStruct((n, d // 2), I32), jax.ShapeDtypeStruct((n, LANES), I32),
                   jax.ShapeDtypeStruct((n, LANES), F32), jax.ShapeDtypeStruct((n, LANES), I32),
                   jax.ShapeDtypeStruct((1, LANES), F32)],
        scratch_shapes=[pltpu.VMEM((1, LANES), F32)],
        compiler_params=_cparams(("arbitrary",)),
        name="moe_route",
    )(x2, g.reshape(1, d), scale[:, None, :], shift[:, None, :], wr, br, tri)


def _invert_kernel(lo_ref, hi_ref, pos_ref, src_ref, *, chunk):
    step = pl.program_id(0)

    @pl.when(step == 0)
    def _():
        def fill_group(g, carry):
            def zero(p, c):
                src_ref[p] = 0
                return c
            lax.fori_loop(lo_ref[g], hi_ref[g], zero, 0)
            return carry
        lax.fori_loop(0, lo_ref.shape[0], fill_group, 0)

    tokens = chunk // MOE_TOPK

    def put(t, carry):
        for k in range(MOE_TOPK):
            src_ref[pos_ref[t * MOE_TOPK + k]] = step * tokens + t
        return carry

    lax.fori_loop(0, tokens, put, 0, unroll=4)


def _moe_invert(pos, pad_lo, pad_hi, n_rows, chunk=2048):
    n_assign = pos.shape[0]
    return pl.pallas_call(
        functools.partial(_invert_kernel, chunk=chunk),
        grid_spec=pltpu.PrefetchScalarGridSpec(
            num_scalar_prefetch=2, grid=(n_assign // chunk,),
            in_specs=[pl.BlockSpec((chunk,), lambda i, lo, hi: (i,), memory_space=pltpu.SMEM)],
            out_specs=pl.BlockSpec((n_rows,), lambda i, lo, hi: (0,), memory_space=pltpu.SMEM)),
        out_shape=jax.ShapeDtypeStruct((n_rows,), I32),
        compiler_params=_cparams(("arbitrary",), unchecked=True),
        name="moe_invert",
    )(pad_lo, pad_hi, pos)


def _expert_kernel(te_ref, nt_ref, src_ref, h_ref, wi_ref, bi_ref, perm_ref, wo_ref, bo_ref, y_ref,
                   xbuf_ref, sem):
    i = pl.program_id(0)
    n_used = nt_ref[0]
    n_slots, tm = xbuf_ref.shape[0], xbuf_ref.shape[1]

    def row_copy(row, slot, t):
        return pltpu.make_async_copy(h_ref.at[pl.ds(src_ref[row], 1)], xbuf_ref.at[slot, pl.ds(t, 1)], sem.at[slot])

    def wait_slot(slot):
        pltpu.make_async_copy(h_ref.at[pl.ds(0, tm)], xbuf_ref.at[slot], sem.at[slot]).wait()

    def gather(tile, slot):
        base = jnp.minimum(tile, n_used - 1) * tm
        for t in range(tm):
            row_copy(base + t, slot, t).start()

    @pl.when(i == 0)
    def _():
        gather(0, 0)
        gather(1, 1)

    for slot in range(n_slots):
        @pl.when((i < n_used) & (i % n_slots == slot))
        def _(slot=slot):
            wait_slot(slot)
            x_lo, x_hi = _unpack_halves(xbuf_ref[slot])
            x_lo, x_hi = x_lo.astype(BF16), x_hi.astype(BF16)
            half = x_lo.shape[1]
            gather(i + 2, (slot + 2) % n_slots)
            hh = (jnp.dot(x_lo, wi_ref[0, :half], preferred_element_type=F32)
                  + jnp.dot(x_hi, wi_ref[0, half:], preferred_element_type=F32) + bi_ref[0]).astype(BF16)
            hp = jnp.dot(hh, perm_ref[...], preferred_element_type=F32)
            x_glu = jnp.minimum(hp[:, :EXPERT_FF], SWIGLU_LIMIT)
            x_lin = jnp.clip(hp[:, EXPERT_FF:], -SWIGLU_LIMIT, SWIGLU_LIMIT)
            act = x_glu * (1.0 / (1.0 + jnp.exp(-SWIGLU_ALPHA * x_glu))) * (x_lin + 1.0)
            y_ref[...] = _pack_halves(jnp.dot(act.astype(BF16), wo_ref[0], preferred_element_type=F32) + bo_ref[0])

            @pl.when(i == n_used - 1)
            def _():
                wait_slot((slot + 1) % n_slots)
                wait_slot((slot + 2) % n_slots)

    @pl.when(i >= n_used)
    def _():
        y_ref[...] = jnp.zeros_like(y_ref)


def _moe_experts(h2, src, tile_expert, n_tiles_used, w_in, b_in, w_out, b_out):
    n, dh = h2.shape
    d = 2 * dh
    r = src.shape[0]
    tm = MOE_ROW_TILE
    f2 = w_in.shape[2]
    ff = w_out.shape[1]
    col = np.arange(f2)
    perm = np.zeros((f2, f2), np.float32)
    perm[col, np.where(col % 2 == 0, col // 2, ff + col // 2)] = 1.0
    grid_spec = pltpu.PrefetchScalarGridSpec(
        num_scalar_prefetch=3, grid=(r // tm,),
        in_specs=[pl.BlockSpec(memory_space=pl.ANY),
                  pl.BlockSpec((1, d, f2), lambda i, te, nt, src: (te[i], 0, 0)),
                  pl.BlockSpec((1, 1, f2), lambda i, te, nt, src: (te[i], 0, 0)),
                  pl.BlockSpec((f2, f2), lambda i, te, nt, src: (0, 0)),
                  pl.BlockSpec((1, ff, d), lambda i, te, nt, src: (te[i], 0, 0)),
                  pl.BlockSpec((1, 1, d), lambda i, te, nt, src: (te[i], 0, 0))],
        out_specs=pl.BlockSpec((tm, dh), lambda i, te, nt, src: (i, 0)),
        scratch_shapes=[pltpu.VMEM((3, tm, dh), I32), pltpu.SemaphoreType.DMA((3,))])
    return pl.pallas_call(
        _expert_kernel, grid_spec=grid_spec,
        out_shape=jax.ShapeDtypeStruct((r, dh), I32),
        compiler_params=_cparams(("arbitrary",), unchecked=True),
        name="moe_experts",
    )(tile_expert, n_tiles_used, src, h2, w_in, b_in, jnp.asarray(perm, BF16), w_out, b_out)


def _combine_kernel(pos_ref, y_ref, x_ref, w_ref, g_ref, o_ref, buf_ref, wb_ref, sem, *, tm):
    i = pl.program_id(0)
    n_steps = pl.num_programs(0)

    def gather(tile, slot):
        base = tile * tm * MOE_TOPK
        for t in range(tm):
            for k in range(MOE_TOPK):
                pltpu.make_async_copy(y_ref.at[pl.ds(pos_ref[base + t * MOE_TOPK + k], 1)],
                                      buf_ref.at[slot, k, pl.ds(t, 1)], sem.at[slot]).start()

    @pl.when(i == 0)
    def _():
        gather(0, 0)

    @pl.when((i + 1 < n_steps) & (i % 2 == 0))
    def _():
        gather(i + 1, 1)

    @pl.when((i + 1 < n_steps) & (i % 2 == 1))
    def _():
        gather(i + 1, 0)

    slot = i % 2
    for k in range(MOE_TOPK):
        pltpu.make_async_copy(y_ref.at[pl.ds(0, tm)], buf_ref.at[slot, k], sem.at[slot]).wait()
    w = w_ref[...]
    half = buf_ref.shape[3]
    for k in range(MOE_TOPK):
        wb_ref[k] = jnp.broadcast_to(w[:, k:k + 1], (tm, LANES))
    for c in range(half // LANES):
        lo_cols = slice(c * LANES, (c + 1) * LANES)
        hi_cols = slice(half + c * LANES, half + (c + 1) * LANES)
        mix_lo = jnp.zeros((tm, LANES), F32)
        mix_hi = jnp.zeros((tm, LANES), F32)
        for k in range(MOE_TOPK):
            y_lo, y_hi = _unpack_halves(buf_ref[slot, k, :, lo_cols])
            mix_lo = mix_lo + y_lo * wb_ref[k]
            mix_hi = mix_hi + y_hi * wb_ref[k]
        o_ref[:, lo_cols] = x_ref[:, lo_cols] + g_ref[0, :, lo_cols] * mix_lo
        o_ref[:, hi_cols] = x_ref[:, hi_cols] + g_ref[0, :, hi_cols] * mix_hi


def _moe_combine(y, pos, x2, wgt, gate, rows_per_batch, tm=128):
    n, d = x2.shape
    tpb = rows_per_batch // tm
    grid_spec = pltpu.PrefetchScalarGridSpec(
        num_scalar_prefetch=1, grid=(n // tm,),
        in_specs=[pl.BlockSpec(memory_space=pl.ANY),
                  pl.BlockSpec((tm, d), lambda i, pos: (i, 0)),
                  pl.BlockSpec((tm, LANES), lambda i, pos: (i, 0)),
                  pl.BlockSpec((1, 1, d), lambda i, pos: (i // tpb, 0, 0))],
        out_specs=pl.BlockSpec((tm, d), lambda i, pos: (i, 0)),
        scratch_shapes=[pltpu.VMEM((2, MOE_TOPK, tm, d // 2), I32), pltpu.VMEM((MOE_TOPK, tm, LANES), F32),
                        pltpu.SemaphoreType.DMA((2,))])
    return pl.pallas_call(
        functools.partial(_combine_kernel, tm=tm), grid_spec=grid_spec,
        out_shape=jax.ShapeDtypeStruct((n, d), F32),
        compiler_params=_cparams(("arbitrary",), unchecked=True),
        name="moe_combine",
    )(pos, y, x2, wgt, gate[:, None, :])


def _moe_block(x2, g, scale, shift, gate, w_router, b_router, w_exp_in, b_exp_in, w_exp_out, b_exp_out,
               rows_per_batch):
    n, d = x2.shape
    tm = MOE_ROW_TILE
    h2, idx, wgt, rank, counts = _moe_route(x2, g, scale, shift, w_router, b_router, rows_per_batch)
    cnt = counts[0, :N_EXPERTS].astype(I32)
    padded = (cnt + tm - 1) // tm * tm
    ends = jnp.cumsum(padded)
    starts = ends - padded
    n_rows = n * MOE_TOPK + N_EXPERTS * tm
    n_tiles = n_rows // tm
    tile_start = jnp.arange(n_tiles, dtype=I32) * tm
    tile_expert = jnp.minimum(jnp.sum((ends[None, :] <= tile_start[:, None]).astype(I32), axis=1), N_EXPERTS - 1)
    n_tiles_used = (ends[-1] // tm).astype(I32).reshape(1)
    e_flat = idx[:, :MOE_TOPK].reshape(-1)
    pos = (starts[e_flat] + rank[:, :MOE_TOPK].reshape(-1)).astype(I32)
    pad_lo = jnp.concatenate([starts + cnt, ends[-1:]]).astype(I32)
    pad_hi = jnp.concatenate([ends, jnp.full((1,), n_rows, I32)]).astype(I32)
    src = _moe_invert(pos, pad_lo, pad_hi, n_rows)
    y = _moe_experts(h2, src, tile_expert, n_tiles_used, w_exp_in.astype(BF16), b_exp_in[:, None, :],
                     w_exp_out.astype(BF16), b_exp_out[:, None, :])
    return _moe_combine(y, pos, x2, wgt, gate, rows_per_batch)


def _even_mixer(h, x2, gate, batch, s_len, w_in, g_cq, w_uq, w_iq, gq_a, gk_a, g_kidx, b_kidx, gq_b, gk_b,
                b_f, w_out):
    gw = GROUP_WIDTH
    o_cq, o_ka, o_va, o_ki, o_wi, o_qb, o_kb, o_vb, o_fb, o_gb = np.cumsum(
        [0, DSA_Q_LORA, HEAD_DIM, HEAD_DIM, HEAD_DIM, DSA_IDX_HEADS, gw, gw, gw, GROUP_HEADS]).tolist()
    cols = lambda o, wdt: w_in[:, o:o + wdt]
    w_main = jnp.concatenate([cols(o_qb, gw), cols(o_kb, gw), cols(o_vb, gw), cols(o_gb, gw),
                              cols(o_cq, DSA_Q_LORA), cols(o_ka, HEAD_DIM), cols(o_va, HEAD_DIM),
                              cols(o_ki, HEAD_DIM)], axis=1).astype(BF16)
    c_qb, c_kb, c_vb, c_gb, c_cq = 0, gw, 2 * gw, 3 * gw, 4 * gw
    c_ka = c_cq + DSA_Q_LORA
    c_va, c_ki = c_ka + HEAD_DIM, c_ka + 2 * HEAD_DIM
    d = w_in.shape[0]
    w_small = jnp.concatenate([cols(o_fb, GROUP_HEADS), cols(o_wi, DSA_IDX_HEADS),
                               jnp.zeros((d, LANES - GROUP_HEADS - DSA_IDX_HEADS), F32)], axis=1).astype(BF16)
    z = _matmul(h, w_main, BF16, "even_in_proj")
    zs = _matmul(h, w_small, F32, "even_in_proj_small")
    scale = Q_SCALE

    cq = _rowprep(z, c_cq, DSA_Q_LORA, batch, s_len, hw=DSA_Q_LORA, norm="rms", gain=g_cq, name="dsa_cq_norm")
    qa_raw = _matmul(cq, w_uq.astype(BF16), BF16, "dsa_q_up")
    qi_raw = _matmul(cq, w_iq.astype(BF16), BF16, "dsa_idx_q_up")
    qa = _rowprep(qa_raw, 0, gw, batch, s_len, norm="rms", gain=gq_a, rope_dim=HEAD_DIM, out_scale=scale,
                  head_major=True, name="dsa_q_prep")
    qi = _rowprep(qi_raw, 0, DSA_IDX_HEADS * HEAD_DIM, batch, s_len, rope_dim=DSA_IDX_ROPE, head_major=True,
                  name="dsa_idx_q_prep")
    ka = _rowprep(z, c_ka, HEAD_DIM, batch, s_len, norm="rms", gain=gk_a, rope_dim=HEAD_DIM, name="dsa_k_prep")
    ki = _rowprep(z, c_ki, HEAD_DIM, batch, s_len, norm="ln", gain=g_kidx, beta=b_kidx, rope_dim=DSA_IDX_ROPE,
                  name="dsa_idx_k_prep")
    va = z[:, c_va:c_va + HEAD_DIM]
    wi = jnp.pad(zs[:, GROUP_HEADS:GROUP_HEADS + DSA_IDX_HEADS] * (DSA_IDX_HEADS ** -0.5 * HEAD_DIM ** -0.5),
                 ((0, 0), (0, LANES - DSA_IDX_HEADS)))
    o_a = _dsa_attention(qi, ki, wi, qa, ka, va, batch, s_len)

    qb = _rowprep(z, c_qb, gw, batch, s_len, norm="rms", gain=gq_b, out_scale=scale, head_major=True,
                  name="fox_q_prep")
    kb = _rowprep(z, c_kb, gw, batch, s_len, norm="rms", gain=gk_b, head_major=True, name="fox_k_prep")
    bias_row = jnp.pad(b_f.astype(F32), (0, LANES - GROUP_HEADS)).reshape(1, LANES)
    cum = _fox_cumsum(zs, bias_row, batch, s_len)
    o_b = _fox_attention(qb, kb, z, c_vb, c_gb, cum, batch, s_len)
    o_cat = jnp.concatenate([o_a, o_b], axis=1)
    return _matmul(o_cat, w_out.astype(BF16), F32, "even_out_proj", residual=x2, gate=gate,
                   rows_per_batch=s_len)


def _odd_mixer(h, x2, gate, batch, s_len, w_in, gq_c, gk_c, gq_d, gk_d, sinks, w_out):
    gw = GROUP_WIDTH
    kvw = SWA_KV_HEADS * HEAD_DIM
    c_qc, c_kc, c_vc, c_qd, c_kd = 0, gw, 2 * gw, 3 * gw, 4 * gw
    c_vd = c_kd + kvw
    z = _matmul(h, w_in.astype(BF16), BF16, "odd_in_proj")
    scale = Q_SCALE
    qc = _rowprep(z, c_qc, gw, batch, s_len, norm="rms", gain=gq_c, rope_dim=HEAD_DIM, out_scale=scale,
                  head_major=True, name="moba_q_prep")
    kc, kmean = _rowprep(z, c_kc, gw, batch, s_len, norm="rms", gain=gk_c, rope_dim=HEAD_DIM, head_major=True,
                         want_mean=True, name="moba_k_prep")
    o_c = _moba_attention(qc, kc, kmean, z, c_vc, batch, s_len)
    qd = _rowprep(z, c_qd, gw, batch, s_len, norm="rms", gain=gq_d, rope_dim=HEAD_DIM, out_scale=scale,
                  head_major=True, name="swa_q_prep")
    kd = _rowprep(z, c_kd, kvw, batch, s_len, norm="rms", gain=gk_d, rope_dim=HEAD_DIM, head_major=True,
                  name="swa_k_prep")
    o_d = _swa_attention(qd, kd, z, c_vd, sinks, batch, s_len)
    o_cat = jnp.concatenate([o_c, o_d], axis=1)
    return _matmul(o_cat, w_out.astype(BF16), F32, "odd_out_proj", residual=x2, gate=gate, rows_per_batch=s_len)


def kernel(x, c, g_norm_mix, g_norm_ffn, w_ada, b_ada, w_in_even, g_cq, w_uq, w_iq, gq_a, gk_a, g_kidx, b_kidx,
           gq_b, gk_b, b_forget, w_out_even, w_in_odd, gq_c, gk_c, gq_d, gk_d, sinks_d, w_out_odd,
           w_router, b_router, w_exp_in, b_exp_in, w_exp_out, b_exp_out):
    batch, s_len, d = x.shape
    depth = w_ada.shape[0]
    mod = _ada_modulation(c, w_ada, b_ada)
    x2 = x.reshape(batch * s_len, d)
    for layer in range(depth):
        shift_m, scale_m, gate_m, shift_f, scale_f, gate_f = [mod[layer, :, i * d:(i + 1) * d] for i in range(6)]
        h = _norm_modulate(x2, g_norm_mix[layer], scale_m, shift_m, s_len)
        j = layer // 2
        if layer % 2 == 0:
            x2 = _even_mixer(h, x2, gate_m, batch, s_len, w_in_even[j], g_cq[j], w_uq[j], w_iq[j], gq_a[j],
                             gk_a[j], g_kidx[j], b_kidx[j], gq_b[j], gk_b[j], b_forget[j], w_out_even[j])
        else:
            x2 = _odd_mixer(h, x2, gate_m, batch, s_len, w_in_odd[j], gq_c[j], gk_c[j], gq_d[j], gk_d[j],
                            sinks_d[j], w_out_odd[j])
        x2 = _moe_block(x2, g_norm_ffn[layer], scale_f, shift_f, gate_f, w_router[layer], b_router[layer],
                        w_exp_in[layer], b_exp_in[layer], w_exp_out[layer], b_exp_out[layer], s_len)
    return x2.reshape(batch, s_len, d)
```

```python
import functools

import numpy as np
import jax
import jax.numpy as jnp
from jax import lax
from jax.experimental import pallas as pl
from jax.experimental.pallas import tpu as pltpu

F32 = jnp.float32
BF16 = jnp.bfloat16
I32 = jnp.int32

LANES = 128
HEAD_DIM = 128
GROUP_HEADS = 16
GROUP_WIDTH = GROUP_HEADS * HEAD_DIM
ROPE_THETA = 10000.0
EPS = 1e-6
DSA_Q_LORA = 1024
DSA_IDX_HEADS = 32
DSA_IDX_ROPE = 64
DSA_TOPK = 256
DSA_QBLK = 128
DSA_CHUNK = 256
DSA_STAGES = 8
DSA_HEAD_GROUP = 2
MOBA_BLOCK = 256
MOBA_TOPK = 3
SWA_KV_HEADS = 2
SWA_WINDOW = 128
N_EXPERTS = 32
MOE_TOPK = 4
EXPERT_FF = 512
SWIGLU_LIMIT = 7.0
SWIGLU_ALPHA = 1.702
MOE_ROW_TILE = 256
NEG_BIG = -1e30
LOG2_E = 1.4426950408889634
Q_SCALE = HEAD_DIM ** -0.5 * LOG2_E
VMEM_LIMIT = 56 * 1024 * 1024


def _cparams(semantics, vmem=VMEM_LIMIT, unchecked=False):
    return pltpu.CompilerParams(dimension_semantics=semantics, vmem_limit_bytes=vmem,
                                disable_bounds_checks=unchecked)


def _ada_kernel(cb_ref, w_ref, b_ref, o_ref):
    nb = cb_ref.shape[0]
    tn = w_ref.shape[2]
    for c in range(tn // LANES):
        w = w_ref[0, :, c * LANES:(c + 1) * LANES]
        for b in range(nb):
            o_ref[0, b:b + 1, c * LANES:(c + 1) * LANES] = (
                jnp.sum(w * cb_ref[b], axis=0, keepdims=True) + b_ref[0, :, c * LANES:(c + 1) * LANES])


def _ada_modulation(c, w_ada, b_ada, tn=512):
    depth, d, n6 = w_ada.shape
    nb = c.shape[0]
    cb = jnp.broadcast_to(jax.nn.silu(c)[:, :, None], (nb, d, LANES))
    return pl.pallas_call(
        _ada_kernel,
        grid=(depth, n6 // tn),
        in_specs=[pl.BlockSpec((nb, d, LANES), lambda l, j: (0, 0, 0)),
                  pl.BlockSpec((1, d, tn), lambda l, j: (l, 0, j)),
                  pl.BlockSpec((1, 1, tn), lambda l, j: (l, 0, j))],
        out_specs=pl.BlockSpec((1, nb, tn), lambda l, j: (l, 0, j)),
        out_shape=jax.ShapeDtypeStruct((depth, nb, n6), F32),
        compiler_params=_cparams(("parallel", "parallel")),
        name="ada_modulation",
    )(cb, w_ada, b_ada.reshape(depth, 1, n6))


def _normmod_kernel(x_ref, g_ref, sc_ref, sh_ref, o_ref):
    x = x_ref[...]
    y = x * lax.rsqrt(jnp.mean(x * x, axis=-1, keepdims=True) + EPS) * g_ref[...]
    o_ref[...] = (y * (1.0 + sc_ref[0]) + sh_ref[0]).astype(o_ref.dtype)


def _norm_modulate(x2, g, scale, shift, rows_per_batch, tm=256):
    n, d = x2.shape
    tpb = rows_per_batch // tm
    return pl.pallas_call(
        _normmod_kernel,
        grid=(n // tm,),
        in_specs=[pl.BlockSpec((tm, d), lambda i: (i, 0)),
                  pl.BlockSpec((1, d), lambda i: (0, 0)),
                  pl.BlockSpec((1, 1, d), lambda i: (i // tpb, 0, 0)),
                  pl.BlockSpec((1, 1, d), lambda i: (i // tpb, 0, 0))],
        out_specs=pl.BlockSpec((tm, d), lambda i: (i, 0)),
        out_shape=jax.ShapeDtypeStruct((n, d), BF16),
        compiler_params=_cparams(("parallel",)),
        name="norm_modulate",
    )(x2, g.reshape(1, d), scale[:, None, :], shift[:, None, :])


def _mm_kernel(a_ref, b_ref, o_ref):
    o_ref[...] = jnp.dot(a_ref[...], b_ref[...], preferred_element_type=F32).astype(o_ref.dtype)


def _mm2_residual_kernel(a1_ref, a2_ref, b_ref, x_ref, g_ref, o_ref):
    k1 = a1_ref.shape[1]
    acc = (jnp.dot(a1_ref[...], b_ref[:k1], preferred_element_type=F32)
           + jnp.dot(a2_ref[...], b_ref[k1:], preferred_element_type=F32))
    o_ref[...] = x_ref[...] + g_ref[0] * acc


def _out_proj(a1, a2, b, residual, gate, rows_per_batch, name):
    m, k1 = a1.shape
    k2 = a2.shape[1]
    n = b.shape[1]
    tm = _pick(m, (1024, 512, 256, 128))
    tn = _pick(n, (512, 256, 128))
    tpb = rows_per_batch // tm
    o_spec = pl.BlockSpec((tm, tn), lambda i, j: (i, j))
    return pl.pallas_call(
        _mm2_residual_kernel, grid=(m // tm, n // tn),
        in_specs=[pl.BlockSpec((tm, k1), lambda i, j: (i, 0)), pl.BlockSpec((tm, k2), lambda i, j: (i, 0)),
                  pl.BlockSpec((k1 + k2, tn), lambda i, j: (0, j)), o_spec,
                  pl.BlockSpec((1, 1, tn), lambda i, j: (i // tpb, 0, j))],
        out_specs=o_spec, out_shape=jax.ShapeDtypeStruct((m, n), F32),
        compiler_params=_cparams(("parallel", "arbitrary")), name=name)(a1, a2, b, residual, gate[:, None, :])


def _pick(n, prefs):
    for t in prefs:
        if n % t == 0:
            return t
    return n


def _matmul(a, b, out_dtype, name):
    m, k = a.shape
    n = b.shape[1]
    tm = _pick(m, (1024, 512, 256, 128))
    tn = _pick(n, (512, 640, 384, 256, 128) if k > 2048 else (1024, 512, 640, 384, 256, 128))
    return pl.pallas_call(
        _mm_kernel, grid=(m // tm, n // tn),
        in_specs=[pl.BlockSpec((tm, k), lambda i, j: (i, 0)), pl.BlockSpec((k, tn), lambda i, j: (0, j))],
        out_specs=pl.BlockSpec((tm, tn), lambda i, j: (i, j)),
        out_shape=jax.ShapeDtypeStruct((m, n), out_dtype),
        compiler_params=_cparams(("parallel", "arbitrary")), name=name)(a, b)


def _rope_tables(s_len, rot_dim):
    half = rot_dim // 2
    inv_freq = ROPE_THETA ** (-jnp.arange(half, dtype=F32) / half)
    ang = jnp.arange(s_len, dtype=F32)[:, None] * inv_freq[None, :]
    cos, sin = jnp.cos(ang), jnp.sin(ang)
    pad = HEAD_DIM - rot_dim
    ones = jnp.ones((s_len, pad), F32)
    zeros = jnp.zeros((s_len, pad), F32)
    zh = jnp.zeros((s_len, half), F32)
    cos_t = jnp.concatenate([cos, cos, ones], axis=1)
    sa = jnp.concatenate([-sin, zh, zeros], axis=1)
    sb = jnp.concatenate([zh, sin, zeros], axis=1)
    return cos_t, sa, sb


def _rowprep_kernel(*refs, hw, norm, rope_half, out_scale, head_major, want_mean):
    it = iter(refs)
    z_ref = next(it)
    g_ref = next(it) if norm != "none" else None
    beta_ref = next(it) if norm == "ln" else None
    if rope_half:
        cos_ref, sa_ref, sb_ref = next(it), next(it), next(it)
    o_ref = next(it)
    mean_ref = next(it) if want_mean else None
    width = z_ref.shape[1]
    for h in range(width // hw):
        x = z_ref[:, h * hw:(h + 1) * hw].astype(F32)
        if norm == "rms":
            x = x * lax.rsqrt(jnp.mean(x * x, axis=-1, keepdims=True) + EPS) * g_ref[...]
        elif norm == "ln":
            mu = jnp.mean(x, axis=-1, keepdims=True)
            xc = x - mu
            x = xc * lax.rsqrt(jnp.mean(xc * xc, axis=-1, keepdims=True) + EPS) * g_ref[...] + beta_ref[...]
        if rope_half:
            if rope_half * 2 == hw:
                x = x * cos_ref[...] + pltpu.roll(x, rope_half, 1) * (sa_ref[...] + sb_ref[...])
            else:
                x = (x * cos_ref[...] + pltpu.roll(x, hw - rope_half, 1) * sa_ref[...]
                     + pltpu.roll(x, rope_half, 1) * sb_ref[...])
        if want_mean:
            mean_ref[0, 0, h:h + 1, :] = jnp.mean(x, axis=0, keepdims=True)
        if out_scale != 1.0:
            x = x * out_scale
        if head_major:
            o_ref[0, h] = x.astype(o_ref.dtype)
        else:
            o_ref[:, h * hw:(h + 1) * hw] = x.astype(o_ref.dtype)


def _rowprep(z, col0, width, batch, s_len, *, hw=HEAD_DIM, norm="none", gain=None, beta=None, rope_dim=0,
             out_scale=1.0, head_major=False, want_mean=False, name="rowprep", tm=256):
    n = z.shape[0]
    tpb = s_len // tm
    assert col0 % width == 0 and n == batch * s_len
    cb = col0 // width
    args = [z]
    specs = [pl.BlockSpec((tm, width), lambda i: (i, cb))]
    if norm != "none":
        args.append(gain.reshape(1, hw).astype(F32))
        specs.append(pl.BlockSpec((1, hw), lambda i: (0, 0)))
    if norm == "ln":
        args.append(beta.reshape(1, hw).astype(F32))
        specs.append(pl.BlockSpec((1, hw), lambda i: (0, 0)))
    if rope_dim:
        args += list(_rope_tables(s_len, rope_dim))
        specs += [pl.BlockSpec((tm, HEAD_DIM), lambda i: (i % tpb, 0))] * 3
    nh = width // hw
    if head_major:
        out_shape = [jax.ShapeDtypeStruct((batch, nh, s_len, hw), BF16)]
        out_specs = [pl.BlockSpec((1, nh, tm, hw), lambda i: (i // tpb, 0, i % tpb, 0))]
    else:
        out_shape = [jax.ShapeDtypeStruct((n, width), BF16)]
        out_specs = [pl.BlockSpec((tm, width), lambda i: (i, 0))]
    if want_mean:
        assert tm == MOBA_BLOCK
        out_shape.append(jax.ShapeDtypeStruct((batch, tpb, nh, hw), F32))
        out_specs.append(pl.BlockSpec((1, 1, nh, hw), lambda i: (i // tpb, i % tpb, 0, 0)))
    kern = functools.partial(_rowprep_kernel, hw=hw, norm=norm, rope_half=rope_dim // 2, out_scale=out_scale,
                             head_major=head_major, want_mean=want_mean)
    res = pl.pallas_call(kern, grid=(n // tm,), in_specs=specs, out_specs=out_specs, out_shape=out_shape,
                         compiler_params=_cparams(("parallel",)), name=name)(*args)
    return res if want_mean else res[0]


def _fox_cum_kernel(z_ref, bf_ref, tri_ref, o_ref, carry_ref):
    @pl.when(pl.program_id(1) == 0)
    def _():
        carry_ref[...] = jnp.zeros_like(carry_ref)

    xv = z_ref[...] + bf_ref[...]
    lf = jnp.minimum(xv, 0.0) - jnp.log(1.0 + jnp.exp(-jnp.abs(xv)))
    hi = lf.astype(BF16)
    r1 = lf - hi.astype(F32)
    mid = r1.astype(BF16)
    lo = (r1 - mid.astype(F32)).astype(BF16)
    tri = tri_ref[...]
    cs = (jnp.dot(tri, hi, preferred_element_type=F32) + jnp.dot(tri, mid, preferred_element_type=F32)
          + jnp.dot(tri, lo, preferred_element_type=F32)) + carry_ref[...]
    o_ref[...] = cs * LOG2_E
    tm = cs.shape[0]
    carry_ref[...] = cs[tm - 1:tm, :]


def _fox_cumsum(zs, bias_row, batch, s_len, tm=256):
    n = zs.shape[0]
    tpb = s_len // tm
    tri = jnp.tril(jnp.ones((tm, tm), F32)).astype(BF16)
    return pl.pallas_call(
        _fox_cum_kernel,
        grid=(batch, tpb),
        in_specs=[pl.BlockSpec((tm, LANES), lambda b, i: (b * tpb + i, 0)),
                  pl.BlockSpec((1, LANES), lambda b, i: (0, 0)),
                  pl.BlockSpec((tm, tm), lambda b, i: (0, 0))],
        out_specs=pl.BlockSpec((tm, LANES), lambda b, i: (b * tpb + i, 0)),
        out_shape=jax.ShapeDtypeStruct((n, LANES), F32),
        scratch_shapes=[pltpu.VMEM((1, LANES), F32)],
        compiler_params=_cparams(("parallel", "arbitrary")),
        name="fox_cumsum",
    )(zs, bias_row, tri)


def _dsa_kernel(qi_ref, kit_ref, wi_ref, qa_ref, kat_ref, va_ref, o_ref, wb_ref, key_ref, mb_ref,
                *, sk, q0, topk):
    q = DSA_QBLK
    ch = DSA_CHUNK
    nch = sk // ch
    n_idx = qi_ref.shape[1]
    n_heads = qa_ref.shape[1]
    t0 = (q0 + pl.program_id(1)) * q
    row = t0 + lax.broadcasted_iota(I32, (q, ch), 0)
    lane = lax.broadcasted_iota(I32, (q, ch), 1)

    wi = wi_ref[...]
    for h in range(n_idx):
        wb_ref[h] = jnp.broadcast_to(wi[:, h:h + 1], (q, ch))
    qi = qi_ref[0].reshape(n_idx * q, HEAD_DIM)

    def idx_chunk(c, carry):
        d = jnp.dot(qi, kit_ref[0, c], preferred_element_type=F32)
        acc = jnp.zeros((q, ch), F32)
        for h in range(n_idx):
            acc = acc + jnp.maximum(d[h * q:(h + 1) * q], 0.0) * wb_ref[h]
        score = jnp.where(c * ch + lane <= row, acc + 0.0, -jnp.inf)
        bits = lax.bitcast_convert_type(score, I32)
        key_ref[c] = jnp.where(bits >= 0, bits, bits ^ jnp.int32(0x7FFFFFFF))
        return carry

    lax.fori_loop(0, nch, idx_chunk, 0)

    def count_ge(cand):
        acc = jnp.zeros((q, LANES), F32)
        for c in range(nch):
            for part in range(ch // LANES):
                acc = acc + jnp.where(key_ref[c, :, part * LANES:(part + 1) * LANES] >= cand, 1.0, 0.0)
        return jnp.sum(acc, axis=-1, keepdims=True)

    int_min = jnp.int32(-2 ** 31)
    thr = jnp.where(count_ge(jnp.zeros((q, 1), I32)) >= topk, jnp.int32(0), int_min)

    def bit_step(j, thr):
        cand = thr | jnp.left_shift(jnp.int32(1), 30 - j)
        return jnp.where(count_ge(cand) >= topk, cand, thr)

    thr = lax.fori_loop(0, 31, bit_step, thr)

    for c in range(nch):
        keep = (key_ref[c] >= thr) & (c * ch + lane <= row)
        mb_ref[:, c * ch:(c + 1) * ch] = jnp.where(keep, 0.0, -jnp.inf)

    g = DSA_HEAD_GROUP

    for hg in range(n_heads // g):
        qh = qa_ref[0, hg * g:(hg + 1) * g].reshape(g * q, HEAD_DIM)
        lg = jnp.dot(qh, kat_ref[0], preferred_element_type=F32).reshape(g, q, sk) + mb_ref[...][None]
        m = jnp.max(lg, axis=-1, keepdims=True)
        p = jnp.exp2(lg - m)
        l = jnp.sum(p, axis=-1, keepdims=True)
        o = jnp.dot(p.reshape(g * q, sk).astype(BF16), va_ref[0], preferred_element_type=F32)
        o = (o.reshape(g, q, HEAD_DIM) / l).astype(o_ref.dtype)
        for hh in range(g):
            head = hg * g + hh
            o_ref[0, :, head * HEAD_DIM:(head + 1) * HEAD_DIM] = o[hh]


def _dsa_attention(qi, ki, wi, qa, ka, va, batch, s_len):
    q, ch = DSA_QBLK, DSA_CHUNK
    n_idx, n_heads = qi.shape[1], qa.shape[1]
    topk = min(DSA_TOPK, s_len // 4)
    nqb = s_len // q
    kit = ki.reshape(batch, s_len // ch, ch, HEAD_DIM).transpose(0, 1, 3, 2)
    kat = ka.reshape(batch, s_len, HEAD_DIM).transpose(0, 2, 1)
    va3 = va.reshape(batch, s_len, HEAD_DIM)
    per_stage = max(nqb // DSA_STAGES, ch // q)
    outs = []
    for q0 in range(0, nqb, per_stage):
        sk = (q0 + per_stage) * q
        nch = sk // ch
        kern = functools.partial(_dsa_kernel, sk=sk, q0=q0, topk=topk)
        outs.append(pl.pallas_call(
            kern,
            grid=(batch, per_stage),
            in_specs=[pl.BlockSpec((1, n_idx, q, HEAD_DIM), lambda b, i, q0=q0: (b, 0, q0 + i, 0)),
                      pl.BlockSpec((1, nch, HEAD_DIM, ch), lambda b, i: (b, 0, 0, 0)),
                      pl.BlockSpec((q, LANES), lambda b, i, q0=q0: (b * nqb + q0 + i, 0)),
                      pl.BlockSpec((1, n_heads, q, HEAD_DIM), lambda b, i, q0=q0: (b, 0, q0 + i, 0)),
                      pl.BlockSpec((1, HEAD_DIM, sk), lambda b, i: (b, 0, 0)),
                      pl.BlockSpec((1, sk, HEAD_DIM), lambda b, i: (b, 0, 0))],
            out_specs=pl.BlockSpec((1, q, n_heads * HEAD_DIM), lambda b, i: (b, i, 0)),
            out_shape=jax.ShapeDtypeStruct((batch, per_stage * q, n_heads * HEAD_DIM), BF16),
            scratch_shapes=[pltpu.VMEM((n_idx, q, ch), F32),
                            pltpu.VMEM((nch, q, ch), I32),
                            pltpu.VMEM((q, sk), F32)],
            compiler_params=_cparams(("parallel", "parallel")),
            name=f"dsa_attention_k{sk}",
        )(qi, kit, wi, qa, kat, va3))
    return jnp.concatenate(outs, axis=1).reshape(batch * s_len, n_heads * HEAD_DIM)


def _tri_tables(nq, ratio):
    qt, kt = [], []
    for i in range(nq):
        for j in range((i + 1) * ratio):
            qt.append(i)
            kt.append(j)
    return jnp.asarray(qt, I32), jnp.asarray(kt, I32)


def _fox_kernel(qt_ref, kt_ref, q_ref, k_ref, v_ref, cq_ref, ck_ref, g_ref, o_ref,
                m_ref, l_ref, acc_ref, cqs_ref, *, tq):
    h = pl.program_id(1)
    step = pl.program_id(2)
    qi = qt_ref[step]
    kj = kt_ref[step]

    @pl.when(kj == 0)
    def _():
        m_ref[...] = jnp.full_like(m_ref, -jnp.inf)
        l_ref[...] = jnp.zeros_like(l_ref)
        acc_ref[...] = jnp.zeros_like(acc_ref)
        lane = lax.broadcasted_iota(I32, cq_ref.shape, 1)
        cqs_ref[...] = jnp.sum(jnp.where(lane == h, cq_ref[...], 0.0), axis=-1, keepdims=True)

    s = lax.dot_general(q_ref[0, 0], k_ref[0, 0], (((1,), (1,)), ((), ())), preferred_element_type=F32)
    s = s + (cqs_ref[...] - ck_ref[0, 0])

    def update(s):
        m_prev = m_ref[...]
        m_new = jnp.maximum(m_prev, jnp.max(s, axis=-1, keepdims=True))
        alpha = jnp.exp2(m_prev - m_new)
        p = jnp.exp2(s - m_new)
        l_ref[...] = alpha * l_ref[...] + jnp.sum(p, axis=-1, keepdims=True)
        acc_ref[...] = alpha * acc_ref[...] + jnp.dot(p.astype(BF16), v_ref[...], preferred_element_type=F32)
        m_ref[...] = m_new

    @pl.when(kj < qi)
    def _():
        update(s)

    @pl.when(kj == qi)
    def _():
        row = lax.broadcasted_iota(I32, s.shape, 0)
        col = lax.broadcasted_iota(I32, s.shape, 1)
        update(jnp.where(col <= row, s, -jnp.inf))
        gate = g_ref[...].astype(F32)
        o_ref[...] = (acc_ref[...] / l_ref[...] * (1.0 / (1.0 + jnp.exp(-gate)))).astype(o_ref.dtype)


def _fox_attention(qb, kb, z, v_col0, g_col0, cum, batch, s_len, tq=1024):
    tq = min(tq, s_len)
    nq = s_len // tq
    n_heads = qb.shape[1]
    qt, kt = _tri_tables(nq, 1)
    cum_t = cum.reshape(batch, s_len, LANES)[:, :, :n_heads].transpose(0, 2, 1)[:, :, None, :]
    vb0, gb0 = v_col0 // HEAD_DIM, g_col0 // HEAD_DIM
    grid_spec = pltpu.PrefetchScalarGridSpec(
        num_scalar_prefetch=2,
        grid=(batch, n_heads, int(qt.shape[0])),
        in_specs=[pl.BlockSpec((1, 1, tq, HEAD_DIM), lambda b, h, s, qt, kt: (b, h, qt[s], 0)),
                  pl.BlockSpec((1, 1, tq, HEAD_DIM), lambda b, h, s, qt, kt: (b, h, kt[s], 0)),
                  pl.BlockSpec((tq, HEAD_DIM), lambda b, h, s, qt, kt: (b * nq + kt[s], vb0 + h)),
                  pl.BlockSpec((tq, LANES), lambda b, h, s, qt, kt: (b * nq + qt[s], 0)),
                  pl.BlockSpec((1, 1, 1, tq), lambda b, h, s, qt, kt: (b, h, 0, kt[s])),
                  pl.BlockSpec((tq, HEAD_DIM), lambda b, h, s, qt, kt: (b * nq + qt[s], gb0 + h))],
        out_specs=pl.BlockSpec((tq, HEAD_DIM), lambda b, h, s, qt, kt: (b * nq + qt[s], h)),
        scratch_shapes=[pltpu.VMEM((tq, 1), F32), pltpu.VMEM((tq, 1), F32),
                        pltpu.VMEM((tq, HEAD_DIM), F32), pltpu.VMEM((tq, 1), F32)])
    return pl.pallas_call(
        functools.partial(_fox_kernel, tq=tq), grid_spec=grid_spec,
        out_shape=jax.ShapeDtypeStruct((batch * s_len, n_heads * HEAD_DIM), BF16),
        compiler_params=_cparams(("parallel", "parallel", "arbitrary")),
        name="fox_attention",
    )(qt, kt, qb, kb, z, cum, cum_t, z)


def _moba_kernel(qt_ref, kt_ref, q_ref, k_ref, e_ref, v_ref, km_ref, o_ref, m_ref, l_ref, acc_ref, qa_ref,
                 *, tq, n_sel):
    step = pl.program_id(2)
    qi = qt_ref[step]
    kj = kt_ref[step]

    @pl.when(kj == 0)
    def _():
        m_ref[...] = jnp.full_like(m_ref, NEG_BIG)
        l_ref[...] = jnp.zeros_like(l_ref)
        acc_ref[...] = jnp.zeros_like(acc_ref)
        own = (qi * tq + lax.broadcasted_iota(I32, (tq, 1), 0)) // MOBA_BLOCK
        gate = jnp.dot(q_ref[0, 0].astype(F32), km_ref[0, 0], preferred_element_type=F32,
                       precision=lax.Precision.HIGHEST)
        lane = lax.broadcasted_iota(I32, gate.shape, 1)
        lane_f = lane.astype(F32)
        gate = jnp.where(lane < own, gate, -jnp.inf)
        allowed = jnp.where(lane == own, 1.0, 0.0)
        for _ in range(n_sel):
            best = jnp.max(gate, axis=-1, keepdims=True)
            first = jnp.min(jnp.where(gate == best, lane_f, float(LANES)), axis=-1, keepdims=True)
            pick = (lane_f == first) & (best > -jnp.inf)
            allowed = jnp.where(pick, 1.0, allowed)
            gate = jnp.where(pick, -jnp.inf, gate)
        qa_ref[:, :HEAD_DIM] = q_ref[0, 0]
        qa_ref[:, HEAD_DIM:] = jnp.where(allowed > 0.0, 0.0, NEG_BIG).astype(qa_ref.dtype)

    k_aug = jnp.concatenate([k_ref[0, 0], e_ref[...]], axis=1)
    s = lax.dot_general(qa_ref[...], k_aug, (((1,), (1,)), ((), ())), preferred_element_type=F32)

    def update(s):
        m_prev = m_ref[...]
        m_new = jnp.maximum(m_prev, jnp.max(s, axis=-1, keepdims=True))
        alpha = jnp.exp2(m_prev - m_new)
        p = jnp.exp2(s - m_new)
        l_ref[...] = alpha * l_ref[...] + jnp.sum(p, axis=-1, keepdims=True)
        acc_ref[...] = alpha * acc_ref[...] + jnp.dot(p.astype(BF16), v_ref[...], preferred_element_type=F32)
        m_ref[...] = m_new

    @pl.when(kj < qi)
    def _():
        update(s)

    @pl.when(kj == qi)
    def _():
        row = lax.broadcasted_iota(I32, s.shape, 0)
        col = lax.broadcasted_iota(I32, s.shape, 1)
        update(jnp.where(col <= row, s, NEG_BIG))
        o_ref[...] = (acc_ref[...] / l_ref[...]).astype(o_ref.dtype)


def _moba_attention(qc, kc, kmean, z, v_col0, batch, s_len, tq=1024):
    tq = min(tq, s_len)
    nq = s_len // tq
    n_heads = qc.shape[1]
    nb = s_len // MOBA_BLOCK
    assert nb <= LANES
    n_sel = min(MOBA_TOPK, nb - 1)
    qt, kt = _tri_tables(nq, 1)
    km_t = jnp.pad(kmean.transpose(0, 2, 3, 1), ((0, 0), (0, 0), (0, 0), (0, LANES - nb)))
    block_onehot = (jnp.arange(s_len, dtype=I32)[:, None] // MOBA_BLOCK
                    == jnp.arange(LANES, dtype=I32)[None, :]).astype(BF16)
    vb0 = v_col0 // HEAD_DIM
    grid_spec = pltpu.PrefetchScalarGridSpec(
        num_scalar_prefetch=2,
        grid=(batch, n_heads, int(qt.shape[0])),
        in_specs=[pl.BlockSpec((1, 1, tq, HEAD_DIM), lambda b, h, s, qt, kt: (b, h, qt[s], 0)),
                  pl.BlockSpec((1, 1, tq, HEAD_DIM), lambda b, h, s, qt, kt: (b, h, kt[s], 0)),
                  pl.BlockSpec((tq, LANES), lambda b, h, s, qt, kt: (kt[s], 0)),
                  pl.BlockSpec((tq, HEAD_DIM), lambda b, h, s, qt, kt: (b * nq + kt[s], vb0 + h)),
                  pl.BlockSpec((1, 1, HEAD_DIM, LANES), lambda b, h, s, qt, kt: (b, h, 0, 0))],
        out_specs=pl.BlockSpec((tq, HEAD_DIM), lambda b, h, s, qt, kt: (b * nq + qt[s], h)),
        scratch_shapes=[pltpu.VMEM((tq, 1), F32), pltpu.VMEM((tq, 1), F32),
                        pltpu.VMEM((tq, HEAD_DIM), F32), pltpu.VMEM((tq, HEAD_DIM + LANES), BF16)])
    return pl.pallas_call(
        functools.partial(_moba_kernel, tq=tq, n_sel=n_sel), grid_spec=grid_spec,
        out_shape=jax.ShapeDtypeStruct((batch * s_len, n_heads * HEAD_DIM), BF16),
        compiler_params=_cparams(("parallel", "parallel", "arbitrary")),
        name="moba_attention",
    )(qt, kt, qc, kc, block_onehot, z, km_t)


def _swa_kernel(q_ref, kp_ref, kc_ref, vp_ref, vc_ref, sink_ref, o_ref):
    n = pl.program_id(1)
    hq, w = q_ref.shape[1], q_ref.shape[2]
    hkv = kc_ref.shape[1]
    grp = hq // hkv
    ti = lax.broadcasted_iota(I32, (w, w), 0)
    si = lax.broadcasted_iota(I32, (w, w), 1)
    cur_ok = (si <= ti)[None]
    prev_ok = ((si > ti) & (n > 0))[None]
    nt = (((1,), (1,)), ((), ()))
    for kv in range(hkv):
        q = q_ref[0, kv * grp:(kv + 1) * grp].reshape(grp * w, HEAD_DIM)
        sc = lax.dot_general(q, kc_ref[0, kv], nt, preferred_element_type=F32).reshape(grp, w, w)
        sp = lax.dot_general(q, kp_ref[0, kv], nt, preferred_element_type=F32).reshape(grp, w, w)
        sc = jnp.where(cur_ok, sc, -jnp.inf)
        sp = jnp.where(prev_ok, sp, -jnp.inf)
        sink = sink_ref[kv * grp * w:(kv + 1) * grp * w].reshape(grp, w, LANES)[:, :, :1]
        m = jnp.maximum(jnp.maximum(jnp.max(sc, axis=-1, keepdims=True), jnp.max(sp, axis=-1, keepdims=True)), sink)
        pc = jnp.exp2(sc - m)
        pp = jnp.exp2(sp - m)
        den = jnp.sum(pc, axis=-1, keepdims=True) + jnp.sum(pp, axis=-1, keepdims=True) + jnp.exp2(sink - m)
        o = (jnp.dot(pc.reshape(grp * w, w).astype(BF16), vc_ref[:, kv * HEAD_DIM:(kv + 1) * HEAD_DIM],
                     preferred_element_type=F32)
             + jnp.dot(pp.reshape(grp * w, w).astype(BF16), vp_ref[:, kv * HEAD_DIM:(kv + 1) * HEAD_DIM],
                       preferred_element_type=F32))
        o = o.reshape(grp, w, HEAD_DIM) / den
        for gh in range(grp):
            hh = kv * grp + gh
            o_ref[:, hh * HEAD_DIM:(hh + 1) * HEAD_DIM] = o[gh].astype(o_ref.dtype)


def _swa_attention(qd, kd, z, v_col0, sinks, batch, s_len):
    w = SWA_WINDOW
    nb = s_len // w
    hq, hkv = qd.shape[1], kd.shape[1]
    vw = hkv * HEAD_DIM
    vb0 = v_col0 // vw
    sink_b = jnp.broadcast_to((sinks.astype(F32) * LOG2_E)[:, None, None], (hq, w, LANES)).reshape(hq * w, LANES)
    return pl.pallas_call(
        _swa_kernel,
        grid=(batch, nb),
        in_specs=[pl.BlockSpec((1, hq, w, HEAD_DIM), lambda b, n: (b, 0, n, 0)),
                  pl.BlockSpec((1, hkv, w, HEAD_DIM), lambda b, n: (b, 0, jnp.maximum(n - 1, 0), 0)),
                  pl.BlockSpec((1, hkv, w, HEAD_DIM), lambda b, n: (b, 0, n, 0)),
                  pl.BlockSpec((w, vw), lambda b, n: (b * nb + jnp.maximum(n - 1, 0), vb0)),
                  pl.BlockSpec((w, vw), lambda b, n: (b * nb + n, vb0)),
                  pl.BlockSpec((hq * w, LANES), lambda b, n: (0, 0))],
        out_specs=pl.BlockSpec((w, hq * HEAD_DIM), lambda b, n: (b * nb + n, 0)),
        out_shape=jax.ShapeDtypeStruct((batch * s_len, hq * HEAD_DIM), BF16),
        compiler_params=_cparams(("parallel", "parallel")),
        name="swa_attention",
    )(qd, kd, kd, z, z, sink_b)


_HIGH_HALF = -65536


def _pack_halves(x):
    half = x.shape[1] // 2
    lo = lax.bitcast_convert_type(x[:, :half].astype(jnp.bfloat16).astype(F32), I32)
    hi = lax.bitcast_convert_type(x[:, half:].astype(jnp.bfloat16).astype(F32), I32)
    return ((lo >> 16) & 0xFFFF) | (hi & _HIGH_HALF)


def _unpack_halves(p):
    lo = lax.bitcast_convert_type(p << 16, F32)
    hi = lax.bitcast_convert_type(p & _HIGH_HALF, F32)
    return lo, hi


def _route_kernel(x_ref, g_ref, sc_ref, sh_ref, wr_ref, br_ref, tri_ref,
                  h_ref, idx_ref, wgt_ref, rank_ref, cnt_ref, carry_ref):
    @pl.when(pl.program_id(0) == 0)
    def _():
        carry_ref[...] = jnp.zeros_like(carry_ref)

    x = x_ref[...]
    y = x * lax.rsqrt(jnp.mean(x * x, axis=-1, keepdims=True) + EPS) * g_ref[...]
    hmod = y * (1.0 + sc_ref[0]) + sh_ref[0]
    h_ref[...] = _pack_halves(hmod)
    logits = jnp.dot(hmod, wr_ref[...], preferred_element_type=F32, precision=lax.Precision.HIGHEST) + br_ref[...]
    lane = lax.broadcasted_iota(I32, logits.shape, 1)
    lane_f = lane.astype(F32)
    logits = jnp.where(lane < N_EXPERTS, logits, -jnp.inf)
    onehots, vals, firsts = [], [], []
    for _ in range(MOE_TOPK):
        best = jnp.max(logits, axis=-1, keepdims=True)
        first = jnp.min(jnp.where(logits == best, lane_f, float(LANES)), axis=-1, keepdims=True)
        pick = lane_f == first
        onehots.append(pick)
        vals.append(best)
        firsts.append(first.astype(I32))
        logits = jnp.where(pick, -jnp.inf, logits)
    exps = [jnp.exp(v - vals[0]) for v in vals]
    den = exps[0]
    for e in exps[1:]:
        den = den + e
    chosen_f = jnp.zeros(logits.shape, F32)
    for o in onehots:
        chosen_f = jnp.where(o, 1.0, chosen_f)
    before = jnp.dot(tri_ref[...], chosen_f.astype(BF16), preferred_element_type=F32) + carry_ref[...]
    idx_out = jnp.zeros(logits.shape, I32)
    wgt_out = jnp.zeros(logits.shape, F32)
    rank_out = jnp.zeros(logits.shape, I32)
    for k in range(MOE_TOPK):
        rk = jnp.sum(jnp.where(onehots[k], before, 0.0), axis=-1, keepdims=True).astype(I32)
        idx_out = jnp.where(lane == k, firsts[k], idx_out)
        wgt_out = jnp.where(lane == k, exps[k] / den, wgt_out)
        rank_out = jnp.where(lane == k, rk, rank_out)
    idx_ref[...] = idx_out
    wgt_ref[...] = wgt_out
    rank_ref[...] = rank_out
    carry_ref[...] = carry_ref[...] + jnp.sum(chosen_f, axis=0, keepdims=True)
    cnt_ref[...] = carry_ref[...]


def _moe_route(x2, g, scale, shift, w_router, b_router, rows_per_batch, tm=256):
    n, d = x2.shape
    tpb = rows_per_batch // tm
    wr = jnp.pad(w_router, ((0, 0), (0, LANES - N_EXPERTS)))
    br = jnp.pad(b_router, (0, LANES - N_EXPERTS)).reshape(1, LANES)
    tri = jnp.tril(jnp.ones((tm, tm), F32), -1).astype(BF16)
    tok_spec = pl.BlockSpec((tm, LANES), lambda i: (i, 0))
    return pl.pallas_call(
        _route_kernel,
        grid=(n // tm,),
        in_specs=[pl.BlockSpec((tm, d), lambda i: (i, 0)),
                  pl.BlockSpec((1, d), lambda i: (0, 0)),
                  pl.BlockSpec((1, 1, d), lambda i: (i // tpb, 0, 0)),
                  pl.BlockSpec((1, 1, d), lambda i: (i // tpb, 0, 0)),
                  pl.BlockSpec((d, LANES), lambda i: (0, 0)),
                  pl.BlockSpec((1, LANES), lambda i: (0, 0)),
                  pl.BlockSpec((tm, tm), lambda i: (0, 0))],
        out_specs=[pl.BlockSpec((tm, d // 2), lambda i: (i, 0)), tok_spec, tok_spec, tok_spec,
                   pl.BlockSpec((1, LANES), lambda i: (0, 0))],
        out_shape=[jax.ShapeDtypeStruct((n, d // 2), I32), jax.ShapeDtypeStruct((n, LANES), I32),
                   jax.ShapeDtypeStruct((n, LANES), F32), jax.ShapeDtypeStruct((n, LANES), I32),
                   jax.ShapeDtypeStruct((1, LANES), F32)],
        scratch_shapes=[pltpu.VMEM((1, LANES), F32)],
        compiler_params=_cparams(("arbitrary",)),
        name="moe_route",
    )(x2, g.reshape(1, d), scale[:, None, :], shift[:, None, :], wr, br, tri)


def _invert_kernel(lo_ref, hi_ref, pos_ref, src_ref, *, chunk):
    step = pl.program_id(0)

    @pl.when(step == 0)
    def _():
        def fill_group(g, carry):
            def zero(p, c):
                src_ref[p] = 0
                return c
            lax.fori_loop(lo_ref[g], hi_ref[g], zero, 0)
            return carry
        lax.fori_loop(0, lo_ref.shape[0], fill_group, 0)

    tokens = chunk // MOE_TOPK

    def put(t, carry):
        for k in range(MOE_TOPK):
            src_ref[pos_ref[t * MOE_TOPK + k]] = step * tokens + t
        return carry

    lax.fori_loop(0, tokens, put, 0, unroll=4)


def _moe_invert(pos, pad_lo, pad_hi, n_rows, chunk=2048):
    n_assign = pos.shape[0]
    return pl.pallas_call(
        functools.partial(_invert_kernel, chunk=chunk),
        grid_spec=pltpu.PrefetchScalarGridSpec(
            num_scalar_prefetch=2, grid=(n_assign // chunk,),
            in_specs=[pl.BlockSpec((chunk,), lambda i, lo, hi: (i,), memory_space=pltpu.SMEM)],
            out_specs=pl.BlockSpec((n_rows,), lambda i, lo, hi: (0,), memory_space=pltpu.SMEM)),
        out_shape=jax.ShapeDtypeStruct((n_rows,), I32),
        compiler_params=_cparams(("arbitrary",), unchecked=True),
        name="moe_invert",
    )(pad_lo, pad_hi, pos)


def _expert_kernel(te_ref, nt_ref, src_ref, h_ref, wi_ref, bi_ref, perm_ref, wo_ref, bo_ref, y_ref,
                   xbuf_ref, sem):
    i = pl.program_id(0)
    n_used = nt_ref[0]
    n_slots, tm = xbuf_ref.shape[0], xbuf_ref.shape[1]

    def row_copy(row, slot, t):
        return pltpu.make_async_copy(h_ref.at[pl.ds(src_ref[row], 1)], xbuf_ref.at[slot, pl.ds(t, 1)], sem.at[slot])

    def wait_slot(slot):
        pltpu.make_async_copy(h_ref.at[pl.ds(0, tm)], xbuf_ref.at[slot], sem.at[slot]).wait()

    def gather(tile, slot):
        base = jnp.minimum(tile, n_used - 1) * tm
        for t in range(tm):
            row_copy(base + t, slot, t).start()

    @pl.when(i == 0)
    def _():
        gather(0, 0)
        gather(1, 1)

    for slot in range(n_slots):
        @pl.when((i < n_used) & (i % n_slots == slot))
        def _(slot=slot):
            wait_slot(slot)
            x_lo, x_hi = _unpack_halves(xbuf_ref[slot])
            x_lo, x_hi = x_lo.astype(BF16), x_hi.astype(BF16)
            half = x_lo.shape[1]
            gather(i + 2, (slot + 2) % n_slots)
            hh = (jnp.dot(x_lo, wi_ref[0, :half], preferred_element_type=F32)
                  + jnp.dot(x_hi, wi_ref[0, half:], preferred_element_type=F32) + bi_ref[0]).astype(BF16)
            hp = jnp.dot(hh, perm_ref[...], preferred_element_type=F32)
            x_glu = jnp.minimum(hp[:, :EXPERT_FF], SWIGLU_LIMIT)
            x_lin = jnp.clip(hp[:, EXPERT_FF:], -SWIGLU_LIMIT, SWIGLU_LIMIT)
            act = x_glu * (1.0 / (1.0 + jnp.exp(-SWIGLU_ALPHA * x_glu))) * (x_lin + 1.0)
            y_ref[...] = _pack_halves(jnp.dot(act.astype(BF16), wo_ref[0], preferred_element_type=F32) + bo_ref[0])

            @pl.when(i == n_used - 1)
            def _():
                wait_slot((slot + 1) % n_slots)
                wait_slot((slot + 2) % n_slots)

    @pl.when(i >= n_used)
    def _():
        y_ref[...] = jnp.zeros_like(y_ref)


def _moe_experts(h2, src, tile_expert, n_tiles_used, w_in, b_in, w_out, b_out):
    n, dh = h2.shape
    d = 2 * dh
    r = src.shape[0]
    tm = MOE_ROW_TILE
    f2 = w_in.shape[2]
    ff = w_out.shape[1]
    col = np.arange(f2)
    perm = np.zeros((f2, f2), np.float32)
    perm[col, np.where(col % 2 == 0, col // 2, ff + col // 2)] = 1.0
    grid_spec = pltpu.PrefetchScalarGridSpec(
        num_scalar_prefetch=3, grid=(r // tm,),
        in_specs=[pl.BlockSpec(memory_space=pl.ANY),
                  pl.BlockSpec((1, d, f2), lambda i, te, nt, src: (te[i], 0, 0)),
                  pl.BlockSpec((1, 1, f2), lambda i, te, nt, src: (te[i], 0, 0)),
                  pl.BlockSpec((f2, f2), lambda i, te, nt, src: (0, 0)),
                  pl.BlockSpec((1, ff, d), lambda i, te, nt, src: (te[i], 0, 0)),
                  pl.BlockSpec((1, 1, d), lambda i, te, nt, src: (te[i], 0, 0))],
        out_specs=pl.BlockSpec((tm, dh), lambda i, te, nt, src: (i, 0)),
        scratch_shapes=[pltpu.VMEM((3, tm, dh), I32), pltpu.SemaphoreType.DMA((3,))])
    return pl.pallas_call(
        _expert_kernel, grid_spec=grid_spec,
        out_shape=jax.ShapeDtypeStruct((r, dh), I32),
        compiler_params=_cparams(("arbitrary",), unchecked=True),
        name="moe_experts",
    )(tile_expert, n_tiles_used, src, h2, w_in, b_in, jnp.asarray(perm, BF16), w_out, b_out)


def _combine_kernel(pos_ref, y_ref, x_ref, w_ref, g_ref, o_ref, buf_ref, wb_ref, sem, *, tm):
    i = pl.program_id(0)
    n_steps = pl.num_programs(0)

    def gather(tile, slot):
        base = tile * tm * MOE_TOPK
        for t in range(tm):
            for k in range(MOE_TOPK):
                pltpu.make_async_copy(y_ref.at[pl.ds(pos_ref[base + t * MOE_TOPK + k], 1)],
                                      buf_ref.at[slot, k, pl.ds(t, 1)], sem.at[slot]).start()

    @pl.when(i == 0)
    def _():
        gather(0, 0)

    @pl.when((i + 1 < n_steps) & (i % 2 == 0))
    def _():
        gather(i + 1, 1)

    @pl.when((i + 1 < n_steps) & (i % 2 == 1))
    def _():
        gather(i + 1, 0)

    slot = i % 2
    for k in range(MOE_TOPK):
        pltpu.make_async_copy(y_ref.at[pl.ds(0, tm)], buf_ref.at[slot, k], sem.at[slot]).wait()
    w = w_ref[...]
    half = buf_ref.shape[3]
    for k in range(MOE_TOPK):
        wb_ref[k] = jnp.broadcast_to(w[:, k:k + 1], (tm, LANES))
    for c in range(half // LANES):
        lo_cols = slice(c * LANES, (c + 1) * LANES)
        hi_cols = slice(half + c * LANES, half + (c + 1) * LANES)
        mix_lo = jnp.zeros((tm, LANES), F32)
        mix_hi = jnp.zeros((tm, LANES), F32)
        for k in range(MOE_TOPK):
            y_lo, y_hi = _unpack_halves(buf_ref[slot, k, :, lo_cols])
            mix_lo = mix_lo + y_lo * wb_ref[k]
            mix_hi = mix_hi + y_hi * wb_ref[k]
        o_ref[:, lo_cols] = x_ref[:, lo_cols] + g_ref[0, :, lo_cols] * mix_lo
        o_ref[:, hi_cols] = x_ref[:, hi_cols] + g_ref[0, :, hi_cols] * mix_hi


def _moe_combine(y, pos, x2, wgt, gate, rows_per_batch, tm=128):
    n, d = x2.shape
    tpb = rows_per_batch // tm
    grid_spec = pltpu.PrefetchScalarGridSpec(
        num_scalar_prefetch=1, grid=(n // tm,),
        in_specs=[pl.BlockSpec(memory_space=pl.ANY),
                  pl.BlockSpec((tm, d), lambda i, pos: (i, 0)),
                  pl.BlockSpec((tm, LANES), lambda i, pos: (i, 0)),
                  pl.BlockSpec((1, 1, d), lambda i, pos: (i // tpb, 0, 0))],
        out_specs=pl.BlockSpec((tm, d), lambda i, pos: (i, 0)),
        scratch_shapes=[pltpu.VMEM((2, MOE_TOPK, tm, d // 2), I32), pltpu.VMEM((MOE_TOPK, tm, LANES), F32),
                        pltpu.SemaphoreType.DMA((2,))])
    return pl.pallas_call(
        functools.partial(_combine_kernel, tm=tm), grid_spec=grid_spec,
        out_shape=jax.ShapeDtypeStruct((n, d), F32),
        compiler_params=_cparams(("arbitrary",), unchecked=True),
        name="moe_combine",
    )(pos, y, x2, wgt, gate[:, None, :])


def _moe_block(x2, g, scale, shift, gate, w_router, b_router, w_exp_in, b_exp_in, w_exp_out, b_exp_out,
               expert_base, rows_per_batch):
    n, d = x2.shape
    tm = MOE_ROW_TILE
    h2, idx, wgt, rank, counts = _moe_route(x2, g, scale, shift, w_router, b_router, rows_per_batch)
    cnt = counts[0, :N_EXPERTS].astype(I32)
    padded = (cnt + tm - 1) // tm * tm
    ends = jnp.cumsum(padded)
    starts = ends - padded
    n_rows = n * MOE_TOPK + N_EXPERTS * tm
    n_tiles = n_rows // tm
    tile_start = jnp.arange(n_tiles, dtype=I32) * tm
    tile_expert = jnp.minimum(jnp.sum((ends[None, :] <= tile_start[:, None]).astype(I32), axis=1), N_EXPERTS - 1)
    n_tiles_used = (ends[-1] // tm).astype(I32).reshape(1)
    e_flat = idx[:, :MOE_TOPK].reshape(-1)
    pos = (starts[e_flat] + rank[:, :MOE_TOPK].reshape(-1)).astype(I32)
    pad_lo = jnp.concatenate([starts + cnt, ends[-1:]]).astype(I32)
    pad_hi = jnp.concatenate([ends, jnp.full((1,), n_rows, I32)]).astype(I32)
    src = _moe_invert(pos, pad_lo, pad_hi, n_rows)
    y = _moe_experts(h2, src, tile_expert + expert_base, n_tiles_used, w_exp_in, b_exp_in, w_exp_out, b_exp_out)
    return _moe_combine(y, pos, x2, wgt, gate, rows_per_batch)


def _even_mixer(h, x2, gate, batch, s_len, w_in, g_cq, w_uq, w_iq, gq_a, gk_a, g_kidx, b_kidx, gq_b, gk_b,
                b_f, w_out):
    gw = GROUP_WIDTH
    o_cq, o_ka, o_va, o_ki, o_wi, o_qb, o_kb, o_vb, o_fb, o_gb = np.cumsum(
        [0, DSA_Q_LORA, HEAD_DIM, HEAD_DIM, HEAD_DIM, DSA_IDX_HEADS, gw, gw, gw, GROUP_HEADS]).tolist()
    cols = lambda o, wdt: w_in[:, o:o + wdt]
    w_main = jnp.concatenate([cols(o_qb, gw), cols(o_kb, gw), cols(o_vb, gw), cols(o_gb, gw),
                              cols(o_cq, DSA_Q_LORA), cols(o_ka, HEAD_DIM), cols(o_va, HEAD_DIM),
                              cols(o_ki, HEAD_DIM)], axis=1).astype(BF16)
    c_qb, c_kb, c_vb, c_gb, c_cq = 0, gw, 2 * gw, 3 * gw, 4 * gw
    c_ka = c_cq + DSA_Q_LORA
    c_va, c_ki = c_ka + HEAD_DIM, c_ka + 2 * HEAD_DIM
    d = w_in.shape[0]
    w_small = jnp.concatenate([cols(o_fb, GROUP_HEADS), cols(o_wi, DSA_IDX_HEADS),
                               jnp.zeros((d, LANES - GROUP_HEADS - DSA_IDX_HEADS), F32)], axis=1).astype(BF16)
    z = _matmul(h, w_main, BF16, "even_in_proj")
    zs = _matmul(h, w_small, F32, "even_in_proj_small")
    scale = Q_SCALE

    cq = _rowprep(z, c_cq, DSA_Q_LORA, batch, s_len, hw=DSA_Q_LORA, norm="rms", gain=g_cq, name="dsa_cq_norm")
    qa_raw = _matmul(cq, w_uq.astype(BF16), BF16, "dsa_q_up")
    qi_raw = _matmul(cq, w_iq.astype(BF16), BF16, "dsa_idx_q_up")
    qa = _rowprep(qa_raw, 0, gw, batch, s_len, norm="rms", gain=gq_a, rope_dim=HEAD_DIM, out_scale=scale,
                  head_major=True, name="dsa_q_prep")
    qi = _rowprep(qi_raw, 0, DSA_IDX_HEADS * HEAD_DIM, batch, s_len, rope_dim=DSA_IDX_ROPE, head_major=True,
                  name="dsa_idx_q_prep")
    ka = _rowprep(z, c_ka, HEAD_DIM, batch, s_len, norm="rms", gain=gk_a, rope_dim=HEAD_DIM, name="dsa_k_prep")
    ki = _rowprep(z, c_ki, HEAD_DIM, batch, s_len, norm="ln", gain=g_kidx, beta=b_kidx, rope_dim=DSA_IDX_ROPE,
                  name="dsa_idx_k_prep")
    va = z[:, c_va:c_va + HEAD_DIM]
    wi = jnp.pad(zs[:, GROUP_HEADS:GROUP_HEADS + DSA_IDX_HEADS] * (DSA_IDX_HEADS ** -0.5 * HEAD_DIM ** -0.5),
                 ((0, 0), (0, LANES - DSA_IDX_HEADS)))
    o_a = _dsa_attention(qi, ki, wi, qa, ka, va, batch, s_len)

    qb = _rowprep(z, c_qb, gw, batch, s_len, norm="rms", gain=gq_b, out_scale=scale, head_major=True,
                  name="fox_q_prep")
    kb = _rowprep(z, c_kb, gw, batch, s_len, norm="rms", gain=gk_b, head_major=True, name="fox_k_prep")
    bias_row = jnp.pad(b_f.astype(F32), (0, LANES - GROUP_HEADS)).reshape(1, LANES)
    cum = _fox_cumsum(zs, bias_row, batch, s_len)
    o_b = _fox_attention(qb, kb, z, c_vb, c_gb, cum, batch, s_len)
    return _out_proj(o_a, o_b, w_out.astype(BF16), x2, gate, s_len, "even_out_proj")


def _odd_mixer(h, x2, gate, batch, s_len, w_in, gq_c, gk_c, gq_d, gk_d, sinks, w_out):
    gw = GROUP_WIDTH
    kvw = SWA_KV_HEADS * HEAD_DIM
    c_qc, c_kc, c_vc, c_qd, c_kd = 0, gw, 2 * gw, 3 * gw, 4 * gw
    c_vd = c_kd + kvw
    z = _matmul(h, w_in.astype(BF16), BF16, "odd_in_proj")
    scale = Q_SCALE
    qc = _rowprep(z, c_qc, gw, batch, s_len, norm="rms", gain=gq_c, rope_dim=HEAD_DIM, out_scale=scale,
                  head_major=True, name="moba_q_prep")
    kc, kmean = _rowprep(z, c_kc, gw, batch, s_len, norm="rms", gain=gk_c, rope_dim=HEAD_DIM, head_major=True,
                         want_mean=True, name="moba_k_prep")
    o_c = _moba_attention(qc, kc, kmean, z, c_vc, batch, s_len)
    qd = _rowprep(z, c_qd, gw, batch, s_len, norm="rms", gain=gq_d, rope_dim=HEAD_DIM, out_scale=scale,
                  head_major=True, name="swa_q_prep")
    kd = _rowprep(z, c_kd, kvw, batch, s_len, norm="rms", gain=gk_d, rope_dim=HEAD_DIM, head_major=True,
                  name="swa_k_prep")
    o_d = _swa_attention(qd, kd, z, c_vd, sinks, batch, s_len)
    return _out_proj(o_c, o_d, w_out.astype(BF16), x2, gate, s_len, "odd_out_proj")


def kernel(x, c, g_norm_mix, g_norm_ffn, w_ada, b_ada, w_in_even, g_cq, w_uq, w_iq, gq_a, gk_a, g_kidx, b_kidx,
           gq_b, gk_b, b_forget, w_out_even, w_in_odd, gq_c, gk_c, gq_d, gk_d, sinks_d, w_out_odd,
           w_router, b_router, w_exp_in, b_exp_in, w_exp_out, b_exp_out):
    batch, s_len, d = x.shape
    depth = w_ada.shape[0]
    mod = _ada_modulation(c, w_ada, b_ada)
    x2 = x.reshape(batch * s_len, d)
    n_exp = w_exp_in.shape[1]
    w_in_all = w_exp_in.astype(BF16).reshape((depth * n_exp,) + w_exp_in.shape[2:])
    w_out_all = w_exp_out.astype(BF16).reshape((depth * n_exp,) + w_exp_out.shape[2:])
    b_in_all = b_exp_in.reshape(depth * n_exp, 1, -1)
    b_out_all = b_exp_out.reshape(depth * n_exp, 1, -1)
    for layer in range(depth):
        shift_m, scale_m, gate_m, shift_f, scale_f, gate_f = [mod[layer, :, i * d:(i + 1) * d] for i in range(6)]
        h = _norm_modulate(x2, g_norm_mix[layer], scale_m, shift_m, s_len)
        j = layer // 2
        if layer % 2 == 0:
            x2 = _even_mixer(h, x2, gate_m, batch, s_len, w_in_even[j], g_cq[j], w_uq[j], w_iq[j], gq_a[j],
                             gk_a[j], g_kidx[j], b_kidx[j], gq_b[j], gk_b[j], b_forget[j], w_out_even[j])
        else:
            x2 = _odd_mixer(h, x2, gate_m, batch, s_len, w_in_odd[j], gq_c[j], gk_c[j], gq_d[j], gk_d[j],
                            sinks_d[j], w_out_odd[j])
        x2 = _moe_block(x2, g_norm_ffn[layer], scale_f, shift_f, gate_f, w_router[layer], b_router[layer],
                        w_in_all, b_in_all, w_out_all, b_out_all, layer * n_exp, s_len)
    return x2.reshape(batch, s_len, d)
```

```python
import functools

import numpy as np
import jax
import jax.numpy as jnp
from jax import lax
from jax.experimental import pallas as pl
from jax.experimental.pallas import tpu as pltpu

F32 = jnp.float32
BF16 = jnp.bfloat16
I32 = jnp.int32

LANES = 128
HEAD_DIM = 128
GROUP_HEADS = 16
GROUP_WIDTH = GROUP_HEADS * HEAD_DIM
ROPE_THETA = 10000.0
EPS = 1e-6
DSA_Q_LORA = 1024
DSA_IDX_HEADS = 32
DSA_IDX_ROPE = 64
DSA_TOPK = 256
DSA_QBLK = 128
DSA_CHUNK = 256
DSA_STAGES = 8
DSA_HEAD_GROUP = 2
MOBA_BLOCK = 256
MOBA_TOPK = 3
SWA_KV_HEADS = 2
SWA_WINDOW = 128
N_EXPERTS = 32
MOE_TOPK = 4
EXPERT_FF = 512
SWIGLU_LIMIT = 7.0
SWIGLU_ALPHA = 1.702
MOE_ROW_TILE = 256
NEG_BIG = -1e30
LOG2_E = 1.4426950408889634
Q_SCALE = HEAD_DIM ** -0.5 * LOG2_E
VMEM_LIMIT = 56 * 1024 * 1024


def _cparams(semantics, vmem=VMEM_LIMIT, unchecked=False):
    return pltpu.CompilerParams(dimension_semantics=semantics, vmem_limit_bytes=vmem,
                                disable_bounds_checks=unchecked)


def _ada_kernel(cb_ref, w_ref, b_ref, o_ref):
    nb = cb_ref.shape[0]
    tn = w_ref.shape[2]
    for c in range(tn // LANES):
        w = w_ref[0, :, c * LANES:(c + 1) * LANES]
        for b in range(nb):
            o_ref[0, b:b + 1, c * LANES:(c + 1) * LANES] = (
                jnp.sum(w * cb_ref[b], axis=0, keepdims=True) + b_ref[0, :, c * LANES:(c + 1) * LANES])


def _ada_modulation(c, w_ada, b_ada, tn=512):
    depth, d, n6 = w_ada.shape
    nb = c.shape[0]
    cb = jnp.broadcast_to(jax.nn.silu(c)[:, :, None], (nb, d, LANES))
    return pl.pallas_call(
        _ada_kernel,
        grid=(depth, n6 // tn),
        in_specs=[pl.BlockSpec((nb, d, LANES), lambda l, j: (0, 0, 0)),
                  pl.BlockSpec((1, d, tn), lambda l, j: (l, 0, j)),
                  pl.BlockSpec((1, 1, tn), lambda l, j: (l, 0, j))],
        out_specs=pl.BlockSpec((1, nb, tn), lambda l, j: (l, 0, j)),
        out_shape=jax.ShapeDtypeStruct((depth, nb, n6), F32),
        compiler_params=_cparams(("parallel", "parallel")),
        name="ada_modulation",
    )(cb, w_ada, b_ada.reshape(depth, 1, n6))


def _normmod_kernel(x_ref, g_ref, sc_ref, sh_ref, o_ref):
    x = x_ref[...]
    y = x * lax.rsqrt(jnp.mean(x * x, axis=-1, keepdims=True) + EPS) * g_ref[...]
    o_ref[...] = (y * (1.0 + sc_ref[0]) + sh_ref[0]).astype(o_ref.dtype)


def _norm_modulate(x2, g, scale, shift, rows_per_batch, tm=256):
    n, d = x2.shape
    tpb = rows_per_batch // tm
    return pl.pallas_call(
        _normmod_kernel,
        grid=(n // tm,),
        in_specs=[pl.BlockSpec((tm, d), lambda i: (i, 0)),
                  pl.BlockSpec((1, d), lambda i: (0, 0)),
                  pl.BlockSpec((1, 1, d), lambda i: (i // tpb, 0, 0)),
                  pl.BlockSpec((1, 1, d), lambda i: (i // tpb, 0, 0))],
        out_specs=pl.BlockSpec((tm, d), lambda i: (i, 0)),
        out_shape=jax.ShapeDtypeStruct((n, d), BF16),
        compiler_params=_cparams(("parallel",)),
        name="norm_modulate",
    )(x2, g.reshape(1, d), scale[:, None, :], shift[:, None, :])


def _mm_kernel(a_ref, b_ref, o_ref):
    o_ref[...] = jnp.dot(a_ref[...], b_ref[...], preferred_element_type=F32).astype(o_ref.dtype)


def _mm2_residual_kernel(a1_ref, a2_ref, b_ref, x_ref, g_ref, o_ref):
    k1 = a1_ref.shape[1]
    acc = (jnp.dot(a1_ref[...], b_ref[:k1], preferred_element_type=F32)
           + jnp.dot(a2_ref[...], b_ref[k1:], preferred_element_type=F32))
    o_ref[...] = x_ref[...] + g_ref[0] * acc


def _out_proj(a1, a2, b, residual, gate, rows_per_batch, name):
    m, k1 = a1.shape
    k2 = a2.shape[1]
    n = b.shape[1]
    tm = _pick(m, (1024, 512, 256, 128))
    tn = _pick(n, (512, 256, 128))
    tpb = rows_per_batch // tm
    o_spec = pl.BlockSpec((tm, tn), lambda i, j: (i, j))
    return pl.pallas_call(
        _mm2_residual_kernel, grid=(m // tm, n // tn),
        in_specs=[pl.BlockSpec((tm, k1), lambda i, j: (i, 0)), pl.BlockSpec((tm, k2), lambda i, j: (i, 0)),
                  pl.BlockSpec((k1 + k2, tn), lambda i, j: (0, j)), o_spec,
                  pl.BlockSpec((1, 1, tn), lambda i, j: (i // tpb, 0, j))],
        out_specs=o_spec, out_shape=jax.ShapeDtypeStruct((m, n), F32),
        compiler_params=_cparams(("parallel", "arbitrary")), name=name)(a1, a2, b, residual, gate[:, None, :])


def _pick(n, prefs):
    for t in prefs:
        if n % t == 0:
            return t
    return n


def _matmul(a, b, out_dtype, name):
    m, k = a.shape
    n = b.shape[1]
    tm = _pick(m, (1024, 512, 256, 128))
    tn = _pick(n, (512, 256, 128) if k > 2048 else (1024, 512, 256, 128))
    return pl.pallas_call(
        _mm_kernel, grid=(m // tm, n // tn),
        in_specs=[pl.BlockSpec((tm, k), lambda i, j: (i, 0)), pl.BlockSpec((k, tn), lambda i, j: (0, j))],
        out_specs=pl.BlockSpec((tm, tn), lambda i, j: (i, j)),
        out_shape=jax.ShapeDtypeStruct((m, n), out_dtype),
        compiler_params=_cparams(("parallel", "arbitrary")), name=name)(a, b)


def _rope_tables(s_len, rot_dim):
    half = rot_dim // 2
    inv_freq = ROPE_THETA ** (-jnp.arange(half, dtype=F32) / half)
    ang = jnp.arange(s_len, dtype=F32)[:, None] * inv_freq[None, :]
    cos, sin = jnp.cos(ang), jnp.sin(ang)
    pad = HEAD_DIM - rot_dim
    ones = jnp.ones((s_len, pad), F32)
    zeros = jnp.zeros((s_len, pad), F32)
    zh = jnp.zeros((s_len, half), F32)
    cos_t = jnp.concatenate([cos, cos, ones], axis=1)
    sa = jnp.concatenate([-sin, zh, zeros], axis=1)
    sb = jnp.concatenate([zh, sin, zeros], axis=1)
    return cos_t, sa, sb


def _rowprep_kernel(*refs, hw, norm, rope_half, out_scale, head_major, want_mean):
    it = iter(refs)
    z_ref = next(it)
    g_ref = next(it) if norm != "none" else None
    beta_ref = next(it) if norm == "ln" else None
    if rope_half:
        cos_ref, sa_ref, sb_ref = next(it), next(it), next(it)
    o_ref = next(it)
    mean_ref = next(it) if want_mean else None
    width = z_ref.shape[1]
    for h in range(width // hw):
        x = z_ref[:, h * hw:(h + 1) * hw].astype(F32)
        if norm == "rms":
            x = x * lax.rsqrt(jnp.mean(x * x, axis=-1, keepdims=True) + EPS) * g_ref[...]
        elif norm == "ln":
            mu = jnp.mean(x, axis=-1, keepdims=True)
            xc = x - mu
            x = xc * lax.rsqrt(jnp.mean(xc * xc, axis=-1, keepdims=True) + EPS) * g_ref[...] + beta_ref[...]
        if rope_half:
            if rope_half * 2 == hw:
                x = x * cos_ref[...] + pltpu.roll(x, rope_half, 1) * (sa_ref[...] + sb_ref[...])
            else:
                x = (x * cos_ref[...] + pltpu.roll(x, hw - rope_half, 1) * sa_ref[...]
                     + pltpu.roll(x, rope_half, 1) * sb_ref[...])
        if want_mean:
            mean_ref[0, 0, h:h + 1, :] = jnp.mean(x, axis=0, keepdims=True)
        if out_scale != 1.0:
            x = x * out_scale
        if head_major:
            o_ref[0, h] = x.astype(o_ref.dtype)
        else:
            o_ref[:, h * hw:(h + 1) * hw] = x.astype(o_ref.dtype)


def _rowprep(z, col0, width, batch, s_len, *, hw=HEAD_DIM, norm="none", gain=None, beta=None, rope_dim=0,
             out_scale=1.0, head_major=False, want_mean=False, name="rowprep", tm=256):
    n = z.shape[0]
    tpb = s_len // tm
    assert col0 % width == 0 and n == batch * s_len
    cb = col0 // width
    args = [z]
    specs = [pl.BlockSpec((tm, width), lambda i: (i, cb))]
    if norm != "none":
        args.append(gain.reshape(1, hw).astype(F32))
        specs.append(pl.BlockSpec((1, hw), lambda i: (0, 0)))
    if norm == "ln":
        args.append(beta.reshape(1, hw).astype(F32))
        specs.append(pl.BlockSpec((1, hw), lambda i: (0, 0)))
    if rope_dim:
        args += list(_rope_tables(s_len, rope_dim))
        specs += [pl.BlockSpec((tm, HEAD_DIM), lambda i: (i % tpb, 0))] * 3
    nh = width // hw
    if head_major:
        out_shape = [jax.ShapeDtypeStruct((batch, nh, s_len, hw), BF16)]
        out_specs = [pl.BlockSpec((1, nh, tm, hw), lambda i: (i // tpb, 0, i % tpb, 0))]
    else:
        out_shape = [jax.ShapeDtypeStruct((n, width), BF16)]
        out_specs = [pl.BlockSpec((tm, width), lambda i: (i, 0))]
    if want_mean:
        assert tm == MOBA_BLOCK
        out_shape.append(jax.ShapeDtypeStruct((batch, tpb, nh, hw), F32))
        out_specs.append(pl.BlockSpec((1, 1, nh, hw), lambda i: (i // tpb, i % tpb, 0, 0)))
    kern = functools.partial(_rowprep_kernel, hw=hw, norm=norm, rope_half=rope_dim // 2, out_scale=out_scale,
                             head_major=head_major, want_mean=want_mean)
    res = pl.pallas_call(kern, grid=(n // tm,), in_specs=specs, out_specs=out_specs, out_shape=out_shape,
                         compiler_params=_cparams(("parallel",)), name=name)(*args)
    return res if want_mean else res[0]


def _fox_cum_kernel(z_ref, bf_ref, tri_ref, o_ref, carry_ref):
    @pl.when(pl.program_id(1) == 0)
    def _():
        carry_ref[...] = jnp.zeros_like(carry_ref)

    xv = z_ref[...] + bf_ref[...]
    lf = jnp.minimum(xv, 0.0) - jnp.log(1.0 + jnp.exp(-jnp.abs(xv)))
    hi = lf.astype(BF16)
    r1 = lf - hi.astype(F32)
    mid = r1.astype(BF16)
    lo = (r1 - mid.astype(F32)).astype(BF16)
    tri = tri_ref[...]
    cs = (jnp.dot(tri, hi, preferred_element_type=F32) + jnp.dot(tri, mid, preferred_element_type=F32)
          + jnp.dot(tri, lo, preferred_element_type=F32)) + carry_ref[...]
    o_ref[...] = cs * LOG2_E
    tm = cs.shape[0]
    carry_ref[...] = cs[tm - 1:tm, :]


def _fox_cumsum(zs, bias_row, batch, s_len, tm=256):
    n = zs.shape[0]
    tpb = s_len // tm
    tri = jnp.tril(jnp.ones((tm, tm), F32)).astype(BF16)
    return pl.pallas_call(
        _fox_cum_kernel,
        grid=(batch, tpb),
        in_specs=[pl.BlockSpec((tm, LANES), lambda b, i: (b * tpb + i, 0)),
                  pl.BlockSpec((1, LANES), lambda b, i: (0, 0)),
                  pl.BlockSpec((tm, tm), lambda b, i: (0, 0))],
        out_specs=pl.BlockSpec((tm, LANES), lambda b, i: (b * tpb + i, 0)),
        out_shape=jax.ShapeDtypeStruct((n, LANES), F32),
        scratch_shapes=[pltpu.VMEM((1, LANES), F32)],
        compiler_params=_cparams(("parallel", "arbitrary")),
        name="fox_cumsum",
    )(zs, bias_row, tri)


def _dsa_kernel(qi_ref, kit_ref, wi_ref, qa_ref, kat_ref, va_ref, o_ref, wb_ref, key_ref, mb_ref,
                *, sk, q0, topk):
    q = DSA_QBLK
    ch = DSA_CHUNK
    nch = sk // ch
    n_idx = qi_ref.shape[1]
    n_heads = qa_ref.shape[1]
    t0 = (q0 + pl.program_id(1)) * q
    row = t0 + lax.broadcasted_iota(I32, (q, ch), 0)
    lane = lax.broadcasted_iota(I32, (q, ch), 1)

    wi = wi_ref[...]
    for h in range(n_idx):
        wb_ref[h] = jnp.broadcast_to(wi[:, h:h + 1], (q, ch))
    qi = qi_ref[0].reshape(n_idx * q, HEAD_DIM)

    def idx_chunk(c, carry):
        d = jnp.dot(qi, kit_ref[0, c], preferred_element_type=F32)
        acc = jnp.zeros((q, ch), F32)
        for h in range(n_idx):
            acc = acc + jnp.maximum(d[h * q:(h + 1) * q], 0.0) * wb_ref[h]
        score = jnp.where(c * ch + lane <= row, acc + 0.0, -jnp.inf)
        bits = lax.bitcast_convert_type(score, I32)
        key_ref[c] = jnp.where(bits >= 0, bits, bits ^ jnp.int32(0x7FFFFFFF))
        return carry

    lax.fori_loop(0, nch, idx_chunk, 0)

    def count_ge(cand):
        acc = jnp.zeros((q, LANES), F32)
        for c in range(nch):
            for part in range(ch // LANES):
                acc = acc + jnp.where(key_ref[c, :, part * LANES:(part + 1) * LANES] >= cand, 1.0, 0.0)
        return jnp.sum(acc, axis=-1, keepdims=True)

    int_min = jnp.int32(-2 ** 31)
    thr = jnp.where(count_ge(jnp.zeros((q, 1), I32)) >= topk, jnp.int32(0), int_min)

    def bit_step(j, thr):
        cand = thr | jnp.left_shift(jnp.int32(1), 30 - j)
        return jnp.where(count_ge(cand) >= topk, cand, thr)

    thr = lax.fori_loop(0, 31, bit_step, thr)

    for c in range(nch):
        keep = (key_ref[c] >= thr) & (c * ch + lane <= row)
        mb_ref[:, c * ch:(c + 1) * ch] = jnp.where(keep, 0.0, -jnp.inf)

    g = DSA_HEAD_GROUP

    for hg in range(n_heads // g):
        qh = qa_ref[0, hg * g:(hg + 1) * g].reshape(g * q, HEAD_DIM)
        lg = jnp.dot(qh, kat_ref[0], preferred_element_type=F32).reshape(g, q, sk) + mb_ref[...][None]
        m = jnp.max(lg, axis=-1, keepdims=True)
        p = jnp.exp2(lg - m)
        l = jnp.sum(p, axis=-1, keepdims=True)
        o = jnp.dot(p.reshape(g * q, sk).astype(BF16), va_ref[0], preferred_element_type=F32)
        o = (o.reshape(g, q, HEAD_DIM) / l).astype(o_ref.dtype)
        for hh in range(g):
            head = hg * g + hh
            o_ref[0, :, head * HEAD_DIM:(head + 1) * HEAD_DIM] = o[hh]


def _dsa_attention(qi, ki, wi, qa, ka, va, batch, s_len):
    q, ch = DSA_QBLK, DSA_CHUNK
    n_idx, n_heads = qi.shape[1], qa.shape[1]
    topk = min(DSA_TOPK, s_len // 4)
    nqb = s_len // q
    kit = ki.reshape(batch, s_len // ch, ch, HEAD_DIM).transpose(0, 1, 3, 2)
    kat = ka.reshape(batch, s_len, HEAD_DIM).transpose(0, 2, 1)
    va3 = va.reshape(batch, s_len, HEAD_DIM)
    per_stage = max(nqb // DSA_STAGES, ch // q)
    outs = []
    for q0 in range(0, nqb, per_stage):
        sk = (q0 + per_stage) * q
        nch = sk // ch
        kern = functools.partial(_dsa_kernel, sk=sk, q0=q0, topk=topk)
        outs.append(pl.pallas_call(
            kern,
            grid=(batch, per_stage),
            in_specs=[pl.BlockSpec((1, n_idx, q, HEAD_DIM), lambda b, i, q0=q0: (b, 0, q0 + i, 0)),
                      pl.BlockSpec((1, nch, HEAD_DIM, ch), lambda b, i: (b, 0, 0, 0)),
                      pl.BlockSpec((q, LANES), lambda b, i, q0=q0: (b * nqb + q0 + i, 0)),
                      pl.BlockSpec((1, n_heads, q, HEAD_DIM), lambda b, i, q0=q0: (b, 0, q0 + i, 0)),
                      pl.BlockSpec((1, HEAD_DIM, sk), lambda b, i: (b, 0, 0)),
                      pl.BlockSpec((1, sk, HEAD_DIM), lambda b, i: (b, 0, 0))],
            out_specs=pl.BlockSpec((1, q, n_heads * HEAD_DIM), lambda b, i: (b, i, 0)),
            out_shape=jax.ShapeDtypeStruct((batch, per_stage * q, n_heads * HEAD_DIM), BF16),
            scratch_shapes=[pltpu.VMEM((n_idx, q, ch), F32),
                            pltpu.VMEM((nch, q, ch), I32),
                            pltpu.VMEM((q, sk), F32)],
            compiler_params=_cparams(("parallel", "parallel")),
            name=f"dsa_attention_k{sk}",
        )(qi, kit, wi, qa, kat, va3))
    return jnp.concatenate(outs, axis=1).reshape(batch * s_len, n_heads * HEAD_DIM)


def _tri_tables(nq, ratio):
    qt, kt = [], []
    for i in range(nq):
        for j in range((i + 1) * ratio):
            qt.append(i)
            kt.append(j)
    return jnp.asarray(qt, I32), jnp.asarray(kt, I32)


def _fox_kernel(qt_ref, kt_ref, q_ref, k_ref, v_ref, cq_ref, ck_ref, g_ref, o_ref,
                m_ref, l_ref, acc_ref, cqs_ref, *, tq):
    h = pl.program_id(1)
    step = pl.program_id(2)
    qi = qt_ref[step]
    kj = kt_ref[step]

    @pl.when(kj == 0)
    def _():
        m_ref[...] = jnp.full_like(m_ref, -jnp.inf)
        l_ref[...] = jnp.zeros_like(l_ref)
        acc_ref[...] = jnp.zeros_like(acc_ref)
        lane = lax.broadcasted_iota(I32, cq_ref.shape, 1)
        cqs_ref[...] = jnp.sum(jnp.where(lane == h, cq_ref[...], 0.0), axis=-1, keepdims=True)

    s = lax.dot_general(q_ref[0, 0], k_ref[0, 0], (((1,), (1,)), ((), ())), preferred_element_type=F32)
    s = s + (cqs_ref[...] - ck_ref[0, 0])

    def update(s):
        m_prev = m_ref[...]
        m_new = jnp.maximum(m_prev, jnp.max(s, axis=-1, keepdims=True))
        alpha = jnp.exp2(m_prev - m_new)
        p = jnp.exp2(s - m_new)
        l_ref[...] = alpha * l_ref[...] + jnp.sum(p, axis=-1, keepdims=True)
        acc_ref[...] = alpha * acc_ref[...] + jnp.dot(p.astype(BF16), v_ref[...], preferred_element_type=F32)
        m_ref[...] = m_new

    @pl.when(kj < qi)
    def _():
        update(s)

    @pl.when(kj == qi)
    def _():
        row = lax.broadcasted_iota(I32, s.shape, 0)
        col = lax.broadcasted_iota(I32, s.shape, 1)
        update(jnp.where(col <= row, s, -jnp.inf))
        gate = g_ref[...].astype(F32)
        o_ref[...] = (acc_ref[...] / l_ref[...] * (1.0 / (1.0 + jnp.exp(-gate)))).astype(o_ref.dtype)


def _fox_attention(qb, kb, z, v_col0, g_col0, cum, batch, s_len, tq=1024):
    tq = min(tq, s_len)
    nq = s_len // tq
    n_heads = qb.shape[1]
    qt, kt = _tri_tables(nq, 1)
    cum_t = cum.reshape(batch, s_len, LANES)[:, :, :n_heads].transpose(0, 2, 1)[:, :, None, :]
    vb0, gb0 = v_col0 // HEAD_DIM, g_col0 // HEAD_DIM
    grid_spec = pltpu.PrefetchScalarGridSpec(
        num_scalar_prefetch=2,
        grid=(batch, n_heads, int(qt.shape[0])),
        in_specs=[pl.BlockSpec((1, 1, tq, HEAD_DIM), lambda b, h, s, qt, kt: (b, h, qt[s], 0)),
                  pl.BlockSpec((1, 1, tq, HEAD_DIM), lambda b, h, s, qt, kt: (b, h, kt[s], 0)),
                  pl.BlockSpec((tq, HEAD_DIM), lambda b, h, s, qt, kt: (b * nq + kt[s], vb0 + h)),
                  pl.BlockSpec((tq, LANES), lambda b, h, s, qt, kt: (b * nq + qt[s], 0)),
                  pl.BlockSpec((1, 1, 1, tq), lambda b, h, s, qt, kt: (b, h, 0, kt[s])),
                  pl.BlockSpec((tq, HEAD_DIM), lambda b, h, s, qt, kt: (b * nq + qt[s], gb0 + h))],
        out_specs=pl.BlockSpec((tq, HEAD_DIM), lambda b, h, s, qt, kt: (b * nq + qt[s], h)),
        scratch_shapes=[pltpu.VMEM((tq, 1), F32), pltpu.VMEM((tq, 1), F32),
                        pltpu.VMEM((tq, HEAD_DIM), F32), pltpu.VMEM((tq, 1), F32)])
    return pl.pallas_call(
        functools.partial(_fox_kernel, tq=tq), grid_spec=grid_spec,
        out_shape=jax.ShapeDtypeStruct((batch * s_len, n_heads * HEAD_DIM), BF16),
        compiler_params=_cparams(("parallel", "parallel", "arbitrary")),
        name="fox_attention",
    )(qt, kt, qb, kb, z, cum, cum_t, z)


def _moba_kernel(qt_ref, kt_ref, q_ref, k_ref, e_ref, v_ref, km_ref, o_ref, m_ref, l_ref, acc_ref, qa_ref,
                 *, tq, n_sel):
    step = pl.program_id(2)
    qi = qt_ref[step]
    kj = kt_ref[step]

    @pl.when(kj == 0)
    def _():
        m_ref[...] = jnp.full_like(m_ref, NEG_BIG)
        l_ref[...] = jnp.zeros_like(l_ref)
        acc_ref[...] = jnp.zeros_like(acc_ref)
        own = (qi * tq + lax.broadcasted_iota(I32, (tq, 1), 0)) // MOBA_BLOCK
        gate = jnp.dot(q_ref[0, 0].astype(F32), km_ref[0, 0], preferred_element_type=F32,
                       precision=lax.Precision.HIGHEST)
        lane = lax.broadcasted_iota(I32, gate.shape, 1)
        lane_f = lane.astype(F32)
        gate = jnp.where(lane < own, gate, -jnp.inf)
        allowed = jnp.where(lane == own, 1.0, 0.0)
        for _ in range(n_sel):
            best = jnp.max(gate, axis=-1, keepdims=True)
            first = jnp.min(jnp.where(gate == best, lane_f, float(LANES)), axis=-1, keepdims=True)
            pick = (lane_f == first) & (best > -jnp.inf)
            allowed = jnp.where(pick, 1.0, allowed)
            gate = jnp.where(pick, -jnp.inf, gate)
        qa_ref[:, :HEAD_DIM] = q_ref[0, 0]
        qa_ref[:, HEAD_DIM:] = jnp.where(allowed > 0.0, 0.0, NEG_BIG).astype(qa_ref.dtype)

    k_aug = jnp.concatenate([k_ref[0, 0], e_ref[...]], axis=1)
    s = lax.dot_general(qa_ref[...], k_aug, (((1,), (1,)), ((), ())), preferred_element_type=F32)

    def update(s):
        m_prev = m_ref[...]
        m_new = jnp.maximum(m_prev, jnp.max(s, axis=-1, keepdims=True))
        alpha = jnp.exp2(m_prev - m_new)
        p = jnp.exp2(s - m_new)
        l_ref[...] = alpha * l_ref[...] + jnp.sum(p, axis=-1, keepdims=True)
        acc_ref[...] = alpha * acc_ref[...] + jnp.dot(p.astype(BF16), v_ref[...], preferred_element_type=F32)
        m_ref[...] = m_new

    @pl.when(kj < qi)
    def _():
        update(s)

    @pl.when(kj == qi)
    def _():
        row = lax.broadcasted_iota(I32, s.shape, 0)
        col = lax.broadcasted_iota(I32, s.shape, 1)
        update(jnp.where(col <= row, s, NEG_BIG))
        o_ref[...] = (acc_ref[...] / l_ref[...]).astype(o_ref.dtype)


def _moba_attention(qc, kc, kmean, z, v_col0, batch, s_len, tq=1024):
    tq = min(tq, s_len)
    nq = s_len // tq
    n_heads = qc.shape[1]
    nb = s_len // MOBA_BLOCK
    assert nb <= LANES
    n_sel = min(MOBA_TOPK, nb - 1)
    qt, kt = _tri_tables(nq, 1)
    km_t = jnp.pad(kmean.transpose(0, 2, 3, 1), ((0, 0), (0, 0), (0, 0), (0, LANES - nb)))
    block_onehot = (jnp.arange(s_len, dtype=I32)[:, None] // MOBA_BLOCK
                    == jnp.arange(LANES, dtype=I32)[None, :]).astype(BF16)
    vb0 = v_col0 // HEAD_DIM
    grid_spec = pltpu.PrefetchScalarGridSpec(
        num_scalar_prefetch=2,
        grid=(batch, n_heads, int(qt.shape[0])),
        in_specs=[pl.BlockSpec((1, 1, tq, HEAD_DIM), lambda b, h, s, qt, kt: (b, h, qt[s], 0)),
                  pl.BlockSpec((1, 1, tq, HEAD_DIM), lambda b, h, s, qt, kt: (b, h, kt[s], 0)),
                  pl.BlockSpec((tq, LANES), lambda b, h, s, qt, kt: (kt[s], 0)),
                  pl.BlockSpec((tq, HEAD_DIM), lambda b, h, s, qt, kt: (b * nq + kt[s], vb0 + h)),
                  pl.BlockSpec((1, 1, HEAD_DIM, LANES), lambda b, h, s, qt, kt: (b, h, 0, 0))],
        out_specs=pl.BlockSpec((tq, HEAD_DIM), lambda b, h, s, qt, kt: (b * nq + qt[s], h)),
        scratch_shapes=[pltpu.VMEM((tq, 1), F32), pltpu.VMEM((tq, 1), F32),
                        pltpu.VMEM((tq, HEAD_DIM), F32), pltpu.VMEM((tq, HEAD_DIM + LANES), BF16)])
    return pl.pallas_call(
        functools.partial(_moba_kernel, tq=tq, n_sel=n_sel), grid_spec=grid_spec,
        out_shape=jax.ShapeDtypeStruct((batch * s_len, n_heads * HEAD_DIM), BF16),
        compiler_params=_cparams(("parallel", "parallel", "arbitrary")),
        name="moba_attention",
    )(qt, kt, qc, kc, block_onehot, z, km_t)


def _swa_kernel(q_ref, kp_ref, kc_ref, vp_ref, vc_ref, sink_ref, o_ref):
    n = pl.program_id(1)
    hq, w = q_ref.shape[1], q_ref.shape[2]
    hkv = kc_ref.shape[1]
    grp = hq // hkv
    ti = lax.broadcasted_iota(I32, (w, w), 0)
    si = lax.broadcasted_iota(I32, (w, w), 1)
    cur_ok = (si <= ti)[None]
    prev_ok = ((si > ti) & (n > 0))[None]
    nt = (((1,), (1,)), ((), ()))
    for kv in range(hkv):
        q = q_ref[0, kv * grp:(kv + 1) * grp].reshape(grp * w, HEAD_DIM)
        sc = lax.dot_general(q, kc_ref[0, kv], nt, preferred_element_type=F32).reshape(grp, w, w)
        sp = lax.dot_general(q, kp_ref[0, kv], nt, preferred_element_type=F32).reshape(grp, w, w)
        sc = jnp.where(cur_ok, sc, -jnp.inf)
        sp = jnp.where(prev_ok, sp, -jnp.inf)
        sink = sink_ref[kv * grp * w:(kv + 1) * grp * w].reshape(grp, w, LANES)[:, :, :1]
        m = jnp.maximum(jnp.maximum(jnp.max(sc, axis=-1, keepdims=True), jnp.max(sp, axis=-1, keepdims=True)), sink)
        pc = jnp.exp2(sc - m)
        pp = jnp.exp2(sp - m)
        den = jnp.sum(pc, axis=-1, keepdims=True) + jnp.sum(pp, axis=-1, keepdims=True) + jnp.exp2(sink - m)
        o = (jnp.dot(pc.reshape(grp * w, w).astype(BF16), vc_ref[:, kv * HEAD_DIM:(kv + 1) * HEAD_DIM],
                     preferred_element_type=F32)
             + jnp.dot(pp.reshape(grp * w, w).astype(BF16), vp_ref[:, kv * HEAD_DIM:(kv + 1) * HEAD_DIM],
                       preferred_element_type=F32))
        o = o.reshape(grp, w, HEAD_DIM) / den
        for gh in range(grp):
            hh = kv * grp + gh
            o_ref[:, hh * HEAD_DIM:(hh + 1) * HEAD_DIM] = o[gh].astype(o_ref.dtype)


def _swa_attention(qd, kd, z, v_col0, sinks, batch, s_len):
    w = SWA_WINDOW
    nb = s_len // w
    hq, hkv = qd.shape[1], kd.shape[1]
    vw = hkv * HEAD_DIM
    vb0 = v_col0 // vw
    sink_b = jnp.broadcast_to((sinks.astype(F32) * LOG2_E)[:, None, None], (hq, w, LANES)).reshape(hq * w, LANES)
    return pl.pallas_call(
        _swa_kernel,
        grid=(batch, nb),
        in_specs=[pl.BlockSpec((1, hq, w, HEAD_DIM), lambda b, n: (b, 0, n, 0)),
                  pl.BlockSpec((1, hkv, w, HEAD_DIM), lambda b, n: (b, 0, jnp.maximum(n - 1, 0), 0)),
                  pl.BlockSpec((1, hkv, w, HEAD_DIM), lambda b, n: (b, 0, n, 0)),
                  pl.BlockSpec((w, vw), lambda b, n: (b * nb + jnp.maximum(n - 1, 0), vb0)),
                  pl.BlockSpec((w, vw), lambda b, n: (b * nb + n, vb0)),
                  pl.BlockSpec((hq * w, LANES), lambda b, n: (0, 0))],
        out_specs=pl.BlockSpec((w, hq * HEAD_DIM), lambda b, n: (b * nb + n, 0)),
        out_shape=jax.ShapeDtypeStruct((batch * s_len, hq * HEAD_DIM), BF16),
        compiler_params=_cparams(("parallel", "parallel")),
        name="swa_attention",
    )(qd, kd, kd, z, z, sink_b)


_HIGH_HALF = -65536


def _pack_halves(x):
    half = x.shape[1] // 2
    lo = lax.bitcast_convert_type(x[:, :half].astype(jnp.bfloat16).astype(F32), I32)
    hi = lax.bitcast_convert_type(x[:, half:].astype(jnp.bfloat16).astype(F32), I32)
    return ((lo >> 16) & 0xFFFF) | (hi & _HIGH_HALF)


def _unpack_halves(p):
    lo = lax.bitcast_convert_type(p << 16, F32)
    hi = lax.bitcast_convert_type(p & _HIGH_HALF, F32)
    return lo, hi


def _route_kernel(x_ref, g_ref, sc_ref, sh_ref, wr_ref, br_ref, tri_ref,
                  h_ref, idx_ref, wgt_ref, rank_ref, cnt_ref, carry_ref):
    @pl.when(pl.program_id(0) == 0)
    def _():
        carry_ref[...] = jnp.zeros_like(carry_ref)

    x = x_ref[...]
    y = x * lax.rsqrt(jnp.mean(x * x, axis=-1, keepdims=True) + EPS) * g_ref[...]
    hmod = y * (1.0 + sc_ref[0]) + sh_ref[0]
    h_ref[...] = _pack_halves(hmod)
    logits = jnp.dot(hmod, wr_ref[...], preferred_element_type=F32, precision=lax.Precision.HIGHEST) + br_ref[...]
    lane = lax.broadcasted_iota(I32, logits.shape, 1)
    lane_f = lane.astype(F32)
    logits = jnp.where(lane < N_EXPERTS, logits, -jnp.inf)
    onehots, vals, firsts = [], [], []
    for _ in range(MOE_TOPK):
        best = jnp.max(logits, axis=-1, keepdims=True)
        first = jnp.min(jnp.where(logits == best, lane_f, float(LANES)), axis=-1, keepdims=True)
        pick = lane_f == first
        onehots.append(pick)
        vals.append(best)
        firsts.append(first.astype(I32))
        logits = jnp.where(pick, -jnp.inf, logits)
    exps = [jnp.exp(v - vals[0]) for v in vals]
    den = exps[0]
    for e in exps[1:]:
        den = den + e
    chosen_f = jnp.zeros(logits.shape, F32)
    for o in onehots:
        chosen_f = jnp.where(o, 1.0, chosen_f)
    before = jnp.dot(tri_ref[...], chosen_f.astype(BF16), preferred_element_type=F32) + carry_ref[...]
    idx_out = jnp.zeros(logits.shape, I32)
    wgt_out = jnp.zeros(logits.shape, F32)
    rank_out = jnp.zeros(logits.shape, I32)
    for k in range(MOE_TOPK):
        rk = jnp.sum(jnp.where(onehots[k], before, 0.0), axis=-1, keepdims=True).astype(I32)
        idx_out = jnp.where(lane == k, firsts[k], idx_out)
        wgt_out = jnp.where(lane == k, exps[k] / den, wgt_out)
        rank_out = jnp.where(lane == k, rk, rank_out)
    idx_ref[...] = idx_out
    wgt_ref[...] = wgt_out
    rank_ref[...] = rank_out
    carry_ref[...] = carry_ref[...] + jnp.sum(chosen_f, axis=0, keepdims=True)
    cnt_ref[...] = carry_ref[...]


def _moe_route(x2, g, scale, shift, w_router, b_router, rows_per_batch, tm=256):
    n, d = x2.shape
    tpb = rows_per_batch // tm
    wr = jnp.pad(w_router, ((0, 0), (0, LANES - N_EXPERTS)))
    br = jnp.pad(b_router, (0, LANES - N_EXPERTS)).reshape(1, LANES)
    tri = jnp.tril(jnp.ones((tm, tm), F32), -1).astype(BF16)
    tok_spec = pl.BlockSpec((tm, LANES), lambda i: (i, 0))
    return pl.pallas_call(
        _route_kernel,
        grid=(n // tm,),
        in_specs=[pl.BlockSpec((tm, d), lambda i: (i, 0)),
                  pl.BlockSpec((1, d), lambda i: (0, 0)),
                  pl.BlockSpec((1, 1, d), lambda i: (i // tpb, 0, 0)),
                  pl.BlockSpec((1, 1, d), lambda i: (i // tpb, 0, 0)),
                  pl.BlockSpec((d, LANES), lambda i: (0, 0)),
                  pl.BlockSpec((1, LANES), lambda i: (0, 0)),
                  pl.BlockSpec((tm, tm), lambda i: (0, 0))],
        out_specs=[pl.BlockSpec((tm, d // 2), lambda i: (i, 0)), tok_spec, tok_spec, tok_spec,
                   pl.BlockSpec((1, LANES), lambda i: (0, 0))],
        out_shape=[jax.ShapeDtypeStruct((n, d // 2), I32), jax.ShapeDtypeStruct((n, LANES), I32),
                   jax.ShapeDtypeStruct((n, LANES), F32), jax.ShapeDtypeStruct((n, LANES), I32),
                   jax.ShapeDtypeStruct((1, LANES), F32)],
        scratch_shapes=[pltpu.VMEM((1, LANES), F32)],
        compiler_params=_cparams(("arbitrary",)),
        name="moe_route",
    )(x2, g.reshape(1, d), scale[:, None, :], shift[:, None, :], wr, br, tri)


def _invert_kernel(lo_ref, hi_ref, pos_ref, src_ref, *, chunk):
    step = pl.program_id(0)

    @pl.when(step == 0)
    def _():
        def fill_group(g, carry):
            def zero(p, c):
                src_ref[p] = 0
                return c
            lax.fori_loop(lo_ref[g], hi_ref[g], zero, 0)
            return carry
        lax.fori_loop(0, lo_ref.shape[0], fill_group, 0)

    tokens = chunk // MOE_TOPK

    def put(t, carry):
        for k in range(MOE_TOPK):
            src_ref[pos_ref[t * MOE_TOPK + k]] = step * tokens + t
        return carry

    lax.fori_loop(0, tokens, put, 0, unroll=4)


def _moe_invert(pos, pad_lo, pad_hi, n_rows, chunk=2048):
    n_assign = pos.shape[0]
    return pl.pallas_call(
        functools.partial(_invert_kernel, chunk=chunk),
        grid_spec=pltpu.PrefetchScalarGridSpec(
            num_scalar_prefetch=2, grid=(n_assign // chunk,),
            in_specs=[pl.BlockSpec((chunk,), lambda i, lo, hi: (i,), memory_space=pltpu.SMEM)],
            out_specs=pl.BlockSpec((n_rows,), lambda i, lo, hi: (0,), memory_space=pltpu.SMEM)),
        out_shape=jax.ShapeDtypeStruct((n_rows,), I32),
        compiler_params=_cparams(("arbitrary",), unchecked=True),
        name="moe_invert",
    )(pad_lo, pad_hi, pos)


def _expert_kernel(te_ref, nt_ref, src_ref, h_ref, wi_ref, bi_ref, perm_ref, wo_ref, bo_ref, y_ref,
                   xbuf_ref, sem):
    i = pl.program_id(0)
    n_used = nt_ref[0]
    n_slots, tm = xbuf_ref.shape[0], xbuf_ref.shape[1]

    def row_copy(row, slot, t):
        return pltpu.make_async_copy(h_ref.at[pl.ds(src_ref[row], 1)], xbuf_ref.at[slot, pl.ds(t, 1)], sem.at[slot])

    def wait_slot(slot):
        pltpu.make_async_copy(h_ref.at[pl.ds(0, tm)], xbuf_ref.at[slot], sem.at[slot]).wait()

    def gather(tile, slot):
        base = jnp.minimum(tile, n_used - 1) * tm
        for t in range(tm):
            row_copy(base + t, slot, t).start()

    @pl.when(i == 0)
    def _():
        gather(0, 0)
        gather(1, 1)

    for slot in range(n_slots):
        @pl.when((i < n_used) & (i % n_slots == slot))
        def _(slot=slot):
            wait_slot(slot)
            x_lo, x_hi = _unpack_halves(xbuf_ref[slot])
            x_lo, x_hi = x_lo.astype(BF16), x_hi.astype(BF16)
            half = x_lo.shape[1]
            gather(i + 2, (slot + 2) % n_slots)
            hh = (jnp.dot(x_lo, wi_ref[0, :half], preferred_element_type=F32)
                  + jnp.dot(x_hi, wi_ref[0, half:], preferred_element_type=F32) + bi_ref[0]).astype(BF16)
            hp = jnp.dot(hh, perm_ref[...], preferred_element_type=F32)
            x_glu = jnp.minimum(hp[:, :EXPERT_FF], SWIGLU_LIMIT)
            x_lin = jnp.clip(hp[:, EXPERT_FF:], -SWIGLU_LIMIT, SWIGLU_LIMIT)
            act = x_glu * (1.0 / (1.0 + jnp.exp(-SWIGLU_ALPHA * x_glu))) * (x_lin + 1.0)
            y_ref[...] = _pack_halves(jnp.dot(act.astype(BF16), wo_ref[0], preferred_element_type=F32) + bo_ref[0])

            @pl.when(i == n_used - 1)
            def _():
                wait_slot((slot + 1) % n_slots)
                wait_slot((slot + 2) % n_slots)

    @pl.when(i >= n_used)
    def _():
        y_ref[...] = jnp.zeros_like(y_ref)


def _moe_experts(h2, src, tile_expert, n_tiles_used, w_in, b_in, w_out, b_out):
    n, dh = h2.shape
    d = 2 * dh
    r = src.shape[0]
    tm = MOE_ROW_TILE
    f2 = w_in.shape[2]
    ff = w_out.shape[1]
    col = np.arange(f2)
    perm = np.zeros((f2, f2), np.float32)
    perm[col, np.where(col % 2 == 0, col // 2, ff + col // 2)] = 1.0
    grid_spec = pltpu.PrefetchScalarGridSpec(
        num_scalar_prefetch=3, grid=(r // tm,),
        in_specs=[pl.BlockSpec(memory_space=pl.ANY),
                  pl.BlockSpec((1, d, f2), lambda i, te, nt, src: (te[i], 0, 0)),
                  pl.BlockSpec((1, 1, f2), lambda i, te, nt, src: (te[i], 0, 0)),
                  pl.BlockSpec((f2, f2), lambda i, te, nt, src: (0, 0)),
                  pl.BlockSpec((1, ff, d), lambda i, te, nt, src: (te[i], 0, 0)),
                  pl.BlockSpec((1, 1, d), lambda i, te, nt, src: (te[i], 0, 0))],
        out_specs=pl.BlockSpec((tm, dh), lambda i, te, nt, src: (i, 0)),
        scratch_shapes=[pltpu.VMEM((3, tm, dh), I32), pltpu.SemaphoreType.DMA((3,))])
    return pl.pallas_call(
        _expert_kernel, grid_spec=grid_spec,
        out_shape=jax.ShapeDtypeStruct((r, dh), I32),
        compiler_params=_cparams(("arbitrary",), unchecked=True),
        name="moe_experts",
    )(tile_expert, n_tiles_used, src, h2, w_in, b_in, jnp.asarray(perm, BF16), w_out, b_out)


def _combine_kernel(pos_ref, y_ref, x_ref, w_ref, g_ref, o_ref, buf_ref, wb_ref, sem, *, tm):
    i = pl.program_id(0)
    n_steps = pl.num_programs(0)

    def gather(tile, slot):
        base = tile * tm * MOE_TOPK
        for t in range(tm):
            for k in range(MOE_TOPK):
                pltpu.make_async_copy(y_ref.at[pl.ds(pos_ref[base + t * MOE_TOPK + k], 1)],
                                      buf_ref.at[slot, k, pl.ds(t, 1)], sem.at[slot]).start()

    @pl.when(i == 0)
    def _():
        gather(0, 0)

    @pl.when((i + 1 < n_steps) & (i % 2 == 0))
    def _():
        gather(i + 1, 1)

    @pl.when((i + 1 < n_steps) & (i % 2 == 1))
    def _():
        gather(i + 1, 0)

    slot = i % 2
    for k in range(MOE_TOPK):
        pltpu.make_async_copy(y_ref.at[pl.ds(0, tm)], buf_ref.at[slot, k], sem.at[slot]).wait()
    w = w_ref[...]
    half = buf_ref.shape[3]
    for k in range(MOE_TOPK):
        wb_ref[k] = jnp.broadcast_to(w[:, k:k + 1], (tm, LANES))
    for c in range(half // LANES):
        lo_cols = slice(c * LANES, (c + 1) * LANES)
        hi_cols = slice(half + c * LANES, half + (c + 1) * LANES)
        mix_lo = jnp.zeros((tm, LANES), F32)
        mix_hi = jnp.zeros((tm, LANES), F32)
        for k in range(MOE_TOPK):
            y_lo, y_hi = _unpack_halves(buf_ref[slot, k, :, lo_cols])
            mix_lo = mix_lo + y_lo * wb_ref[k]
            mix_hi = mix_hi + y_hi * wb_ref[k]
        o_ref[:, lo_cols] = x_ref[:, lo_cols] + g_ref[0, :, lo_cols] * mix_lo
        o_ref[:, hi_cols] = x_ref[:, hi_cols] + g_ref[0, :, hi_cols] * mix_hi


def _moe_combine(y, pos, x2, wgt, gate, rows_per_batch, tm=128):
    n, d = x2.shape
    tpb = rows_per_batch // tm
    grid_spec = pltpu.PrefetchScalarGridSpec(
        num_scalar_prefetch=1, grid=(n // tm,),
        in_specs=[pl.BlockSpec(memory_space=pl.ANY),
                  pl.BlockSpec((tm, d), lambda i, pos: (i, 0)),
                  pl.BlockSpec((tm, LANES), lambda i, pos: (i, 0)),
                  pl.BlockSpec((1, 1, d), lambda i, pos: (i // tpb, 0, 0))],
        out_specs=pl.BlockSpec((tm, d), lambda i, pos: (i, 0)),
        scratch_shapes=[pltpu.VMEM((2, MOE_TOPK, tm, d // 2), I32), pltpu.VMEM((MOE_TOPK, tm, LANES), F32),
                        pltpu.SemaphoreType.DMA((2,))])
    return pl.pallas_call(
        functools.partial(_combine_kernel, tm=tm), grid_spec=grid_spec,
        out_shape=jax.ShapeDtypeStruct((n, d), F32),
        compiler_params=_cparams(("arbitrary",), unchecked=True),
        name="moe_combine",
    )(pos, y, x2, wgt, gate[:, None, :])


def _moe_block(x2, g, scale, shift, gate, w_router, b_router, w_exp_in, b_exp_in, w_exp_out, b_exp_out,
               expert_base, rows_per_batch):
    n, d = x2.shape
    tm = MOE_ROW_TILE
    h2, idx, wgt, rank, counts = _moe_route(x2, g, scale, shift, w_router, b_router, rows_per_batch)
    cnt = counts[0, :N_EXPERTS].astype(I32)
    padded = (cnt + tm - 1) // tm * tm
    ends = jnp.cumsum(padded)
    starts = ends - padded
    n_rows = n * MOE_TOPK + N_EXPERTS * tm
    n_tiles = n_rows // tm
    tile_start = jnp.arange(n_tiles, dtype=I32) * tm
    tile_expert = jnp.minimum(jnp.sum((ends[None, :] <= tile_start[:, None]).astype(I32), axis=1), N_EXPERTS - 1)
    n_tiles_used = (ends[-1] // tm).astype(I32).reshape(1)
    e_flat = idx[:, :MOE_TOPK].reshape(-1)
    pos = (starts[e_flat] + rank[:, :MOE_TOPK].reshape(-1)).astype(I32)
    pad_lo = jnp.concatenate([starts + cnt, ends[-1:]]).astype(I32)
    pad_hi = jnp.concatenate([ends, jnp.full((1,), n_rows, I32)]).astype(I32)
    src = _moe_invert(pos, pad_lo, pad_hi, n_rows)
    y = _moe_experts(h2, src, tile_expert + expert_base, n_tiles_used, w_exp_in, b_exp_in, w_exp_out, b_exp_out)
    return _moe_combine(y, pos, x2, wgt, gate, rows_per_batch)


def _even_mixer(h, x2, gate, batch, s_len, w_in, g_cq, w_uq, w_iq, gq_a, gk_a, g_kidx, b_kidx, gq_b, gk_b,
                b_f, w_out):
    gw = GROUP_WIDTH
    o_cq, o_ka, o_va, o_ki, o_wi, o_qb, o_kb, o_vb, o_fb, o_gb = np.cumsum(
        [0, DSA_Q_LORA, HEAD_DIM, HEAD_DIM, HEAD_DIM, DSA_IDX_HEADS, gw, gw, gw, GROUP_HEADS]).tolist()
    cols = lambda o, wdt: w_in[:, o:o + wdt]
    n_main = 4 * gw + DSA_Q_LORA + 3 * HEAD_DIM
    w_main = jnp.concatenate([cols(o_qb, gw), cols(o_kb, gw), cols(o_vb, gw), cols(o_gb, gw),
                              cols(o_cq, DSA_Q_LORA), cols(o_ka, HEAD_DIM), cols(o_va, HEAD_DIM),
                              cols(o_ki, HEAD_DIM), jnp.zeros((w_in.shape[0], -n_main % 512), F32)],
                             axis=1).astype(BF16)
    c_qb, c_kb, c_vb, c_gb, c_cq = 0, gw, 2 * gw, 3 * gw, 4 * gw
    c_ka = c_cq + DSA_Q_LORA
    c_va, c_ki = c_ka + HEAD_DIM, c_ka + 2 * HEAD_DIM
    d = w_in.shape[0]
    w_small = jnp.concatenate([cols(o_fb, GROUP_HEADS), cols(o_wi, DSA_IDX_HEADS),
                               jnp.zeros((d, LANES - GROUP_HEADS - DSA_IDX_HEADS), F32)], axis=1).astype(BF16)
    z = _matmul(h, w_main, BF16, "even_in_proj")
    zs = _matmul(h, w_small, F32, "even_in_proj_small")
    scale = Q_SCALE

    cq = _rowprep(z, c_cq, DSA_Q_LORA, batch, s_len, hw=DSA_Q_LORA, norm="rms", gain=g_cq, name="dsa_cq_norm")
    qa_raw = _matmul(cq, w_uq.astype(BF16), BF16, "dsa_q_up")
    qi_raw = _matmul(cq, w_iq.astype(BF16), BF16, "dsa_idx_q_up")
    qa = _rowprep(qa_raw, 0, gw, batch, s_len, norm="rms", gain=gq_a, rope_dim=HEAD_DIM, out_scale=scale,
                  head_major=True, name="dsa_q_prep")
    qi = _rowprep(qi_raw, 0, DSA_IDX_HEADS * HEAD_DIM, batch, s_len, rope_dim=DSA_IDX_ROPE, head_major=True,
                  name="dsa_idx_q_prep")
    ka = _rowprep(z, c_ka, HEAD_DIM, batch, s_len, norm="rms", gain=gk_a, rope_dim=HEAD_DIM, name="dsa_k_prep")
    ki = _rowprep(z, c_ki, HEAD_DIM, batch, s_len, norm="ln", gain=g_kidx, beta=b_kidx, rope_dim=DSA_IDX_ROPE,
                  name="dsa_idx_k_prep")
    va = z[:, c_va:c_va + HEAD_DIM]
    wi = jnp.pad(zs[:, GROUP_HEADS:GROUP_HEADS + DSA_IDX_HEADS] * (DSA_IDX_HEADS ** -0.5 * HEAD_DIM ** -0.5),
                 ((0, 0), (0, LANES - DSA_IDX_HEADS)))
    o_a = _dsa_attention(qi, ki, wi, qa, ka, va, batch, s_len)

    qb = _rowprep(z, c_qb, gw, batch, s_len, norm="rms", gain=gq_b, out_scale=scale, head_major=True,
                  name="fox_q_prep")
    kb = _rowprep(z, c_kb, gw, batch, s_len, norm="rms", gain=gk_b, head_major=True, name="fox_k_prep")
    bias_row = jnp.pad(b_f.astype(F32), (0, LANES - GROUP_HEADS)).reshape(1, LANES)
    cum = _fox_cumsum(zs, bias_row, batch, s_len)
    o_b = _fox_attention(qb, kb, z, c_vb, c_gb, cum, batch, s_len)
    return _out_proj(o_a, o_b, w_out.astype(BF16), x2, gate, s_len, "even_out_proj")


def _odd_mixer(h, x2, gate, batch, s_len, w_in, gq_c, gk_c, gq_d, gk_d, sinks, w_out):
    gw = GROUP_WIDTH
    kvw = SWA_KV_HEADS * HEAD_DIM
    c_qc, c_kc, c_vc, c_qd, c_kd = 0, gw, 2 * gw, 3 * gw, 4 * gw
    c_vd = c_kd + kvw
    z = _matmul(h, w_in.astype(BF16), BF16, "odd_in_proj")
    scale = Q_SCALE
    qc = _rowprep(z, c_qc, gw, batch, s_len, norm="rms", gain=gq_c, rope_dim=HEAD_DIM, out_scale=scale,
                  head_major=True, name="moba_q_prep")
    kc, kmean = _rowprep(z, c_kc, gw, batch, s_len, norm="rms", gain=gk_c, rope_dim=HEAD_DIM, head_major=True,
                         want_mean=True, name="moba_k_prep")
    o_c = _moba_attention(qc, kc, kmean, z, c_vc, batch, s_len)
    qd = _rowprep(z, c_qd, gw, batch, s_len, norm="rms", gain=gq_d, rope_dim=HEAD_DIM, out_scale=scale,
                  head_major=True, name="swa_q_prep")
    kd = _rowprep(z, c_kd, kvw, batch, s_len, norm="rms", gain=gk_d, rope_dim=HEAD_DIM, head_major=True,
                  name="swa_k_prep")
    o_d = _swa_attention(qd, kd, z, c_vd, sinks, batch, s_len)
    return _out_proj(o_c, o_d, w_out.astype(BF16), x2, gate, s_len, "odd_out_proj")


def kernel(x, c, g_norm_mix, g_norm_ffn, w_ada, b_ada, w_in_even, g_cq, w_uq, w_iq, gq_a, gk_a, g_kidx, b_kidx,
           gq_b, gk_b, b_forget, w_out_even, w_in_odd, gq_c, gk_c, gq_d, gk_d, sinks_d, w_out_odd,
           w_router, b_router, w_exp_in, b_exp_in, w_exp_out, b_exp_out):
    batch, s_len, d = x.shape
    depth = w_ada.shape[0]
    mod = _ada_modulation(c, w_ada, b_ada)
    x2 = x.reshape(batch * s_len, d)
    n_exp = w_exp_in.shape[1]
    w_in_all = w_exp_in.astype(BF16).reshape((depth * n_exp,) + w_exp_in.shape[2:])
    w_out_all = w_exp_out.astype(BF16).reshape((depth * n_exp,) + w_exp_out.shape[2:])
    b_in_all = b_exp_in.reshape(depth * n_exp, 1, -1)
    b_out_all = b_exp_out.reshape(depth * n_exp, 1, -1)
    for layer in range(depth):
        shift_m, scale_m, gate_m, shift_f, scale_f, gate_f = [mod[layer, :, i * d:(i + 1) * d] for i in range(6)]
        h = _norm_modulate(x2, g_norm_mix[layer], scale_m, shift_m, s_len)
        j = layer // 2
        if layer % 2 == 0:
            x2 = _even_mixer(h, x2, gate_m, batch, s_len, w_in_even[j], g_cq[j], w_uq[j], w_iq[j], gq_a[j],
                             gk_a[j], g_kidx[j], b_kidx[j], gq_b[j], gk_b[j], b_forget[j], w_out_even[j])
        else:
            x2 = _odd_mixer(h, x2, gate_m, batch, s_len, w_in_odd[j], gq_c[j], gk_c[j], gq_d[j], gk_d[j],
                            sinks_d[j], w_out_odd[j])
        x2 = _moe_block(x2, g_norm_ffn[layer], scale_f, shift_f, gate_f, w_router[layer], b_router[layer],
                        w_in_all, b_in_all, w_out_all, b_out_all, layer * n_exp, s_len)
    return x2.reshape(batch, s_len, d)
```

```python
import functools

import numpy as np
import jax
import jax.numpy as jnp
from jax import lax
from jax.experimental import pallas as pl
from jax.experimental.pallas import tpu as pltpu

F32 = jnp.float32
BF16 = jnp.bfloat16
I32 = jnp.int32

LANES = 128
HEAD_DIM = 128
GROUP_HEADS = 16
GROUP_WIDTH = GROUP_HEADS * HEAD_DIM
ROPE_THETA = 10000.0
EPS = 1e-6
DSA_Q_LORA = 1024
DSA_IDX_HEADS = 32
DSA_IDX_ROPE = 64
DSA_TOPK = 256
DSA_QBLK = 128
DSA_CHUNK = 256
DSA_STAGES = 8
DSA_HEAD_GROUP = 2
MOBA_BLOCK = 256
MOBA_TOPK = 3
SWA_KV_HEADS = 2
SWA_WINDOW = 128
N_EXPERTS = 32
MOE_TOPK = 4
EXPERT_FF = 512
SWIGLU_LIMIT = 7.0
SWIGLU_ALPHA = 1.702
MOE_ROW_TILE = 256
NEG_BIG = -1e30
LOG2_E = 1.4426950408889634
Q_SCALE = HEAD_DIM ** -0.5 * LOG2_E
VMEM_LIMIT = 56 * 1024 * 1024


def _cparams(semantics, vmem=VMEM_LIMIT, unchecked=False):
    return pltpu.CompilerParams(dimension_semantics=semantics, vmem_limit_bytes=vmem,
                                disable_bounds_checks=unchecked)


def _ada_kernel(cb_ref, w_ref, b_ref, o_ref):
    nb = cb_ref.shape[0]
    tn = w_ref.shape[2]
    for c in range(tn // LANES):
        w = w_ref[0, :, c * LANES:(c + 1) * LANES]
        for b in range(nb):
            o_ref[0, b:b + 1, c * LANES:(c + 1) * LANES] = (
                jnp.sum(w * cb_ref[b], axis=0, keepdims=True) + b_ref[0, :, c * LANES:(c + 1) * LANES])


def _ada_modulation(c, w_ada, b_ada, tn=512):
    depth, d, n6 = w_ada.shape
    nb = c.shape[0]
    cb = jnp.broadcast_to(jax.nn.silu(c)[:, :, None], (nb, d, LANES))
    return pl.pallas_call(
        _ada_kernel,
        grid=(depth, n6 // tn),
        in_specs=[pl.BlockSpec((nb, d, LANES), lambda l, j: (0, 0, 0)),
                  pl.BlockSpec((1, d, tn), lambda l, j: (l, 0, j)),
                  pl.BlockSpec((1, 1, tn), lambda l, j: (l, 0, j))],
        out_specs=pl.BlockSpec((1, nb, tn), lambda l, j: (l, 0, j)),
        out_shape=jax.ShapeDtypeStruct((depth, nb, n6), F32),
        compiler_params=_cparams(("parallel", "parallel")),
        name="ada_modulation",
    )(cb, w_ada, b_ada.reshape(depth, 1, n6))


def _normmod_kernel(x_ref, g_ref, sc_ref, sh_ref, o_ref):
    x = x_ref[...]
    y = x * lax.rsqrt(jnp.mean(x * x, axis=-1, keepdims=True) + EPS) * g_ref[...]
    o_ref[...] = (y * (1.0 + sc_ref[0]) + sh_ref[0]).astype(o_ref.dtype)


def _norm_modulate(x2, g, scale, shift, rows_per_batch, tm=512):
    n, d = x2.shape
    tpb = rows_per_batch // tm
    return pl.pallas_call(
        _normmod_kernel,
        grid=(n // tm,),
        in_specs=[pl.BlockSpec((tm, d), lambda i: (i, 0)),
                  pl.BlockSpec((1, d), lambda i: (0, 0)),
                  pl.BlockSpec((1, 1, d), lambda i: (i // tpb, 0, 0)),
                  pl.BlockSpec((1, 1, d), lambda i: (i // tpb, 0, 0))],
        out_specs=pl.BlockSpec((tm, d), lambda i: (i, 0)),
        out_shape=jax.ShapeDtypeStruct((n, d), BF16),
        compiler_params=_cparams(("parallel",)),
        name="norm_modulate",
    )(x2, g.reshape(1, d), scale[:, None, :], shift[:, None, :])


def _mm_kernel(a_ref, b_ref, o_ref):
    o_ref[...] = jnp.dot(a_ref[...], b_ref[...], preferred_element_type=F32).astype(o_ref.dtype)


def _mm2_residual_kernel(a1_ref, a2_ref, b_ref, x_ref, g_ref, o_ref):
    k1 = a1_ref.shape[1]
    acc = (jnp.dot(a1_ref[...], b_ref[:k1], preferred_element_type=F32)
           + jnp.dot(a2_ref[...], b_ref[k1:], preferred_element_type=F32))
    o_ref[...] = x_ref[...] + g_ref[0] * acc


def _out_proj(a1, a2, b, residual, gate, rows_per_batch, name):
    m, k1 = a1.shape
    k2 = a2.shape[1]
    n = b.shape[1]
    tm = _pick(m, (1024, 512, 256, 128))
    tn = _pick(n, (512, 256, 128))
    tpb = rows_per_batch // tm
    o_spec = pl.BlockSpec((tm, tn), lambda i, j: (i, j))
    return pl.pallas_call(
        _mm2_residual_kernel, grid=(m // tm, n // tn),
        in_specs=[pl.BlockSpec((tm, k1), lambda i, j: (i, 0)), pl.BlockSpec((tm, k2), lambda i, j: (i, 0)),
                  pl.BlockSpec((k1 + k2, tn), lambda i, j: (0, j)), o_spec,
                  pl.BlockSpec((1, 1, tn), lambda i, j: (i // tpb, 0, j))],
        out_specs=o_spec, out_shape=jax.ShapeDtypeStruct((m, n), F32),
        compiler_params=_cparams(("parallel", "arbitrary")), name=name)(a1, a2, b, residual, gate[:, None, :])


def _pick(n, prefs):
    for t in prefs:
        if n % t == 0:
            return t
    return n


def _matmul(a, b, out_dtype, name):
    m, k = a.shape
    n = b.shape[1]
    tm = _pick(m, (1024, 512, 256, 128))
    tn = _pick(n, (512, 256, 128) if k > 2048 else (1024, 512, 256, 128))
    return pl.pallas_call(
        _mm_kernel, grid=(m // tm, n // tn),
        in_specs=[pl.BlockSpec((tm, k), lambda i, j: (i, 0)), pl.BlockSpec((k, tn), lambda i, j: (0, j))],
        out_specs=pl.BlockSpec((tm, tn), lambda i, j: (i, j)),
        out_shape=jax.ShapeDtypeStruct((m, n), out_dtype),
        compiler_params=_cparams(("parallel", "arbitrary")), name=name)(a, b)


def _rope_tables(s_len, rot_dim):
    half = rot_dim // 2
    inv_freq = ROPE_THETA ** (-jnp.arange(half, dtype=F32) / half)
    ang = jnp.arange(s_len, dtype=F32)[:, None] * inv_freq[None, :]
    cos, sin = jnp.cos(ang), jnp.sin(ang)
    pad = HEAD_DIM - rot_dim
    ones = jnp.ones((s_len, pad), F32)
    zeros = jnp.zeros((s_len, pad), F32)
    zh = jnp.zeros((s_len, half), F32)
    cos_t = jnp.concatenate([cos, cos, ones], axis=1)
    sa = jnp.concatenate([-sin, zh, zeros], axis=1)
    sb = jnp.concatenate([zh, sin, zeros], axis=1)
    return cos_t, sa, sb


def _rowprep_kernel(*refs, hw, norm, rope_half, out_scale, head_major, want_mean):
    it = iter(refs)
    z_ref = next(it)
    g_ref = next(it) if norm != "none" else None
    beta_ref = next(it) if norm == "ln" else None
    if rope_half:
        cos_ref, sa_ref, sb_ref = next(it), next(it), next(it)
    o_ref = next(it)
    mean_ref = next(it) if want_mean else None
    width = z_ref.shape[1]
    for h in range(width // hw):
        x = z_ref[:, h * hw:(h + 1) * hw].astype(F32)
        if norm == "rms":
            x = x * lax.rsqrt(jnp.mean(x * x, axis=-1, keepdims=True) + EPS) * g_ref[...]
        elif norm == "ln":
            mu = jnp.mean(x, axis=-1, keepdims=True)
            xc = x - mu
            x = xc * lax.rsqrt(jnp.mean(xc * xc, axis=-1, keepdims=True) + EPS) * g_ref[...] + beta_ref[...]
        if rope_half:
            if rope_half * 2 == hw:
                x = x * cos_ref[...] + pltpu.roll(x, rope_half, 1) * (sa_ref[...] + sb_ref[...])
            else:
                x = (x * cos_ref[...] + pltpu.roll(x, hw - rope_half, 1) * sa_ref[...]
                     + pltpu.roll(x, rope_half, 1) * sb_ref[...])
        if want_mean:
            for blk in range(x.shape[0] // MOBA_BLOCK):
                mean_ref[0, blk, h:h + 1, :] = jnp.mean(x[blk * MOBA_BLOCK:(blk + 1) * MOBA_BLOCK], axis=0,
                                                        keepdims=True)
        if out_scale != 1.0:
            x = x * out_scale
        if head_major:
            o_ref[0, h] = x.astype(o_ref.dtype)
        else:
            o_ref[:, h * hw:(h + 1) * hw] = x.astype(o_ref.dtype)


def _rowprep(z, col0, width, batch, s_len, *, hw=HEAD_DIM, norm="none", gain=None, beta=None, rope_dim=0,
             out_scale=1.0, head_major=False, want_mean=False, name="rowprep", tm=1024):
    n = z.shape[0]
    tm = min(tm, s_len)
    tpb = s_len // tm
    assert col0 % width == 0 and n == batch * s_len
    cb = col0 // width
    args = [z]
    specs = [pl.BlockSpec((tm, width), lambda i: (i, cb))]
    if norm != "none":
        args.append(gain.reshape(1, hw).astype(F32))
        specs.append(pl.BlockSpec((1, hw), lambda i: (0, 0)))
    if norm == "ln":
        args.append(beta.reshape(1, hw).astype(F32))
        specs.append(pl.BlockSpec((1, hw), lambda i: (0, 0)))
    if rope_dim:
        args += list(_rope_tables(s_len, rope_dim))
        specs += [pl.BlockSpec((tm, HEAD_DIM), lambda i: (i % tpb, 0))] * 3
    nh = width // hw
    if head_major:
        out_shape = [jax.ShapeDtypeStruct((batch, nh, s_len, hw), BF16)]
        out_specs = [pl.BlockSpec((1, nh, tm, hw), lambda i: (i // tpb, 0, i % tpb, 0))]
    else:
        out_shape = [jax.ShapeDtypeStruct((n, width), BF16)]
        out_specs = [pl.BlockSpec((tm, width), lambda i: (i, 0))]
    if want_mean:
        bpt = tm // MOBA_BLOCK
        out_shape.append(jax.ShapeDtypeStruct((batch, tpb * bpt, nh, hw), F32))
        out_specs.append(pl.BlockSpec((1, bpt, nh, hw), lambda i: (i // tpb, i % tpb, 0, 0)))
    kern = functools.partial(_rowprep_kernel, hw=hw, norm=norm, rope_half=rope_dim // 2, out_scale=out_scale,
                             head_major=head_major, want_mean=want_mean)
    res = pl.pallas_call(kern, grid=(n // tm,), in_specs=specs, out_specs=out_specs, out_shape=out_shape,
                         compiler_params=_cparams(("parallel",)), name=name)(*args)
    return res if want_mean else res[0]


def _fox_cum_kernel(z_ref, bf_ref, tri_ref, o_ref, carry_ref):
    @pl.when(pl.program_id(1) == 0)
    def _():
        carry_ref[...] = jnp.zeros_like(carry_ref)

    xv = z_ref[...] + bf_ref[...]
    lf = jnp.minimum(xv, 0.0) - jnp.log(1.0 + jnp.exp(-jnp.abs(xv)))
    hi = lf.astype(BF16)
    r1 = lf - hi.astype(F32)
    mid = r1.astype(BF16)
    lo = (r1 - mid.astype(F32)).astype(BF16)
    tri = tri_ref[...]
    cs = (jnp.dot(tri, hi, preferred_element_type=F32) + jnp.dot(tri, mid, preferred_element_type=F32)
          + jnp.dot(tri, lo, preferred_element_type=F32)) + carry_ref[...]
    o_ref[...] = cs * LOG2_E
    tm = cs.shape[0]
    carry_ref[...] = cs[tm - 1:tm, :]


def _fox_cumsum(zs, bias_row, batch, s_len, tm=256):
    n = zs.shape[0]
    tpb = s_len // tm
    tri = jnp.tril(jnp.ones((tm, tm), F32)).astype(BF16)
    return pl.pallas_call(
        _fox_cum_kernel,
        grid=(batch, tpb),
        in_specs=[pl.BlockSpec((tm, LANES), lambda b, i: (b * tpb + i, 0)),
                  pl.BlockSpec((1, LANES), lambda b, i: (0, 0)),
                  pl.BlockSpec((tm, tm), lambda b, i: (0, 0))],
        out_specs=pl.BlockSpec((tm, LANES), lambda b, i: (b * tpb + i, 0)),
        out_shape=jax.ShapeDtypeStruct((n, LANES), F32),
        scratch_shapes=[pltpu.VMEM((1, LANES), F32)],
        compiler_params=_cparams(("parallel", "arbitrary")),
        name="fox_cumsum",
    )(zs, bias_row, tri)


def _dsa_kernel(qi_ref, kit_ref, wi_ref, qa_ref, kat_ref, va_ref, o_ref, wb_ref, key_ref, mb_ref,
                *, sk, q0, topk):
    q = DSA_QBLK
    ch = DSA_CHUNK
    nch = sk // ch
    n_idx = qi_ref.shape[1]
    n_heads = qa_ref.shape[1]
    t0 = (q0 + pl.program_id(1)) * q
    row = t0 + lax.broadcasted_iota(I32, (q, ch), 0)
    lane = lax.broadcasted_iota(I32, (q, ch), 1)

    wi = wi_ref[...]
    for h in range(n_idx):
        wb_ref[h] = jnp.broadcast_to(wi[:, h:h + 1], (q, ch))
    qi = qi_ref[0].reshape(n_idx * q, HEAD_DIM)

    def idx_chunk(c, carry):
        d = jnp.dot(qi, kit_ref[0, c], preferred_element_type=F32)
        acc = jnp.zeros((q, ch), F32)
        for h in range(n_idx):
            acc = acc + jnp.maximum(d[h * q:(h + 1) * q], 0.0) * wb_ref[h]
        score = jnp.where(c * ch + lane <= row, acc + 0.0, -jnp.inf)
        bits = lax.bitcast_convert_type(score, I32)
        key_ref[c] = jnp.where(bits >= 0, bits, bits ^ jnp.int32(0x7FFFFFFF))
        return carry

    lax.fori_loop(0, nch, idx_chunk, 0)

    def count_ge(cand):
        acc = jnp.zeros((q, LANES), F32)
        for c in range(nch):
            for part in range(ch // LANES):
                acc = acc + jnp.where(key_ref[c, :, part * LANES:(part + 1) * LANES] >= cand, 1.0, 0.0)
        return jnp.sum(acc, axis=-1, keepdims=True)

    int_min = jnp.int32(-2 ** 31)
    thr = jnp.where(count_ge(jnp.zeros((q, 1), I32)) >= topk, jnp.int32(0), int_min)

    def bit_step(j, thr):
        cand = thr | jnp.left_shift(jnp.int32(1), 30 - j)
        return jnp.where(count_ge(cand) >= topk, cand, thr)

    thr = lax.fori_loop(0, 31, bit_step, thr)

    for c in range(nch):
        keep = (key_ref[c] >= thr) & (c * ch + lane <= row)
        mb_ref[:, c * ch:(c + 1) * ch] = jnp.where(keep, 0.0, -jnp.inf)

    g = DSA_HEAD_GROUP

    for hg in range(n_heads // g):
        qh = qa_ref[0, hg * g:(hg + 1) * g].reshape(g * q, HEAD_DIM)
        lg = jnp.dot(qh, kat_ref[0], preferred_element_type=F32).reshape(g, q, sk) + mb_ref[...][None]
        m = jnp.max(lg, axis=-1, keepdims=True)
        p = jnp.exp2(lg - m)
        l = jnp.sum(p, axis=-1, keepdims=True)
        o = jnp.dot(p.reshape(g * q, sk).astype(BF16), va_ref[0], preferred_element_type=F32)
        o = (o.reshape(g, q, HEAD_DIM) / l).astype(o_ref.dtype)
        for hh in range(g):
            head = hg * g + hh
            o_ref[0, :, head * HEAD_DIM:(head + 1) * HEAD_DIM] = o[hh]


def _dsa_attention(qi, ki, wi, qa, ka, va, batch, s_len):
    q, ch = DSA_QBLK, DSA_CHUNK
    n_idx, n_heads = qi.shape[1], qa.shape[1]
    topk = min(DSA_TOPK, s_len // 4)
    nqb = s_len // q
    kit = ki.reshape(batch, s_len // ch, ch, HEAD_DIM).transpose(0, 1, 3, 2)
    kat = ka.reshape(batch, s_len, HEAD_DIM).transpose(0, 2, 1)
    va3 = va.reshape(batch, s_len, HEAD_DIM)
    per_stage = max(nqb // DSA_STAGES, ch // q)
    outs = []
    for q0 in range(0, nqb, per_stage):
        sk = (q0 + per_stage) * q
        nch = sk // ch
        kern = functools.partial(_dsa_kernel, sk=sk, q0=q0, topk=topk)
        outs.append(pl.pallas_call(
            kern,
            grid=(batch, per_stage),
            in_specs=[pl.BlockSpec((1, n_idx, q, HEAD_DIM), lambda b, i, q0=q0: (b, 0, q0 + i, 0)),
                      pl.BlockSpec((1, nch, HEAD_DIM, ch), lambda b, i: (b, 0, 0, 0)),
                      pl.BlockSpec((q, LANES), lambda b, i, q0=q0: (b * nqb + q0 + i, 0)),
                      pl.BlockSpec((1, n_heads, q, HEAD_DIM), lambda b, i, q0=q0: (b, 0, q0 + i, 0)),
                      pl.BlockSpec((1, HEAD_DIM, sk), lambda b, i: (b, 0, 0)),
                      pl.BlockSpec((1, sk, HEAD_DIM), lambda b, i: (b, 0, 0))],
            out_specs=pl.BlockSpec((1, q, n_heads * HEAD_DIM), lambda b, i: (b, i, 0)),
            out_shape=jax.ShapeDtypeStruct((batch, per_stage * q, n_heads * HEAD_DIM), BF16),
            scratch_shapes=[pltpu.VMEM((n_idx, q, ch), F32),
                            pltpu.VMEM((nch, q, ch), I32),
                            pltpu.VMEM((q, sk), F32)],
            compiler_params=_cparams(("parallel", "parallel")),
            name=f"dsa_attention_k{sk}",
        )(qi, kit, wi, qa, kat, va3))
    return jnp.concatenate(outs, axis=1).reshape(batch * s_len, n_heads * HEAD_DIM)


def _tri_tables(nq, ratio):
    qt, kt = [], []
    for i in range(nq):
        for j in range((i + 1) * ratio):
            qt.append(i)
            kt.append(j)
    return jnp.asarray(qt, I32), jnp.asarray(kt, I32)


def _fox_kernel(qt_ref, kt_ref, q_ref, k_ref, v_ref, cq_ref, ck_ref, g_ref, o_ref,
                m_ref, l_ref, acc_ref, cqs_ref, *, tq):
    h = pl.program_id(1)
    step = pl.program_id(2)
    qi = qt_ref[step]
    kj = kt_ref[step]

    @pl.when(kj == 0)
    def _():
        m_ref[...] = jnp.full_like(m_ref, -jnp.inf)
        l_ref[...] = jnp.zeros_like(l_ref)
        acc_ref[...] = jnp.zeros_like(acc_ref)
        lane = lax.broadcasted_iota(I32, cq_ref.shape, 1)
        cqs_ref[...] = jnp.sum(jnp.where(lane == h, cq_ref[...], 0.0), axis=-1, keepdims=True)

    s = lax.dot_general(q_ref[0, 0], k_ref[0, 0], (((1,), (1,)), ((), ())), preferred_element_type=F32)
    s = s + (cqs_ref[...] - ck_ref[0, 0])

    def update(s):
        m_prev = m_ref[...]
        m_new = jnp.maximum(m_prev, jnp.max(s, axis=-1, keepdims=True))
        alpha = jnp.exp2(m_prev - m_new)
        p = jnp.exp2(s - m_new)
        l_ref[...] = alpha * l_ref[...] + jnp.sum(p, axis=-1, keepdims=True)
        acc_ref[...] = alpha * acc_ref[...] + jnp.dot(p.astype(BF16), v_ref[...], preferred_element_type=F32)
        m_ref[...] = m_new

    @pl.when(kj < qi)
    def _():
        update(s)

    @pl.when(kj == qi)
    def _():
        row = lax.broadcasted_iota(I32, s.shape, 0)
        col = lax.broadcasted_iota(I32, s.shape, 1)
        update(jnp.where(col <= row, s, -jnp.inf))
        gate = g_ref[...].astype(F32)
        o_ref[...] = (acc_ref[...] / l_ref[...] * (1.0 / (1.0 + jnp.exp(-gate)))).astype(o_ref.dtype)


def _fox_attention(qb, kb, z, v_col0, g_col0, cum, batch, s_len, tq=1024):
    tq = min(tq, s_len)
    nq = s_len // tq
    n_heads = qb.shape[1]
    qt, kt = _tri_tables(nq, 1)
    cum_t = cum.reshape(batch, s_len, LANES)[:, :, :n_heads].transpose(0, 2, 1)[:, :, None, :]
    vb0, gb0 = v_col0 // HEAD_DIM, g_col0 // HEAD_DIM
    grid_spec = pltpu.PrefetchScalarGridSpec(
        num_scalar_prefetch=2,
        grid=(batch, n_heads, int(qt.shape[0])),
        in_specs=[pl.BlockSpec((1, 1, tq, HEAD_DIM), lambda b, h, s, qt, kt: (b, h, qt[s], 0)),
                  pl.BlockSpec((1, 1, tq, HEAD_DIM), lambda b, h, s, qt, kt: (b, h, kt[s], 0)),
                  pl.BlockSpec((tq, HEAD_DIM), lambda b, h, s, qt, kt: (b * nq + kt[s], vb0 + h)),
                  pl.BlockSpec((tq, LANES), lambda b, h, s, qt, kt: (b * nq + qt[s], 0)),
                  pl.BlockSpec((1, 1, 1, tq), lambda b, h, s, qt, kt: (b, h, 0, kt[s])),
                  pl.BlockSpec((tq, HEAD_DIM), lambda b, h, s, qt, kt: (b * nq + qt[s], gb0 + h))],
        out_specs=pl.BlockSpec((tq, HEAD_DIM), lambda b, h, s, qt, kt: (b * nq + qt[s], h)),
        scratch_shapes=[pltpu.VMEM((tq, 1), F32), pltpu.VMEM((tq, 1), F32),
                        pltpu.VMEM((tq, HEAD_DIM), F32), pltpu.VMEM((tq, 1), F32)])
    return pl.pallas_call(
        functools.partial(_fox_kernel, tq=tq), grid_spec=grid_spec,
        out_shape=jax.ShapeDtypeStruct((batch * s_len, n_heads * HEAD_DIM), BF16),
        compiler_params=_cparams(("parallel", "parallel", "arbitrary")),
        name="fox_attention",
    )(qt, kt, qb, kb, z, cum, cum_t, z)


def _moba_kernel(qt_ref, kt_ref, q_ref, k_ref, e_ref, v_ref, km_ref, o_ref, m_ref, l_ref, acc_ref, qa_ref,
                 *, tq, n_sel):
    step = pl.program_id(2)
    qi = qt_ref[step]
    kj = kt_ref[step]

    @pl.when(kj == 0)
    def _():
        m_ref[...] = jnp.full_like(m_ref, NEG_BIG)
        l_ref[...] = jnp.zeros_like(l_ref)
        acc_ref[...] = jnp.zeros_like(acc_ref)
        own = (qi * tq + lax.broadcasted_iota(I32, (tq, 1), 0)) // MOBA_BLOCK
        gate = jnp.dot(q_ref[0, 0].astype(F32), km_ref[0, 0], preferred_element_type=F32,
                       precision=lax.Precision.HIGHEST)
        lane = lax.broadcasted_iota(I32, gate.shape, 1)
        lane_f = lane.astype(F32)
        gate = jnp.where(lane < own, gate, -jnp.inf)
        allowed = jnp.where(lane == own, 1.0, 0.0)
        for _ in range(n_sel):
            best = jnp.max(gate, axis=-1, keepdims=True)
            first = jnp.min(jnp.where(gate == best, lane_f, float(LANES)), axis=-1, keepdims=True)
            pick = (lane_f == first) & (best > -jnp.inf)
            allowed = jnp.where(pick, 1.0, allowed)
            gate = jnp.where(pick, -jnp.inf, gate)
        qa_ref[:, :HEAD_DIM] = q_ref[0, 0]
        qa_ref[:, HEAD_DIM:] = jnp.where(allowed > 0.0, 0.0, NEG_BIG).astype(qa_ref.dtype)

    k_aug = jnp.concatenate([k_ref[0, 0], e_ref[...]], axis=1)
    s = lax.dot_general(qa_ref[...], k_aug, (((1,), (1,)), ((), ())), preferred_element_type=F32)

    def update(s):
        m_prev = m_ref[...]
        m_new = jnp.maximum(m_prev, jnp.max(s, axis=-1, keepdims=True))
        alpha = jnp.exp2(m_prev - m_new)
        p = jnp.exp2(s - m_new)
        l_ref[...] = alpha * l_ref[...] + jnp.sum(p, axis=-1, keepdims=True)
        acc_ref[...] = alpha * acc_ref[...] + jnp.dot(p.astype(BF16), v_ref[...], preferred_element_type=F32)
        m_ref[...] = m_new

    @pl.when(kj < qi)
    def _():
        update(s)

    @pl.when(kj == qi)
    def _():
        row = lax.broadcasted_iota(I32, s.shape, 0)
        col = lax.broadcasted_iota(I32, s.shape, 1)
        update(jnp.where(col <= row, s, NEG_BIG))
        o_ref[...] = (acc_ref[...] / l_ref[...]).astype(o_ref.dtype)


def _moba_attention(qc, kc, kmean, z, v_col0, batch, s_len, tq=1024):
    tq = min(tq, s_len)
    nq = s_len // tq
    n_heads = qc.shape[1]
    nb = s_len // MOBA_BLOCK
    assert nb <= LANES
    n_sel = min(MOBA_TOPK, nb - 1)
    qt, kt = _tri_tables(nq, 1)
    km_t = jnp.pad(kmean.transpose(0, 2, 3, 1), ((0, 0), (0, 0), (0, 0), (0, LANES - nb)))
    block_onehot = (jnp.arange(s_len, dtype=I32)[:, None] // MOBA_BLOCK
                    == jnp.arange(LANES, dtype=I32)[None, :]).astype(BF16)
    vb0 = v_col0 // HEAD_DIM
    grid_spec = pltpu.PrefetchScalarGridSpec(
        num_scalar_prefetch=2,
        grid=(batch, n_heads, int(qt.shape[0])),
        in_specs=[pl.BlockSpec((1, 1, tq, HEAD_DIM), lambda b, h, s, qt, kt: (b, h, qt[s], 0)),
                  pl.BlockSpec((1, 1, tq, HEAD_DIM), lambda b, h, s, qt, kt: (b, h, kt[s], 0)),
                  pl.BlockSpec((tq, LANES), lambda b, h, s, qt, kt: (kt[s], 0)),
                  pl.BlockSpec((tq, HEAD_DIM), lambda b, h, s, qt, kt: (b * nq + kt[s], vb0 + h)),
                  pl.BlockSpec((1, 1, HEAD_DIM, LANES), lambda b, h, s, qt, kt: (b, h, 0, 0))],
        out_specs=pl.BlockSpec((tq, HEAD_DIM), lambda b, h, s, qt, kt: (b * nq + qt[s], h)),
        scratch_shapes=[pltpu.VMEM((tq, 1), F32), pltpu.VMEM((tq, 1), F32),
                        pltpu.VMEM((tq, HEAD_DIM), F32), pltpu.VMEM((tq, HEAD_DIM + LANES), BF16)])
    return pl.pallas_call(
        functools.partial(_moba_kernel, tq=tq, n_sel=n_sel), grid_spec=grid_spec,
        out_shape=jax.ShapeDtypeStruct((batch * s_len, n_heads * HEAD_DIM), BF16),
        compiler_params=_cparams(("parallel", "parallel", "arbitrary")),
        name="moba_attention",
    )(qt, kt, qc, kc, block_onehot, z, km_t)


def _swa_kernel(q_ref, kp_ref, kc_ref, vp_ref, vc_ref, sink_ref, o_ref):
    n = pl.program_id(1)
    hq, w = q_ref.shape[1], q_ref.shape[2]
    hkv = kc_ref.shape[1]
    grp = hq // hkv
    ti = lax.broadcasted_iota(I32, (w, w), 0)
    si = lax.broadcasted_iota(I32, (w, w), 1)
    cur_ok = (si <= ti)[None]
    prev_ok = ((si > ti) & (n > 0))[None]
    nt = (((1,), (1,)), ((), ()))
    for kv in range(hkv):
        q = q_ref[0, kv * grp:(kv + 1) * grp].reshape(grp * w, HEAD_DIM)
        sc = lax.dot_general(q, kc_ref[0, kv], nt, preferred_element_type=F32).reshape(grp, w, w)
        sp = lax.dot_general(q, kp_ref[0, kv], nt, preferred_element_type=F32).reshape(grp, w, w)
        sc = jnp.where(cur_ok, sc, -jnp.inf)
        sp = jnp.where(prev_ok, sp, -jnp.inf)
        sink = sink_ref[kv * grp * w:(kv + 1) * grp * w].reshape(grp, w, LANES)[:, :, :1]
        m = jnp.maximum(jnp.maximum(jnp.max(sc, axis=-1, keepdims=True), jnp.max(sp, axis=-1, keepdims=True)), sink)
        pc = jnp.exp2(sc - m)
        pp = jnp.exp2(sp - m)
        den = jnp.sum(pc, axis=-1, keepdims=True) + jnp.sum(pp, axis=-1, keepdims=True) + jnp.exp2(sink - m)
        o = (jnp.dot(pc.reshape(grp * w, w).astype(BF16), vc_ref[:, kv * HEAD_DIM:(kv + 1) * HEAD_DIM],
                     preferred_element_type=F32)
             + jnp.dot(pp.reshape(grp * w, w).astype(BF16), vp_ref[:, kv * HEAD_DIM:(kv + 1) * HEAD_DIM],
                       preferred_element_type=F32))
        o = o.reshape(grp, w, HEAD_DIM) / den
        for gh in range(grp):
            hh = kv * grp + gh
            o_ref[:, hh * HEAD_DIM:(hh + 1) * HEAD_DIM] = o[gh].astype(o_ref.dtype)


def _swa_attention(qd, kd, z, v_col0, sinks, batch, s_len):
    w = SWA_WINDOW
    nb = s_len // w
    hq, hkv = qd.shape[1], kd.shape[1]
    vw = hkv * HEAD_DIM
    vb0 = v_col0 // vw
    sink_b = jnp.broadcast_to((sinks.astype(F32) * LOG2_E)[:, None, None], (hq, w, LANES)).reshape(hq * w, LANES)
    return pl.pallas_call(
        _swa_kernel,
        grid=(batch, nb),
        in_specs=[pl.BlockSpec((1, hq, w, HEAD_DIM), lambda b, n: (b, 0, n, 0)),
                  pl.BlockSpec((1, hkv, w, HEAD_DIM), lambda b, n: (b, 0, jnp.maximum(n - 1, 0), 0)),
                  pl.BlockSpec((1, hkv, w, HEAD_DIM), lambda b, n: (b, 0, n, 0)),
                  pl.BlockSpec((w, vw), lambda b, n: (b * nb + jnp.maximum(n - 1, 0), vb0)),
                  pl.BlockSpec((w, vw), lambda b, n: (b * nb + n, vb0)),
                  pl.BlockSpec((hq * w, LANES), lambda b, n: (0, 0))],
        out_specs=pl.BlockSpec((w, hq * HEAD_DIM), lambda b, n: (b * nb + n, 0)),
        out_shape=jax.ShapeDtypeStruct((batch * s_len, hq * HEAD_DIM), BF16),
        compiler_params=_cparams(("parallel", "parallel")),
        name="swa_attention",
    )(qd, kd, kd, z, z, sink_b)


_HIGH_HALF = -65536


def _pack_halves(x):
    half = x.shape[1] // 2
    lo = lax.bitcast_convert_type(x[:, :half].astype(jnp.bfloat16).astype(F32), I32)
    hi = lax.bitcast_convert_type(x[:, half:].astype(jnp.bfloat16).astype(F32), I32)
    return ((lo >> 16) & 0xFFFF) | (hi & _HIGH_HALF)


def _unpack_halves(p):
    lo = lax.bitcast_convert_type(p << 16, F32)
    hi = lax.bitcast_convert_type(p & _HIGH_HALF, F32)
    return lo, hi


def _route_kernel(x_ref, g_ref, sc_ref, sh_ref, wr_ref, br_ref, tri_ref,
                  h_ref, idx_ref, wgt_ref, rank_ref, cnt_ref, carry_ref):
    @pl.when(pl.program_id(0) == 0)
    def _():
        carry_ref[...] = jnp.zeros_like(carry_ref)

    x = x_ref[...]
    y = x * lax.rsqrt(jnp.mean(x * x, axis=-1, keepdims=True) + EPS) * g_ref[...]
    hmod = y * (1.0 + sc_ref[0]) + sh_ref[0]
    h_ref[...] = _pack_halves(hmod)
    logits = jnp.dot(hmod, wr_ref[...], preferred_element_type=F32, precision=lax.Precision.HIGHEST) + br_ref[...]
    lane = lax.broadcasted_iota(I32, logits.shape, 1)
    lane_f = lane.astype(F32)
    logits = jnp.where(lane < N_EXPERTS, logits, -jnp.inf)
    onehots, vals, firsts = [], [], []
    for _ in range(MOE_TOPK):
        best = jnp.max(logits, axis=-1, keepdims=True)
        first = jnp.min(jnp.where(logits == best, lane_f, float(LANES)), axis=-1, keepdims=True)
        pick = lane_f == first
        onehots.append(pick)
        vals.append(best)
        firsts.append(first.astype(I32))
        logits = jnp.where(pick, -jnp.inf, logits)
    exps = [jnp.exp(v - vals[0]) for v in vals]
    den = exps[0]
    for e in exps[1:]:
        den = den + e
    chosen_f = jnp.zeros(logits.shape, F32)
    for o in onehots:
        chosen_f = jnp.where(o, 1.0, chosen_f)
    before = jnp.dot(tri_ref[...], chosen_f.astype(BF16), preferred_element_type=F32) + carry_ref[...]
    idx_out = jnp.zeros(logits.shape, I32)
    wgt_out = jnp.zeros(logits.shape, F32)
    rank_out = jnp.zeros(logits.shape, I32)
    for k in range(MOE_TOPK):
        rk = jnp.sum(jnp.where(onehots[k], before, 0.0), axis=-1, keepdims=True).astype(I32)
        idx_out = jnp.where(lane == k, firsts[k], idx_out)
        wgt_out = jnp.where(lane == k, exps[k] / den, wgt_out)
        rank_out = jnp.where(lane == k, rk, rank_out)
    idx_ref[...] = idx_out
    wgt_ref[...] = wgt_out
    rank_ref[...] = rank_out
    carry_ref[...] = carry_ref[...] + jnp.sum(chosen_f, axis=0, keepdims=True)
    cnt_ref[...] = carry_ref[...]


def _moe_route(x2, g, scale, shift, w_router, b_router, rows_per_batch, tm=256):
    n, d = x2.shape
    tpb = rows_per_batch // tm
    wr = jnp.pad(w_router, ((0, 0), (0, LANES - N_EXPERTS)))
    br = jnp.pad(b_router, (0, LANES - N_EXPERTS)).reshape(1, LANES)
    tri = jnp.tril(jnp.ones((tm, tm), F32), -1).astype(BF16)
    tok_spec = pl.BlockSpec((tm, LANES), lambda i: (i, 0))
    return pl.pallas_call(
        _route_kernel,
        grid=(n // tm,),
        in_specs=[pl.BlockSpec((tm, d), lambda i: (i, 0)),
                  pl.BlockSpec((1, d), lambda i: (0, 0)),
                  pl.BlockSpec((1, 1, d), lambda i: (i // tpb, 0, 0)),
                  pl.BlockSpec((1, 1, d), lambda i: (i // tpb, 0, 0)),
                  pl.BlockSpec((d, LANES), lambda i: (0, 0)),
                  pl.BlockSpec((1, LANES), lambda i: (0, 0)),
                  pl.BlockSpec((tm, tm), lambda i: (0, 0))],
        out_specs=[pl.BlockSpec((tm, d // 2), lambda i: (i, 0)), tok_spec, tok_spec, tok_spec,
                   pl.BlockSpec((1, LANES), lambda i: (0, 0))],
        out_shape=[jax.ShapeDtypeStruct((n, d // 2), I32), jax.ShapeDtypeStruct((n, LANES), I32),
                   jax.ShapeDtypeStruct((n, LANES), F32), jax.ShapeDtypeStruct((n, LANES), I32),
                   jax.ShapeDtypeStruct((1, LANES), F32)],
        scratch_shapes=[pltpu.VMEM((1, LANES), F32)],
        compiler_params=_cparams(("arbitrary",)),
        name="moe_route",
    )(x2, g.reshape(1, d), scale[:, None, :], shift[:, None, :], wr, br, tri)


def _invert_kernel(lo_ref, hi_ref, pos_ref, src_ref, *, chunk):
    step = pl.program_id(0)

    @pl.when(step == 0)
    def _():
        def fill_group(g, carry):
            def zero(p, c):
                src_ref[p] = 0
                return c
            lax.fori_loop(lo_ref[g], hi_ref[g], zero, 0)
            return carry
        lax.fori_loop(0, lo_ref.shape[0], fill_group, 0)

    tokens = chunk // MOE_TOPK

    def put(t, carry):
        for k in range(MOE_TOPK):
            src_ref[pos_ref[t * MOE_TOPK + k]] = step * tokens + t
        return carry

    lax.fori_loop(0, tokens, put, 0, unroll=4)


def _moe_invert(pos, pad_lo, pad_hi, n_rows, chunk=2048):
    n_assign = pos.shape[0]
    return pl.pallas_call(
        functools.partial(_invert_kernel, chunk=chunk),
        grid_spec=pltpu.PrefetchScalarGridSpec(
            num_scalar_prefetch=2, grid=(n_assign // chunk,),
            in_specs=[pl.BlockSpec((chunk,), lambda i, lo, hi: (i,), memory_space=pltpu.SMEM)],
            out_specs=pl.BlockSpec((n_rows,), lambda i, lo, hi: (0,), memory_space=pltpu.SMEM)),
        out_shape=jax.ShapeDtypeStruct((n_rows,), I32),
        compiler_params=_cparams(("arbitrary",), unchecked=True),
        name="moe_invert",
    )(pad_lo, pad_hi, pos)


def _expert_kernel(te_ref, nt_ref, src_ref, h_ref, wi_ref, bi_ref, perm_ref, wo_ref, bo_ref, y_ref,
                   xbuf_ref, sem):
    i = pl.program_id(0)
    n_used = nt_ref[0]
    n_slots, tm = xbuf_ref.shape[0], xbuf_ref.shape[1]

    def row_copy(row, slot, t):
        return pltpu.make_async_copy(h_ref.at[pl.ds(src_ref[row], 1)], xbuf_ref.at[slot, pl.ds(t, 1)], sem.at[slot])

    def wait_slot(slot):
        pltpu.make_async_copy(h_ref.at[pl.ds(0, tm)], xbuf_ref.at[slot], sem.at[slot]).wait()

    def gather(tile, slot):
        base = jnp.minimum(tile, n_used - 1) * tm
        for t in range(tm):
            row_copy(base + t, slot, t).start()

    @pl.when(i == 0)
    def _():
        gather(0, 0)
        gather(1, 1)

    for slot in range(n_slots):
        @pl.when((i < n_used) & (i % n_slots == slot))
        def _(slot=slot):
            wait_slot(slot)
            x_lo, x_hi = _unpack_halves(xbuf_ref[slot])
            x_lo, x_hi = x_lo.astype(BF16), x_hi.astype(BF16)
            half = x_lo.shape[1]
            gather(i + 2, (slot + 2) % n_slots)
            hh = (jnp.dot(x_lo, wi_ref[0, :half], preferred_element_type=F32)
                  + jnp.dot(x_hi, wi_ref[0, half:], preferred_element_type=F32) + bi_ref[0]).astype(BF16)
            hp = jnp.dot(hh, perm_ref[...], preferred_element_type=F32)
            x_glu = jnp.minimum(hp[:, :EXPERT_FF], SWIGLU_LIMIT)
            x_lin = jnp.clip(hp[:, EXPERT_FF:], -SWIGLU_LIMIT, SWIGLU_LIMIT)
            act = x_glu * (1.0 / (1.0 + jnp.exp(-SWIGLU_ALPHA * x_glu))) * (x_lin + 1.0)
            y_ref[...] = _pack_halves(jnp.dot(act.astype(BF16), wo_ref[0], preferred_element_type=F32) + bo_ref[0])

            @pl.when(i == n_used - 1)
            def _():
                wait_slot((slot + 1) % n_slots)
                wait_slot((slot + 2) % n_slots)

    @pl.when(i >= n_used)
    def _():
        y_ref[...] = jnp.zeros_like(y_ref)


def _moe_experts(h2, src, tile_expert, n_tiles_used, w_in, b_in, w_out, b_out):
    n, dh = h2.shape
    d = 2 * dh
    r = src.shape[0]
    tm = MOE_ROW_TILE
    f2 = w_in.shape[2]
    ff = w_out.shape[1]
    col = np.arange(f2)
    perm = np.zeros((f2, f2), np.float32)
    perm[col, np.where(col % 2 == 0, col // 2, ff + col // 2)] = 1.0
    grid_spec = pltpu.PrefetchScalarGridSpec(
        num_scalar_prefetch=3, grid=(r // tm,),
        in_specs=[pl.BlockSpec(memory_space=pl.ANY),
                  pl.BlockSpec((1, d, f2), lambda i, te, nt, src: (te[i], 0, 0)),
                  pl.BlockSpec((1, 1, f2), lambda i, te, nt, src: (te[i], 0, 0)),
                  pl.BlockSpec((f2, f2), lambda i, te, nt, src: (0, 0)),
                  pl.BlockSpec((1, ff, d), lambda i, te, nt, src: (te[i], 0, 0)),
                  pl.BlockSpec((1, 1, d), lambda i, te, nt, src: (te[i], 0, 0))],
        out_specs=pl.BlockSpec((tm, dh), lambda i, te, nt, src: (i, 0)),
        scratch_shapes=[pltpu.VMEM((3, tm, dh), I32), pltpu.SemaphoreType.DMA((3,))])
    return pl.pallas_call(
        _expert_kernel, grid_spec=grid_spec,
        out_shape=jax.ShapeDtypeStruct((r, dh), I32),
        compiler_params=_cparams(("arbitrary",), unchecked=True),
        name="moe_experts",
    )(tile_expert, n_tiles_used, src, h2, w_in, b_in, jnp.asarray(perm, BF16), w_out, b_out)


def _combine_kernel(pos_ref, y_ref, x_ref, w_ref, g_ref, o_ref, buf_ref, wb_ref, sem, *, tm):
    i = pl.program_id(0)
    n_steps = pl.num_programs(0)

    def gather(tile, slot):
        base = tile * tm * MOE_TOPK
        for t in range(tm):
            for k in range(MOE_TOPK):
                pltpu.make_async_copy(y_ref.at[pl.ds(pos_ref[base + t * MOE_TOPK + k], 1)],
                                      buf_ref.at[slot, k, pl.ds(t, 1)], sem.at[slot]).start()

    @pl.when(i == 0)
    def _():
        gather(0, 0)

    @pl.when((i + 1 < n_steps) & (i % 2 == 0))
    def _():
        gather(i + 1, 1)

    @pl.when((i + 1 < n_steps) & (i % 2 == 1))
    def _():
        gather(i + 1, 0)

    slot = i % 2
    for k in range(MOE_TOPK):
        pltpu.make_async_copy(y_ref.at[pl.ds(0, tm)], buf_ref.at[slot, k], sem.at[slot]).wait()
    w = w_ref[...]
    half = buf_ref.shape[3]
    for k in range(MOE_TOPK):
        wb_ref[k] = jnp.broadcast_to(w[:, k:k + 1], (tm, LANES))
    for c in range(half // LANES):
        lo_cols = slice(c * LANES, (c + 1) * LANES)
        hi_cols = slice(half + c * LANES, half + (c + 1) * LANES)
        mix_lo = jnp.zeros((tm, LANES), F32)
        mix_hi = jnp.zeros((tm, LANES), F32)
        for k in range(MOE_TOPK):
            y_lo, y_hi = _unpack_halves(buf_ref[slot, k, :, lo_cols])
            mix_lo = mix_lo + y_lo * wb_ref[k]
            mix_hi = mix_hi + y_hi * wb_ref[k]
        o_ref[:, lo_cols] = x_ref[:, lo_cols] + g_ref[0, :, lo_cols] * mix_lo
        o_ref[:, hi_cols] = x_ref[:, hi_cols] + g_ref[0, :, hi_cols] * mix_hi


def _moe_combine(y, pos, x2, wgt, gate, rows_per_batch, tm=128):
    n, d = x2.shape
    tpb = rows_per_batch // tm
    grid_spec = pltpu.PrefetchScalarGridSpec(
        num_scalar_prefetch=1, grid=(n // tm,),
        in_specs=[pl.BlockSpec(memory_space=pl.ANY),
                  pl.BlockSpec((tm, d), lambda i, pos: (i, 0)),
                  pl.BlockSpec((tm, LANES), lambda i, pos: (i, 0)),
                  pl.BlockSpec((1, 1, d), lambda i, pos: (i // tpb, 0, 0))],
        out_specs=pl.BlockSpec((tm, d), lambda i, pos: (i, 0)),
        scratch_shapes=[pltpu.VMEM((2, MOE_TOPK, tm, d // 2), I32), pltpu.VMEM((MOE_TOPK, tm, LANES), F32),
                        pltpu.SemaphoreType.DMA((2,))])
    return pl.pallas_call(
        functools.partial(_combine_kernel, tm=tm), grid_spec=grid_spec,
        out_shape=jax.ShapeDtypeStruct((n, d), F32),
        compiler_params=_cparams(("arbitrary",), unchecked=True),
        name="moe_combine",
    )(pos, y, x2, wgt, gate[:, None, :])


def _moe_block(x2, g, scale, shift, gate, w_router, b_router, w_exp_in, b_exp_in, w_exp_out, b_exp_out,
               expert_base, rows_per_batch):
    n, d = x2.shape
    tm = MOE_ROW_TILE
    h2, idx, wgt, rank, counts = _moe_route(x2, g, scale, shift, w_router, b_router, rows_per_batch)
    cnt = counts[0, :N_EXPERTS].astype(I32)
    padded = (cnt + tm - 1) // tm * tm
    ends = jnp.cumsum(padded)
    starts = ends - padded
    n_rows = n * MOE_TOPK + N_EXPERTS * tm
    n_tiles = n_rows // tm
    tile_start = jnp.arange(n_tiles, dtype=I32) * tm
    tile_expert = jnp.minimum(jnp.sum((ends[None, :] <= tile_start[:, None]).astype(I32), axis=1), N_EXPERTS - 1)
    n_tiles_used = (ends[-1] // tm).astype(I32).reshape(1)
    e_flat = idx[:, :MOE_TOPK].reshape(-1)
    pos = (starts[e_flat] + rank[:, :MOE_TOPK].reshape(-1)).astype(I32)
    pad_lo = jnp.concatenate([starts + cnt, ends[-1:]]).astype(I32)
    pad_hi = jnp.concatenate([ends, jnp.full((1,), n_rows, I32)]).astype(I32)
    src = _moe_invert(pos, pad_lo, pad_hi, n_rows)
    y = _moe_experts(h2, src, tile_expert + expert_base, n_tiles_used, w_exp_in, b_exp_in, w_exp_out, b_exp_out)
    return _moe_combine(y, pos, x2, wgt, gate, rows_per_batch)


def _even_mixer(h, x2, gate, batch, s_len, w_in, g_cq, w_uq, w_iq, gq_a, gk_a, g_kidx, b_kidx, gq_b, gk_b,
                b_f, w_out):
    gw = GROUP_WIDTH
    o_cq, o_ka, o_va, o_ki, o_wi, o_qb, o_kb, o_vb, o_fb, o_gb = np.cumsum(
        [0, DSA_Q_LORA, HEAD_DIM, HEAD_DIM, HEAD_DIM, DSA_IDX_HEADS, gw, gw, gw, GROUP_HEADS]).tolist()
    cols = lambda o, wdt: w_in[:, o:o + wdt]
    n_main = 4 * gw + DSA_Q_LORA + 3 * HEAD_DIM
    w_main = jnp.concatenate([cols(o_qb, gw), cols(o_kb, gw), cols(o_vb, gw), cols(o_gb, gw),
                              cols(o_cq, DSA_Q_LORA), cols(o_ka, HEAD_DIM), cols(o_va, HEAD_DIM),
                              cols(o_ki, HEAD_DIM), jnp.zeros((w_in.shape[0], -n_main % 512), F32)],
                             axis=1).astype(BF16)
    c_qb, c_kb, c_vb, c_gb, c_cq = 0, gw, 2 * gw, 3 * gw, 4 * gw
    c_ka = c_cq + DSA_Q_LORA
    c_va, c_ki = c_ka + HEAD_DIM, c_ka + 2 * HEAD_DIM
    d = w_in.shape[0]
    w_small = jnp.concatenate([cols(o_fb, GROUP_HEADS), cols(o_wi, DSA_IDX_HEADS),
                               jnp.zeros((d, LANES - GROUP_HEADS - DSA_IDX_HEADS), F32)], axis=1).astype(BF16)
    z = _matmul(h, w_main, BF16, "even_in_proj")
    zs = _matmul(h, w_small, F32, "even_in_proj_small")
    scale = Q_SCALE

    cq = _rowprep(z, c_cq, DSA_Q_LORA, batch, s_len, hw=DSA_Q_LORA, norm="rms", gain=g_cq, name="dsa_cq_norm")
    qa_raw = _matmul(cq, w_uq.astype(BF16), BF16, "dsa_q_up")
    qi_raw = _matmul(cq, w_iq.astype(BF16), BF16, "dsa_idx_q_up")
    qa = _rowprep(qa_raw, 0, gw, batch, s_len, norm="rms", gain=gq_a, rope_dim=HEAD_DIM, out_scale=scale,
                  head_major=True, name="dsa_q_prep")
    qi = _rowprep(qi_raw, 0, DSA_IDX_HEADS * HEAD_DIM, batch, s_len, rope_dim=DSA_IDX_ROPE, head_major=True,
                  name="dsa_idx_q_prep")
    ka = _rowprep(z, c_ka, HEAD_DIM, batch, s_len, norm="rms", gain=gk_a, rope_dim=HEAD_DIM, name="dsa_k_prep")
    ki = _rowprep(z, c_ki, HEAD_DIM, batch, s_len, norm="ln", gain=g_kidx, beta=b_kidx, rope_dim=DSA_IDX_ROPE,
                  name="dsa_idx_k_prep")
    va = z[:, c_va:c_va + HEAD_DIM]
    wi = jnp.pad(zs[:, GROUP_HEADS:GROUP_HEADS + DSA_IDX_HEADS] * (DSA_IDX_HEADS ** -0.5 * HEAD_DIM ** -0.5),
                 ((0, 0), (0, LANES - DSA_IDX_HEADS)))
    o_a = _dsa_attention(qi, ki, wi, qa, ka, va, batch, s_len)

    qb = _rowprep(z, c_qb, gw, batch, s_len, norm="rms", gain=gq_b, out_scale=scale, head_major=True,
                  name="fox_q_prep")
    kb = _rowprep(z, c_kb, gw, batch, s_len, norm="rms", gain=gk_b, head_major=True, name="fox_k_prep")
    bias_row = jnp.pad(b_f.astype(F32), (0, LANES - GROUP_HEADS)).reshape(1, LANES)
    cum = _fox_cumsum(zs, bias_row, batch, s_len)
    o_b = _fox_attention(qb, kb, z, c_vb, c_gb, cum, batch, s_len)
    return _out_proj(o_a, o_b, w_out.astype(BF16), x2, gate, s_len, "even_out_proj")


def _odd_mixer(h, x2, gate, batch, s_len, w_in, gq_c, gk_c, gq_d, gk_d, sinks, w_out):
    gw = GROUP_WIDTH
    kvw = SWA_KV_HEADS * HEAD_DIM
    c_qc, c_kc, c_vc, c_qd, c_kd = 0, gw, 2 * gw, 3 * gw, 4 * gw
    c_vd = c_kd + kvw
    z = _matmul(h, w_in.astype(BF16), BF16, "odd_in_proj")
    scale = Q_SCALE
    qc = _rowprep(z, c_qc, gw, batch, s_len, norm="rms", gain=gq_c, rope_dim=HEAD_DIM, out_scale=scale,
                  head_major=True, name="moba_q_prep")
    kc, kmean = _rowprep(z, c_kc, gw, batch, s_len, norm="rms", gain=gk_c, rope_dim=HEAD_DIM, head_major=True,
                         want_mean=True, name="moba_k_prep")
    o_c = _moba_attention(qc, kc, kmean, z, c_vc, batch, s_len)
    qd = _rowprep(z, c_qd, gw, batch, s_len, norm="rms", gain=gq_d, rope_dim=HEAD_DIM, out_scale=scale,
                  head_major=True, name="swa_q_prep")
    kd = _rowprep(z, c_kd, kvw, batch, s_len, norm="rms", gain=gk_d, rope_dim=HEAD_DIM, head_major=True,
                  name="swa_k_prep")
    o_d = _swa_attention(qd, kd, z, c_vd, sinks, batch, s_len)
    return _out_proj(o_c, o_d, w_out.astype(BF16), x2, gate, s_len, "odd_out_proj")


def kernel(x, c, g_norm_mix, g_norm_ffn, w_ada, b_ada, w_in_even, g_cq, w_uq, w_iq, gq_a, gk_a, g_kidx, b_kidx,
           gq_b, gk_b, b_forget, w_out_even, w_in_odd, gq_c, gk_c, gq_d, gk_d, sinks_d, w_out_odd,
           w_router, b_router, w_exp_in, b_exp_in, w_exp_out, b_exp_out):
    batch, s_len, d = x.shape
    depth = w_ada.shape[0]
    mod = _ada_modulation(c, w_ada, b_ada)
    x2 = x.reshape(batch * s_len, d)
    n_exp = w_exp_in.shape[1]
    w_in_all = w_exp_in.astype(BF16).reshape((depth * n_exp,) + w_exp_in.shape[2:])
    w_out_all = w_exp_out.astype(BF16).reshape((depth * n_exp,) + w_exp_out.shape[2:])
    b_in_all = b_exp_in.reshape(depth * n_exp, 1, -1)
    b_out_all = b_exp_out.reshape(depth * n_exp, 1, -1)
    for layer in range(depth):
        shift_m, scale_m, gate_m, shift_f, scale_f, gate_f = [mod[layer, :, i * d:(i + 1) * d] for i in range(6)]
        h = _norm_modulate(x2, g_norm_mix[layer], scale_m, shift_m, s_len)
        j = layer // 2
        if layer % 2 == 0:
            x2 = _even_mixer(h, x2, gate_m, batch, s_len, w_in_even[j], g_cq[j], w_uq[j], w_iq[j], gq_a[j],
                             gk_a[j], g_kidx[j], b_kidx[j], gq_b[j], gk_b[j], b_forget[j], w_out_even[j])
        else:
            x2 = _odd_mixer(h, x2, gate_m, batch, s_len, w_in_odd[j], gq_c[j], gk_c[j], gq_d[j], gk_d[j],
                            sinks_d[j], w_out_odd[j])
        x2 = _moe_block(x2, g_norm_ffn[layer], scale_f, shift_f, gate_f, w_router[layer], b_router[layer],
                        w_in_all, b_in_all, w_out_all, b_out_all, layer * n_exp, s_len)
    return x2.reshape(batch, s_len, d)
```

```python
import functools

import numpy as np
import jax
import jax.numpy as jnp
from jax import lax
from jax.experimental import pallas as pl
from jax.experimental.pallas import tpu as pltpu

F32 = jnp.float32
BF16 = jnp.bfloat16
I32 = jnp.int32

LANES = 128
HEAD_DIM = 128
GROUP_HEADS = 16
GROUP_WIDTH = GROUP_HEADS * HEAD_DIM
ROPE_THETA = 10000.0
EPS = 1e-6
DSA_Q_LORA = 1024
DSA_IDX_HEADS = 32
DSA_IDX_ROPE = 64
DSA_TOPK = 256
DSA_QBLK = 128
DSA_CHUNK = 256
DSA_STAGES = 8
DSA_HEAD_GROUP = 2
MOBA_BLOCK = 256
MOBA_TOPK = 3
SWA_KV_HEADS = 2
SWA_WINDOW = 128
N_EXPERTS = 32
MOE_TOPK = 4
EXPERT_FF = 512
SWIGLU_LIMIT = 7.0
SWIGLU_ALPHA = 1.702
MOE_ROW_TILE = 256
NEG_BIG = -1e30
LOG2_E = 1.4426950408889634
Q_SCALE = HEAD_DIM ** -0.5 * LOG2_E
VMEM_LIMIT = 56 * 1024 * 1024


def _cparams(semantics, vmem=VMEM_LIMIT, unchecked=False):
    return pltpu.CompilerParams(dimension_semantics=semantics, vmem_limit_bytes=vmem,
                                disable_bounds_checks=unchecked)


def _ada_kernel(cb_ref, w_ref, b_ref, o_ref):
    nb = cb_ref.shape[0]
    tn = w_ref.shape[2]
    for c in range(tn // LANES):
        w = w_ref[0, :, c * LANES:(c + 1) * LANES]
        for b in range(nb):
            o_ref[0, b:b + 1, c * LANES:(c + 1) * LANES] = (
                jnp.sum(w * cb_ref[b], axis=0, keepdims=True) + b_ref[0, :, c * LANES:(c + 1) * LANES])


def _ada_modulation(c, w_ada, b_ada, tn=512):
    depth, d, n6 = w_ada.shape
    nb = c.shape[0]
    cb = jnp.broadcast_to(jax.nn.silu(c)[:, :, None], (nb, d, LANES))
    return pl.pallas_call(
        _ada_kernel,
        grid=(depth, n6 // tn),
        in_specs=[pl.BlockSpec((nb, d, LANES), lambda l, j: (0, 0, 0)),
                  pl.BlockSpec((1, d, tn), lambda l, j: (l, 0, j)),
                  pl.BlockSpec((1, 1, tn), lambda l, j: (l, 0, j))],
        out_specs=pl.BlockSpec((1, nb, tn), lambda l, j: (l, 0, j)),
        out_shape=jax.ShapeDtypeStruct((depth, nb, n6), F32),
        compiler_params=_cparams(("parallel", "parallel")),
        name="ada_modulation",
    )(cb, w_ada, b_ada.reshape(depth, 1, n6))


def _normmod_kernel(x_ref, g_ref, sc_ref, sh_ref, o_ref):
    x = x_ref[...]
    y = x * lax.rsqrt(jnp.mean(x * x, axis=-1, keepdims=True) + EPS) * g_ref[...]
    o_ref[...] = (y * (1.0 + sc_ref[0]) + sh_ref[0]).astype(o_ref.dtype)


def _norm_modulate(x2, g, scale, shift, rows_per_batch, tm=512):
    n, d = x2.shape
    tpb = rows_per_batch // tm
    return pl.pallas_call(
        _normmod_kernel,
        grid=(n // tm,),
        in_specs=[pl.BlockSpec((tm, d), lambda i: (i, 0)),
                  pl.BlockSpec((1, d), lambda i: (0, 0)),
                  pl.BlockSpec((1, 1, d), lambda i: (i // tpb, 0, 0)),
                  pl.BlockSpec((1, 1, d), lambda i: (i // tpb, 0, 0))],
        out_specs=pl.BlockSpec((tm, d), lambda i: (i, 0)),
        out_shape=jax.ShapeDtypeStruct((n, d), BF16),
        compiler_params=_cparams(("parallel",)),
        name="norm_modulate",
    )(x2, g.reshape(1, d), scale[:, None, :], shift[:, None, :])


def _mm_kernel(a_ref, b_ref, o_ref):
    o_ref[...] = jnp.dot(a_ref[...], b_ref[...], preferred_element_type=F32).astype(o_ref.dtype)


def _mm2_residual_kernel(a1_ref, a2_ref, b_ref, x_ref, g_ref, o_ref):
    k1 = a1_ref.shape[1]
    acc = (jnp.dot(a1_ref[...], b_ref[:k1], preferred_element_type=F32)
           + jnp.dot(a2_ref[...], b_ref[k1:], preferred_element_type=F32))
    o_ref[...] = x_ref[...] + g_ref[0] * acc


def _out_proj(a1, a2, b, residual, gate, rows_per_batch, name):
    m, k1 = a1.shape
    k2 = a2.shape[1]
    n = b.shape[1]
    tm = _pick(m, (1024, 512, 256, 128))
    tn = _pick(n, (512, 256, 128))
    tpb = rows_per_batch // tm
    o_spec = pl.BlockSpec((tm, tn), lambda i, j: (i, j))
    return pl.pallas_call(
        _mm2_residual_kernel, grid=(m // tm, n // tn),
        in_specs=[pl.BlockSpec((tm, k1), lambda i, j: (i, 0)), pl.BlockSpec((tm, k2), lambda i, j: (i, 0)),
                  pl.BlockSpec((k1 + k2, tn), lambda i, j: (0, j)), o_spec,
                  pl.BlockSpec((1, 1, tn), lambda i, j: (i // tpb, 0, j))],
        out_specs=o_spec, out_shape=jax.ShapeDtypeStruct((m, n), F32),
        compiler_params=_cparams(("parallel", "arbitrary")), name=name)(a1, a2, b, residual, gate[:, None, :])


def _pick(n, prefs):
    for t in prefs:
        if n % t == 0:
            return t
    return n


def _matmul(a, b, out_dtype, name):
    m, k = a.shape
    n = b.shape[1]
    tm = _pick(m, (1024, 512, 256, 128))
    tn = _pick(n, (512, 256, 128) if k > 2048 else (1024, 512, 256, 128))
    return pl.pallas_call(
        _mm_kernel, grid=(m // tm, n // tn),
        in_specs=[pl.BlockSpec((tm, k), lambda i, j: (i, 0)), pl.BlockSpec((k, tn), lambda i, j: (0, j))],
        out_specs=pl.BlockSpec((tm, tn), lambda i, j: (i, j)),
        out_shape=jax.ShapeDtypeStruct((m, n), out_dtype),
        compiler_params=_cparams(("parallel", "arbitrary")), name=name)(a, b)


def _rope_tables(s_len, rot_dim):
    half = rot_dim // 2
    inv_freq = ROPE_THETA ** (-jnp.arange(half, dtype=F32) / half)
    ang = jnp.arange(s_len, dtype=F32)[:, None] * inv_freq[None, :]
    cos, sin = jnp.cos(ang), jnp.sin(ang)
    pad = HEAD_DIM - rot_dim
    ones = jnp.ones((s_len, pad), F32)
    zeros = jnp.zeros((s_len, pad), F32)
    zh = jnp.zeros((s_len, half), F32)
    cos_t = jnp.concatenate([cos, cos, ones], axis=1)
    sa = jnp.concatenate([-sin, zh, zeros], axis=1)
    sb = jnp.concatenate([zh, sin, zeros], axis=1)
    return cos_t, sa, sb


def _rowprep_kernel(*refs, hw, norm, rope_half, out_scale, head_major, want_mean):
    it = iter(refs)
    z_ref = next(it)
    g_ref = next(it) if norm != "none" else None
    beta_ref = next(it) if norm == "ln" else None
    if rope_half:
        cos_ref, sa_ref, sb_ref = next(it), next(it), next(it)
    o_ref = next(it)
    mean_ref = next(it) if want_mean else None
    width = z_ref.shape[1]
    for h in range(width // hw):
        x = z_ref[:, h * hw:(h + 1) * hw].astype(F32)
        if norm == "rms":
            x = x * lax.rsqrt(jnp.mean(x * x, axis=-1, keepdims=True) + EPS) * g_ref[...]
        elif norm == "ln":
            mu = jnp.mean(x, axis=-1, keepdims=True)
            xc = x - mu
            x = xc * lax.rsqrt(jnp.mean(xc * xc, axis=-1, keepdims=True) + EPS) * g_ref[...] + beta_ref[...]
        if rope_half:
            if rope_half * 2 == hw:
                x = x * cos_ref[...] + pltpu.roll(x, rope_half, 1) * (sa_ref[...] + sb_ref[...])
            else:
                x = (x * cos_ref[...] + pltpu.roll(x, hw - rope_half, 1) * sa_ref[...]
                     + pltpu.roll(x, rope_half, 1) * sb_ref[...])
        if want_mean:
            for blk in range(x.shape[0] // MOBA_BLOCK):
                mean_ref[0, blk, h:h + 1, :] = jnp.mean(x[blk * MOBA_BLOCK:(blk + 1) * MOBA_BLOCK], axis=0,
                                                        keepdims=True)
        if out_scale != 1.0:
            x = x * out_scale
        if head_major:
            o_ref[0, h] = x.astype(o_ref.dtype)
        else:
            o_ref[:, h * hw:(h + 1) * hw] = x.astype(o_ref.dtype)


def _rowprep(z, col0, width, batch, s_len, *, hw=HEAD_DIM, norm="none", gain=None, beta=None, rope_dim=0,
             out_scale=1.0, head_major=False, want_mean=False, name="rowprep", tm=1024):
    n = z.shape[0]
    tm = min(tm, s_len)
    tpb = s_len // tm
    assert col0 % width == 0 and n == batch * s_len
    cb = col0 // width
    args = [z]
    specs = [pl.BlockSpec((tm, width), lambda i: (i, cb))]
    if norm != "none":
        args.append(gain.reshape(1, hw).astype(F32))
        specs.append(pl.BlockSpec((1, hw), lambda i: (0, 0)))
    if norm == "ln":
        args.append(beta.reshape(1, hw).astype(F32))
        specs.append(pl.BlockSpec((1, hw), lambda i: (0, 0)))
    if rope_dim:
        args += list(_rope_tables(s_len, rope_dim))
        specs += [pl.BlockSpec((tm, HEAD_DIM), lambda i: (i % tpb, 0))] * 3
    nh = width // hw
    if head_major:
        out_shape = [jax.ShapeDtypeStruct((batch, nh, s_len, hw), BF16)]
        out_specs = [pl.BlockSpec((1, nh, tm, hw), lambda i: (i // tpb, 0, i % tpb, 0))]
    else:
        out_shape = [jax.ShapeDtypeStruct((n, width), BF16)]
        out_specs = [pl.BlockSpec((tm, width), lambda i: (i, 0))]
    if want_mean:
        bpt = tm // MOBA_BLOCK
        out_shape.append(jax.ShapeDtypeStruct((batch, tpb * bpt, nh, hw), F32))
        out_specs.append(pl.BlockSpec((1, bpt, nh, hw), lambda i: (i // tpb, i % tpb, 0, 0)))
    kern = functools.partial(_rowprep_kernel, hw=hw, norm=norm, rope_half=rope_dim // 2, out_scale=out_scale,
                             head_major=head_major, want_mean=want_mean)
    res = pl.pallas_call(kern, grid=(n // tm,), in_specs=specs, out_specs=out_specs, out_shape=out_shape,
                         compiler_params=_cparams(("parallel",)), name=name)(*args)
    return res if want_mean else res[0]


def _fox_cum_kernel(z_ref, bf_ref, tri_ref, o_ref, carry_ref):
    @pl.when(pl.program_id(1) == 0)
    def _():
        carry_ref[...] = jnp.zeros_like(carry_ref)

    xv = z_ref[...] + bf_ref[...]
    lf = jnp.minimum(xv, 0.0) - jnp.log(1.0 + jnp.exp(-jnp.abs(xv)))
    hi = lf.astype(BF16)
    r1 = lf - hi.astype(F32)
    mid = r1.astype(BF16)
    lo = (r1 - mid.astype(F32)).astype(BF16)
    tri = tri_ref[...]
    cs = (jnp.dot(tri, hi, preferred_element_type=F32) + jnp.dot(tri, mid, preferred_element_type=F32)
          + jnp.dot(tri, lo, preferred_element_type=F32)) + carry_ref[...]
    o_ref[...] = cs * LOG2_E
    tm = cs.shape[0]
    carry_ref[...] = cs[tm - 1:tm, :]


def _fox_cumsum(zs, bias_row, batch, s_len, tm=256):
    n = zs.shape[0]
    tpb = s_len // tm
    tri = jnp.tril(jnp.ones((tm, tm), F32)).astype(BF16)
    return pl.pallas_call(
        _fox_cum_kernel,
        grid=(batch, tpb),
        in_specs=[pl.BlockSpec((tm, LANES), lambda b, i: (b * tpb + i, 0)),
                  pl.BlockSpec((1, LANES), lambda b, i: (0, 0)),
                  pl.BlockSpec((tm, tm), lambda b, i: (0, 0))],
        out_specs=pl.BlockSpec((tm, LANES), lambda b, i: (b * tpb + i, 0)),
        out_shape=jax.ShapeDtypeStruct((n, LANES), F32),
        scratch_shapes=[pltpu.VMEM((1, LANES), F32)],
        compiler_params=_cparams(("parallel", "arbitrary")),
        name="fox_cumsum",
    )(zs, bias_row, tri)


def _dsa_kernel(qi_ref, kit_ref, wi_ref, qa_ref, kat_ref, va_ref, o_ref, wb_ref, key_ref, mb_ref,
                *, sk, q0, topk):
    q = DSA_QBLK
    ch = DSA_CHUNK
    nch = sk // ch
    n_idx = qi_ref.shape[1]
    n_heads = qa_ref.shape[1]
    t0 = (q0 + pl.program_id(1)) * q
    row = t0 + lax.broadcasted_iota(I32, (q, ch), 0)
    lane = lax.broadcasted_iota(I32, (q, ch), 1)

    wi = wi_ref[...]
    for h in range(n_idx):
        wb_ref[h] = jnp.broadcast_to(wi[:, h:h + 1], (q, ch))
    qi = qi_ref[0].reshape(n_idx * q, HEAD_DIM)

    def idx_chunk(c, carry):
        d = jnp.dot(qi, kit_ref[0, c], preferred_element_type=F32)
        acc = jnp.zeros((q, ch), F32)
        for h in range(n_idx):
            acc = acc + jnp.maximum(d[h * q:(h + 1) * q], 0.0) * wb_ref[h]
        score = jnp.where(c * ch + lane <= row, acc + 0.0, -jnp.inf)
        bits = lax.bitcast_convert_type(score, I32)
        key_ref[c] = jnp.where(bits >= 0, bits, bits ^ jnp.int32(0x7FFFFFFF))
        return carry

    lax.fori_loop(0, nch, idx_chunk, 0)

    def count_ge(cand):
        acc = jnp.zeros((q, LANES), F32)
        for c in range(nch):
            for part in range(ch // LANES):
                acc = acc + jnp.where(key_ref[c, :, part * LANES:(part + 1) * LANES] >= cand, 1.0, 0.0)
        return jnp.sum(acc, axis=-1, keepdims=True)

    int_min = jnp.int32(-2 ** 31)
    thr = jnp.where(count_ge(jnp.zeros((q, 1), I32)) >= topk, jnp.int32(0), int_min)

    def bit_step(j, thr):
        cand = thr | jnp.left_shift(jnp.int32(1), 30 - j)
        return jnp.where(count_ge(cand) >= topk, cand, thr)

    thr = lax.fori_loop(0, 31, bit_step, thr)

    for c in range(nch):
        keep = (key_ref[c] >= thr) & (c * ch + lane <= row)
        mb_ref[:, c * ch:(c + 1) * ch] = jnp.where(keep, 0.0, -jnp.inf)

    g = DSA_HEAD_GROUP

    for hg in range(n_heads // g):
        qh = qa_ref[0, hg * g:(hg + 1) * g].reshape(g * q, HEAD_DIM)
        lg = jnp.dot(qh, kat_ref[0], preferred_element_type=F32).reshape(g, q, sk) + mb_ref[...][None]
        m = jnp.max(lg, axis=-1, keepdims=True)
        p = jnp.exp2(lg - m)
        l = jnp.sum(p, axis=-1, keepdims=True)
        o = jnp.dot(p.reshape(g * q, sk).astype(BF16), va_ref[0], preferred_element_type=F32)
        o = (o.reshape(g, q, HEAD_DIM) / l).astype(o_ref.dtype)
        for hh in range(g):
            head = hg * g + hh
            o_ref[0, :, head * HEAD_DIM:(head + 1) * HEAD_DIM] = o[hh]


def _dsa_attention(qi, ki, wi, qa, ka, va, batch, s_len):
    q, ch = DSA_QBLK, DSA_CHUNK
    n_idx, n_heads = qi.shape[1], qa.shape[1]
    topk = min(DSA_TOPK, s_len // 4)
    nqb = s_len // q
    kit = ki.reshape(batch, s_len // ch, ch, HEAD_DIM).transpose(0, 1, 3, 2)
    kat = ka.reshape(batch, s_len, HEAD_DIM).transpose(0, 2, 1)
    va3 = va.reshape(batch, s_len, HEAD_DIM)
    per_stage = max(nqb // DSA_STAGES, ch // q)
    outs = []
    for q0 in range(0, nqb, per_stage):
        sk = (q0 + per_stage) * q
        nch = sk // ch
        kern = functools.partial(_dsa_kernel, sk=sk, q0=q0, topk=topk)
        outs.append(pl.pallas_call(
            kern,
            grid=(batch, per_stage),
            in_specs=[pl.BlockSpec((1, n_idx, q, HEAD_DIM), lambda b, i, q0=q0: (b, 0, q0 + i, 0)),
                      pl.BlockSpec((1, nch, HEAD_DIM, ch), lambda b, i: (b, 0, 0, 0)),
                      pl.BlockSpec((q, LANES), lambda b, i, q0=q0: (b * nqb + q0 + i, 0)),
                      pl.BlockSpec((1, n_heads, q, HEAD_DIM), lambda b, i, q0=q0: (b, 0, q0 + i, 0)),
                      pl.BlockSpec((1, HEAD_DIM, sk), lambda b, i: (b, 0, 0)),
                      pl.BlockSpec((1, sk, HEAD_DIM), lambda b, i: (b, 0, 0))],
            out_specs=pl.BlockSpec((1, q, n_heads * HEAD_DIM), lambda b, i: (b, i, 0)),
            out_shape=jax.ShapeDtypeStruct((batch, per_stage * q, n_heads * HEAD_DIM), BF16),
            scratch_shapes=[pltpu.VMEM((n_idx, q, ch), F32),
                            pltpu.VMEM((nch, q, ch), I32),
                            pltpu.VMEM((q, sk), F32)],
            compiler_params=_cparams(("parallel", "parallel")),
            name=f"dsa_attention_k{sk}",
        )(qi, kit, wi, qa, kat, va3))
    return jnp.concatenate(outs, axis=1).reshape(batch * s_len, n_heads * HEAD_DIM)


def _tri_tables(nq, ratio):
    qt, kt = [], []
    for i in range(nq):
        for j in range((i + 1) * ratio):
            qt.append(i)
            kt.append(j)
    return jnp.asarray(qt, I32), jnp.asarray(kt, I32)


def _fox_kernel(qt_ref, kt_ref, q_ref, k_ref, v_ref, cq_ref, ck_ref, g_ref, o_ref,
                m_ref, l_ref, acc_ref, cqs_ref, *, tq):
    h = pl.program_id(1)
    step = pl.program_id(2)
    qi = qt_ref[step]
    kj = kt_ref[step]

    @pl.when(kj == 0)
    def _():
        m_ref[...] = jnp.full_like(m_ref, -jnp.inf)
        l_ref[...] = jnp.zeros_like(l_ref)
        acc_ref[...] = jnp.zeros_like(acc_ref)
        lane = lax.broadcasted_iota(I32, cq_ref.shape, 1)
        cqs_ref[...] = jnp.sum(jnp.where(lane == h, cq_ref[...], 0.0), axis=-1, keepdims=True)

    s = lax.dot_general(q_ref[0, 0], k_ref[0, 0], (((1,), (1,)), ((), ())), preferred_element_type=F32)
    s = s + (cqs_ref[...] - ck_ref[0, 0])

    def update(s):
        m_prev = m_ref[...]
        m_new = jnp.maximum(m_prev, jnp.max(s, axis=-1, keepdims=True))
        alpha = jnp.exp2(m_prev - m_new)
        p = jnp.exp2(s - m_new)
        v_ones = jnp.concatenate([v_ref[...], jnp.ones(v_ref.shape, v_ref.dtype)], axis=1)
        acc_ref[...] = alpha * acc_ref[...] + jnp.dot(p.astype(BF16), v_ones, preferred_element_type=F32)
        m_ref[...] = m_new

    @pl.when(kj < qi)
    def _():
        update(s)

    @pl.when(kj == qi)
    def _():
        row = lax.broadcasted_iota(I32, s.shape, 0)
        col = lax.broadcasted_iota(I32, s.shape, 1)
        update(jnp.where(col <= row, s, -jnp.inf))
        gate = g_ref[...].astype(F32)
        o_ref[...] = (acc_ref[:, :HEAD_DIM] / acc_ref[:, HEAD_DIM:HEAD_DIM + 1]
                      * (1.0 / (1.0 + jnp.exp(-gate)))).astype(o_ref.dtype)


def _fox_attention(qb, kb, z, v_col0, g_col0, cum, batch, s_len, tq=1024):
    tq = min(tq, s_len)
    nq = s_len // tq
    n_heads = qb.shape[1]
    qt, kt = _tri_tables(nq, 1)
    cum_t = cum.reshape(batch, s_len, LANES)[:, :, :n_heads].transpose(0, 2, 1)[:, :, None, :]
    vb0, gb0 = v_col0 // HEAD_DIM, g_col0 // HEAD_DIM
    grid_spec = pltpu.PrefetchScalarGridSpec(
        num_scalar_prefetch=2,
        grid=(batch, n_heads, int(qt.shape[0])),
        in_specs=[pl.BlockSpec((1, 1, tq, HEAD_DIM), lambda b, h, s, qt, kt: (b, h, qt[s], 0)),
                  pl.BlockSpec((1, 1, tq, HEAD_DIM), lambda b, h, s, qt, kt: (b, h, kt[s], 0)),
                  pl.BlockSpec((tq, HEAD_DIM), lambda b, h, s, qt, kt: (b * nq + kt[s], vb0 + h)),
                  pl.BlockSpec((tq, LANES), lambda b, h, s, qt, kt: (b * nq + qt[s], 0)),
                  pl.BlockSpec((1, 1, 1, tq), lambda b, h, s, qt, kt: (b, h, 0, kt[s])),
                  pl.BlockSpec((tq, HEAD_DIM), lambda b, h, s, qt, kt: (b * nq + qt[s], gb0 + h))],
        out_specs=pl.BlockSpec((tq, HEAD_DIM), lambda b, h, s, qt, kt: (b * nq + qt[s], h)),
        scratch_shapes=[pltpu.VMEM((tq, 1), F32), pltpu.VMEM((tq, 1), F32),
                        pltpu.VMEM((tq, 2 * HEAD_DIM), F32), pltpu.VMEM((tq, 1), F32)])
    return pl.pallas_call(
        functools.partial(_fox_kernel, tq=tq), grid_spec=grid_spec,
        out_shape=jax.ShapeDtypeStruct((batch * s_len, n_heads * HEAD_DIM), BF16),
        compiler_params=_cparams(("parallel", "parallel", "arbitrary")),
        name="fox_attention",
    )(qt, kt, qb, kb, z, cum, cum_t, z)


def _moba_kernel(qt_ref, kt_ref, q_ref, k_ref, e_ref, v_ref, km_ref, o_ref, m_ref, l_ref, acc_ref, qa_ref,
                 *, tq, n_sel):
    step = pl.program_id(2)
    qi = qt_ref[step]
    kj = kt_ref[step]

    @pl.when(kj == 0)
    def _():
        m_ref[...] = jnp.full_like(m_ref, NEG_BIG)
        l_ref[...] = jnp.zeros_like(l_ref)
        acc_ref[...] = jnp.zeros_like(acc_ref)
        own = (qi * tq + lax.broadcasted_iota(I32, (tq, 1), 0)) // MOBA_BLOCK
        gate = jnp.dot(q_ref[0, 0].astype(F32), km_ref[0, 0], preferred_element_type=F32,
                       precision=lax.Precision.HIGHEST)
        lane = lax.broadcasted_iota(I32, gate.shape, 1)
        lane_f = lane.astype(F32)
        gate = jnp.where(lane < own, gate, -jnp.inf)
        allowed = jnp.where(lane == own, 1.0, 0.0)
        for _ in range(n_sel):
            best = jnp.max(gate, axis=-1, keepdims=True)
            first = jnp.min(jnp.where(gate == best, lane_f, float(LANES)), axis=-1, keepdims=True)
            pick = (lane_f == first) & (best > -jnp.inf)
            allowed = jnp.where(pick, 1.0, allowed)
            gate = jnp.where(pick, -jnp.inf, gate)
        qa_ref[:, :HEAD_DIM] = q_ref[0, 0]
        qa_ref[:, HEAD_DIM:] = jnp.where(allowed > 0.0, 0.0, NEG_BIG).astype(qa_ref.dtype)

    k_aug = jnp.concatenate([k_ref[0, 0], e_ref[...]], axis=1)
    s = lax.dot_general(qa_ref[...], k_aug, (((1,), (1,)), ((), ())), preferred_element_type=F32)

    def update(s):
        m_prev = m_ref[...]
        m_new = jnp.maximum(m_prev, jnp.max(s, axis=-1, keepdims=True))
        alpha = jnp.exp2(m_prev - m_new)
        p = jnp.exp2(s - m_new)
        l_ref[...] = alpha * l_ref[...] + jnp.sum(p, axis=-1, keepdims=True)
        acc_ref[...] = alpha * acc_ref[...] + jnp.dot(p.astype(BF16), v_ref[...], preferred_element_type=F32)
        m_ref[...] = m_new

    @pl.when(kj < qi)
    def _():
        update(s)

    @pl.when(kj == qi)
    def _():
        row = lax.broadcasted_iota(I32, s.shape, 0)
        col = lax.broadcasted_iota(I32, s.shape, 1)
        update(jnp.where(col <= row, s, NEG_BIG))
        o_ref[...] = (acc_ref[...] / l_ref[...]).astype(o_ref.dtype)


def _moba_attention(qc, kc, kmean, z, v_col0, batch, s_len, tq=1024):
    tq = min(tq, s_len)
    nq = s_len // tq
    n_heads = qc.shape[1]
    nb = s_len // MOBA_BLOCK
    assert nb <= LANES
    n_sel = min(MOBA_TOPK, nb - 1)
    qt, kt = _tri_tables(nq, 1)
    km_t = jnp.pad(kmean.transpose(0, 2, 3, 1), ((0, 0), (0, 0), (0, 0), (0, LANES - nb)))
    block_onehot = (jnp.arange(s_len, dtype=I32)[:, None] // MOBA_BLOCK
                    == jnp.arange(LANES, dtype=I32)[None, :]).astype(BF16)
    vb0 = v_col0 // HEAD_DIM
    grid_spec = pltpu.PrefetchScalarGridSpec(
        num_scalar_prefetch=2,
        grid=(batch, n_heads, int(qt.shape[0])),
        in_specs=[pl.BlockSpec((1, 1, tq, HEAD_DIM), lambda b, h, s, qt, kt: (b, h, qt[s], 0)),
                  pl.BlockSpec((1, 1, tq, HEAD_DIM), lambda b, h, s, qt, kt: (b, h, kt[s], 0)),
                  pl.BlockSpec((tq, LANES), lambda b, h, s, qt, kt: (kt[s], 0)),
                  pl.BlockSpec((tq, HEAD_DIM), lambda b, h, s, qt, kt: (b * nq + kt[s], vb0 + h)),
                  pl.BlockSpec((1, 1, HEAD_DIM, LANES), lambda b, h, s, qt, kt: (b, h, 0, 0))],
        out_specs=pl.BlockSpec((tq, HEAD_DIM), lambda b, h, s, qt, kt: (b * nq + qt[s], h)),
        scratch_shapes=[pltpu.VMEM((tq, 1), F32), pltpu.VMEM((tq, 1), F32),
                        pltpu.VMEM((tq, HEAD_DIM), F32), pltpu.VMEM((tq, HEAD_DIM + LANES), BF16)])
    return pl.pallas_call(
        functools.partial(_moba_kernel, tq=tq, n_sel=n_sel), grid_spec=grid_spec,
        out_shape=jax.ShapeDtypeStruct((batch * s_len, n_heads * HEAD_DIM), BF16),
        compiler_params=_cparams(("parallel", "parallel", "arbitrary")),
        name="moba_attention",
    )(qt, kt, qc, kc, block_onehot, z, km_t)


def _swa_kernel(q_ref, kp_ref, kc_ref, vp_ref, vc_ref, sink_ref, o_ref):
    n = pl.program_id(1)
    hq, w = q_ref.shape[1], q_ref.shape[2]
    hkv = kc_ref.shape[1]
    grp = hq // hkv
    ti = lax.broadcasted_iota(I32, (w, w), 0)
    si = lax.broadcasted_iota(I32, (w, w), 1)
    cur_ok = (si <= ti)[None]
    prev_ok = ((si > ti) & (n > 0))[None]
    nt = (((1,), (1,)), ((), ()))
    for kv in range(hkv):
        q = q_ref[0, kv * grp:(kv + 1) * grp].reshape(grp * w, HEAD_DIM)
        sc = lax.dot_general(q, kc_ref[0, kv], nt, preferred_element_type=F32).reshape(grp, w, w)
        sp = lax.dot_general(q, kp_ref[0, kv], nt, preferred_element_type=F32).reshape(grp, w, w)
        sc = jnp.where(cur_ok, sc, -jnp.inf)
        sp = jnp.where(prev_ok, sp, -jnp.inf)
        sink = sink_ref[kv * grp * w:(kv + 1) * grp * w].reshape(grp, w, LANES)[:, :, :1]
        m = jnp.maximum(jnp.maximum(jnp.max(sc, axis=-1, keepdims=True), jnp.max(sp, axis=-1, keepdims=True)), sink)
        pc = jnp.exp2(sc - m)
        pp = jnp.exp2(sp - m)
        den = jnp.sum(pc, axis=-1, keepdims=True) + jnp.sum(pp, axis=-1, keepdims=True) + jnp.exp2(sink - m)
        o = (jnp.dot(pc.reshape(grp * w, w).astype(BF16), vc_ref[:, kv * HEAD_DIM:(kv + 1) * HEAD_DIM],
                     preferred_element_type=F32)
             + jnp.dot(pp.reshape(grp * w, w).astype(BF16), vp_ref[:, kv * HEAD_DIM:(kv + 1) * HEAD_DIM],
                       preferred_element_type=F32))
        o = o.reshape(grp, w, HEAD_DIM) / den
        for gh in range(grp):
            hh = kv * grp + gh
            o_ref[:, hh * HEAD_DIM:(hh + 1) * HEAD_DIM] = o[gh].astype(o_ref.dtype)


def _swa_attention(qd, kd, z, v_col0, sinks, batch, s_len):
    w = SWA_WINDOW
    nb = s_len // w
    hq, hkv = qd.shape[1], kd.shape[1]
    vw = hkv * HEAD_DIM
    vb0 = v_col0 // vw
    sink_b = jnp.broadcast_to((sinks.astype(F32) * LOG2_E)[:, None, None], (hq, w, LANES)).reshape(hq * w, LANES)
    return pl.pallas_call(
        _swa_kernel,
        grid=(batch, nb),
        in_specs=[pl.BlockSpec((1, hq, w, HEAD_DIM), lambda b, n: (b, 0, n, 0)),
                  pl.BlockSpec((1, hkv, w, HEAD_DIM), lambda b, n: (b, 0, jnp.maximum(n - 1, 0), 0)),
                  pl.BlockSpec((1, hkv, w, HEAD_DIM), lambda b, n: (b, 0, n, 0)),
                  pl.BlockSpec((w, vw), lambda b, n: (b * nb + jnp.maximum(n - 1, 0), vb0)),
                  pl.BlockSpec((w, vw), lambda b, n: (b * nb + n, vb0)),
                  pl.BlockSpec((hq * w, LANES), lambda b, n: (0, 0))],
        out_specs=pl.BlockSpec((w, hq * HEAD_DIM), lambda b, n: (b * nb + n, 0)),
        out_shape=jax.ShapeDtypeStruct((batch * s_len, hq * HEAD_DIM), BF16),
        compiler_params=_cparams(("parallel", "parallel")),
        name="swa_attention",
    )(qd, kd, kd, z, z, sink_b)


_HIGH_HALF = -65536


def _pack_halves(x):
    half = x.shape[1] // 2
    lo = lax.bitcast_convert_type(x[:, :half].astype(jnp.bfloat16).astype(F32), I32)
    hi = lax.bitcast_convert_type(x[:, half:].astype(jnp.bfloat16).astype(F32), I32)
    return ((lo >> 16) & 0xFFFF) | (hi & _HIGH_HALF)


def _unpack_halves(p):
    lo = lax.bitcast_convert_type(p << 16, F32)
    hi = lax.bitcast_convert_type(p & _HIGH_HALF, F32)
    return lo, hi


def _route_kernel(x_ref, g_ref, sc_ref, sh_ref, wr_ref, br_ref, tri_ref,
                  h_ref, idx_ref, wgt_ref, rank_ref, cnt_ref, carry_ref):
    @pl.when(pl.program_id(0) == 0)
    def _():
        carry_ref[...] = jnp.zeros_like(carry_ref)

    x = x_ref[...]
    y = x * lax.rsqrt(jnp.mean(x * x, axis=-1, keepdims=True) + EPS) * g_ref[...]
    hmod = y * (1.0 + sc_ref[0]) + sh_ref[0]
    h_ref[...] = _pack_halves(hmod)
    logits = jnp.dot(hmod, wr_ref[...], preferred_element_type=F32, precision=lax.Precision.HIGHEST) + br_ref[...]
    lane = lax.broadcasted_iota(I32, logits.shape, 1)
    lane_f = lane.astype(F32)
    logits = jnp.where(lane < N_EXPERTS, logits, -jnp.inf)
    onehots, vals, firsts = [], [], []
    for _ in range(MOE_TOPK):
        best = jnp.max(logits, axis=-1, keepdims=True)
        first = jnp.min(jnp.where(logits == best, lane_f, float(LANES)), axis=-1, keepdims=True)
        pick = lane_f == first
        onehots.append(pick)
        vals.append(best)
        firsts.append(first.astype(I32))
        logits = jnp.where(pick, -jnp.inf, logits)
    exps = [jnp.exp(v - vals[0]) for v in vals]
    den = exps[0]
    for e in exps[1:]:
        den = den + e
    chosen_f = jnp.zeros(logits.shape, F32)
    for o in onehots:
        chosen_f = jnp.where(o, 1.0, chosen_f)
    before = jnp.dot(tri_ref[...], chosen_f.astype(BF16), preferred_element_type=F32) + carry_ref[...]
    idx_out = jnp.zeros(logits.shape, I32)
    wgt_out = jnp.zeros(logits.shape, F32)
    rank_out = jnp.zeros(logits.shape, I32)
    for k in range(MOE_TOPK):
        rk = jnp.sum(jnp.where(onehots[k], before, 0.0), axis=-1, keepdims=True).astype(I32)
        idx_out = jnp.where(lane == k, firsts[k], idx_out)
        wgt_out = jnp.where(lane == k, exps[k] / den, wgt_out)
        rank_out = jnp.where(lane == k, rk, rank_out)
    idx_ref[...] = idx_out
    wgt_ref[...] = wgt_out
    rank_ref[...] = rank_out
    carry_ref[...] = carry_ref[...] + jnp.sum(chosen_f, axis=0, keepdims=True)
    cnt_ref[...] = carry_ref[...]


def _moe_route(x2, g, scale, shift, w_router, b_router, rows_per_batch, tm=256):
    n, d = x2.shape
    tpb = rows_per_batch // tm
    wr = jnp.pad(w_router, ((0, 0), (0, LANES - N_EXPERTS)))
    br = jnp.pad(b_router, (0, LANES - N_EXPERTS)).reshape(1, LANES)
    tri = jnp.tril(jnp.ones((tm, tm), F32), -1).astype(BF16)
    tok_spec = pl.BlockSpec((tm, LANES), lambda i: (i, 0))
    return pl.pallas_call(
        _route_kernel,
        grid=(n // tm,),
        in_specs=[pl.BlockSpec((tm, d), lambda i: (i, 0)),
                  pl.BlockSpec((1, d), lambda i: (0, 0)),
                  pl.BlockSpec((1, 1, d), lambda i: (i // tpb, 0, 0)),
                  pl.BlockSpec((1, 1, d), lambda i: (i // tpb, 0, 0)),
                  pl.BlockSpec((d, LANES), lambda i: (0, 0)),
                  pl.BlockSpec((1, LANES), lambda i: (0, 0)),
                  pl.BlockSpec((tm, tm), lambda i: (0, 0))],
        out_specs=[pl.BlockSpec((tm, d // 2), lambda i: (i, 0)), tok_spec, tok_spec, tok_spec,
                   pl.BlockSpec((1, LANES), lambda i: (0, 0))],
        out_shape=[jax.ShapeDtypeStruct((n, d // 2), I32), jax.ShapeDtypeStruct((n, LANES), I32),
                   jax.ShapeDtypeStruct((n, LANES), F32), jax.ShapeDtypeStruct((n, LANES), I32),
                   jax.ShapeDtypeStruct((1, LANES), F32)],
        scratch_shapes=[pltpu.VMEM((1, LANES), F32)],
        compiler_params=_cparams(("arbitrary",)),
        name="moe_route",
    )(x2, g.reshape(1, d), scale[:, None, :], shift[:, None, :], wr, br, tri)


def _invert_kernel(lo_ref, hi_ref, pos_ref, src_ref, *, chunk):
    step = pl.program_id(0)

    @pl.when(step == 0)
    def _():
        def fill_group(g, carry):
            def zero(p, c):
                src_ref[p] = 0
                return c
            lax.fori_loop(lo_ref[g], hi_ref[g], zero, 0)
            return carry
        lax.fori_loop(0, lo_ref.shape[0], fill_group, 0)

    tokens = chunk // MOE_TOPK

    def put(t, carry):
        for k in range(MOE_TOPK):
            src_ref[pos_ref[t * MOE_TOPK + k]] = step * tokens + t
        return carry

    lax.fori_loop(0, tokens, put, 0, unroll=4)


def _moe_invert(pos, pad_lo, pad_hi, n_rows, chunk=2048):
    n_assign = pos.shape[0]
    return pl.pallas_call(
        functools.partial(_invert_kernel, chunk=chunk),
        grid_spec=pltpu.PrefetchScalarGridSpec(
            num_scalar_prefetch=2, grid=(n_assign // chunk,),
            in_specs=[pl.BlockSpec((chunk,), lambda i, lo, hi: (i,), memory_space=pltpu.SMEM)],
            out_specs=pl.BlockSpec((n_rows,), lambda i, lo, hi: (0,), memory_space=pltpu.SMEM)),
        out_shape=jax.ShapeDtypeStruct((n_rows,), I32),
        compiler_params=_cparams(("arbitrary",), unchecked=True),
        name="moe_invert",
    )(pad_lo, pad_hi, pos)


def _expert_kernel(te_ref, nt_ref, src_ref, h_ref, wi_ref, bi_ref, perm_ref, wo_ref, bo_ref, y_ref,
                   xbuf_ref, sem):
    i = pl.program_id(0)
    n_used = nt_ref[0]
    n_slots, tm = xbuf_ref.shape[0], xbuf_ref.shape[1]

    def row_copy(row, slot, t):
        return pltpu.make_async_copy(h_ref.at[pl.ds(src_ref[row], 1)], xbuf_ref.at[slot, pl.ds(t, 1)], sem.at[slot])

    def wait_slot(slot):
        pltpu.make_async_copy(h_ref.at[pl.ds(0, tm)], xbuf_ref.at[slot], sem.at[slot]).wait()

    def gather(tile, slot):
        base = jnp.minimum(tile, n_used - 1) * tm
        for t in range(tm):
            row_copy(base + t, slot, t).start()

    @pl.when(i == 0)
    def _():
        gather(0, 0)
        gather(1, 1)

    for slot in range(n_slots):
        @pl.when((i < n_used) & (i % n_slots == slot))
        def _(slot=slot):
            wait_slot(slot)
            x_lo, x_hi = _unpack_halves(xbuf_ref[slot])
            x_lo, x_hi = x_lo.astype(BF16), x_hi.astype(BF16)
            half = x_lo.shape[1]
            gather(i + 2, (slot + 2) % n_slots)
            hh = (jnp.dot(x_lo, wi_ref[0, :half], preferred_element_type=F32)
                  + jnp.dot(x_hi, wi_ref[0, half:], preferred_element_type=F32) + bi_ref[0]).astype(BF16)
            hp = jnp.dot(hh, perm_ref[...], preferred_element_type=F32)
            x_glu = jnp.minimum(hp[:, :EXPERT_FF], SWIGLU_LIMIT)
            x_lin = jnp.clip(hp[:, EXPERT_FF:], -SWIGLU_LIMIT, SWIGLU_LIMIT)
            act = x_glu * (1.0 / (1.0 + jnp.exp(-SWIGLU_ALPHA * x_glu))) * (x_lin + 1.0)
            y_ref[...] = _pack_halves(jnp.dot(act.astype(BF16), wo_ref[0], preferred_element_type=F32) + bo_ref[0])

            @pl.when(i == n_used - 1)
            def _():
                wait_slot((slot + 1) % n_slots)
                wait_slot((slot + 2) % n_slots)

    @pl.when(i >= n_used)
    def _():
        y_ref[...] = jnp.zeros_like(y_ref)


def _moe_experts(h2, src, tile_expert, n_tiles_used, w_in, b_in, w_out, b_out):
    n, dh = h2.shape
    d = 2 * dh
    r = src.shape[0]
    tm = MOE_ROW_TILE
    f2 = w_in.shape[2]
    ff = w_out.shape[1]
    col = np.arange(f2)
    perm = np.zeros((f2, f2), np.float32)
    perm[col, np.where(col % 2 == 0, col // 2, ff + col // 2)] = 1.0
    grid_spec = pltpu.PrefetchScalarGridSpec(
        num_scalar_prefetch=3, grid=(r // tm,),
        in_specs=[pl.BlockSpec(memory_space=pl.ANY),
                  pl.BlockSpec((1, d, f2), lambda i, te, nt, src: (te[i], 0, 0)),
                  pl.BlockSpec((1, 1, f2), lambda i, te, nt, src: (te[i], 0, 0)),
                  pl.BlockSpec((f2, f2), lambda i, te, nt, src: (0, 0)),
                  pl.BlockSpec((1, ff, d), lambda i, te, nt, src: (te[i], 0, 0)),
                  pl.BlockSpec((1, 1, d), lambda i, te, nt, src: (te[i], 0, 0))],
        out_specs=pl.BlockSpec((tm, dh), lambda i, te, nt, src: (i, 0)),
        scratch_shapes=[pltpu.VMEM((3, tm, dh), I32), pltpu.SemaphoreType.DMA((3,))])
    return pl.pallas_call(
        _expert_kernel, grid_spec=grid_spec,
        out_shape=jax.ShapeDtypeStruct((r, dh), I32),
        compiler_params=_cparams(("arbitrary",), unchecked=True),
        name="moe_experts",
    )(tile_expert, n_tiles_used, src, h2, w_in, b_in, jnp.asarray(perm, BF16), w_out, b_out)


def _combine_kernel(pos_ref, y_ref, x_ref, w_ref, g_ref, o_ref, buf_ref, wb_ref, sem, *, tm):
    i = pl.program_id(0)
    n_steps = pl.num_programs(0)

    def gather(tile, slot):
        base = tile * tm * MOE_TOPK
        for t in range(tm):
            for k in range(MOE_TOPK):
                pltpu.make_async_copy(y_ref.at[pl.ds(pos_ref[base + t * MOE_TOPK + k], 1)],
                                      buf_ref.at[slot, k, pl.ds(t, 1)], sem.at[slot]).start()

    @pl.when(i == 0)
    def _():
        gather(0, 0)

    @pl.when((i + 1 < n_steps) & (i % 2 == 0))
    def _():
        gather(i + 1, 1)

    @pl.when((i + 1 < n_steps) & (i % 2 == 1))
    def _():
        gather(i + 1, 0)

    slot = i % 2
    for k in range(MOE_TOPK):
        pltpu.make_async_copy(y_ref.at[pl.ds(0, tm)], buf_ref.at[slot, k], sem.at[slot]).wait()
    w = w_ref[...]
    half = buf_ref.shape[3]
    for k in range(MOE_TOPK):
        wb_ref[k] = jnp.broadcast_to(w[:, k:k + 1], (tm, LANES))
    for c in range(half // LANES):
        lo_cols = slice(c * LANES, (c + 1) * LANES)
        hi_cols = slice(half + c * LANES, half + (c + 1) * LANES)
        mix_lo = jnp.zeros((tm, LANES), F32)
        mix_hi = jnp.zeros((tm, LANES), F32)
        for k in range(MOE_TOPK):
            y_lo, y_hi = _unpack_halves(buf_ref[slot, k, :, lo_cols])
            mix_lo = mix_lo + y_lo * wb_ref[k]
            mix_hi = mix_hi + y_hi * wb_ref[k]
        o_ref[:, lo_cols] = x_ref[:, lo_cols] + g_ref[0, :, lo_cols] * mix_lo
        o_ref[:, hi_cols] = x_ref[:, hi_cols] + g_ref[0, :, hi_cols] * mix_hi


def _moe_combine(y, pos, x2, wgt, gate, rows_per_batch, tm=128):
    n, d = x2.shape
    tpb = rows_per_batch // tm
    grid_spec = pltpu.PrefetchScalarGridSpec(
        num_scalar_prefetch=1, grid=(n // tm,),
        in_specs=[pl.BlockSpec(memory_space=pl.ANY),
                  pl.BlockSpec((tm, d), lambda i, pos: (i, 0)),
                  pl.BlockSpec((tm, LANES), lambda i, pos: (i, 0)),
                  pl.BlockSpec((1, 1, d), lambda i, pos: (i // tpb, 0, 0))],
        out_specs=pl.BlockSpec((tm, d), lambda i, pos: (i, 0)),
        scratch_shapes=[pltpu.VMEM((2, MOE_TOPK, tm, d // 2), I32), pltpu.VMEM((MOE_TOPK, tm, LANES), F32),
                        pltpu.SemaphoreType.DMA((2,))])
    return pl.pallas_call(
        functools.partial(_combine_kernel, tm=tm), grid_spec=grid_spec,
        out_shape=jax.ShapeDtypeStruct((n, d), F32),
        compiler_params=_cparams(("arbitrary",), unchecked=True),
        name="moe_combine",
    )(pos, y, x2, wgt, gate[:, None, :])


def _moe_block(x2, g, scale, shift, gate, w_router, b_router, w_exp_in, b_exp_in, w_exp_out, b_exp_out,
               expert_base, rows_per_batch):
    n, d = x2.shape
    tm = MOE_ROW_TILE
    h2, idx, wgt, rank, counts = _moe_route(x2, g, scale, shift, w_router, b_router, rows_per_batch)
    cnt = counts[0, :N_EXPERTS].astype(I32)
    padded = (cnt + tm - 1) // tm * tm
    ends = jnp.cumsum(padded)
    starts = ends - padded
    n_rows = n * MOE_TOPK + N_EXPERTS * tm
    n_tiles = n_rows // tm
    tile_start = jnp.arange(n_tiles, dtype=I32) * tm
    tile_expert = jnp.minimum(jnp.sum((ends[None, :] <= tile_start[:, None]).astype(I32), axis=1), N_EXPERTS - 1)
    n_tiles_used = (ends[-1] // tm).astype(I32).reshape(1)
    e_flat = idx[:, :MOE_TOPK].reshape(-1)
    pos = (starts[e_flat] + rank[:, :MOE_TOPK].reshape(-1)).astype(I32)
    pad_lo = jnp.concatenate([starts + cnt, ends[-1:]]).astype(I32)
    pad_hi = jnp.concatenate([ends, jnp.full((1,), n_rows, I32)]).astype(I32)
    src = _moe_invert(pos, pad_lo, pad_hi, n_rows)
    y = _moe_experts(h2, src, tile_expert + expert_base, n_tiles_used, w_exp_in, b_exp_in, w_exp_out, b_exp_out)
    return _moe_combine(y, pos, x2, wgt, gate, rows_per_batch)


def _even_mixer(h, x2, gate, batch, s_len, w_in, g_cq, w_uq, w_iq, gq_a, gk_a, g_kidx, b_kidx, gq_b, gk_b,
                b_f, w_out):
    gw = GROUP_WIDTH
    o_cq, o_ka, o_va, o_ki, o_wi, o_qb, o_kb, o_vb, o_fb, o_gb = np.cumsum(
        [0, DSA_Q_LORA, HEAD_DIM, HEAD_DIM, HEAD_DIM, DSA_IDX_HEADS, gw, gw, gw, GROUP_HEADS]).tolist()
    cols = lambda o, wdt: w_in[:, o:o + wdt]
    n_main = 4 * gw + DSA_Q_LORA + 3 * HEAD_DIM
    w_main = jnp.concatenate([cols(o_qb, gw), cols(o_kb, gw), cols(o_vb, gw), cols(o_gb, gw),
                              cols(o_cq, DSA_Q_LORA), cols(o_ka, HEAD_DIM), cols(o_va, HEAD_DIM),
                              cols(o_ki, HEAD_DIM), jnp.zeros((w_in.shape[0], -n_main % 512), F32)],
                             axis=1).astype(BF16)
    c_qb, c_kb, c_vb, c_gb, c_cq = 0, gw, 2 * gw, 3 * gw, 4 * gw
    c_ka = c_cq + DSA_Q_LORA
    c_va, c_ki = c_ka + HEAD_DIM, c_ka + 2 * HEAD_DIM
    d = w_in.shape[0]
    w_small = jnp.concatenate([cols(o_fb, GROUP_HEADS), cols(o_wi, DSA_IDX_HEADS),
                               jnp.zeros((d, LANES - GROUP_HEADS - DSA_IDX_HEADS), F32)], axis=1).astype(BF16)
    z = _matmul(h, w_main, BF16, "even_in_proj")
    zs = _matmul(h, w_small, F32, "even_in_proj_small")
    scale = Q_SCALE

    cq = _rowprep(z, c_cq, DSA_Q_LORA, batch, s_len, hw=DSA_Q_LORA, norm="rms", gain=g_cq, name="dsa_cq_norm")
    qa_raw = _matmul(cq, w_uq.astype(BF16), BF16, "dsa_q_up")
    qi_raw = _matmul(cq, w_iq.astype(BF16), BF16, "dsa_idx_q_up")
    qa = _rowprep(qa_raw, 0, gw, batch, s_len, norm="rms", gain=gq_a, rope_dim=HEAD_DIM, out_scale=scale,
                  head_major=True, name="dsa_q_prep")
    qi = _rowprep(qi_raw, 0, DSA_IDX_HEADS * HEAD_DIM, batch, s_len, rope_dim=DSA_IDX_ROPE, head_major=True,
                  name="dsa_idx_q_prep")
    ka = _rowprep(z, c_ka, HEAD_DIM, batch, s_len, norm="rms", gain=gk_a, rope_dim=HEAD_DIM, name="dsa_k_prep")
    ki = _rowprep(z, c_ki, HEAD_DIM, batch, s_len, norm="ln", gain=g_kidx, beta=b_kidx, rope_dim=DSA_IDX_ROPE,
                  name="dsa_idx_k_prep")
    va = z[:, c_va:c_va + HEAD_DIM]
    wi = jnp.pad(zs[:, GROUP_HEADS:GROUP_HEADS + DSA_IDX_HEADS] * (DSA_IDX_HEADS ** -0.5 * HEAD_DIM ** -0.5),
                 ((0, 0), (0, LANES - DSA_IDX_HEADS)))
    o_a = _dsa_attention(qi, ki, wi, qa, ka, va, batch, s_len)

    qb = _rowprep(z, c_qb, gw, batch, s_len, norm="rms", gain=gq_b, out_scale=scale, head_major=True,
                  name="fox_q_prep")
    kb = _rowprep(z, c_kb, gw, batch, s_len, norm="rms", gain=gk_b, head_major=True, name="fox_k_prep")
    bias_row = jnp.pad(b_f.astype(F32), (0, LANES - GROUP_HEADS)).reshape(1, LANES)
    cum = _fox_cumsum(zs, bias_row, batch, s_len)
    o_b = _fox_attention(qb, kb, z, c_vb, c_gb, cum, batch, s_len)
    return _out_proj(o_a, o_b, w_out.astype(BF16), x2, gate, s_len, "even_out_proj")


def _odd_mixer(h, x2, gate, batch, s_len, w_in, gq_c, gk_c, gq_d, gk_d, sinks, w_out):
    gw = GROUP_WIDTH
    kvw = SWA_KV_HEADS * HEAD_DIM
    c_qc, c_kc, c_vc, c_qd, c_kd = 0, gw, 2 * gw, 3 * gw, 4 * gw
    c_vd = c_kd + kvw
    z = _matmul(h, w_in.astype(BF16), BF16, "odd_in_proj")
    scale = Q_SCALE
    qc = _rowprep(z, c_qc, gw, batch, s_len, norm="rms", gain=gq_c, rope_dim=HEAD_DIM, out_scale=scale,
                  head_major=True, name="moba_q_prep")
    kc, kmean = _rowprep(z, c_kc, gw, batch, s_len, norm="rms", gain=gk_c, rope_dim=HEAD_DIM, head_major=True,
                         want_mean=True, name="moba_k_prep")
    o_c = _moba_attention(qc, kc, kmean, z, c_vc, batch, s_len)
    qd = _rowprep(z, c_qd, gw, batch, s_len, norm="rms", gain=gq_d, rope_dim=HEAD_DIM, out_scale=scale,
                  head_major=True, name="swa_q_prep")
    kd = _rowprep(z, c_kd, kvw, batch, s_len, norm="rms", gain=gk_d, rope_dim=HEAD_DIM, head_major=True,
                  name="swa_k_prep")
    o_d = _swa_attention(qd, kd, z, c_vd, sinks, batch, s_len)
    return _out_proj(o_c, o_d, w_out.astype(BF16), x2, gate, s_len, "odd_out_proj")


def kernel(x, c, g_norm_mix, g_norm_ffn, w_ada, b_ada, w_in_even, g_cq, w_uq, w_iq, gq_a, gk_a, g_kidx, b_kidx,
           gq_b, gk_b, b_forget, w_out_even, w_in_odd, gq_c, gk_c, gq_d, gk_d, sinks_d, w_out_odd,
           w_router, b_router, w_exp_in, b_exp_in, w_exp_out, b_exp_out):
    batch, s_len, d = x.shape
    depth = w_ada.shape[0]
    mod = _ada_modulation(c, w_ada, b_ada)
    x2 = x.reshape(batch * s_len, d)
    n_exp = w_exp_in.shape[1]
    w_in_all = w_exp_in.astype(BF16).reshape((depth * n_exp,) + w_exp_in.shape[2:])
    w_out_all = w_exp_out.astype(BF16).reshape((depth * n_exp,) + w_exp_out.shape[2:])
    b_in_all = b_exp_in.reshape(depth * n_exp, 1, -1)
    b_out_all = b_exp_out.reshape(depth * n_exp, 1, -1)
    for layer in range(depth):
        shift_m, scale_m, gate_m, shift_f, scale_f, gate_f = [mod[layer, :, i * d:(i + 1) * d] for i in range(6)]
        h = _norm_modulate(x2, g_norm_mix[layer], scale_m, shift_m, s_len)
        j = layer // 2
        if layer % 2 == 0:
            x2 = _even_mixer(h, x2, gate_m, batch, s_len, w_in_even[j], g_cq[j], w_uq[j], w_iq[j], gq_a[j],
                             gk_a[j], g_kidx[j], b_kidx[j], gq_b[j], gk_b[j], b_forget[j], w_out_even[j])
        else:
            x2 = _odd_mixer(h, x2, gate_m, batch, s_len, w_in_odd[j], gq_c[j], gk_c[j], gq_d[j], gk_d[j],
                            sinks_d[j], w_out_odd[j])
        x2 = _moe_block(x2, g_norm_ffn[layer], scale_f, shift_f, gate_f, w_router[layer], b_router[layer],
                        w_in_all, b_in_all, w_out_all, b_out_all, layer * n_exp, s_len)
    return x2.reshape(batch, s_len, d)
```
